```python
import math
import jax
import jax.numpy as jnp
from jax import lax
import numpy as np

D_MODEL = 2048
BATCH = 1
SEQ = 16384
DEPTH = 4

D_FF = 5504
N_SUB = 3
N_MOD = 3 * N_SUB
A_HEADS = 8
A_QK_DIM = 64
A_V_DIM = 2 * A_QK_DIM
A_WIDTH = A_HEADS * A_V_DIM
B_GROUPS = 8
B_GROUP_DIM = 128
B_WIDTH = B_GROUPS * B_GROUP_DIM
CHUNK = 128
AB_IN = 3 * A_WIDTH + 2 * B_WIDTH
AB_OUT = A_WIDTH + B_WIDTH
C_HEADS = 16
C_KV_HEADS = 4
C_GROUP = C_HEADS // C_KV_HEADS
C_HEAD_DIM = 128
C_WIDTH = C_HEADS * C_HEAD_DIM
C_KV_WIDTH = C_KV_HEADS * C_HEAD_DIM
IDX_HEADS = 16
IDX_DIM = 64
C_IN = C_WIDTH + 2 * C_KV_WIDTH + IDX_HEADS * IDX_DIM + IDX_DIM + IDX_HEADS
TOPK_MAX = 256
REL_BUCKETS = 32
REL_MAX_DIST = 128
REL_HEADS = 16
Q_BLOCK = 128
EPS = 1e-6
N_EVEN = (DEPTH + 1) // 2
N_ODD = DEPTH // 2

kernel_name = 'hybrid_diffattn_sgmlp_dsa_macaron_adaln'


def rms_norm(x, g):
    xf = x.astype(jnp.float32)
    y = xf * lax.rsqrt(jnp.mean(xf * xf, axis=-1, keepdims=True) + EPS)
    return (y * g.astype(jnp.float32)).astype(x.dtype)


def layer_norm(x, g, b):
    xf = x.astype(jnp.float32)
    mu = jnp.mean(xf, axis=-1, keepdims=True)
    var = jnp.mean(jnp.square(xf - mu), axis=-1, keepdims=True)
    y = (xf - mu) * lax.rsqrt(var + EPS)
    return (y * g.astype(jnp.float32) + b.astype(jnp.float32)).astype(x.dtype)


def rel_bucket(dist):
    n = jnp.maximum(dist, 0)
    max_exact = REL_BUCKETS // 2
    nf = jnp.maximum(n, 1).astype(jnp.float32)
    large = max_exact + (jnp.log(nf / max_exact) / math.log(REL_MAX_DIST / max_exact)
                         * (REL_BUCKETS - max_exact)).astype(jnp.int32)
    large = jnp.minimum(large, REL_BUCKETS - 1)
    return jnp.where(n < max_exact, n, large)


def swiglu(h, w1, w2):
    g, u = jnp.split(h @ w1, 2, axis=-1)
    return (jax.nn.silu(g) * u) @ w2


def diff_attention(q, k, v, lam, lam_init, subln_g, rel_table):
    B, S = q.shape[0], q.shape[1]
    nblk = S // Q_BLOCK
    key_pos = jnp.arange(S)
    table = rel_table.reshape(REL_BUCKETS, A_HEADS, 2)
    qb = q.reshape(B, nblk, Q_BLOCK, A_HEADS, 2, A_QK_DIM).swapaxes(0, 1)
    scale = A_QK_DIM ** -0.5

    def block(args):
        qblk, i = args
        qpos = i * Q_BLOCK + jnp.arange(Q_BLOCK)
        dist = qpos[:, None] - key_pos[None, :]
        bias = table[rel_bucket(dist)].transpose(2, 3, 0, 1)
        logits = (jnp.einsum('bqhmd,bshmd->bhmqs', qblk, k).astype(jnp.float32) * scale
                  + bias.astype(jnp.float32))
        logits = jnp.where(dist >= 0, logits, -jnp.inf)
        p = jax.nn.softmax(logits, axis=-1)
        attn = (p[:, :, 0] - lam * p[:, :, 1]).astype(v.dtype)
        return jnp.einsum('bhqs,bshe->bqhe', attn, v)

    out = lax.map(block, (qb, jnp.arange(nblk)))
    out = out.swapaxes(0, 1).reshape(B, S, A_HEADS, A_V_DIM)
    out = rms_norm(out, subln_g) * (1.0 - lam_init)
    return out.reshape(B, S, A_WIDTH)


def chunked_spatial_gating(u, z, ln_g, ln_b, w_s, b_s):
    B, S = u.shape[0], u.shape[1]
    zn = layer_norm(z, ln_g, ln_b)
    zc = zn.reshape(B, S // CHUNK, CHUNK, B_GROUPS, B_GROUP_DIM)
    mask = jnp.tril(jnp.ones((CHUNK, CHUNK), dtype=bool))
    w = jnp.where(mask, w_s, 0)
    sz = jnp.einsum('gts,bnsgc->bntgc', w, zc) + b_s.T[None, None, :, :, None]
    return u * sz.reshape(B, S, B_WIDTH)


def ab_mixer(h, w_in, w_out, lam_p, subln_g, ln_g, ln_b, w_s, b_s, rel_table, layer_idx):
    B, S = h.shape[0], h.shape[1]
    proj = h @ w_in
    qa, ka, va, zb = jnp.split(proj, [A_WIDTH, 2 * A_WIDTH, 3 * A_WIDTH], axis=-1)
    q = qa.reshape(B, S, A_HEADS, 2, A_QK_DIM)
    k = ka.reshape(B, S, A_HEADS, 2, A_QK_DIM)
    v = va.reshape(B, S, A_HEADS, A_V_DIM)
    lam_init = 0.8 - 0.6 * math.exp(-0.3 * layer_idx)
    lp = lam_p.astype(jnp.float32)
    lam = jnp.exp(jnp.sum(lp[0] * lp[1])) - jnp.exp(jnp.sum(lp[2] * lp[3])) + lam_init
    ya = diff_attention(q, k, v, lam, lam_init, subln_g, rel_table)
    u, z = jnp.split(jax.nn.gelu(zb), 2, axis=-1)
    yb = chunked_spatial_gating(u, z, ln_g, ln_b, w_s, b_s)
    return jnp.concatenate([ya, yb], axis=-1) @ w_out


def dsa_mixer(h, w_in, w_out, rel_table):
    B, S = h.shape[0], h.shape[1]
    proj = h @ w_in
    s1 = C_WIDTH
    s2 = s1 + C_KV_WIDTH
    s3 = s2 + C_KV_WIDTH
    s4 = s3 + IDX_HEADS * IDX_DIM
    s5 = s4 + IDX_DIM
    qc, kc, vc, qi, ki, wi = jnp.split(proj, [s1, s2, s3, s4, s5], axis=-1)
    q = qc.reshape(B, S, C_KV_HEADS, C_GROUP, C_HEAD_DIM)
    k = kc.reshape(B, S, C_KV_HEADS, C_HEAD_DIM)
    v = vc.reshape(B, S, C_KV_HEADS, C_HEAD_DIM)
    q_idx = qi.reshape(B, S, IDX_HEADS, IDX_DIM)
    k_idx = ki
    topk = min(TOPK_MAX, S // 4)
    nblk = S // Q_BLOCK
    key_pos = jnp.arange(S)
    gather = jax.vmap(lambda arr, idx: arr[idx])

    def to_blocks(a):
        return a.reshape(B, nblk, Q_BLOCK, *a.shape[2:]).swapaxes(0, 1)

    def block(args):
        qblk, qiblk, wblk, i = args
        qpos = i * Q_BLOCK + jnp.arange(Q_BLOCK)
        sc = jnp.einsum('bqhd,bsd->bqhs', qiblk, k_idx).astype(jnp.float32) * IDX_DIM ** -0.5
        iscore = jnp.einsum('bqhs,bqh->bqs', jax.nn.relu(sc), wblk.astype(jnp.float32))
        iscore = jnp.where(key_pos[None, None, :] <= qpos[None, :, None], iscore, -jnp.inf)
        _, sel = lax.top_k(iscore, topk)
        valid = sel <= qpos[None, :, None]
        k_sel = gather(k, sel)
        v_sel = gather(v, sel)
        bias = rel_table[rel_bucket(qpos[None, :, None] - sel)]
        bias = bias.reshape(B, Q_BLOCK, topk, C_KV_HEADS, C_GROUP).transpose(0, 3, 4, 1, 2)
        logits = (jnp.einsum('bqgrd,bqkgd->bgrqk', qblk, k_sel).astype(jnp.float32)
                  * C_HEAD_DIM ** -0.5 + bias.astype(jnp.float32))
        logits = jnp.where(valid[:, None, None], logits, -jnp.inf)
        p = jax.nn.softmax(logits, axis=-1).astype(v.dtype)
        return jnp.einsum('bgrqk,bqkgd->bqgrd', p, v_sel)

    out = lax.map(block, (to_blocks(q), to_blocks(q_idx), to_blocks(wi), jnp.arange(nblk)))
    out = out.swapaxes(0, 1).reshape(B, S, C_WIDTH)
    return out @ w_out


def setup_inputs(seed: int = 0) -> dict:
    key = jax.random.key(seed)
    ks = jax.random.split(key, 19)

    def nrm(k, shape, scale):
        return jax.random.normal(k, shape, jnp.float32) * scale

    return {
        'x': nrm(ks[0], (BATCH, SEQ, D_MODEL), 1.0),
        'c': nrm(ks[1], (BATCH, D_MODEL), 1.0),
        'norm_g': 1.0 + nrm(ks[2], (DEPTH, N_SUB, D_MODEL), 0.02),
        'mod_w': nrm(ks[3], (DEPTH, D_MODEL, N_MOD * D_MODEL), 0.1 * D_MODEL ** -0.5),
        'mod_b': nrm(ks[4], (DEPTH, N_MOD * D_MODEL), 0.01),
        'ffn_w1': nrm(ks[5], (DEPTH, 2, D_MODEL, 2 * D_FF), D_MODEL ** -0.5),
        'ffn_w2': nrm(ks[6], (DEPTH, 2, D_FF, D_MODEL), D_FF ** -0.5),
        'rel_table': nrm(ks[7], (REL_BUCKETS, REL_HEADS), 0.5),
        'ab_w_in': nrm(ks[8], (N_EVEN, D_MODEL, AB_IN), D_MODEL ** -0.5),
        'ab_w_out': nrm(ks[9], (N_EVEN, AB_OUT, D_MODEL), AB_OUT ** -0.5),
        'diff_lam': nrm(ks[10], (N_EVEN, 4, A_QK_DIM), 0.1),
        'diff_subln_g': 1.0 + nrm(ks[11], (N_EVEN, A_V_DIM), 0.02),
        'sg_ln_g': 1.0 + nrm(ks[12], (N_EVEN, B_WIDTH), 0.02),
        'sg_ln_b': nrm(ks[13], (N_EVEN, B_WIDTH), 0.02),
        'sg_w': nrm(ks[14], (N_EVEN, B_GROUPS, CHUNK, CHUNK), CHUNK ** -0.5),
        'sg_b': 1.0 + nrm(ks[15], (N_EVEN, B_GROUPS, CHUNK), 0.02),
        'dsa_w_in': nrm(ks[16], (N_ODD, D_MODEL, C_IN), D_MODEL ** -0.5),
        'dsa_w_out': nrm(ks[17], (N_ODD, C_WIDTH, D_MODEL), C_WIDTH ** -0.5),
        'final_g': 1.0 + nrm(ks[18], (D_MODEL,), 0.02),
    }


def reference(x, c, norm_g, mod_w, mod_b, ffn_w1, ffn_w2, rel_table, ab_w_in, ab_w_out,
              diff_lam, diff_subln_g, sg_ln_g, sg_ln_b, sg_w, sg_b, dsa_w_in, dsa_w_out,
              final_g):
    B = x.shape[0]
    cs = jax.nn.silu(c)
    for li in range(DEPTH):
        mod = (cs @ mod_w[li] + mod_b[li]).reshape(B, N_MOD, D_MODEL)

        def pre(h, j):
            shift, scale = mod[:, 3 * j], mod[:, 3 * j + 1]
            return rms_norm(h, norm_g[li, j]) * (1.0 + scale[:, None, :]) + shift[:, None, :]

        def gate(j):
            return (1.0 + mod[:, 3 * j + 2])[:, None, :]

        x = x + 0.5 * gate(0) * swiglu(pre(x, 0), ffn_w1[li, 0], ffn_w2[li, 0])
        h = pre(x, 1)
        if li % 2 == 0:
            j = li // 2
            y = ab_mixer(h, ab_w_in[j], ab_w_out[j], diff_lam[j], diff_subln_g[j],
                         sg_ln_g[j], sg_ln_b[j], sg_w[j], sg_b[j], rel_table, li)
        else:
            j = li // 2
            y = dsa_mixer(h, dsa_w_in[j], dsa_w_out[j], rel_table)
        x = x + gate(1) * y
        x = x + 0.5 * gate(2) * swiglu(pre(x, 2), ffn_w1[li, 1], ffn_w2[li, 1])
    return rms_norm(x, final_g)
```

```python
import functools
import math

import jax
import jax.numpy as jnp
import numpy as np
from jax import lax
from jax.experimental import pallas as pl
from jax.experimental.pallas import tpu as pltpu

F32 = jnp.float32
BF16 = jnp.bfloat16
I32 = jnp.int32

EPS = 1e-6
MASKED = -1e30
LANES = 128
INT_MIN = -(2 ** 31)

A_HEADS = 8
A_QK_DIM = 64
A_V_DIM = 128
A_WIDTH = A_HEADS * A_V_DIM
B_GROUPS = 8
B_GROUP_DIM = 128
B_WIDTH = B_GROUPS * B_GROUP_DIM
CHUNK = 128
C_HEADS = 16
C_KV_HEADS = 4
C_GROUP = C_HEADS // C_KV_HEADS
C_HEAD_DIM = 128
C_WIDTH = C_HEADS * C_HEAD_DIM
C_KV_WIDTH = C_KV_HEADS * C_HEAD_DIM
IDX_HEADS = 16
IDX_DIM = 64
TOPK_MAX = 256
REL_BUCKETS = 32
REL_MAX_DIST = 128
IDX_QBLOCK = 128

VMEM_LIMIT = 56 * 1024 * 1024


def _cparams(sem):
    return pltpu.CompilerParams(dimension_semantics=sem, vmem_limit_bytes=VMEM_LIMIT)


def _tile(n, want):
    if n <= want:
        return n
    t = want
    while n % t:
        t //= 2
    return t


def _mod_kernel(c_ref, w_ref, b_ref, o_ref):
    c = c_ref[...]
    cs = c * (1.0 / (1.0 + jnp.exp(-c)))
    o_ref[0] = jnp.sum(cs * w_ref[0], axis=0, keepdims=True) + b_ref[0]


def _modulation(c, mod_w, mod_b):
    depth, d, n = mod_w.shape
    tn = _tile(n, 1024)
    out = pl.pallas_call(
        _mod_kernel,
        grid=(depth, n // tn),
        in_specs=[pl.BlockSpec((d, 1), lambda l, j: (0, 0)),
                  pl.BlockSpec((1, d, tn), lambda l, j: (l, 0, j)),
                  pl.BlockSpec((1, 1, tn), lambda l, j: (l, 0, j))],
        out_specs=pl.BlockSpec((1, 1, tn), lambda l, j: (l, 0, j)),
        out_shape=jax.ShapeDtypeStruct((depth, 1, n), F32),
        compiler_params=_cparams(("arbitrary", "arbitrary")),
        name="adaln_mod",
    )(c.reshape(d, 1), mod_w, mod_b.reshape(depth, 1, n))
    return out.reshape(depth, n)


def _prenorm(x, vec_ref):
    ms = jnp.mean(x * x, axis=-1, keepdims=True)
    y = x * lax.rsqrt(ms + EPS) * vec_ref[0:1, :]
    return y * (1.0 + vec_ref[2:3, :]) + vec_ref[1:2, :]


def _ffn_kernel(x_ref, vec_ref, w1g_ref, w1u_ref, w2_ref, o_ref, hn_sc, *, nf, final):
    f = pl.program_id(1)

    @pl.when(f == 0)
    def _():
        hn_sc[...] = _prenorm(x_ref[...], vec_ref).astype(BF16)
        o_ref[...] = jnp.zeros_like(o_ref)

    hn = hn_sc[...]
    g = jnp.dot(hn, w1g_ref[...], preferred_element_type=F32)
    u = jnp.dot(hn, w1u_ref[...], preferred_element_type=F32)
    a = (g * (1.0 / (1.0 + jnp.exp(-g))) * u).astype(BF16)
    o_ref[...] += jnp.dot(a, w2_ref[...], preferred_element_type=F32)

    @pl.when(f == nf - 1)
    def _():
        y = x_ref[...] + 0.5 * (1.0 + vec_ref[3:4, :]) * o_ref[...]
        if final:
            ms = jnp.mean(y * y, axis=-1, keepdims=True)
            y = y * lax.rsqrt(ms + EPS) * vec_ref[4:5, :]
        o_ref[...] = y


def _ffn(x, vec, w1g, w1u, w2, *, final):
    s, d = x.shape
    fp = w1g.shape[1]
    tm = _tile(s, 512)
    tf = _tile(fp, 512)
    nf = fp // tf
    return pl.pallas_call(
        functools.partial(_ffn_kernel, nf=nf, final=final),
        grid=(s // tm, nf),
        in_specs=[pl.BlockSpec((tm, d), lambda i, f: (i, 0)),
                  pl.BlockSpec((8, d), lambda i, f: (0, 0)),
                  pl.BlockSpec((d, tf), lambda i, f: (0, f)),
                  pl.BlockSpec((d, tf), lambda i, f: (0, f)),
                  pl.BlockSpec((tf, d), lambda i, f: (f, 0))],
        out_specs=pl.BlockSpec((tm, d), lambda i, f: (i, 0)),
        out_shape=jax.ShapeDtypeStruct((s, d), F32),
        scratch_shapes=[pltpu.VMEM((tm, d), BF16)],
        compiler_params=_cparams(("parallel", "arbitrary")),
        name="swiglu_halfstep",
    )(x, vec, w1g, w1u, w2)


def _proj_kernel(x_ref, vec_ref, w_ref, o_ref, hn_sc):
    @pl.when(pl.program_id(1) == 0)
    def _():
        hn_sc[...] = _prenorm(x_ref[...], vec_ref).astype(BF16)

    o_ref[...] = jnp.dot(hn_sc[...], w_ref[...], preferred_element_type=F32).astype(o_ref.dtype)


def _proj(x, vec, w, out_dtype):
    s, d = x.shape
    n = w.shape[1]
    tm = _tile(s, 512)
    tn = _tile(n, 512)
    return pl.pallas_call(
        _proj_kernel,
        grid=(s // tm, n // tn),
        in_specs=[pl.BlockSpec((tm, d), lambda i, j: (i, 0)),
                  pl.BlockSpec((8, d), lambda i, j: (0, 0)),
                  pl.BlockSpec((d, tn), lambda i, j: (0, j))],
        out_specs=pl.BlockSpec((tm, tn), lambda i, j: (i, j)),
        out_shape=jax.ShapeDtypeStruct((s, n), out_dtype),
        scratch_shapes=[pltpu.VMEM((tm, d), BF16)],
        compiler_params=_cparams(("parallel", "arbitrary")),
        name="norm_mod_proj",
    )(x, vec, w)


def _outproj_kernel(*refs, n_in):
    x_ref, vec_ref = refs[0], refs[1]
    lhs = refs[2:2 + n_in]
    ws = refs[2 + n_in:2 + 2 * n_in]
    o_ref = refs[2 + 2 * n_in]
    acc = jnp.dot(lhs[0][...], ws[0][...], preferred_element_type=F32)
    for a, w in zip(lhs[1:], ws[1:]):
        acc += jnp.dot(a[...], w[...], preferred_element_type=F32)
    o_ref[...] = x_ref[...] + (1.0 + vec_ref[3:4, :]) * acc


def _outproj(x, vec, lhs, ws):
    s, d = x.shape
    tm = _tile(s, 512)
    tn = _tile(d, 512)
    n_in = len(lhs)
    in_specs = [pl.BlockSpec((tm, tn), lambda i, j: (i, j)),
                pl.BlockSpec((8, tn), lambda i, j: (0, j))]
    in_specs += [pl.BlockSpec((tm, a.shape[1]), lambda i, j: (i, 0)) for a in lhs]
    in_specs += [pl.BlockSpec((w.shape[0], tn), lambda i, j: (0, j)) for w in ws]
    return pl.pallas_call(
        functools.partial(_outproj_kernel, n_in=n_in),
        grid=(s // tm, d // tn),
        in_specs=in_specs,
        out_specs=pl.BlockSpec((tm, tn), lambda i, j: (i, j)),
        out_shape=jax.ShapeDtypeStruct((s, d), F32),
        compiler_params=_cparams(("parallel", "arbitrary")),
        name="outproj_residual",
    )(x, vec, *lhs, *ws)


def _flash_kernel(*refs, nh, tq, diff, lam_scale):
    if diff:
        lam_ref, q_ref, k_ref, v_ref, nb_ref, g_ref, o_ref, qs_sc, m_sc, l_sc, acc_sc = refs
        mask_ref = None
    else:
        q_ref, k_ref, v_ref, nb_ref, mask_ref, o_ref, qs_sc, m_sc, l_sc, acc_sc = refs
    tk = tq
    rows = nh * tq
    hd = k_ref.shape[1]
    i = pl.program_id(1)

    if diff:
        q = q_ref[...]
        lane = lax.broadcasted_iota(I32, q.shape, 1)
        zero = jnp.zeros_like(q)
        qs_sc[0:tq, :] = jnp.where(lane < A_QK_DIM, q, zero)
        qs_sc[tq:2 * tq, :] = jnp.where(lane >= A_QK_DIM, q, zero)
    else:
        for r in range(nh):
            qs_sc[r * tq:(r + 1) * tq, :] = q_ref[:, r * hd:(r + 1) * hd]
    m_sc[...] = jnp.full(m_sc.shape, -jnp.inf, F32)
    l_sc[...] = jnp.zeros(l_sc.shape, F32)
    acc_sc[...] = jnp.zeros(acc_sc.shape, F32)

    def step(j, bias):
        start = pl.multiple_of(j * tk, tk)
        kb = k_ref[pl.ds(start, tk), :]
        vb = v_ref[pl.ds(start, tk), :]
        s = lax.dot_general(qs_sc[...], kb, (((1,), (1,)), ((), ())), preferred_element_type=F32)
        if bias is not None:
            s = s + bias
        if mask_ref is not None:
            mk = mask_ref[j].astype(F32)
            s = (s.reshape(nh, tq, tk) + mk[None]).reshape(rows, tk)
        m_old = m_sc[...]
        m_new = jnp.maximum(m_old, jnp.max(s, axis=1, keepdims=True))
        alpha = jnp.exp(m_old - m_new)
        p = jnp.exp(s - m_new)
        l_sc[...] = alpha * l_sc[...] + jnp.sum(p, axis=1, keepdims=True)
        acc_sc[...] = alpha * acc_sc[...] + jnp.dot(p.astype(BF16), vb, preferred_element_type=F32)
        m_sc[...] = m_new

    def far_body(j, carry):
        step(j, None)
        return carry

    lax.fori_loop(0, jnp.maximum(i - 1, 0), far_body, 0)

    @pl.when(i > 0)
    def _():
        step(i - 1, nb_ref[0, :, 0:tk])

    step(i, nb_ref[0, :, tk:2 * tk])

    o = acc_sc[...] / l_sc[...]
    if diff:
        dlt = o[0:tq] - lam_ref[0] * o[tq:2 * tq]
        ms = jnp.mean(dlt * dlt, axis=-1, keepdims=True)
        o_ref[...] = ((dlt * lax.rsqrt(ms + EPS) * g_ref[...]) * lam_scale).astype(o_ref.dtype)
    else:
        for r in range(nh):
            o_ref[:, r * hd:(r + 1) * hd] = o[r * tq:(r + 1) * tq].astype(o_ref.dtype)


def _rel_bucket(dist):
    n = jnp.maximum(dist, 0)
    max_exact = REL_BUCKETS // 2
    nf = jnp.maximum(n, 1).astype(F32)
    large = max_exact + (jnp.log(nf / max_exact) / math.log(REL_MAX_DIST / max_exact)
                         * (REL_BUCKETS - max_exact)).astype(I32)
    large = jnp.minimum(large, REL_BUCKETS - 1)
    return jnp.where(n < max_exact, n, large)


def _near_bias(rel_table, tq):
    assert tq >= LANES, "keys older than one block must all fall in the last bucket"
    r = jnp.arange(tq, dtype=I32)[:, None]
    c = jnp.arange(2 * tq, dtype=I32)[None, :]
    dist = r + tq - c
    rel = rel_table - rel_table[REL_BUCKETS - 1][None, :]
    b = rel[_rel_bucket(dist)]
    b = jnp.where((dist >= 0)[:, :, None], b, MASKED)
    return b.transpose(2, 0, 1)


def _diff_attention(qkv, near, lam, subln_g, lam_scale, tq):
    s = qkv.shape[0]
    hd = A_V_DIM
    nh = 2
    kcol = A_WIDTH // hd
    return pl.pallas_call(
        functools.partial(_flash_kernel, nh=nh, tq=tq, diff=True, lam_scale=lam_scale),
        grid=(A_HEADS, s // tq),
        in_specs=[pl.BlockSpec(memory_space=pltpu.SMEM),
                  pl.BlockSpec((tq, hd), lambda h, i: (i, h)),
                  pl.BlockSpec((s, hd), lambda h, i: (0, kcol + h)),
                  pl.BlockSpec((s, hd), lambda h, i: (0, 2 * kcol + h)),
                  pl.BlockSpec((1, nh * tq, 2 * tq), lambda h, i: (h, 0, 0)),
                  pl.BlockSpec((1, hd), lambda h, i: (0, 0))],
        out_specs=pl.BlockSpec((tq, hd), lambda h, i: (i, h)),
        out_shape=jax.ShapeDtypeStruct((s, A_WIDTH), BF16),
        scratch_shapes=[pltpu.VMEM((nh * tq, hd), BF16),
                        pltpu.VMEM((nh * tq, 1), F32),
                        pltpu.VMEM((nh * tq, 1), F32),
                        pltpu.VMEM((nh * tq, hd), F32)],
        compiler_params=_cparams(("parallel", "arbitrary")),
        name="diff_attention",
    )(lam, qkv, qkv, qkv, near, subln_g)


def _masked_attention(qkv, near, mask, tq):
    s = qkv.shape[0]
    hd = C_HEAD_DIM
    nh = C_GROUP
    kcol = C_WIDTH // hd
    vcol = kcol + C_KV_HEADS
    nk = s // tq
    return pl.pallas_call(
        functools.partial(_flash_kernel, nh=nh, tq=tq, diff=False, lam_scale=1.0),
        grid=(C_KV_HEADS, s // tq),
        in_specs=[pl.BlockSpec((tq, nh * hd), lambda g, i: (i, g)),
                  pl.BlockSpec((s, hd), lambda g, i: (0, kcol + g)),
                  pl.BlockSpec((s, hd), lambda g, i: (0, vcol + g)),
                  pl.BlockSpec((1, nh * tq, 2 * tq), lambda g, i: (g, 0, 0)),
                  pl.BlockSpec((nk, tq, tq), lambda g, i: (0, i, 0))],
        out_specs=pl.BlockSpec((tq, nh * hd), lambda g, i: (i, g)),
        out_shape=jax.ShapeDtypeStruct((s, C_WIDTH), BF16),
        scratch_shapes=[pltpu.VMEM((nh * tq, hd), BF16),
                        pltpu.VMEM((nh * tq, 1), F32),
                        pltpu.VMEM((nh * tq, 1), F32),
                        pltpu.VMEM((nh * tq, hd), F32)],
        compiler_params=_cparams(("parallel", "arbitrary")),
        name="selected_attention",
    )(qkv, qkv, qkv, near, mask)


def _sg_kernel(zb_ref, lng_ref, lnb_ref, w_ref, bs_ref, o_ref, *, nchunk):
    zb = zb_ref[...]
    gl = zb * (0.5 * (1.0 + jnp.tanh(np.float32(np.sqrt(2.0 / np.pi)) * (zb + 0.044715 * (zb * zb * zb)))))
    u = gl[:, :B_WIDTH]
    z = gl[:, B_WIDTH:]
    mu = jnp.mean(z, axis=-1, keepdims=True)
    zc = z - mu
    var = jnp.mean(zc * zc, axis=-1, keepdims=True)
    zn = (zc * lax.rsqrt(var + EPS) * lng_ref[...] + lnb_ref[...]).astype(BF16)
    row = lax.broadcasted_iota(I32, (CHUNK, CHUNK), 0)
    col = lax.broadcasted_iota(I32, (CHUNK, CHUNK), 1)
    for g in range(B_GROUPS):
        w = jnp.where(row >= col, w_ref[g], 0.0).astype(BF16)
        bias = bs_ref[g]
        lo = g * B_GROUP_DIM
        for c in range(nchunk):
            r0 = c * CHUNK
            sz = jnp.dot(w, zn[r0:r0 + CHUNK, lo:lo + B_GROUP_DIM], preferred_element_type=F32) + bias
            o_ref[r0:r0 + CHUNK, lo:lo + B_GROUP_DIM] = (u[r0:r0 + CHUNK, lo:lo + B_GROUP_DIM] * sz).astype(o_ref.dtype)


def _spatial_gating(zb, ln_g, ln_b, w_s, b_s):
    s = zb.shape[0]
    t = _tile(s, 256)
    return pl.pallas_call(
        functools.partial(_sg_kernel, nchunk=t // CHUNK),
        grid=(s // t,),
        in_specs=[pl.BlockSpec((t, 2 * B_WIDTH), lambda i: (i, 0)),
                  pl.BlockSpec((1, B_WIDTH), lambda i: (0, 0)),
                  pl.BlockSpec((1, B_WIDTH), lambda i: (0, 0)),
                  pl.BlockSpec((B_GROUPS, CHUNK, CHUNK), lambda i: (0, 0, 0)),
                  pl.BlockSpec((B_GROUPS, CHUNK, 1), lambda i: (0, 0, 0))],
        out_specs=pl.BlockSpec((t, B_WIDTH), lambda i: (i, 0)),
        out_shape=jax.ShapeDtypeStruct((s, B_WIDTH), BF16),
        compiler_params=_cparams(("parallel",)),
        name="spatial_gating",
    )(zb, ln_g.reshape(1, B_WIDTH), ln_b.reshape(1, B_WIDTH), w_s, b_s.reshape(B_GROUPS, CHUNK, 1))


def _select_kernel(qi_ref, kt_ref, w_ref, o_ref, keys_sc, *, tkc, topk):
    tqi = IDX_QBLOCK
    i = pl.program_id(0)
    nk = o_ref.shape[0]
    nch = (i * tqi + tqi + tkc - 1) // tkc
    qpos = i * tqi + lax.broadcasted_iota(I32, (tqi, tkc), 0)
    kloc = lax.broadcasted_iota(I32, (tqi, tkc), 1)
    wgt = w_ref[...] * np.float32(IDX_DIM ** -0.5)

    def score_body(c, carry):
        sc = jnp.dot(qi_ref[0], kt_ref[c], preferred_element_type=F32)
        acc = jnp.zeros((tqi, tkc), F32)
        for h in range(IDX_HEADS):
            acc += jnp.maximum(sc[h * tqi:(h + 1) * tqi], 0.0) * wgt[:, h:h + 1]
        acc = acc + 0.0
        bits = pltpu.bitcast(acc, I32)
        key = jnp.where(bits < 0, bits ^ jnp.int32(0x7FFFFFFF), bits)
        keys_sc[c] = jnp.where(c * tkc + kloc <= qpos, key, jnp.int32(INT_MIN))
        return carry

    lax.fori_loop(0, nch, score_body, 0)

    def bit_body(b, thr):
        cand = thr ^ jnp.left_shift(jnp.int32(1), 31 - b)
        candb = jnp.broadcast_to(cand, (tqi, LANES))

        def count_body(c, cnt):
            kk = keys_sc[c]
            for u in range(tkc // LANES):
                cnt += jnp.where(kk[:, u * LANES:(u + 1) * LANES] >= candb, 1, 0)
            return cnt

        cnt = lax.fori_loop(0, nch, count_body, jnp.zeros((tqi, LANES), I32))
        total = jnp.sum(cnt.astype(F32), axis=1, keepdims=True)
        return jnp.where(total >= np.float32(topk), cand, thr)

    thr = lax.fori_loop(0, 32, bit_body, jnp.full((tqi, 1), INT_MIN, I32))
    thrb = jnp.broadcast_to(thr, (tqi, tkc))

    def mask_body(c, carry):
        sel = (keys_sc[c] >= thrb) & (c * tkc + kloc <= qpos)
        o_ref[c] = jnp.where(sel, 0.0, MASKED).astype(o_ref.dtype)
        return carry

    lax.fori_loop(0, nch, mask_body, 0)

    def fill_body(c, carry):
        o_ref[c] = jnp.full((tqi, tkc), MASKED, o_ref.dtype)
        return carry

    lax.fori_loop(nch, nk, fill_body, 0)


def _select_mask(qi_stack, kt, wi, tkc, topk):
    nq, rows, _ = qi_stack.shape
    nk = kt.shape[0]
    s = nq * IDX_QBLOCK
    return pl.pallas_call(
        functools.partial(_select_kernel, tkc=tkc, topk=topk),
        grid=(nq,),
        in_specs=[pl.BlockSpec((1, rows, IDX_DIM), lambda i: (i, 0, 0)),
                  pl.BlockSpec((nk, IDX_DIM, tkc), lambda i: (0, 0, 0)),
                  pl.BlockSpec((IDX_QBLOCK, IDX_HEADS), lambda i: (i, 0))],
        out_specs=pl.BlockSpec((nk, IDX_QBLOCK, tkc), lambda i: (0, i, 0)),
        out_shape=jax.ShapeDtypeStruct((nk, s, tkc), BF16),
        scratch_shapes=[pltpu.VMEM((nk, IDX_QBLOCK, tkc), I32)],
        compiler_params=_cparams(("parallel",)),
        name="indexer_select",
    )(qi_stack, kt, wi)


def _vec_pack(d, *rows):
    rows = [r.reshape(1, d).astype(F32) for r in rows]
    rows += [jnp.zeros((1, d), F32)] * (8 - len(rows))
    return jnp.concatenate(rows, axis=0)


def _pad_cols(w, n):
    return jnp.pad(w, ((0, 0), (0, n - w.shape[1])))


def kernel(x, c, norm_g, mod_w, mod_b, ffn_w1, ffn_w2, rel_table, ab_w_in, ab_w_out, diff_lam,
           diff_subln_g, sg_ln_g, sg_ln_b, sg_w, sg_b, dsa_w_in, dsa_w_out, final_g):
    batch, s, d = x.shape
    depth = norm_g.shape[0]
    d_ff = ffn_w2.shape[2]
    assert batch == 1 and s % IDX_QBLOCK == 0
    ff_pad = -(-d_ff // 512) * 512 if d_ff > 512 else d_ff

    tq_a = _tile(s, 512)
    tq_c = _tile(s, 256)
    topk = min(TOPK_MAX, s // 4)

    mod = _modulation(c, mod_w, mod_b).reshape(depth, 9, d)
    near_a = _near_bias(rel_table, tq_a)
    near_a = near_a.reshape(A_HEADS, 2 * tq_a, 2 * tq_a)
    near_c = _near_bias(rel_table, tq_c).reshape(C_KV_HEADS, C_GROUP * tq_c, 2 * tq_c)

    xs = x.reshape(s, d)
    zeros_d = jnp.zeros((d,), F32)
    for li in range(depth):
        def vec(j, li=li):
            last = final_g if (li == depth - 1 and j == 2) else zeros_d
            return _vec_pack(d, norm_g[li, j], mod[li, 3 * j], mod[li, 3 * j + 1], mod[li, 3 * j + 2], last)

        def ffn(xs, j, k, final=False, li=li):
            w1 = ffn_w1[li, k]
            w1g = _pad_cols(w1[:, :d_ff], ff_pad).astype(BF16)
            w1u = _pad_cols(w1[:, d_ff:], ff_pad).astype(BF16)
            w2 = jnp.pad(ffn_w2[li, k], ((0, ff_pad - d_ff), (0, 0))).astype(BF16)
            return _ffn(xs, vec(j), w1g, w1u, w2, final=final)

        xs = ffn(xs, 0, 0)

        v1 = vec(1)
        jj = li // 2
        if li % 2 == 0:
            w_in = ab_w_in[jj]
            w_qkv = jnp.concatenate([w_in[:, :A_WIDTH] * np.float32(A_QK_DIM ** -0.5),
                                     w_in[:, A_WIDTH:3 * A_WIDTH]], axis=1).astype(BF16)
            w_zb = w_in[:, 3 * A_WIDTH:].astype(BF16)
            qkv = _proj(xs, v1, w_qkv, BF16)
            zb = _proj(xs, v1, w_zb, F32)
            lam_init = 0.8 - 0.6 * math.exp(-0.3 * li)
            lp = diff_lam[jj].astype(F32)
            lam = jnp.exp(jnp.sum(lp[0] * lp[1])) - jnp.exp(jnp.sum(lp[2] * lp[3])) + lam_init
            ya = _diff_attention(qkv, near_a, lam.reshape(1), diff_subln_g[jj].reshape(1, A_V_DIM),
                                 1.0 - lam_init, tq_a)
            yb = _spatial_gating(zb, sg_ln_g[jj], sg_ln_b[jj], sg_w[jj], sg_b[jj])
            w_out = ab_w_out[jj].astype(BF16)
            xs = _outproj(xs, v1, [ya, yb], [w_out[:A_WIDTH], w_out[A_WIDTH:]])
        else:
            w_in = dsa_w_in[jj]
            o_idx = C_WIDTH + 2 * C_KV_WIDTH
            o_ki = o_idx + IDX_HEADS * IDX_DIM
            w_main = jnp.concatenate([w_in[:, :C_WIDTH] * np.float32(C_HEAD_DIM ** -0.5),
                                      w_in[:, C_WIDTH:o_ki]], axis=1).astype(BF16)
            w_kiw = _pad_cols(w_in[:, o_ki:], LANES).astype(BF16)
            main = _proj(xs, v1, w_main, BF16)
            kiw = _proj(xs, v1, w_kiw, F32)
            nq = s // IDX_QBLOCK
            qi = main[:, o_idx:o_ki].reshape(nq, IDX_QBLOCK, IDX_HEADS, IDX_DIM)
            qi = qi.transpose(0, 2, 1, 3).reshape(nq, IDX_HEADS * IDX_QBLOCK, IDX_DIM)
            kt = kiw[:, :IDX_DIM].astype(BF16).reshape(s // tq_c, tq_c, IDX_DIM).transpose(0, 2, 1)
            wi = kiw[:, IDX_DIM:IDX_DIM + IDX_HEADS]
            mask = _select_mask(qi, kt, wi, tq_c, topk)
            yc = _masked_attention(main, near_c, mask, tq_c)
            xs = _outproj(xs, v1, [yc], [dsa_w_out[jj].astype(BF16)])

        xs = ffn(xs, 2, 1, final=(li == depth - 1))
    return xs.reshape(batch, s, d)
```

```python
import functools
import math

import jax
import jax.numpy as jnp
import numpy as np
from jax import lax
from jax.experimental import pallas as pl
from jax.experimental.pallas import tpu as pltpu

F32 = jnp.float32
BF16 = jnp.bfloat16
I32 = jnp.int32

EPS = 1e-6
MASKED = -1e30
LANES = 128
INT_MIN = -(2 ** 31)
LOG2E = math.log2(math.e)
ATTN_SUB_ROWS = 256

A_HEADS = 8
A_QK_DIM = 64
A_V_DIM = 128
A_WIDTH = A_HEADS * A_V_DIM
B_GROUPS = 8
B_GROUP_DIM = 128
B_WIDTH = B_GROUPS * B_GROUP_DIM
CHUNK = 128
C_HEADS = 16
C_KV_HEADS = 4
C_GROUP = C_HEADS // C_KV_HEADS
C_HEAD_DIM = 128
C_WIDTH = C_HEADS * C_HEAD_DIM
C_KV_WIDTH = C_KV_HEADS * C_HEAD_DIM
IDX_HEADS = 16
IDX_DIM = 64
TOPK_MAX = 256
REL_BUCKETS = 32
REL_MAX_DIST = 128
IDX_QBLOCK = 128

VMEM_LIMIT = 56 * 1024 * 1024


def _cparams(sem):
    return pltpu.CompilerParams(dimension_semantics=sem, vmem_limit_bytes=VMEM_LIMIT)


def _tile(n, want):
    if n <= want:
        return n
    t = want
    while n % t:
        t //= 2
    return t


def _mod_kernel(c_ref, w_ref, b_ref, o_ref):
    c = c_ref[...]
    cs = c * (1.0 / (1.0 + jnp.exp(-c)))
    o_ref[0] = jnp.sum(cs * w_ref[0], axis=0, keepdims=True) + b_ref[0]


def _modulation(c, mod_w, mod_b):
    depth, d, n = mod_w.shape
    tn = _tile(n, 1024)
    out = pl.pallas_call(
        _mod_kernel,
        grid=(depth, n // tn),
        in_specs=[pl.BlockSpec((d, 1), lambda l, j: (0, 0)),
                  pl.BlockSpec((1, d, tn), lambda l, j: (l, 0, j)),
                  pl.BlockSpec((1, 1, tn), lambda l, j: (l, 0, j))],
        out_specs=pl.BlockSpec((1, 1, tn), lambda l, j: (l, 0, j)),
        out_shape=jax.ShapeDtypeStruct((depth, 1, n), F32),
        compiler_params=_cparams(("arbitrary", "arbitrary")),
        name="adaln_mod",
    )(c.reshape(d, 1), mod_w, mod_b.reshape(depth, 1, n))
    return out.reshape(depth, n)


def _prenorm(x, vec_ref):
    ms = jnp.mean(x * x, axis=-1, keepdims=True)
    y = x * lax.rsqrt(ms + EPS) * vec_ref[0:1, :]
    return y * (1.0 + vec_ref[2:3, :]) + vec_ref[1:2, :]


def _ffn_kernel(x_ref, vec_ref, w1g_ref, w1u_ref, w2_ref, o_ref, hn_sc, *, nf, final):
    f = pl.program_id(1)

    @pl.when(f == 0)
    def _():
        hn_sc[...] = _prenorm(x_ref[...], vec_ref).astype(BF16)
        o_ref[...] = jnp.zeros_like(o_ref)

    hn = hn_sc[...]
    g = jnp.dot(hn, w1g_ref[...], preferred_element_type=F32)
    u = jnp.dot(hn, w1u_ref[...], preferred_element_type=F32)
    a = (g * (1.0 / (1.0 + jnp.exp(-g))) * u).astype(BF16)
    o_ref[...] += jnp.dot(a, w2_ref[...], preferred_element_type=F32)

    @pl.when(f == nf - 1)
    def _():
        y = x_ref[...] + 0.5 * (1.0 + vec_ref[3:4, :]) * o_ref[...]
        if final:
            ms = jnp.mean(y * y, axis=-1, keepdims=True)
            y = y * lax.rsqrt(ms + EPS) * vec_ref[4:5, :]
        o_ref[...] = y


def _ffn(x, vec, w1g, w1u, w2, *, final):
    s, d = x.shape
    fp = w1g.shape[1]
    tm = _tile(s, 512)
    tf = _tile(fp, 512)
    nf = fp // tf
    return pl.pallas_call(
        functools.partial(_ffn_kernel, nf=nf, final=final),
        grid=(s // tm, nf),
        in_specs=[pl.BlockSpec((tm, d), lambda i, f: (i, 0)),
                  pl.BlockSpec((8, d), lambda i, f: (0, 0)),
                  pl.BlockSpec((d, tf), lambda i, f: (0, f)),
                  pl.BlockSpec((d, tf), lambda i, f: (0, f)),
                  pl.BlockSpec((tf, d), lambda i, f: (f, 0))],
        out_specs=pl.BlockSpec((tm, d), lambda i, f: (i, 0)),
        out_shape=jax.ShapeDtypeStruct((s, d), F32),
        scratch_shapes=[pltpu.VMEM((tm, d), BF16)],
        compiler_params=_cparams(("parallel", "arbitrary")),
        name="swiglu_halfstep",
    )(x, vec, w1g, w1u, w2)


def _proj_kernel(x_ref, vec_ref, w_ref, o_ref, hn_sc):
    @pl.when(pl.program_id(1) == 0)
    def _():
        hn_sc[...] = _prenorm(x_ref[...], vec_ref).astype(BF16)

    o_ref[...] = jnp.dot(hn_sc[...], w_ref[...], preferred_element_type=F32).astype(o_ref.dtype)


def _proj(x, vec, w, out_dtype):
    s, d = x.shape
    n = w.shape[1]
    tm = _tile(s, 512)
    tn = _tile(n, 512)
    return pl.pallas_call(
        _proj_kernel,
        grid=(s // tm, n // tn),
        in_specs=[pl.BlockSpec((tm, d), lambda i, j: (i, 0)),
                  pl.BlockSpec((8, d), lambda i, j: (0, 0)),
                  pl.BlockSpec((d, tn), lambda i, j: (0, j))],
        out_specs=pl.BlockSpec((tm, tn), lambda i, j: (i, j)),
        out_shape=jax.ShapeDtypeStruct((s, n), out_dtype),
        scratch_shapes=[pltpu.VMEM((tm, d), BF16)],
        compiler_params=_cparams(("parallel", "arbitrary")),
        name="norm_mod_proj",
    )(x, vec, w)


def _outproj_kernel(*refs, n_in):
    x_ref, vec_ref = refs[0], refs[1]
    lhs = refs[2:2 + n_in]
    ws = refs[2 + n_in:2 + 2 * n_in]
    o_ref = refs[2 + 2 * n_in]
    acc = jnp.dot(lhs[0][...], ws[0][...], preferred_element_type=F32)
    for a, w in zip(lhs[1:], ws[1:]):
        acc += jnp.dot(a[...], w[...], preferred_element_type=F32)
    o_ref[...] = x_ref[...] + (1.0 + vec_ref[3:4, :]) * acc


def _outproj(x, vec, lhs, ws):
    s, d = x.shape
    tm = _tile(s, 512)
    tn = _tile(d, 512)
    n_in = len(lhs)
    in_specs = [pl.BlockSpec((tm, tn), lambda i, j: (i, j)),
                pl.BlockSpec((8, tn), lambda i, j: (0, j))]
    in_specs += [pl.BlockSpec((tm, a.shape[1]), lambda i, j: (i, 0)) for a in lhs]
    in_specs += [pl.BlockSpec((w.shape[0], tn), lambda i, j: (0, j)) for w in ws]
    return pl.pallas_call(
        functools.partial(_outproj_kernel, n_in=n_in),
        grid=(s // tm, d // tn),
        in_specs=in_specs,
        out_specs=pl.BlockSpec((tm, tn), lambda i, j: (i, j)),
        out_shape=jax.ShapeDtypeStruct((s, d), F32),
        compiler_params=_cparams(("parallel", "arbitrary")),
        name="outproj_residual",
    )(x, vec, *lhs, *ws)


def _flash_kernel(*refs, nh, tq, sub, diff, lam_scale):
    if diff:
        lam_ref, q_ref, k_ref, v_ref, nb_ref, g_ref, o_ref, qs_sc, m_sc, acc_sc = refs
        mask_ref = None
    else:
        q_ref, k_ref, v_ref, nb_ref, mask_ref, o_ref, qs_sc, m_sc, acc_sc = refs
    tk = tq
    rows = nh * tq
    hd = k_ref.shape[1]
    assert hd == LANES and rows % sub == 0 and tq % sub == 0
    i = pl.program_id(1)

    if diff:
        q = q_ref[...]
        lane = lax.broadcasted_iota(I32, q.shape, 1)
        zero = jnp.zeros_like(q)
        qs_sc[0:tq, :] = jnp.where(lane < A_QK_DIM, q, zero)
        qs_sc[tq:2 * tq, :] = jnp.where(lane >= A_QK_DIM, q, zero)
    else:
        for r in range(nh):
            qs_sc[r * tq:(r + 1) * tq, :] = q_ref[:, r * hd:(r + 1) * hd]
    m_sc[...] = jnp.full(m_sc.shape, -jnp.inf, F32)
    acc_sc[...] = jnp.zeros(acc_sc.shape, F32)

    def step(j, bias_col):
        start = pl.multiple_of(j * tk, tk)
        kb = k_ref[pl.ds(start, tk), :]
        vbe = jnp.concatenate([v_ref[pl.ds(start, tk), :], jnp.ones((tk, hd), BF16)], axis=1)
        mk = None if mask_ref is None else mask_ref[j].astype(F32)
        for r in range(rows // sub):
            rs = slice(r * sub, (r + 1) * sub)
            s = lax.dot_general(qs_sc[rs, :], kb, (((1,), (1,)), ((), ())), preferred_element_type=F32)
            if bias_col is not None:
                s = s + nb_ref[0, rs, bias_col * tk:(bias_col + 1) * tk]
            if mk is not None:
                off = (r * sub) % tq
                s = s + mk[off:off + sub]
            tiles = [s[:, t * LANES:(t + 1) * LANES] for t in range(tk // LANES)]
            cmax = tiles[0]
            for t in tiles[1:]:
                cmax = jnp.maximum(cmax, t)
            m_old = m_sc[rs, :]
            m_new = jnp.maximum(m_old, jnp.max(cmax, axis=1, keepdims=True))
            alpha = jnp.exp2(m_old - m_new)
            p = jnp.concatenate([jnp.exp2(t - m_new) for t in tiles], axis=1).astype(BF16)
            pv = jnp.dot(p, vbe, preferred_element_type=F32)
            acc_sc[rs, :] = jnp.concatenate([alpha, alpha], axis=1) * acc_sc[rs, :] + pv
            m_sc[rs, :] = m_new

    def far_body(j, carry):
        step(j, None)
        return carry

    lax.fori_loop(0, jnp.maximum(i - 1, 0), far_body, 0)

    @pl.when(i > 0)
    def _():
        step(i - 1, 0)

    step(i, 1)

    if diff:
        o0 = acc_sc[0:tq, 0:hd] / acc_sc[0:tq, hd:2 * hd]
        o1 = acc_sc[tq:2 * tq, 0:hd] / acc_sc[tq:2 * tq, hd:2 * hd]
        dlt = o0 - lam_ref[0] * o1
        ms = jnp.mean(dlt * dlt, axis=-1, keepdims=True)
        o_ref[...] = ((dlt * lax.rsqrt(ms + EPS) * g_ref[...]) * lam_scale).astype(o_ref.dtype)
    else:
        for r in range(nh):
            rs = slice(r * tq, (r + 1) * tq)
            o_ref[:, r * hd:(r + 1) * hd] = (acc_sc[rs, 0:hd] / acc_sc[rs, hd:2 * hd]).astype(o_ref.dtype)


def _rel_bucket(dist):
    n = jnp.maximum(dist, 0)
    max_exact = REL_BUCKETS // 2
    nf = jnp.maximum(n, 1).astype(F32)
    large = max_exact + (jnp.log(nf / max_exact) / math.log(REL_MAX_DIST / max_exact)
                         * (REL_BUCKETS - max_exact)).astype(I32)
    large = jnp.minimum(large, REL_BUCKETS - 1)
    return jnp.where(n < max_exact, n, large)


def _near_bias(rel_table, tq):
    assert tq >= LANES, "keys older than one block must all fall in the last bucket"
    r = jnp.arange(tq, dtype=I32)[:, None]
    c = jnp.arange(2 * tq, dtype=I32)[None, :]
    dist = r + tq - c
    rel = (rel_table - rel_table[REL_BUCKETS - 1][None, :]) * np.float32(LOG2E)
    onehot = jax.nn.one_hot(_rel_bucket(dist), REL_BUCKETS, dtype=F32)
    b = jnp.einsum("rcb,bh->hrc", onehot, rel, precision=lax.Precision.HIGHEST)
    return jnp.where((dist >= 0)[None], b, MASKED)


def _diff_attention(qkv, near, lam, subln_g, lam_scale, tq):
    s = qkv.shape[0]
    hd = A_V_DIM
    nh = 2
    kcol = A_WIDTH // hd
    return pl.pallas_call(
        functools.partial(_flash_kernel, nh=nh, tq=tq, sub=min(tq, ATTN_SUB_ROWS), diff=True,
                          lam_scale=lam_scale),
        grid=(A_HEADS, s // tq),
        in_specs=[pl.BlockSpec(memory_space=pltpu.SMEM),
                  pl.BlockSpec((tq, hd), lambda h, i: (i, h)),
                  pl.BlockSpec((s, hd), lambda h, i: (0, kcol + h)),
                  pl.BlockSpec((s, hd), lambda h, i: (0, 2 * kcol + h)),
                  pl.BlockSpec((1, nh * tq, 2 * tq), lambda h, i: (h, 0, 0)),
                  pl.BlockSpec((1, hd), lambda h, i: (0, 0))],
        out_specs=pl.BlockSpec((tq, hd), lambda h, i: (i, h)),
        out_shape=jax.ShapeDtypeStruct((s, A_WIDTH), BF16),
        scratch_shapes=[pltpu.VMEM((nh * tq, hd), BF16),
                        pltpu.VMEM((nh * tq, LANES), F32),
                        pltpu.VMEM((nh * tq, 2 * hd), F32)],
        compiler_params=_cparams(("parallel", "arbitrary")),
        name="diff_attention",
    )(lam, qkv, qkv, qkv, near, subln_g)


def _masked_attention(qkv, near, mask, tq):
    s = qkv.shape[0]
    hd = C_HEAD_DIM
    nh = C_GROUP
    kcol = C_WIDTH // hd
    vcol = kcol + C_KV_HEADS
    nk = s // tq
    return pl.pallas_call(
        functools.partial(_flash_kernel, nh=nh, tq=tq, sub=min(tq, ATTN_SUB_ROWS), diff=False,
                          lam_scale=1.0),
        grid=(C_KV_HEADS, s // tq),
        in_specs=[pl.BlockSpec((tq, nh * hd), lambda g, i: (i, g)),
                  pl.BlockSpec((s, hd), lambda g, i: (0, kcol + g)),
                  pl.BlockSpec((s, hd), lambda g, i: (0, vcol + g)),
                  pl.BlockSpec((1, nh * tq, 2 * tq), lambda g, i: (g, 0, 0)),
                  pl.BlockSpec((nk, tq, tq), lambda g, i: (0, i, 0))],
        out_specs=pl.BlockSpec((tq, nh * hd), lambda g, i: (i, g)),
        out_shape=jax.ShapeDtypeStruct((s, C_WIDTH), BF16),
        scratch_shapes=[pltpu.VMEM((nh * tq, hd), BF16),
                        pltpu.VMEM((nh * tq, LANES), F32),
                        pltpu.VMEM((nh * tq, 2 * hd), F32)],
        compiler_params=_cparams(("parallel", "arbitrary")),
        name="selected_attention",
    )(qkv, qkv, qkv, near, mask)


def _sg_kernel(zb_ref, lng_ref, lnb_ref, w_ref, bs_ref, o_ref, *, nchunk):
    zb = zb_ref[...]
    gl = zb * (0.5 * (1.0 + jnp.tanh(np.float32(np.sqrt(2.0 / np.pi)) * (zb + 0.044715 * (zb * zb * zb)))))
    u = gl[:, :B_WIDTH]
    z = gl[:, B_WIDTH:]
    mu = jnp.mean(z, axis=-1, keepdims=True)
    zc = z - mu
    var = jnp.mean(zc * zc, axis=-1, keepdims=True)
    zn = (zc * lax.rsqrt(var + EPS) * lng_ref[...] + lnb_ref[...]).astype(BF16)
    row = lax.broadcasted_iota(I32, (CHUNK, CHUNK), 0)
    col = lax.broadcasted_iota(I32, (CHUNK, CHUNK), 1)
    for g in range(B_GROUPS):
        w = jnp.where(row >= col, w_ref[g], 0.0).astype(BF16)
        bias = bs_ref[g]
        lo = g * B_GROUP_DIM
        for c in range(nchunk):
            r0 = c * CHUNK
            sz = jnp.dot(w, zn[r0:r0 + CHUNK, lo:lo + B_GROUP_DIM], preferred_element_type=F32) + bias
            o_ref[r0:r0 + CHUNK, lo:lo + B_GROUP_DIM] = (u[r0:r0 + CHUNK, lo:lo + B_GROUP_DIM] * sz).astype(o_ref.dtype)


def _spatial_gating(zb, ln_g, ln_b, w_s, b_s):
    s = zb.shape[0]
    t = _tile(s, 256)
    return pl.pallas_call(
        functools.partial(_sg_kernel, nchunk=t // CHUNK),
        grid=(s // t,),
        in_specs=[pl.BlockSpec((t, 2 * B_WIDTH), lambda i: (i, 0)),
                  pl.BlockSpec((1, B_WIDTH), lambda i: (0, 0)),
                  pl.BlockSpec((1, B_WIDTH), lambda i: (0, 0)),
                  pl.BlockSpec((B_GROUPS, CHUNK, CHUNK), lambda i: (0, 0, 0)),
                  pl.BlockSpec((B_GROUPS, CHUNK, 1), lambda i: (0, 0, 0))],
        out_specs=pl.BlockSpec((t, B_WIDTH), lambda i: (i, 0)),
        out_shape=jax.ShapeDtypeStruct((s, B_WIDTH), BF16),
        compiler_params=_cparams(("parallel",)),
        name="spatial_gating",
    )(zb, ln_g.reshape(1, B_WIDTH), ln_b.reshape(1, B_WIDTH), w_s, b_s.reshape(B_GROUPS, CHUNK, 1))


def _select_kernel(qi_ref, kt_ref, w_ref, o_ref, keys_sc, *, tkc, topk):
    tqi = IDX_QBLOCK
    i = pl.program_id(0)
    nk = o_ref.shape[0]
    nch = (i * tqi + tqi + tkc - 1) // tkc
    qpos = i * tqi + lax.broadcasted_iota(I32, (tqi, tkc), 0)
    kloc = lax.broadcasted_iota(I32, (tqi, tkc), 1)
    wgt = w_ref[...] * np.float32(IDX_DIM ** -0.5)

    def score_body(c, carry):
        sc = jnp.dot(qi_ref[0], kt_ref[c], preferred_element_type=F32)
        acc = jnp.zeros((tqi, tkc), F32)
        for h in range(IDX_HEADS):
            acc += jnp.maximum(sc[h * tqi:(h + 1) * tqi], 0.0) * wgt[:, h:h + 1]
        acc = acc + 0.0
        bits = pltpu.bitcast(acc, I32)
        key = jnp.where(bits < 0, bits ^ jnp.int32(0x7FFFFFFF), bits)
        keys_sc[c] = jnp.where(c * tkc + kloc <= qpos, key, jnp.int32(INT_MIN))
        return carry

    lax.fori_loop(0, nch, score_body, 0)

    def bit_body(b, thr):
        cand = thr ^ jnp.left_shift(jnp.int32(1), 31 - b)
        candb = jnp.broadcast_to(cand, (tqi, LANES))

        def count_body(c, cnt):
            kk = keys_sc[c]
            for u in range(tkc // LANES):
                cnt += jnp.where(kk[:, u * LANES:(u + 1) * LANES] >= candb, 1, 0)
            return cnt

        cnt = lax.fori_loop(0, nch, count_body, jnp.zeros((tqi, LANES), I32))
        total = jnp.sum(cnt.astype(F32), axis=1, keepdims=True)
        return jnp.where(total >= np.float32(topk), cand, thr)

    thr = lax.fori_loop(0, 32, bit_body, jnp.full((tqi, 1), INT_MIN, I32))
    thrb = jnp.broadcast_to(thr, (tqi, tkc))

    def mask_body(c, carry):
        sel = (keys_sc[c] >= thrb) & (c * tkc + kloc <= qpos)
        o_ref[c] = jnp.where(sel, 0.0, MASKED).astype(o_ref.dtype)
        return carry

    lax.fori_loop(0, nch, mask_body, 0)

    def fill_body(c, carry):
        o_ref[c] = jnp.full((tqi, tkc), MASKED, o_ref.dtype)
        return carry

    lax.fori_loop(nch, nk, fill_body, 0)


def _select_mask(qi_stack, kt, wi, tkc, topk):
    nq, rows, _ = qi_stack.shape
    nk = kt.shape[0]
    s = nq * IDX_QBLOCK
    return pl.pallas_call(
        functools.partial(_select_kernel, tkc=tkc, topk=topk),
        grid=(nq,),
        in_specs=[pl.BlockSpec((1, rows, IDX_DIM), lambda i: (i, 0, 0)),
                  pl.BlockSpec((nk, IDX_DIM, tkc), lambda i: (0, 0, 0)),
                  pl.BlockSpec((IDX_QBLOCK, IDX_HEADS), lambda i: (i, 0))],
        out_specs=pl.BlockSpec((nk, IDX_QBLOCK, tkc), lambda i: (0, i, 0)),
        out_shape=jax.ShapeDtypeStruct((nk, s, tkc), BF16),
        scratch_shapes=[pltpu.VMEM((nk, IDX_QBLOCK, tkc), I32)],
        compiler_params=_cparams(("parallel",)),
        name="indexer_select",
    )(qi_stack, kt, wi)


def _vec_pack(d, *rows):
    rows = [r.reshape(1, d).astype(F32) for r in rows]
    rows += [jnp.zeros((1, d), F32)] * (8 - len(rows))
    return jnp.concatenate(rows, axis=0)


def _pad_cols(w, n):
    return jnp.pad(w, ((0, 0), (0, n - w.shape[1])))


def kernel(x, c, norm_g, mod_w, mod_b, ffn_w1, ffn_w2, rel_table, ab_w_in, ab_w_out, diff_lam,
           diff_subln_g, sg_ln_g, sg_ln_b, sg_w, sg_b, dsa_w_in, dsa_w_out, final_g):
    batch, s, d = x.shape
    depth = norm_g.shape[0]
    d_ff = ffn_w2.shape[2]
    assert batch == 1 and s % IDX_QBLOCK == 0
    ff_pad = -(-d_ff // 512) * 512 if d_ff > 512 else d_ff

    tq_a = _tile(s, 512)
    tq_c = _tile(s, 256)
    topk = min(TOPK_MAX, s // 4)

    mod = _modulation(c, mod_w, mod_b).reshape(depth, 9, d)
    near_a = _near_bias(rel_table, tq_a)
    near_a = near_a.reshape(A_HEADS, 2 * tq_a, 2 * tq_a)
    near_c = _near_bias(rel_table, tq_c).reshape(C_KV_HEADS, C_GROUP * tq_c, 2 * tq_c)

    xs = x.reshape(s, d)
    zeros_d = jnp.zeros((d,), F32)
    for li in range(depth):
        def vec(j, li=li):
            last = final_g if (li == depth - 1 and j == 2) else zeros_d
            return _vec_pack(d, norm_g[li, j], mod[li, 3 * j], mod[li, 3 * j + 1], mod[li, 3 * j + 2], last)

        def ffn(xs, j, k, final=False, li=li):
            w1 = ffn_w1[li, k]
            w1g = _pad_cols(w1[:, :d_ff], ff_pad).astype(BF16)
            w1u = _pad_cols(w1[:, d_ff:], ff_pad).astype(BF16)
            w2 = jnp.pad(ffn_w2[li, k], ((0, ff_pad - d_ff), (0, 0))).astype(BF16)
            return _ffn(xs, vec(j), w1g, w1u, w2, final=final)

        xs = ffn(xs, 0, 0)

        v1 = vec(1)
        jj = li // 2
        if li % 2 == 0:
            w_in = ab_w_in[jj]
            w_qkv = jnp.concatenate([w_in[:, :A_WIDTH] * np.float32(A_QK_DIM ** -0.5 * LOG2E),
                                     w_in[:, A_WIDTH:3 * A_WIDTH]], axis=1).astype(BF16)
            w_zb = w_in[:, 3 * A_WIDTH:].astype(BF16)
            qkv = _proj(xs, v1, w_qkv, BF16)
            zb = _proj(xs, v1, w_zb, F32)
            lam_init = 0.8 - 0.6 * math.exp(-0.3 * li)
            lp = diff_lam[jj].astype(F32)
            lam = jnp.exp(jnp.sum(lp[0] * lp[1])) - jnp.exp(jnp.sum(lp[2] * lp[3])) + lam_init
            ya = _diff_attention(qkv, near_a, lam.reshape(1), diff_subln_g[jj].reshape(1, A_V_DIM),
                                 1.0 - lam_init, tq_a)
            yb = _spatial_gating(zb, sg_ln_g[jj], sg_ln_b[jj], sg_w[jj], sg_b[jj])
            w_out = ab_w_out[jj].astype(BF16)
            xs = _outproj(xs, v1, [ya, yb], [w_out[:A_WIDTH], w_out[A_WIDTH:]])
        else:
            w_in = dsa_w_in[jj]
            o_idx = C_WIDTH + 2 * C_KV_WIDTH
            o_ki = o_idx + IDX_HEADS * IDX_DIM
            w_main = jnp.concatenate([w_in[:, :C_WIDTH] * np.float32(C_HEAD_DIM ** -0.5 * LOG2E),
                                      w_in[:, C_WIDTH:o_ki]], axis=1).astype(BF16)
            w_kiw = _pad_cols(w_in[:, o_ki:], LANES).astype(BF16)
            main = _proj(xs, v1, w_main, BF16)
            kiw = _proj(xs, v1, w_kiw, F32)
            nq = s // IDX_QBLOCK
            qi = main[:, o_idx:o_ki].reshape(nq, IDX_QBLOCK, IDX_HEADS, IDX_DIM)
            qi = qi.transpose(0, 2, 1, 3).reshape(nq, IDX_HEADS * IDX_QBLOCK, IDX_DIM)
            kt = kiw[:, :IDX_DIM].astype(BF16).reshape(s // tq_c, tq_c, IDX_DIM).transpose(0, 2, 1)
            wi = kiw[:, IDX_DIM:IDX_DIM + IDX_HEADS]
            mask = _select_mask(qi, kt, wi, tq_c, topk)
            yc = _masked_attention(main, near_c, mask, tq_c)
            xs = _outproj(xs, v1, [yc], [dsa_w_out[jj].astype(BF16)])

        xs = ffn(xs, 2, 1, final=(li == depth - 1))
    return xs.reshape(batch, s, d)
```

```python
import functools
import math

import jax
import jax.numpy as jnp
import numpy as np
from jax import lax
from jax.experimental import pallas as pl
from jax.experimental.pallas import tpu as pltpu

F32 = jnp.float32
BF16 = jnp.bfloat16
I32 = jnp.int32

EPS = 1e-6
MASKED = -1e30
LANES = 128
INT_MIN = -(2 ** 31)
LOG2E = math.log2(math.e)
ATTN_SUB_ROWS = 128

A_HEADS = 8
A_QK_DIM = 64
A_V_DIM = 128
A_WIDTH = A_HEADS * A_V_DIM
B_GROUPS = 8
B_GROUP_DIM = 128
B_WIDTH = B_GROUPS * B_GROUP_DIM
CHUNK = 128
C_HEADS = 16
C_KV_HEADS = 4
C_GROUP = C_HEADS // C_KV_HEADS
C_HEAD_DIM = 128
C_WIDTH = C_HEADS * C_HEAD_DIM
C_KV_WIDTH = C_KV_HEADS * C_HEAD_DIM
IDX_HEADS = 16
IDX_DIM = 64
TOPK_MAX = 256
REL_BUCKETS = 32
REL_MAX_DIST = 128
IDX_QBLOCK = 128

VMEM_LIMIT = 56 * 1024 * 1024


def _cparams(sem):
    return pltpu.CompilerParams(dimension_semantics=sem, vmem_limit_bytes=VMEM_LIMIT)


def _tile(n, want):
    if n <= want:
        return n
    t = want
    while n % t:
        t //= 2
    return t


def _mod_kernel(c_ref, w_ref, b_ref, o_ref):
    c = c_ref[...]
    cs = c * (1.0 / (1.0 + jnp.exp(-c)))
    o_ref[0] = jnp.sum(cs * w_ref[0], axis=0, keepdims=True) + b_ref[0]


def _modulation(c, mod_w, mod_b):
    depth, d, n = mod_w.shape
    tn = _tile(n, 1024)
    out = pl.pallas_call(
        _mod_kernel,
        grid=(depth, n // tn),
        in_specs=[pl.BlockSpec((d, 1), lambda l, j: (0, 0)),
                  pl.BlockSpec((1, d, tn), lambda l, j: (l, 0, j)),
                  pl.BlockSpec((1, 1, tn), lambda l, j: (l, 0, j))],
        out_specs=pl.BlockSpec((1, 1, tn), lambda l, j: (l, 0, j)),
        out_shape=jax.ShapeDtypeStruct((depth, 1, n), F32),
        compiler_params=_cparams(("arbitrary", "arbitrary")),
        name="adaln_mod",
    )(c.reshape(d, 1), mod_w, mod_b.reshape(depth, 1, n))
    return out.reshape(depth, n)


def _prenorm(x, vec_ref):
    ms = jnp.mean(x * x, axis=-1, keepdims=True)
    y = x * lax.rsqrt(ms + EPS) * vec_ref[0:1, :]
    return y * (1.0 + vec_ref[2:3, :]) + vec_ref[1:2, :]


def _ffn_kernel(x_ref, vec_ref, w1g_ref, w1u_ref, w2_ref, o_ref, hn_sc, *, nf, final):
    f = pl.program_id(1)

    @pl.when(f == 0)
    def _():
        hn_sc[...] = _prenorm(x_ref[...], vec_ref).astype(BF16)
        o_ref[...] = jnp.zeros_like(o_ref)

    hn = hn_sc[...]
    g = jnp.dot(hn, w1g_ref[...], preferred_element_type=F32)
    u = jnp.dot(hn, w1u_ref[...], preferred_element_type=F32)
    a = (g * (1.0 / (1.0 + jnp.exp(-g))) * u).astype(BF16)
    o_ref[...] += jnp.dot(a, w2_ref[...], preferred_element_type=F32)

    @pl.when(f == nf - 1)
    def _():
        y = x_ref[...] + 0.5 * (1.0 + vec_ref[3:4, :]) * o_ref[...]
        if final:
            ms = jnp.mean(y * y, axis=-1, keepdims=True)
            y = y * lax.rsqrt(ms + EPS) * vec_ref[4:5, :]
        o_ref[...] = y


def _ffn(x, vec, w1g, w1u, w2, *, final):
    s, d = x.shape
    fp = w1g.shape[1]
    tm = _tile(s, 512)
    tf = _tile(fp, 512)
    nf = fp // tf
    return pl.pallas_call(
        functools.partial(_ffn_kernel, nf=nf, final=final),
        grid=(s // tm, nf),
        in_specs=[pl.BlockSpec((tm, d), lambda i, f: (i, 0)),
                  pl.BlockSpec((8, d), lambda i, f: (0, 0)),
                  pl.BlockSpec((d, tf), lambda i, f: (0, f)),
                  pl.BlockSpec((d, tf), lambda i, f: (0, f)),
                  pl.BlockSpec((tf, d), lambda i, f: (f, 0))],
        out_specs=pl.BlockSpec((tm, d), lambda i, f: (i, 0)),
        out_shape=jax.ShapeDtypeStruct((s, d), F32),
        scratch_shapes=[pltpu.VMEM((tm, d), BF16)],
        compiler_params=_cparams(("parallel", "arbitrary")),
        name="swiglu_halfstep",
    )(x, vec, w1g, w1u, w2)


def _proj_kernel(x_ref, vec_ref, w_ref, o_ref, hn_sc):
    @pl.when(pl.program_id(1) == 0)
    def _():
        hn_sc[...] = _prenorm(x_ref[...], vec_ref).astype(BF16)

    o_ref[...] = jnp.dot(hn_sc[...], w_ref[...], preferred_element_type=F32).astype(o_ref.dtype)


def _proj(x, vec, w, out_dtype):
    s, d = x.shape
    n = w.shape[1]
    tm = _tile(s, 512)
    tn = _tile(n, 512)
    return pl.pallas_call(
        _proj_kernel,
        grid=(s // tm, n // tn),
        in_specs=[pl.BlockSpec((tm, d), lambda i, j: (i, 0)),
                  pl.BlockSpec((8, d), lambda i, j: (0, 0)),
                  pl.BlockSpec((d, tn), lambda i, j: (0, j))],
        out_specs=pl.BlockSpec((tm, tn), lambda i, j: (i, j)),
        out_shape=jax.ShapeDtypeStruct((s, n), out_dtype),
        scratch_shapes=[pltpu.VMEM((tm, d), BF16)],
        compiler_params=_cparams(("parallel", "arbitrary")),
        name="norm_mod_proj",
    )(x, vec, w)


def _outproj_kernel(*refs, n_in):
    x_ref, vec_ref = refs[0], refs[1]
    lhs = refs[2:2 + n_in]
    ws = refs[2 + n_in:2 + 2 * n_in]
    o_ref = refs[2 + 2 * n_in]
    acc = jnp.dot(lhs[0][...], ws[0][...], preferred_element_type=F32)
    for a, w in zip(lhs[1:], ws[1:]):
        acc += jnp.dot(a[...], w[...], preferred_element_type=F32)
    o_ref[...] = x_ref[...] + (1.0 + vec_ref[3:4, :]) * acc


def _outproj(x, vec, lhs, ws):
    s, d = x.shape
    tm = _tile(s, 512)
    tn = _tile(d, 512)
    n_in = len(lhs)
    in_specs = [pl.BlockSpec((tm, tn), lambda i, j: (i, j)),
                pl.BlockSpec((8, tn), lambda i, j: (0, j))]
    in_specs += [pl.BlockSpec((tm, a.shape[1]), lambda i, j: (i, 0)) for a in lhs]
    in_specs += [pl.BlockSpec((w.shape[0], tn), lambda i, j: (0, j)) for w in ws]
    return pl.pallas_call(
        functools.partial(_outproj_kernel, n_in=n_in),
        grid=(s // tm, d // tn),
        in_specs=in_specs,
        out_specs=pl.BlockSpec((tm, tn), lambda i, j: (i, j)),
        out_shape=jax.ShapeDtypeStruct((s, d), F32),
        compiler_params=_cparams(("parallel", "arbitrary")),
        name="outproj_residual",
    )(x, vec, *lhs, *ws)


def _flash_kernel(*refs, nh, tq, sub, diff, lam_scale):
    if diff:
        lam_ref, q_ref, k_ref, v_ref, nb_ref, g_ref, o_ref, qs_sc, m_sc, acc_sc, s_sc, p_sc, al_sc = refs
        mask_ref = None
    else:
        q_ref, k_ref, v_ref, nb_ref, mask_ref, o_ref, qs_sc, m_sc, acc_sc, s_sc, p_sc, al_sc = refs
    tk = tq
    rows = nh * tq
    hd = k_ref.shape[1]
    assert hd == LANES and rows % sub == 0 and tq % sub == 0
    i = pl.program_id(1)

    if diff:
        q = q_ref[...]
        lane = lax.broadcasted_iota(I32, q.shape, 1)
        zero = jnp.zeros_like(q)
        qs_sc[0:tq, :] = jnp.where(lane < A_QK_DIM, q, zero)
        qs_sc[tq:2 * tq, :] = jnp.where(lane >= A_QK_DIM, q, zero)
    else:
        for r in range(nh):
            qs_sc[r * tq:(r + 1) * tq, :] = q_ref[:, r * hd:(r + 1) * hd]
    m_sc[...] = jnp.full(m_sc.shape, -jnp.inf, F32)
    acc_sc[...] = jnp.zeros(acc_sc.shape, F32)

    nblk = i + 1

    def bias_index(j):
        return jnp.where(j >= nblk, 3, jnp.clip(j - (i - 2), 0, 2))

    def stage_qk(j, slot):
        start = pl.multiple_of(jnp.minimum(j, i) * tk, tk)
        kb = k_ref[pl.ds(start, tk), :]
        s_sc[slot] = lax.dot_general(qs_sc[...], kb, (((1,), (1,)), ((), ())), preferred_element_type=F32)

    def stage_softmax(j, slot, biased):
        jc = jnp.minimum(j, i)
        col = bias_index(j) if biased else None
        for r in range(rows // sub):
            rs = slice(r * sub, (r + 1) * sub)
            s = s_sc[slot, rs, :]
            if biased:
                s = s + nb_ref[0, col, rs, :]
            if mask_ref is not None:
                off = (r * sub) % tq
                s = s + mask_ref[jc, off:off + sub, :].astype(F32)
            tiles = [s[:, t * LANES:(t + 1) * LANES] for t in range(tk // LANES)]
            cmax = tiles[0]
            for t in tiles[1:]:
                cmax = jnp.maximum(cmax, t)
            m_old = m_sc[rs, :]
            m_new = jnp.maximum(m_old, jnp.max(cmax, axis=1, keepdims=True))
            al_sc[slot, rs, :] = jnp.exp2(m_old - m_new)
            p_sc[slot, rs, :] = jnp.concatenate([jnp.exp2(t - m_new) for t in tiles], axis=1).astype(BF16)
            m_sc[rs, :] = m_new

    def stage_pv(j, slot):
        start = pl.multiple_of(jnp.minimum(j, i) * tk, tk)
        vbe = jnp.concatenate([v_ref[pl.ds(start, tk), :], jnp.ones((tk, hd), BF16)], axis=1)
        pv = jnp.dot(p_sc[slot], vbe, preferred_element_type=F32)
        alpha = al_sc[slot]
        acc_sc[...] = jnp.concatenate([alpha, alpha], axis=1) * acc_sc[...] + pv

    stage_qk(0, 0)
    stage_qk(1, 1)
    stage_softmax(0, 0, True)

    npairs = jnp.maximum((i - 2) // 2, 0)

    def far_pair(u, carry):
        t = 2 * u
        stage_pv(t, 0)
        stage_softmax(t + 1, 1, False)
        stage_qk(t + 2, 0)
        stage_pv(t + 1, 1)
        stage_softmax(t + 2, 0, False)
        stage_qk(t + 3, 1)
        return carry

    lax.fori_loop(0, npairs, far_pair, 0)

    t0 = 2 * npairs
    stage_pv(t0, 0)
    stage_softmax(t0 + 1, 1, True)
    stage_qk(t0 + 2, 0)
    stage_pv(t0 + 1, 1)
    stage_softmax(t0 + 2, 0, True)
    stage_qk(t0 + 3, 1)
    stage_pv(t0 + 2, 0)
    stage_softmax(t0 + 3, 1, True)
    stage_pv(t0 + 3, 1)

    if diff:
        o0 = acc_sc[0:tq, 0:hd] / acc_sc[0:tq, hd:2 * hd]
        o1 = acc_sc[tq:2 * tq, 0:hd] / acc_sc[tq:2 * tq, hd:2 * hd]
        dlt = o0 - lam_ref[0] * o1
        ms = jnp.mean(dlt * dlt, axis=-1, keepdims=True)
        o_ref[...] = ((dlt * lax.rsqrt(ms + EPS) * g_ref[...]) * lam_scale).astype(o_ref.dtype)
    else:
        for r in range(nh):
            rs = slice(r * tq, (r + 1) * tq)
            o_ref[:, r * hd:(r + 1) * hd] = (acc_sc[rs, 0:hd] / acc_sc[rs, hd:2 * hd]).astype(o_ref.dtype)


def _rel_bucket(dist):
    n = jnp.maximum(dist, 0)
    max_exact = REL_BUCKETS // 2
    nf = jnp.maximum(n, 1).astype(F32)
    large = max_exact + (jnp.log(nf / max_exact) / math.log(REL_MAX_DIST / max_exact)
                         * (REL_BUCKETS - max_exact)).astype(I32)
    large = jnp.minimum(large, REL_BUCKETS - 1)
    return jnp.where(n < max_exact, n, large)


def _near_bias(rel_table, tq, groups, nh):
    assert tq >= LANES, "keys older than one block must all fall in the last bucket"
    r = jnp.arange(tq, dtype=I32)[:, None]
    c = jnp.arange(2 * tq, dtype=I32)[None, :]
    dist = r + tq - c
    rel = (rel_table - rel_table[REL_BUCKETS - 1][None, :]) * np.float32(LOG2E)
    onehot = jax.nn.one_hot(_rel_bucket(dist), REL_BUCKETS, dtype=F32)
    b = jnp.einsum("rcb,bh->hrc", onehot, rel, precision=lax.Precision.HIGHEST)
    b = jnp.where((dist >= 0)[None], b, MASKED)
    heads = b.shape[0]
    tiles = jnp.stack([jnp.zeros((heads, tq, tq), F32), b[:, :, :tq], b[:, :, tq:],
                       jnp.full((heads, tq, tq), MASKED, F32)], axis=1)
    tiles = tiles.reshape(groups, nh, 4, tq, tq).transpose(0, 2, 1, 3, 4)
    return tiles.reshape(groups, 4, nh * tq, tq)


def _diff_attention(qkv, near, lam, subln_g, lam_scale, tq):
    s = qkv.shape[0]
    hd = A_V_DIM
    nh = 2
    kcol = A_WIDTH // hd
    return pl.pallas_call(
        functools.partial(_flash_kernel, nh=nh, tq=tq, sub=min(tq, ATTN_SUB_ROWS), diff=True,
                          lam_scale=lam_scale),
        grid=(A_HEADS, s // tq),
        in_specs=[pl.BlockSpec(memory_space=pltpu.SMEM),
                  pl.BlockSpec((tq, hd), lambda h, i: (i, h)),
                  pl.BlockSpec((s, hd), lambda h, i: (0, kcol + h)),
                  pl.BlockSpec((s, hd), lambda h, i: (0, 2 * kcol + h)),
                  pl.BlockSpec((1, 4, nh * tq, tq), lambda h, i: (h, 0, 0, 0)),
                  pl.BlockSpec((1, hd), lambda h, i: (0, 0))],
        out_specs=pl.BlockSpec((tq, hd), lambda h, i: (i, h)),
        out_shape=jax.ShapeDtypeStruct((s, A_WIDTH), BF16),
        scratch_shapes=[pltpu.VMEM((nh * tq, hd), BF16),
                        pltpu.VMEM((nh * tq, LANES), F32),
                        pltpu.VMEM((nh * tq, 2 * hd), F32),
                        pltpu.VMEM((2, nh * tq, tq), F32),
                        pltpu.VMEM((2, nh * tq, tq), BF16),
                        pltpu.VMEM((2, nh * tq, LANES), F32)],
        compiler_params=_cparams(("parallel", "arbitrary")),
        name="diff_attention",
    )(lam, qkv, qkv, qkv, near, subln_g)


def _masked_attention(qkv, near, mask, tq):
    s = qkv.shape[0]
    hd = C_HEAD_DIM
    nh = C_GROUP
    kcol = C_WIDTH // hd
    vcol = kcol + C_KV_HEADS
    nk = s // tq
    return pl.pallas_call(
        functools.partial(_flash_kernel, nh=nh, tq=tq, sub=min(tq, ATTN_SUB_ROWS), diff=False,
                          lam_scale=1.0),
        grid=(C_KV_HEADS, s // tq),
        in_specs=[pl.BlockSpec((tq, nh * hd), lambda g, i: (i, g)),
                  pl.BlockSpec((s, hd), lambda g, i: (0, kcol + g)),
                  pl.BlockSpec((s, hd), lambda g, i: (0, vcol + g)),
                  pl.BlockSpec((1, 4, nh * tq, tq), lambda g, i: (g, 0, 0, 0)),
                  pl.BlockSpec((nk, tq, tq), lambda g, i: (0, i, 0))],
        out_specs=pl.BlockSpec((tq, nh * hd), lambda g, i: (i, g)),
        out_shape=jax.ShapeDtypeStruct((s, C_WIDTH), BF16),
        scratch_shapes=[pltpu.VMEM((nh * tq, hd), BF16),
                        pltpu.VMEM((nh * tq, LANES), F32),
                        pltpu.VMEM((nh * tq, 2 * hd), F32),
                        pltpu.VMEM((2, nh * tq, tq), F32),
                        pltpu.VMEM((2, nh * tq, tq), BF16),
                        pltpu.VMEM((2, nh * tq, LANES), F32)],
        compiler_params=_cparams(("parallel", "arbitrary")),
        name="selected_attention",
    )(qkv, qkv, qkv, near, mask)


def _sg_kernel(zb_ref, lng_ref, lnb_ref, w_ref, bs_ref, o_ref, *, nchunk):
    zb = zb_ref[...]
    gl = zb * (0.5 * (1.0 + jnp.tanh(np.float32(np.sqrt(2.0 / np.pi)) * (zb + 0.044715 * (zb * zb * zb)))))
    u = gl[:, :B_WIDTH]
    z = gl[:, B_WIDTH:]
    mu = jnp.mean(z, axis=-1, keepdims=True)
    zc = z - mu
    var = jnp.mean(zc * zc, axis=-1, keepdims=True)
    zn = (zc * lax.rsqrt(var + EPS) * lng_ref[...] + lnb_ref[...]).astype(BF16)
    row = lax.broadcasted_iota(I32, (CHUNK, CHUNK), 0)
    col = lax.broadcasted_iota(I32, (CHUNK, CHUNK), 1)
    for g in range(B_GROUPS):
        w = jnp.where(row >= col, w_ref[g], 0.0).astype(BF16)
        bias = bs_ref[g]
        lo = g * B_GROUP_DIM
        for c in range(nchunk):
            r0 = c * CHUNK
            sz = jnp.dot(w, zn[r0:r0 + CHUNK, lo:lo + B_GROUP_DIM], preferred_element_type=F32) + bias
            o_ref[r0:r0 + CHUNK, lo:lo + B_GROUP_DIM] = (u[r0:r0 + CHUNK, lo:lo + B_GROUP_DIM] * sz).astype(o_ref.dtype)


def _spatial_gating(zb, ln_g, ln_b, w_s, b_s):
    s = zb.shape[0]
    t = _tile(s, 256)
    return pl.pallas_call(
        functools.partial(_sg_kernel, nchunk=t // CHUNK),
        grid=(s // t,),
        in_specs=[pl.BlockSpec((t, 2 * B_WIDTH), lambda i: (i, 0)),
                  pl.BlockSpec((1, B_WIDTH), lambda i: (0, 0)),
                  pl.BlockSpec((1, B_WIDTH), lambda i: (0, 0)),
                  pl.BlockSpec((B_GROUPS, CHUNK, CHUNK), lambda i: (0, 0, 0)),
                  pl.BlockSpec((B_GROUPS, CHUNK, 1), lambda i: (0, 0, 0))],
        out_specs=pl.BlockSpec((t, B_WIDTH), lambda i: (i, 0)),
        out_shape=jax.ShapeDtypeStruct((s, B_WIDTH), BF16),
        compiler_params=_cparams(("parallel",)),
        name="spatial_gating",
    )(zb, ln_g.reshape(1, B_WIDTH), ln_b.reshape(1, B_WIDTH), w_s, b_s.reshape(B_GROUPS, CHUNK, 1))


def _select_kernel(qi_ref, kt_ref, w_ref, o_ref, keys_sc, *, tkc, topk):
    tqi = IDX_QBLOCK
    i = pl.program_id(0)
    nk = o_ref.shape[0]
    nch = (i * tqi + tqi + tkc - 1) // tkc
    qpos = i * tqi + lax.broadcasted_iota(I32, (tqi, tkc), 0)
    kloc = lax.broadcasted_iota(I32, (tqi, tkc), 1)
    wgt = w_ref[...] * np.float32(IDX_DIM ** -0.5)

    def score_body(c, carry):
        sc = jnp.dot(qi_ref[0], kt_ref[c], preferred_element_type=F32)
        acc = jnp.zeros((tqi, tkc), F32)
        for h in range(IDX_HEADS):
            acc += jnp.maximum(sc[h * tqi:(h + 1) * tqi], 0.0) * wgt[:, h:h + 1]
        acc = acc + 0.0
        bits = pltpu.bitcast(acc, I32)
        key = jnp.where(bits < 0, bits ^ jnp.int32(0x7FFFFFFF), bits)
        keys_sc[c] = jnp.where(c * tkc + kloc <= qpos, key, jnp.int32(INT_MIN))
        return carry

    lax.fori_loop(0, nch, score_body, 0)

    def bit_body(b, thr):
        cand = thr ^ jnp.left_shift(jnp.int32(1), 31 - b)
        candb = jnp.broadcast_to(cand, (tqi, LANES))

        def count_body(c, cnt):
            kk = keys_sc[c]
            for u in range(tkc // LANES):
                cnt += jnp.where(kk[:, u * LANES:(u + 1) * LANES] >= candb, 1, 0)
            return cnt

        cnt = lax.fori_loop(0, nch, count_body, jnp.zeros((tqi, LANES), I32))
        total = jnp.sum(cnt.astype(F32), axis=1, keepdims=True)
        return jnp.where(total >= np.float32(topk), cand, thr)

    thr = lax.fori_loop(0, 32, bit_body, jnp.full((tqi, 1), INT_MIN, I32))
    thrb = jnp.broadcast_to(thr, (tqi, tkc))

    def mask_body(c, carry):
        sel = (keys_sc[c] >= thrb) & (c * tkc + kloc <= qpos)
        o_ref[c] = jnp.where(sel, 0.0, MASKED).astype(o_ref.dtype)
        return carry

    lax.fori_loop(0, nch, mask_body, 0)

    def fill_body(c, carry):
        o_ref[c] = jnp.full((tqi, tkc), MASKED, o_ref.dtype)
        return carry

    lax.fori_loop(nch, nk, fill_body, 0)


def _select_mask(qi_stack, kt, wi, tkc, topk):
    nq, rows, _ = qi_stack.shape
    nk = kt.shape[0]
    s = nq * IDX_QBLOCK
    return pl.pallas_call(
        functools.partial(_select_kernel, tkc=tkc, topk=topk),
        grid=(nq,),
        in_specs=[pl.BlockSpec((1, rows, IDX_DIM), lambda i: (i, 0, 0)),
                  pl.BlockSpec((nk, IDX_DIM, tkc), lambda i: (0, 0, 0)),
                  pl.BlockSpec((IDX_QBLOCK, IDX_HEADS), lambda i: (i, 0))],
        out_specs=pl.BlockSpec((nk, IDX_QBLOCK, tkc), lambda i: (0, i, 0)),
        out_shape=jax.ShapeDtypeStruct((nk, s, tkc), BF16),
        scratch_shapes=[pltpu.VMEM((nk, IDX_QBLOCK, tkc), I32)],
        compiler_params=_cparams(("parallel",)),
        name="indexer_select",
    )(qi_stack, kt, wi)


def _vec_pack(d, *rows):
    rows = [r.reshape(1, d).astype(F32) for r in rows]
    rows += [jnp.zeros((1, d), F32)] * (8 - len(rows))
    return jnp.concatenate(rows, axis=0)


def _pad_cols(w, n):
    return jnp.pad(w, ((0, 0), (0, n - w.shape[1])))


def kernel(x, c, norm_g, mod_w, mod_b, ffn_w1, ffn_w2, rel_table, ab_w_in, ab_w_out, diff_lam,
           diff_subln_g, sg_ln_g, sg_ln_b, sg_w, sg_b, dsa_w_in, dsa_w_out, final_g):
    batch, s, d = x.shape
    depth = norm_g.shape[0]
    d_ff = ffn_w2.shape[2]
    assert batch == 1 and s % IDX_QBLOCK == 0
    ff_pad = -(-d_ff // 512) * 512 if d_ff > 512 else d_ff

    tq_a = _tile(s, 512)
    tq_c = _tile(s, 256)
    topk = min(TOPK_MAX, s // 4)

    mod = _modulation(c, mod_w, mod_b).reshape(depth, 9, d)
    near_a = _near_bias(rel_table, tq_a, A_HEADS, 2)
    near_c = _near_bias(rel_table, tq_c, C_KV_HEADS, C_GROUP)

    xs = x.reshape(s, d)
    zeros_d = jnp.zeros((d,), F32)
    for li in range(depth):
        def vec(j, li=li):
            last = final_g if (li == depth - 1 and j == 2) else zeros_d
            return _vec_pack(d, norm_g[li, j], mod[li, 3 * j], mod[li, 3 * j + 1], mod[li, 3 * j + 2], last)

        def ffn(xs, j, k, final=False, li=li):
            w1 = ffn_w1[li, k]
            w1g = _pad_cols(w1[:, :d_ff], ff_pad).astype(BF16)
            w1u = _pad_cols(w1[:, d_ff:], ff_pad).astype(BF16)
            w2 = jnp.pad(ffn_w2[li, k], ((0, ff_pad - d_ff), (0, 0))).astype(BF16)
            return _ffn(xs, vec(j), w1g, w1u, w2, final=final)

        xs = ffn(xs, 0, 0)

        v1 = vec(1)
        jj = li // 2
        if li % 2 == 0:
            w_in = ab_w_in[jj]
            w_qkv = jnp.concatenate([w_in[:, :A_WIDTH] * np.float32(A_QK_DIM ** -0.5 * LOG2E),
                                     w_in[:, A_WIDTH:3 * A_WIDTH]], axis=1).astype(BF16)
            w_zb = w_in[:, 3 * A_WIDTH:].astype(BF16)
            qkv = _proj(xs, v1, w_qkv, BF16)
            zb = _proj(xs, v1, w_zb, F32)
            lam_init = 0.8 - 0.6 * math.exp(-0.3 * li)
            lp = diff_lam[jj].astype(F32)
            lam = jnp.exp(jnp.sum(lp[0] * lp[1])) - jnp.exp(jnp.sum(lp[2] * lp[3])) + lam_init
            ya = _diff_attention(qkv, near_a, lam.reshape(1), diff_subln_g[jj].reshape(1, A_V_DIM),
                                 1.0 - lam_init, tq_a)
            yb = _spatial_gating(zb, sg_ln_g[jj], sg_ln_b[jj], sg_w[jj], sg_b[jj])
            w_out = ab_w_out[jj].astype(BF16)
            xs = _outproj(xs, v1, [ya, yb], [w_out[:A_WIDTH], w_out[A_WIDTH:]])
        else:
            w_in = dsa_w_in[jj]
            o_idx = C_WIDTH + 2 * C_KV_WIDTH
            o_ki = o_idx + IDX_HEADS * IDX_DIM
            w_main = jnp.concatenate([w_in[:, :C_WIDTH] * np.float32(C_HEAD_DIM ** -0.5 * LOG2E),
                                      w_in[:, C_WIDTH:o_ki]], axis=1).astype(BF16)
            w_kiw = _pad_cols(w_in[:, o_ki:], LANES).astype(BF16)
            main = _proj(xs, v1, w_main, BF16)
            kiw = _proj(xs, v1, w_kiw, F32)
            nq = s // IDX_QBLOCK
            qi = main[:, o_idx:o_ki].reshape(nq, IDX_QBLOCK, IDX_HEADS, IDX_DIM)
            qi = qi.transpose(0, 2, 1, 3).reshape(nq, IDX_HEADS * IDX_QBLOCK, IDX_DIM)
            kt = kiw[:, :IDX_DIM].astype(BF16).reshape(s // tq_c, tq_c, IDX_DIM).transpose(0, 2, 1)
            wi = kiw[:, IDX_DIM:IDX_DIM + IDX_HEADS]
            mask = _select_mask(qi, kt, wi, tq_c, topk)
            yc = _masked_attention(main, near_c, mask, tq_c)
            xs = _outproj(xs, v1, [yc], [dsa_w_out[jj].astype(BF16)])

        xs = ffn(xs, 2, 1, final=(li == depth - 1))
    return xs.reshape(batch, s, d)
```

```python
import functools
import math

import jax
import jax.numpy as jnp
import numpy as np
from jax import lax
from jax.experimental import pallas as pl
from jax.experimental.pallas import tpu as pltpu

F32 = jnp.float32
BF16 = jnp.bfloat16
I32 = jnp.int32
I16 = jnp.int16
I16_MIN = -(2 ** 15)

EPS = 1e-6
MASKED = -1e30
LANES = 128
INT_MIN = -(2 ** 31)
LOG2E = math.log2(math.e)
ATTN_SUB_ROWS = 128

A_HEADS = 8
A_QK_DIM = 64
A_V_DIM = 128
A_WIDTH = A_HEADS * A_V_DIM
B_GROUPS = 8
B_GROUP_DIM = 128
B_WIDTH = B_GROUPS * B_GROUP_DIM
CHUNK = 128
C_HEADS = 16
C_KV_HEADS = 4
C_GROUP = C_HEADS // C_KV_HEADS
C_HEAD_DIM = 128
C_WIDTH = C_HEADS * C_HEAD_DIM
C_KV_WIDTH = C_KV_HEADS * C_HEAD_DIM
IDX_HEADS = 16
IDX_DIM = 64
TOPK_MAX = 256
REL_BUCKETS = 32
REL_MAX_DIST = 128
IDX_QBLOCK = 128

VMEM_LIMIT = 56 * 1024 * 1024


def _cparams(sem):
    return pltpu.CompilerParams(dimension_semantics=sem, vmem_limit_bytes=VMEM_LIMIT)


FFN_ROWS, FFN_COLS = 512, 512
PROJ_ROWS, PROJ_COLS = 1024, 512


def _col_blocks(w, tn):
    k, n = w.shape
    tn = _tile(n, tn)
    return w.reshape(k, n // tn, tn).transpose(1, 0, 2)


def _tile(n, want):
    if n <= want:
        return n
    t = want
    while n % t:
        t //= 2
    return t


def _mod_kernel(c_ref, w_ref, b_ref, o_ref):
    c = c_ref[...]
    cs = c * (1.0 / (1.0 + jnp.exp(-c)))
    o_ref[0] = jnp.sum(cs * w_ref[0], axis=0, keepdims=True) + b_ref[0]


def _modulation(c, mod_w, mod_b):
    depth, d, n = mod_w.shape
    tn = _tile(n, 1024)
    out = pl.pallas_call(
        _mod_kernel,
        grid=(depth, n // tn),
        in_specs=[pl.BlockSpec((d, 1), lambda l, j: (0, 0)),
                  pl.BlockSpec((1, d, tn), lambda l, j: (l, 0, j)),
                  pl.BlockSpec((1, 1, tn), lambda l, j: (l, 0, j))],
        out_specs=pl.BlockSpec((1, 1, tn), lambda l, j: (l, 0, j)),
        out_shape=jax.ShapeDtypeStruct((depth, 1, n), F32),
        compiler_params=_cparams(("arbitrary", "arbitrary")),
        name="adaln_mod",
    )(c.reshape(d, 1), mod_w, mod_b.reshape(depth, 1, n))
    return out.reshape(depth, n)


def _prenorm(x, vec_ref):
    ms = jnp.mean(x * x, axis=-1, keepdims=True)
    y = x * lax.rsqrt(ms + EPS) * vec_ref[0:1, :]
    return y * (1.0 + vec_ref[2:3, :]) + vec_ref[1:2, :]


def _ffn_kernel(x_ref, vec_ref, w1g_ref, w1u_ref, w2_ref, o_ref, hn_sc, *, nf, final):
    f = pl.program_id(1)

    @pl.when(f == 0)
    def _():
        hn_sc[...] = _prenorm(x_ref[...], vec_ref).astype(BF16)
        o_ref[...] = jnp.zeros_like(o_ref)

    hn = hn_sc[...]
    g = jnp.dot(hn, w1g_ref[0], preferred_element_type=F32)
    u = jnp.dot(hn, w1u_ref[0], preferred_element_type=F32)
    a = (g * (1.0 / (1.0 + jnp.exp(-g))) * u).astype(BF16)
    o_ref[...] += jnp.dot(a, w2_ref[...], preferred_element_type=F32)

    @pl.when(f == nf - 1)
    def _():
        y = x_ref[...] + 0.5 * (1.0 + vec_ref[3:4, :]) * o_ref[...]
        if final:
            ms = jnp.mean(y * y, axis=-1, keepdims=True)
            y = y * lax.rsqrt(ms + EPS) * vec_ref[4:5, :]
        o_ref[...] = y


def _ffn(x, vec, w1g, w1u, w2, *, final):
    s, d = x.shape
    nf, _, tf = w1g.shape
    tm = _tile(s, FFN_ROWS)
    return pl.pallas_call(
        functools.partial(_ffn_kernel, nf=nf, final=final),
        grid=(s // tm, nf),
        in_specs=[pl.BlockSpec((tm, d), lambda i, f: (i, 0)),
                  pl.BlockSpec((8, d), lambda i, f: (0, 0)),
                  pl.BlockSpec((1, d, tf), lambda i, f: (f, 0, 0)),
                  pl.BlockSpec((1, d, tf), lambda i, f: (f, 0, 0)),
                  pl.BlockSpec((tf, d), lambda i, f: (f, 0))],
        out_specs=pl.BlockSpec((tm, d), lambda i, f: (i, 0)),
        out_shape=jax.ShapeDtypeStruct((s, d), F32),
        scratch_shapes=[pltpu.VMEM((tm, d), BF16)],
        compiler_params=_cparams(("parallel", "arbitrary")),
        name="swiglu_halfstep",
    )(x, vec, w1g, w1u, w2)


def _proj_kernel(x_ref, vec_ref, w_ref, o_ref, hn_sc):
    @pl.when(pl.program_id(1) == 0)
    def _():
        hn_sc[...] = _prenorm(x_ref[...], vec_ref).astype(BF16)

    o_ref[...] = jnp.dot(hn_sc[...], w_ref[0], preferred_element_type=F32).astype(o_ref.dtype)


def _proj(x, vec, w, out_dtype):
    s, d = x.shape
    nn, _, tn = w.shape
    n = nn * tn
    tm = _tile(s, PROJ_ROWS)
    return pl.pallas_call(
        _proj_kernel,
        grid=(s // tm, nn),
        in_specs=[pl.BlockSpec((tm, d), lambda i, j: (i, 0)),
                  pl.BlockSpec((8, d), lambda i, j: (0, 0)),
                  pl.BlockSpec((1, d, tn), lambda i, j: (j, 0, 0))],
        out_specs=pl.BlockSpec((tm, tn), lambda i, j: (i, j)),
        out_shape=jax.ShapeDtypeStruct((s, n), out_dtype),
        scratch_shapes=[pltpu.VMEM((tm, d), BF16)],
        compiler_params=_cparams(("parallel", "arbitrary")),
        name="norm_mod_proj",
    )(x, vec, w)


def _outproj_kernel(*refs, n_in):
    x_ref, vec_ref = refs[0], refs[1]
    lhs = refs[2:2 + n_in]
    ws = refs[2 + n_in:2 + 2 * n_in]
    o_ref = refs[2 + 2 * n_in]
    acc = jnp.dot(lhs[0][...], ws[0][0], preferred_element_type=F32)
    for a, w in zip(lhs[1:], ws[1:]):
        acc += jnp.dot(a[...], w[0], preferred_element_type=F32)
    o_ref[...] = x_ref[...] + (1.0 + vec_ref[3:4, :]) * acc


def _outproj(x, vec, lhs, ws):
    s, d = x.shape
    tm = _tile(s, PROJ_ROWS)
    tn = ws[0].shape[2]
    n_in = len(lhs)
    in_specs = [pl.BlockSpec((tm, tn), lambda i, j: (i, j)),
                pl.BlockSpec((8, tn), lambda i, j: (0, j))]
    in_specs += [pl.BlockSpec((tm, a.shape[1]), lambda i, j: (i, 0)) for a in lhs]
    in_specs += [pl.BlockSpec((1, w.shape[1], tn), lambda i, j: (j, 0, 0)) for w in ws]
    return pl.pallas_call(
        functools.partial(_outproj_kernel, n_in=n_in),
        grid=(s // tm, d // tn),
        in_specs=in_specs,
        out_specs=pl.BlockSpec((tm, tn), lambda i, j: (i, j)),
        out_shape=jax.ShapeDtypeStruct((s, d), F32),
        compiler_params=_cparams(("parallel", "arbitrary")),
        name="outproj_residual",
    )(x, vec, *lhs, *ws)


def _flash_kernel(*refs, nh, tq, sub, diff, lam_scale):
    if diff:
        lam_ref, q_ref, k_ref, v_ref, nb_ref, g_ref, o_ref, qs_sc, m_sc, acc_sc, s_sc, p_sc, al_sc = refs
        mask_ref = None
    else:
        q_ref, k_ref, v_ref, nb_ref, mask_ref, o_ref, qs_sc, m_sc, acc_sc, s_sc, p_sc, al_sc = refs
    tk = tq
    rows = nh * tq
    hd = k_ref.shape[1]
    assert hd == LANES and rows % sub == 0 and tq % sub == 0
    i = pl.program_id(1)

    if diff:
        q = q_ref[...]
        lane = lax.broadcasted_iota(I32, q.shape, 1)
        zero = jnp.zeros_like(q)
        qs_sc[0:tq, :] = jnp.where(lane < A_QK_DIM, q, zero)
        qs_sc[tq:2 * tq, :] = jnp.where(lane >= A_QK_DIM, q, zero)
    else:
        for r in range(nh):
            qs_sc[r * tq:(r + 1) * tq, :] = q_ref[:, r * hd:(r + 1) * hd]
    m_sc[...] = jnp.full(m_sc.shape, -jnp.inf, F32)
    acc_sc[...] = jnp.zeros(acc_sc.shape, F32)

    nblk = i + 1

    def bias_index(j):
        return jnp.where(j >= nblk, 3, jnp.clip(j - (i - 2), 0, 2))

    def stage_qk(j, slot):
        start = pl.multiple_of(jnp.minimum(j, i) * tk, tk)
        kb = k_ref[pl.ds(start, tk), :]
        s_sc[slot] = lax.dot_general(qs_sc[...], kb, (((1,), (1,)), ((), ())), preferred_element_type=F32)

    def stage_softmax(j, slot, biased):
        jc = jnp.minimum(j, i)
        col = bias_index(j) if biased else None
        for r in range(rows // sub):
            rs = slice(r * sub, (r + 1) * sub)
            s = s_sc[slot, rs, :]
            if biased:
                s = s + nb_ref[0, col, rs, :]
            if mask_ref is not None:
                off = (r * sub) % tq
                s = s + mask_ref[jc, off:off + sub, :].astype(F32)
            tiles = [s[:, t * LANES:(t + 1) * LANES] for t in range(tk // LANES)]
            cmax = tiles[0]
            for t in tiles[1:]:
                cmax = jnp.maximum(cmax, t)
            m_old = m_sc[rs, :]
            m_new = jnp.maximum(m_old, jnp.max(cmax, axis=1, keepdims=True))
            al_sc[slot, rs, :] = jnp.exp2(m_old - m_new)
            p_sc[slot, rs, :] = jnp.concatenate([jnp.exp2(t - m_new) for t in tiles], axis=1).astype(BF16)
            m_sc[rs, :] = m_new

    def stage_pv(j, slot):
        start = pl.multiple_of(jnp.minimum(j, i) * tk, tk)
        vbe = jnp.concatenate([v_ref[pl.ds(start, tk), :], jnp.ones((tk, hd), BF16)], axis=1)
        pv = jnp.dot(p_sc[slot], vbe, preferred_element_type=F32)
        alpha = al_sc[slot]
        acc_sc[...] = jnp.concatenate([alpha, alpha], axis=1) * acc_sc[...] + pv

    stage_qk(0, 0)
    stage_qk(1, 1)
    stage_softmax(0, 0, True)

    npairs = jnp.maximum((i - 2) // 2, 0)

    def far_pair(u, carry):
        t = 2 * u
        stage_pv(t, 0)
        stage_softmax(t + 1, 1, False)
        stage_qk(t + 2, 0)
        stage_pv(t + 1, 1)
        stage_softmax(t + 2, 0, False)
        stage_qk(t + 3, 1)
        return carry

    lax.fori_loop(0, npairs, far_pair, 0)

    t0 = 2 * npairs
    stage_pv(t0, 0)
    stage_softmax(t0 + 1, 1, True)
    stage_qk(t0 + 2, 0)
    stage_pv(t0 + 1, 1)
    stage_softmax(t0 + 2, 0, True)
    stage_qk(t0 + 3, 1)
    stage_pv(t0 + 2, 0)
    stage_softmax(t0 + 3, 1, True)
    stage_pv(t0 + 3, 1)

    if diff:
        o0 = acc_sc[0:tq, 0:hd] / acc_sc[0:tq, hd:2 * hd]
        o1 = acc_sc[tq:2 * tq, 0:hd] / acc_sc[tq:2 * tq, hd:2 * hd]
        dlt = o0 - lam_ref[0] * o1
        ms = jnp.mean(dlt * dlt, axis=-1, keepdims=True)
        o_ref[...] = ((dlt * lax.rsqrt(ms + EPS) * g_ref[...]) * lam_scale).astype(o_ref.dtype)
    else:
        for r in range(nh):
            rs = slice(r * tq, (r + 1) * tq)
            o_ref[:, r * hd:(r + 1) * hd] = (acc_sc[rs, 0:hd] / acc_sc[rs, hd:2 * hd]).astype(o_ref.dtype)


def _rel_bucket(dist):
    n = jnp.maximum(dist, 0)
    max_exact = REL_BUCKETS // 2
    nf = jnp.maximum(n, 1).astype(F32)
    large = max_exact + (jnp.log(nf / max_exact) / math.log(REL_MAX_DIST / max_exact)
                         * (REL_BUCKETS - max_exact)).astype(I32)
    large = jnp.minimum(large, REL_BUCKETS - 1)
    return jnp.where(n < max_exact, n, large)


def _near_bias(rel_table, tq, groups, nh):
    assert tq >= LANES, "keys older than one block must all fall in the last bucket"
    r = jnp.arange(tq, dtype=I32)[:, None]
    c = jnp.arange(2 * tq, dtype=I32)[None, :]
    dist = r + tq - c
    rel = (rel_table - rel_table[REL_BUCKETS - 1][None, :]) * np.float32(LOG2E)
    onehot = jax.nn.one_hot(_rel_bucket(dist), REL_BUCKETS, dtype=F32)
    b = jnp.einsum("rcb,bh->hrc", onehot, rel, precision=lax.Precision.HIGHEST)
    b = jnp.where((dist >= 0)[None], b, MASKED)
    heads = b.shape[0]
    tiles = jnp.stack([jnp.zeros((heads, tq, tq), F32), b[:, :, :tq], b[:, :, tq:],
                       jnp.full((heads, tq, tq), MASKED, F32)], axis=1)
    tiles = tiles.reshape(groups, nh, 4, tq, tq).transpose(0, 2, 1, 3, 4)
    return tiles.reshape(groups, 4, nh * tq, tq)


def _diff_attention(qkv, near, lam, subln_g, lam_scale, tq):
    s = qkv.shape[0]
    hd = A_V_DIM
    nh = 2
    kcol = A_WIDTH // hd
    return pl.pallas_call(
        functools.partial(_flash_kernel, nh=nh, tq=tq, sub=min(tq, ATTN_SUB_ROWS), diff=True,
                          lam_scale=lam_scale),
        grid=(A_HEADS, s // tq),
        in_specs=[pl.BlockSpec(memory_space=pltpu.SMEM),
                  pl.BlockSpec((tq, hd), lambda h, i: (i, h)),
                  pl.BlockSpec((s, hd), lambda h, i: (0, kcol + h)),
                  pl.BlockSpec((s, hd), lambda h, i: (0, 2 * kcol + h)),
                  pl.BlockSpec((1, 4, nh * tq, tq), lambda h, i: (h, 0, 0, 0)),
                  pl.BlockSpec((1, hd), lambda h, i: (0, 0))],
        out_specs=pl.BlockSpec((tq, hd), lambda h, i: (i, h)),
        out_shape=jax.ShapeDtypeStruct((s, A_WIDTH), BF16),
        scratch_shapes=[pltpu.VMEM((nh * tq, hd), BF16),
                        pltpu.VMEM((nh * tq, LANES), F32),
                        pltpu.VMEM((nh * tq, 2 * hd), F32),
                        pltpu.VMEM((2, nh * tq, tq), F32),
                        pltpu.VMEM((2, nh * tq, tq), BF16),
                        pltpu.VMEM((2, nh * tq, LANES), F32)],
        compiler_params=_cparams(("parallel", "arbitrary")),
        name="diff_attention",
    )(lam, qkv, qkv, qkv, near, subln_g)


def _masked_attention(qkv, near, mask, tq):
    s = qkv.shape[0]
    hd = C_HEAD_DIM
    nh = C_GROUP
    kcol = C_WIDTH // hd
    vcol = kcol + C_KV_HEADS
    nk = s // tq
    return pl.pallas_call(
        functools.partial(_flash_kernel, nh=nh, tq=tq, sub=min(tq, ATTN_SUB_ROWS), diff=False,
                          lam_scale=1.0),
        grid=(C_KV_HEADS, s // tq),
        in_specs=[pl.BlockSpec((tq, nh * hd), lambda g, i: (i, g)),
                  pl.BlockSpec((s, hd), lambda g, i: (0, kcol + g)),
                  pl.BlockSpec((s, hd), lambda g, i: (0, vcol + g)),
                  pl.BlockSpec((1, 4, nh * tq, tq), lambda g, i: (g, 0, 0, 0)),
                  pl.BlockSpec((nk, tq, tq), lambda g, i: (0, i, 0))],
        out_specs=pl.BlockSpec((tq, nh * hd), lambda g, i: (i, g)),
        out_shape=jax.ShapeDtypeStruct((s, C_WIDTH), BF16),
        scratch_shapes=[pltpu.VMEM((nh * tq, hd), BF16),
                        pltpu.VMEM((nh * tq, LANES), F32),
                        pltpu.VMEM((nh * tq, 2 * hd), F32),
                        pltpu.VMEM((2, nh * tq, tq), F32),
                        pltpu.VMEM((2, nh * tq, tq), BF16),
                        pltpu.VMEM((2, nh * tq, LANES), F32)],
        compiler_params=_cparams(("parallel", "arbitrary")),
        name="selected_attention",
    )(qkv, qkv, qkv, near, mask)


def _sg_kernel(zb_ref, lng_ref, lnb_ref, w_ref, bs_ref, o_ref, *, nchunk):
    zb = zb_ref[...]
    gl = zb * (0.5 * (1.0 + jnp.tanh(np.float32(np.sqrt(2.0 / np.pi)) * (zb + 0.044715 * (zb * zb * zb)))))
    u = gl[:, :B_WIDTH]
    z = gl[:, B_WIDTH:]
    mu = jnp.mean(z, axis=-1, keepdims=True)
    zc = z - mu
    var = jnp.mean(zc * zc, axis=-1, keepdims=True)
    zn = (zc * lax.rsqrt(var + EPS) * lng_ref[...] + lnb_ref[...]).astype(BF16)
    row = lax.broadcasted_iota(I32, (CHUNK, CHUNK), 0)
    col = lax.broadcasted_iota(I32, (CHUNK, CHUNK), 1)
    for g in range(B_GROUPS):
        w = jnp.where(row >= col, w_ref[g], 0.0).astype(BF16)
        bias = bs_ref[g]
        lo = g * B_GROUP_DIM
        for c in range(nchunk):
            r0 = c * CHUNK
            sz = jnp.dot(w, zn[r0:r0 + CHUNK, lo:lo + B_GROUP_DIM], preferred_element_type=F32) + bias
            o_ref[r0:r0 + CHUNK, lo:lo + B_GROUP_DIM] = (u[r0:r0 + CHUNK, lo:lo + B_GROUP_DIM] * sz).astype(o_ref.dtype)


def _spatial_gating(zb, ln_g, ln_b, w_s, b_s):
    s = zb.shape[0]
    t = _tile(s, 256)
    return pl.pallas_call(
        functools.partial(_sg_kernel, nchunk=t // CHUNK),
        grid=(s // t,),
        in_specs=[pl.BlockSpec((t, 2 * B_WIDTH), lambda i: (i, 0)),
                  pl.BlockSpec((1, B_WIDTH), lambda i: (0, 0)),
                  pl.BlockSpec((1, B_WIDTH), lambda i: (0, 0)),
                  pl.BlockSpec((B_GROUPS, CHUNK, CHUNK), lambda i: (0, 0, 0)),
                  pl.BlockSpec((B_GROUPS, CHUNK, 1), lambda i: (0, 0, 0))],
        out_specs=pl.BlockSpec((t, B_WIDTH), lambda i: (i, 0)),
        out_shape=jax.ShapeDtypeStruct((s, B_WIDTH), BF16),
        compiler_params=_cparams(("parallel",)),
        name="spatial_gating",
    )(zb, ln_g.reshape(1, B_WIDTH), ln_b.reshape(1, B_WIDTH), w_s, b_s.reshape(B_GROUPS, CHUNK, 1))


def _select_kernel(qi_ref, kt_ref, w_ref, o_ref, keys_sc, half_sc, *, tkc, topk):
    tqi = IDX_QBLOCK
    i = pl.program_id(0)
    nk = o_ref.shape[0]
    nch = (i * tqi + tqi + tkc - 1) // tkc
    qpos = i * tqi + lax.broadcasted_iota(I32, (tqi, tkc), 0)
    kloc = lax.broadcasted_iota(I32, (tqi, tkc), 1)
    wgt = w_ref[...] * np.float32(IDX_DIM ** -0.5)

    def score_body(c, carry):
        sc = jnp.dot(qi_ref[0], kt_ref[c], preferred_element_type=F32)
        acc = jnp.zeros((tqi, tkc), F32)
        for h in range(IDX_HEADS):
            acc += jnp.maximum(sc[h * tqi:(h + 1) * tqi], 0.0) * wgt[:, h:h + 1]
        acc = acc + 0.0
        bits = pltpu.bitcast(acc, I32)
        ordered = jnp.where(bits < 0, bits ^ jnp.int32(0x7FFFFFFF), bits)
        key = jnp.where(c * tkc + kloc <= qpos, ordered, jnp.int32(INT_MIN))
        keys_sc[c] = key
        half_sc[c] = (key >> 16).astype(I16)
        return carry

    lax.fori_loop(0, nch, score_body, 0)

    def search_half(need):
        def bit_body(b, carry):
            thr, above = carry
            cand = thr + jnp.left_shift(jnp.int32(1), 15 - b)
            candb = jnp.broadcast_to(cand, (tqi, LANES)).astype(I16)

            def count_body(c, cnt):
                kk = half_sc[c]
                for u in range(tkc // LANES):
                    cnt += jnp.where(kk[:, u * LANES:(u + 1) * LANES] >= candb, jnp.int16(1), jnp.int16(0))
                return cnt

            cnt = lax.fori_loop(0, nch, count_body, jnp.zeros((tqi, LANES), I16))
            total = jnp.sum(cnt.astype(F32), axis=1, keepdims=True)
            take = total >= need
            return jnp.where(take, cand, thr), jnp.where(take, above, total)

        return lax.fori_loop(0, 16, bit_body, (jnp.full((tqi, 1), I16_MIN, I32), jnp.zeros((tqi, 1), F32)))

    thr_hi, above = search_half(jnp.full((tqi, 1), np.float32(topk), F32))
    thr_hi_b = jnp.broadcast_to(thr_hi, (tqi, tkc))

    def lower_body(c, carry):
        key = keys_sc[c]
        low = (key & jnp.int32(0xFFFF)) + jnp.int32(I16_MIN)
        half_sc[c] = jnp.where((key >> 16) == thr_hi_b, low, jnp.int32(I16_MIN)).astype(I16)
        return carry

    lax.fori_loop(0, nch, lower_body, 0)
    thr_lo, _ = search_half(np.float32(topk) - above)
    thr = jnp.left_shift(thr_hi, 16) + (thr_lo - jnp.int32(I16_MIN))
    thrb = jnp.broadcast_to(thr, (tqi, tkc))

    def mask_body(c, carry):
        sel = (keys_sc[c] >= thrb) & (c * tkc + kloc <= qpos)
        o_ref[c] = jnp.where(sel, 0.0, MASKED).astype(o_ref.dtype)
        return carry

    lax.fori_loop(0, nch, mask_body, 0)

    def fill_body(c, carry):
        o_ref[c] = jnp.full((tqi, tkc), MASKED, o_ref.dtype)
        return carry

    lax.fori_loop(nch, nk, fill_body, 0)


def _select_mask(qi_stack, kt, wi, tkc, topk):
    nq, rows, _ = qi_stack.shape
    nk = kt.shape[0]
    s = nq * IDX_QBLOCK
    return pl.pallas_call(
        functools.partial(_select_kernel, tkc=tkc, topk=topk),
        grid=(nq,),
        in_specs=[pl.BlockSpec((1, rows, IDX_DIM), lambda i: (i, 0, 0)),
                  pl.BlockSpec((nk, IDX_DIM, tkc), lambda i: (0, 0, 0)),
                  pl.BlockSpec((IDX_QBLOCK, IDX_HEADS), lambda i: (i, 0))],
        out_specs=pl.BlockSpec((nk, IDX_QBLOCK, tkc), lambda i: (0, i, 0)),
        out_shape=jax.ShapeDtypeStruct((nk, s, tkc), BF16),
        scratch_shapes=[pltpu.VMEM((nk, IDX_QBLOCK, tkc), I32),
                        pltpu.VMEM((nk, IDX_QBLOCK, tkc), I16)],
        compiler_params=_cparams(("parallel",)),
        name="indexer_select",
    )(qi_stack, kt, wi)


def _vec_pack(d, *rows):
    rows = [r.reshape(1, d).astype(F32) for r in rows]
    rows += [jnp.zeros((1, d), F32)] * (8 - len(rows))
    return jnp.concatenate(rows, axis=0)


def _pad_cols(w, n):
    return jnp.pad(w, ((0, 0), (0, n - w.shape[1])))


def kernel(x, c, norm_g, mod_w, mod_b, ffn_w1, ffn_w2, rel_table, ab_w_in, ab_w_out, diff_lam,
           diff_subln_g, sg_ln_g, sg_ln_b, sg_w, sg_b, dsa_w_in, dsa_w_out, final_g):
    batch, s, d = x.shape
    depth = norm_g.shape[0]
    d_ff = ffn_w2.shape[2]
    assert batch == 1 and s % IDX_QBLOCK == 0
    ff_pad = -(-d_ff // FFN_COLS) * FFN_COLS if d_ff > FFN_COLS else d_ff

    tq_a = _tile(s, 512)
    tq_c = _tile(s, 256)
    topk = min(TOPK_MAX, s // 4)

    mod = _modulation(c, mod_w, mod_b).reshape(depth, 9, d)
    near_a = _near_bias(rel_table, tq_a, A_HEADS, 2)
    near_c = _near_bias(rel_table, tq_c, C_KV_HEADS, C_GROUP)

    xs = x.reshape(s, d)
    zeros_d = jnp.zeros((d,), F32)
    for li in range(depth):
        def vec(j, li=li):
            last = final_g if (li == depth - 1 and j == 2) else zeros_d
            return _vec_pack(d, norm_g[li, j], mod[li, 3 * j], mod[li, 3 * j + 1], mod[li, 3 * j + 2], last)

        def ffn(xs, j, k, final=False, li=li):
            w1 = ffn_w1[li, k]
            w1g = _col_blocks(_pad_cols(w1[:, :d_ff], ff_pad).astype(BF16), FFN_COLS)
            w1u = _col_blocks(_pad_cols(w1[:, d_ff:], ff_pad).astype(BF16), FFN_COLS)
            w2 = jnp.pad(ffn_w2[li, k], ((0, ff_pad - d_ff), (0, 0))).astype(BF16)
            return _ffn(xs, vec(j), w1g, w1u, w2, final=final)

        xs = ffn(xs, 0, 0)

        v1 = vec(1)
        jj = li // 2
        if li % 2 == 0:
            w_in = ab_w_in[jj]
            w_qkv = jnp.concatenate([w_in[:, :A_WIDTH] * np.float32(A_QK_DIM ** -0.5 * LOG2E),
                                     w_in[:, A_WIDTH:3 * A_WIDTH]], axis=1).astype(BF16)
            w_zb = w_in[:, 3 * A_WIDTH:].astype(BF16)
            qkv = _proj(xs, v1, _col_blocks(w_qkv, PROJ_COLS), BF16)
            zb = _proj(xs, v1, _col_blocks(w_zb, PROJ_COLS), F32)
            lam_init = 0.8 - 0.6 * math.exp(-0.3 * li)
            lp = diff_lam[jj].astype(F32)
            lam = jnp.exp(jnp.sum(lp[0] * lp[1])) - jnp.exp(jnp.sum(lp[2] * lp[3])) + lam_init
            ya = _diff_attention(qkv, near_a, lam.reshape(1), diff_subln_g[jj].reshape(1, A_V_DIM),
                                 1.0 - lam_init, tq_a)
            yb = _spatial_gating(zb, sg_ln_g[jj], sg_ln_b[jj], sg_w[jj], sg_b[jj])
            w_out = ab_w_out[jj].astype(BF16)
            xs = _outproj(xs, v1, [ya, yb], [_col_blocks(w_out[:A_WIDTH], PROJ_COLS),
                                             _col_blocks(w_out[A_WIDTH:], PROJ_COLS)])
        else:
            w_in = dsa_w_in[jj]
            o_idx = C_WIDTH + 2 * C_KV_WIDTH
            o_ki = o_idx + IDX_HEADS * IDX_DIM
            w_main = jnp.concatenate([w_in[:, :C_WIDTH] * np.float32(C_HEAD_DIM ** -0.5 * LOG2E),
                                      w_in[:, C_WIDTH:o_ki]], axis=1).astype(BF16)
            w_kiw = _pad_cols(w_in[:, o_ki:], LANES).astype(BF16)
            main = _proj(xs, v1, _col_blocks(w_main, PROJ_COLS), BF16)
            kiw = _proj(xs, v1, _col_blocks(w_kiw, PROJ_COLS), F32)
            nq = s // IDX_QBLOCK
            qi = main[:, o_idx:o_ki].reshape(nq, IDX_QBLOCK, IDX_HEADS, IDX_DIM)
            qi = qi.transpose(0, 2, 1, 3).reshape(nq, IDX_HEADS * IDX_QBLOCK, IDX_DIM)
            kt = kiw[:, :IDX_DIM].astype(BF16).reshape(s // tq_c, tq_c, IDX_DIM).transpose(0, 2, 1)
            wi = kiw[:, IDX_DIM:IDX_DIM + IDX_HEADS]
            mask = _select_mask(qi, kt, wi, tq_c, topk)
            yc = _masked_attention(main, near_c, mask, tq_c)
            xs = _outproj(xs, v1, [yc], [_col_blocks(dsa_w_out[jj].astype(BF16), PROJ_COLS)])

        xs = ffn(xs, 2, 1, final=(li == depth - 1))
    return xs.reshape(batch, s, d)
```

```python
import functools
import math

import jax
import jax.numpy as jnp
import numpy as np
from jax import lax
from jax.experimental import pallas as pl
from jax.experimental.pallas import tpu as pltpu

F32 = jnp.float32
BF16 = jnp.bfloat16
I32 = jnp.int32

EPS = 1e-6
MASKED = -1e30
LANES = 128
INT_MIN = -(2 ** 31)
LOG2E = math.log2(math.e)
ATTN_SUB_ROWS = 128

A_HEADS = 8
A_QK_DIM = 64
A_V_DIM = 128
A_WIDTH = A_HEADS * A_V_DIM
B_GROUPS = 8
B_GROUP_DIM = 128
B_WIDTH = B_GROUPS * B_GROUP_DIM
CHUNK = 128
C_HEADS = 16
C_KV_HEADS = 4
C_GROUP = C_HEADS // C_KV_HEADS
C_HEAD_DIM = 128
C_WIDTH = C_HEADS * C_HEAD_DIM
C_KV_WIDTH = C_KV_HEADS * C_HEAD_DIM
IDX_HEADS = 16
IDX_DIM = 64
TOPK_MAX = 256
REL_BUCKETS = 32
REL_MAX_DIST = 128
IDX_QBLOCK = 128

VMEM_LIMIT = 56 * 1024 * 1024


def _cparams(sem):
    return pltpu.CompilerParams(dimension_semantics=sem, vmem_limit_bytes=VMEM_LIMIT)


FFN_ROWS, FFN_COLS = 512, 512
PROJ_ROWS, PROJ_COLS = 1024, 512


def _col_blocks(w, tn):
    k, n = w.shape
    tn = _tile(n, tn)
    return w.reshape(k, n // tn, tn).transpose(1, 0, 2)


def _tile(n, want):
    if n <= want:
        return n
    t = want
    while n % t:
        t //= 2
    return t


def _mod_kernel(c_ref, w_ref, b_ref, o_ref):
    c = c_ref[...]
    cs = c * (1.0 / (1.0 + jnp.exp(-c)))
    o_ref[0] = jnp.sum(cs * w_ref[0], axis=0, keepdims=True) + b_ref[0]


def _modulation(c, mod_w, mod_b):
    depth, d, n = mod_w.shape
    tn = _tile(n, 1024)
    out = pl.pallas_call(
        _mod_kernel,
        grid=(depth, n // tn),
        in_specs=[pl.BlockSpec((d, 1), lambda l, j: (0, 0)),
                  pl.BlockSpec((1, d, tn), lambda l, j: (l, 0, j)),
                  pl.BlockSpec((1, 1, tn), lambda l, j: (l, 0, j))],
        out_specs=pl.BlockSpec((1, 1, tn), lambda l, j: (l, 0, j)),
        out_shape=jax.ShapeDtypeStruct((depth, 1, n), F32),
        compiler_params=_cparams(("arbitrary", "arbitrary")),
        name="adaln_mod",
    )(c.reshape(d, 1), mod_w, mod_b.reshape(depth, 1, n))
    return out.reshape(depth, n)


def _prenorm(x, vec_ref):
    ms = jnp.mean(x * x, axis=-1, keepdims=True)
    y = x * lax.rsqrt(ms + EPS) * vec_ref[0:1, :]
    return y * (1.0 + vec_ref[2:3, :]) + vec_ref[1:2, :]


def _ffn_kernel(x_ref, vec_ref, w1g_ref, w1u_ref, w2_ref, o_ref, hn_sc, *, nf, final):
    f = pl.program_id(1)

    @pl.when(f == 0)
    def _():
        hn_sc[...] = _prenorm(x_ref[...], vec_ref).astype(BF16)
        o_ref[...] = jnp.zeros_like(o_ref)

    hn = hn_sc[...]
    g = jnp.dot(hn, w1g_ref[0], preferred_element_type=F32)
    u = jnp.dot(hn, w1u_ref[0], preferred_element_type=F32)
    a = (g * (1.0 / (1.0 + jnp.exp(-g))) * u).astype(BF16)
    o_ref[...] += jnp.dot(a, w2_ref[...], preferred_element_type=F32)

    @pl.when(f == nf - 1)
    def _():
        y = x_ref[...] + 0.5 * (1.0 + vec_ref[3:4, :]) * o_ref[...]
        if final:
            ms = jnp.mean(y * y, axis=-1, keepdims=True)
            y = y * lax.rsqrt(ms + EPS) * vec_ref[4:5, :]
        o_ref[...] = y


def _ffn(x, vec, w1g, w1u, w2, *, final):
    s, d = x.shape
    nf, _, tf = w1g.shape
    tm = _tile(s, FFN_ROWS)
    return pl.pallas_call(
        functools.partial(_ffn_kernel, nf=nf, final=final),
        grid=(s // tm, nf),
        in_specs=[pl.BlockSpec((tm, d), lambda i, f: (i, 0)),
                  pl.BlockSpec((8, d), lambda i, f: (0, 0)),
                  pl.BlockSpec((1, d, tf), lambda i, f: (f, 0, 0)),
                  pl.BlockSpec((1, d, tf), lambda i, f: (f, 0, 0)),
                  pl.BlockSpec((tf, d), lambda i, f: (f, 0))],
        out_specs=pl.BlockSpec((tm, d), lambda i, f: (i, 0)),
        out_shape=jax.ShapeDtypeStruct((s, d), F32),
        scratch_shapes=[pltpu.VMEM((tm, d), BF16)],
        compiler_params=_cparams(("parallel", "arbitrary")),
        name="swiglu_halfstep",
    )(x, vec, w1g, w1u, w2)


def _proj_kernel(x_ref, vec_ref, w_ref, o_ref, hn_sc):
    @pl.when(pl.program_id(1) == 0)
    def _():
        hn_sc[...] = _prenorm(x_ref[...], vec_ref).astype(BF16)

    o_ref[...] = jnp.dot(hn_sc[...], w_ref[0], preferred_element_type=F32).astype(o_ref.dtype)


def _proj(x, vec, w, out_dtype):
    s, d = x.shape
    nn, _, tn = w.shape
    n = nn * tn
    tm = _tile(s, PROJ_ROWS)
    return pl.pallas_call(
        _proj_kernel,
        grid=(s // tm, nn),
        in_specs=[pl.BlockSpec((tm, d), lambda i, j: (i, 0)),
                  pl.BlockSpec((8, d), lambda i, j: (0, 0)),
                  pl.BlockSpec((1, d, tn), lambda i, j: (j, 0, 0))],
        out_specs=pl.BlockSpec((tm, tn), lambda i, j: (i, j)),
        out_shape=jax.ShapeDtypeStruct((s, n), out_dtype),
        scratch_shapes=[pltpu.VMEM((tm, d), BF16)],
        compiler_params=_cparams(("parallel", "arbitrary")),
        name="norm_mod_proj",
    )(x, vec, w)


def _outproj_kernel(*refs, n_in):
    x_ref, vec_ref = refs[0], refs[1]
    lhs = refs[2:2 + n_in]
    ws = refs[2 + n_in:2 + 2 * n_in]
    o_ref = refs[2 + 2 * n_in]
    acc = jnp.dot(lhs[0][...], ws[0][0], preferred_element_type=F32)
    for a, w in zip(lhs[1:], ws[1:]):
        acc += jnp.dot(a[...], w[0], preferred_element_type=F32)
    o_ref[...] = x_ref[...] + (1.0 + vec_ref[3:4, :]) * acc


def _outproj(x, vec, lhs, ws):
    s, d = x.shape
    tm = _tile(s, PROJ_ROWS)
    tn = ws[0].shape[2]
    n_in = len(lhs)
    in_specs = [pl.BlockSpec((tm, tn), lambda i, j: (i, j)),
                pl.BlockSpec((8, tn), lambda i, j: (0, j))]
    in_specs += [pl.BlockSpec((tm, a.shape[1]), lambda i, j: (i, 0)) for a in lhs]
    in_specs += [pl.BlockSpec((1, w.shape[1], tn), lambda i, j: (j, 0, 0)) for w in ws]
    return pl.pallas_call(
        functools.partial(_outproj_kernel, n_in=n_in),
        grid=(s // tm, d // tn),
        in_specs=in_specs,
        out_specs=pl.BlockSpec((tm, tn), lambda i, j: (i, j)),
        out_shape=jax.ShapeDtypeStruct((s, d), F32),
        compiler_params=_cparams(("parallel", "arbitrary")),
        name="outproj_residual",
    )(x, vec, *lhs, *ws)


def _flash_kernel(*refs, nh, tq, sub, diff, lam_scale):
    if diff:
        lam_ref, q_ref, k_ref, v_ref, nb_ref, g_ref, o_ref, qs_sc, m_sc, acc_sc, s_sc, p_sc, al_sc = refs
        mask_ref = None
    else:
        q_ref, k_ref, v_ref, nb_ref, mask_ref, o_ref, qs_sc, m_sc, acc_sc, s_sc, p_sc, al_sc = refs
    tk = tq
    rows = nh * tq
    hd = k_ref.shape[1]
    assert hd == LANES and rows % sub == 0 and tq % sub == 0
    i = pl.program_id(1)

    if diff:
        q = q_ref[...]
        lane = lax.broadcasted_iota(I32, q.shape, 1)
        zero = jnp.zeros_like(q)
        qs_sc[0:tq, :] = jnp.where(lane < A_QK_DIM, q, zero)
        qs_sc[tq:2 * tq, :] = jnp.where(lane >= A_QK_DIM, q, zero)
    else:
        for r in range(nh):
            qs_sc[r * tq:(r + 1) * tq, :] = q_ref[:, r * hd:(r + 1) * hd]
    m_sc[...] = jnp.full(m_sc.shape, -jnp.inf, F32)
    acc_sc[...] = jnp.zeros(acc_sc.shape, F32)

    nblk = i + 1

    def bias_index(j):
        return jnp.where(j >= nblk, 3, jnp.clip(j - (i - 2), 0, 2))

    def stage_qk(j, slot):
        start = pl.multiple_of(jnp.minimum(j, i) * tk, tk)
        kb = k_ref[pl.ds(start, tk), :]
        s_sc[slot] = lax.dot_general(qs_sc[...], kb, (((1,), (1,)), ((), ())), preferred_element_type=F32)

    def stage_softmax(j, slot, biased):
        jc = jnp.minimum(j, i)
        col = bias_index(j) if biased else None
        for r in range(rows // sub):
            rs = slice(r * sub, (r + 1) * sub)
            s = s_sc[slot, rs, :]
            if biased:
                s = s + nb_ref[0, col, rs, :]
            if mask_ref is not None:
                off = (r * sub) % tq
                s = s + mask_ref[jc, off:off + sub, :].astype(F32)
            tiles = [s[:, t * LANES:(t + 1) * LANES] for t in range(tk // LANES)]
            cmax = tiles[0]
            for t in tiles[1:]:
                cmax = jnp.maximum(cmax, t)
            m_old = m_sc[rs, :]
            m_new = jnp.maximum(m_old, jnp.max(cmax, axis=1, keepdims=True))
            al_sc[slot, rs, :] = jnp.exp2(m_old - m_new)
            p_sc[slot, rs, :] = jnp.concatenate([jnp.exp2(t - m_new) for t in tiles], axis=1).astype(BF16)
            m_sc[rs, :] = m_new

    def stage_pv(j, slot):
        start = pl.multiple_of(jnp.minimum(j, i) * tk, tk)
        vbe = jnp.concatenate([v_ref[pl.ds(start, tk), :], jnp.ones((tk, hd), BF16)], axis=1)
        pv = jnp.dot(p_sc[slot], vbe, preferred_element_type=F32)
        alpha = al_sc[slot]
        acc_sc[...] = jnp.concatenate([alpha, alpha], axis=1) * acc_sc[...] + pv

    stage_qk(0, 0)
    stage_qk(1, 1)
    stage_softmax(0, 0, True)

    npairs = jnp.maximum((i - 2) // 2, 0)

    def far_pair(u, carry):
        t = 2 * u
        stage_pv(t, 0)
        stage_softmax(t + 1, 1, False)
        stage_qk(t + 2, 0)
        stage_pv(t + 1, 1)
        stage_softmax(t + 2, 0, False)
        stage_qk(t + 3, 1)
        return carry

    lax.fori_loop(0, npairs, far_pair, 0)

    t0 = 2 * npairs
    stage_pv(t0, 0)
    stage_softmax(t0 + 1, 1, True)
    stage_qk(t0 + 2, 0)
    stage_pv(t0 + 1, 1)
    stage_softmax(t0 + 2, 0, True)
    stage_qk(t0 + 3, 1)
    stage_pv(t0 + 2, 0)
    stage_softmax(t0 + 3, 1, True)
    stage_pv(t0 + 3, 1)

    if diff:
        o0 = acc_sc[0:tq, 0:hd] / acc_sc[0:tq, hd:2 * hd]
        o1 = acc_sc[tq:2 * tq, 0:hd] / acc_sc[tq:2 * tq, hd:2 * hd]
        dlt = o0 - lam_ref[0] * o1
        ms = jnp.mean(dlt * dlt, axis=-1, keepdims=True)
        o_ref[...] = ((dlt * lax.rsqrt(ms + EPS) * g_ref[...]) * lam_scale).astype(o_ref.dtype)
    else:
        for r in range(nh):
            rs = slice(r * tq, (r + 1) * tq)
            o_ref[:, r * hd:(r + 1) * hd] = (acc_sc[rs, 0:hd] / acc_sc[rs, hd:2 * hd]).astype(o_ref.dtype)


def _rel_bucket(dist):
    n = jnp.maximum(dist, 0)
    max_exact = REL_BUCKETS // 2
    nf = jnp.maximum(n, 1).astype(F32)
    large = max_exact + (jnp.log(nf / max_exact) / math.log(REL_MAX_DIST / max_exact)
                         * (REL_BUCKETS - max_exact)).astype(I32)
    large = jnp.minimum(large, REL_BUCKETS - 1)
    return jnp.where(n < max_exact, n, large)


def _near_bias(rel_table, tq, groups, nh):
    assert tq >= LANES, "keys older than one block must all fall in the last bucket"
    r = jnp.arange(tq, dtype=I32)[:, None]
    c = jnp.arange(2 * tq, dtype=I32)[None, :]
    dist = r + tq - c
    rel = (rel_table - rel_table[REL_BUCKETS - 1][None, :]) * np.float32(LOG2E)
    onehot = jax.nn.one_hot(_rel_bucket(dist), REL_BUCKETS, dtype=F32)
    b = jnp.einsum("rcb,bh->hrc", onehot, rel, precision=lax.Precision.HIGHEST)
    b = jnp.where((dist >= 0)[None], b, MASKED)
    heads = b.shape[0]
    tiles = jnp.stack([jnp.zeros((heads, tq, tq), F32), b[:, :, :tq], b[:, :, tq:],
                       jnp.full((heads, tq, tq), MASKED, F32)], axis=1)
    tiles = tiles.reshape(groups, nh, 4, tq, tq).transpose(0, 2, 1, 3, 4)
    return tiles.reshape(groups, 4, nh * tq, tq)


def _diff_attention(qkv, near, lam, subln_g, lam_scale, tq):
    s = qkv.shape[0]
    hd = A_V_DIM
    nh = 2
    kcol = A_WIDTH // hd
    return pl.pallas_call(
        functools.partial(_flash_kernel, nh=nh, tq=tq, sub=min(tq, ATTN_SUB_ROWS), diff=True,
                          lam_scale=lam_scale),
        grid=(A_HEADS, s // tq),
        in_specs=[pl.BlockSpec(memory_space=pltpu.SMEM),
                  pl.BlockSpec((tq, hd), lambda h, i: (i, h)),
                  pl.BlockSpec((s, hd), lambda h, i: (0, kcol + h)),
                  pl.BlockSpec((s, hd), lambda h, i: (0, 2 * kcol + h)),
                  pl.BlockSpec((1, 4, nh * tq, tq), lambda h, i: (h, 0, 0, 0)),
                  pl.BlockSpec((1, hd), lambda h, i: (0, 0))],
        out_specs=pl.BlockSpec((tq, hd), lambda h, i: (i, h)),
        out_shape=jax.ShapeDtypeStruct((s, A_WIDTH), BF16),
        scratch_shapes=[pltpu.VMEM((nh * tq, hd), BF16),
                        pltpu.VMEM((nh * tq, LANES), F32),
                        pltpu.VMEM((nh * tq, 2 * hd), F32),
                        pltpu.VMEM((2, nh * tq, tq), F32),
                        pltpu.VMEM((2, nh * tq, tq), BF16),
                        pltpu.VMEM((2, nh * tq, LANES), F32)],
        compiler_params=_cparams(("parallel", "arbitrary")),
        name="diff_attention",
    )(lam, qkv, qkv, qkv, near, subln_g)


def _masked_attention(qkv, near, mask, tq):
    s = qkv.shape[0]
    hd = C_HEAD_DIM
    nh = C_GROUP
    kcol = C_WIDTH // hd
    vcol = kcol + C_KV_HEADS
    nk = s // tq
    return pl.pallas_call(
        functools.partial(_flash_kernel, nh=nh, tq=tq, sub=min(tq, ATTN_SUB_ROWS), diff=False,
                          lam_scale=1.0),
        grid=(C_KV_HEADS, s // tq),
        in_specs=[pl.BlockSpec((tq, nh * hd), lambda g, i: (i, g)),
                  pl.BlockSpec((s, hd), lambda g, i: (0, kcol + g)),
                  pl.BlockSpec((s, hd), lambda g, i: (0, vcol + g)),
                  pl.BlockSpec((1, 4, nh * tq, tq), lambda g, i: (g, 0, 0, 0)),
                  pl.BlockSpec((nk, tq, tq), lambda g, i: (0, i, 0))],
        out_specs=pl.BlockSpec((tq, nh * hd), lambda g, i: (i, g)),
        out_shape=jax.ShapeDtypeStruct((s, C_WIDTH), BF16),
        scratch_shapes=[pltpu.VMEM((nh * tq, hd), BF16),
                        pltpu.VMEM((nh * tq, LANES), F32),
                        pltpu.VMEM((nh * tq, 2 * hd), F32),
                        pltpu.VMEM((2, nh * tq, tq), F32),
                        pltpu.VMEM((2, nh * tq, tq), BF16),
                        pltpu.VMEM((2, nh * tq, LANES), F32)],
        compiler_params=_cparams(("parallel", "arbitrary")),
        name="selected_attention",
    )(qkv, qkv, qkv, near, mask)


def _sg_kernel(zb_ref, lng_ref, lnb_ref, w_ref, bs_ref, o_ref, *, nchunk):
    zb = zb_ref[...]
    gl = zb * (0.5 * (1.0 + jnp.tanh(np.float32(np.sqrt(2.0 / np.pi)) * (zb + 0.044715 * (zb * zb * zb)))))
    u = gl[:, :B_WIDTH]
    z = gl[:, B_WIDTH:]
    mu = jnp.mean(z, axis=-1, keepdims=True)
    zc = z - mu
    var = jnp.mean(zc * zc, axis=-1, keepdims=True)
    zn = (zc * lax.rsqrt(var + EPS) * lng_ref[...] + lnb_ref[...]).astype(BF16)
    row = lax.broadcasted_iota(I32, (CHUNK, CHUNK), 0)
    col = lax.broadcasted_iota(I32, (CHUNK, CHUNK), 1)
    for g in range(B_GROUPS):
        w = jnp.where(row >= col, w_ref[g], 0.0).astype(BF16)
        bias = bs_ref[g]
        lo = g * B_GROUP_DIM
        for c in range(nchunk):
            r0 = c * CHUNK
            sz = jnp.dot(w, zn[r0:r0 + CHUNK, lo:lo + B_GROUP_DIM], preferred_element_type=F32) + bias
            o_ref[r0:r0 + CHUNK, lo:lo + B_GROUP_DIM] = (u[r0:r0 + CHUNK, lo:lo + B_GROUP_DIM] * sz).astype(o_ref.dtype)


def _spatial_gating(zb, ln_g, ln_b, w_s, b_s):
    s = zb.shape[0]
    t = _tile(s, 256)
    return pl.pallas_call(
        functools.partial(_sg_kernel, nchunk=t // CHUNK),
        grid=(s // t,),
        in_specs=[pl.BlockSpec((t, 2 * B_WIDTH), lambda i: (i, 0)),
                  pl.BlockSpec((1, B_WIDTH), lambda i: (0, 0)),
                  pl.BlockSpec((1, B_WIDTH), lambda i: (0, 0)),
                  pl.BlockSpec((B_GROUPS, CHUNK, CHUNK), lambda i: (0, 0, 0)),
                  pl.BlockSpec((B_GROUPS, CHUNK, 1), lambda i: (0, 0, 0))],
        out_specs=pl.BlockSpec((t, B_WIDTH), lambda i: (i, 0)),
        out_shape=jax.ShapeDtypeStruct((s, B_WIDTH), BF16),
        compiler_params=_cparams(("parallel",)),
        name="spatial_gating",
    )(zb, ln_g.reshape(1, B_WIDTH), ln_b.reshape(1, B_WIDTH), w_s, b_s.reshape(B_GROUPS, CHUNK, 1))


def _select_kernel(qi_ref, kt_ref, w_ref, o_ref, keys_sc, sc_sc, wb_sc, *, tkc, topk):
    tqi = IDX_QBLOCK
    i = pl.program_id(0)
    nk = o_ref.shape[0]
    nch = (i * tqi + tqi + tkc - 1) // tkc
    qpos = i * tqi + lax.broadcasted_iota(I32, (tqi, tkc), 0)
    kloc = lax.broadcasted_iota(I32, (tqi, tkc), 1)
    qrow = i * tqi + lax.broadcasted_iota(I32, (tqi, LANES), 0)
    klane = lax.broadcasted_iota(I32, (tqi, LANES), 1)

    wgt = w_ref[...] * np.float32(IDX_DIM ** -0.5)
    for h in range(IDX_HEADS):
        wb_sc[h] = jnp.broadcast_to(wgt[:, h:h + 1], (tqi, LANES))

    def stage_dot(c, slot):
        sc_sc[slot] = jnp.dot(qi_ref[0], kt_ref[jnp.minimum(c, nch - 1)], preferred_element_type=F32)

    def stage_reduce(c, slot):
        c = jnp.minimum(c, nch - 1)
        for u in range(tkc // LANES):
            ls = slice(u * LANES, (u + 1) * LANES)
            acc = jnp.zeros((tqi, LANES), F32)
            for h in range(IDX_HEADS):
                acc += jnp.maximum(sc_sc[slot, h * tqi:(h + 1) * tqi, ls], 0.0) * wb_sc[h]
            acc = acc + 0.0
            bits = pltpu.bitcast(acc, I32)
            ordered = jnp.where(bits < 0, bits ^ jnp.int32(0x7FFFFFFF), bits)
            causal = c * tkc + u * LANES + klane <= qrow
            keys_sc[c, :, ls] = jnp.where(causal, ordered, jnp.int32(INT_MIN))

    stage_dot(0, 0)

    def score_pair(u, carry):
        c = 2 * u
        stage_dot(c + 1, 1)
        stage_reduce(c, 0)
        stage_dot(c + 2, 0)
        stage_reduce(c + 1, 1)
        return carry

    lax.fori_loop(0, (nch + 1) // 2, score_pair, 0)

    def bit_body(b, thr):
        cand = thr ^ jnp.left_shift(jnp.int32(1), 31 - b)
        candb = jnp.broadcast_to(cand, (tqi, LANES))

        def count_body(c, cnt):
            kk = keys_sc[c]
            for u in range(tkc // LANES):
                cnt += jnp.where(kk[:, u * LANES:(u + 1) * LANES] >= candb, 1, 0)
            return cnt

        cnt = lax.fori_loop(0, nch, count_body, jnp.zeros((tqi, LANES), I32))
        total = jnp.sum(cnt.astype(F32), axis=1, keepdims=True)
        return jnp.where(total >= np.float32(topk), cand, thr)

    thr = lax.fori_loop(0, 32, bit_body, jnp.full((tqi, 1), INT_MIN, I32))
    thrb = jnp.broadcast_to(thr, (tqi, tkc))

    def mask_body(c, carry):
        sel = (keys_sc[c] >= thrb) & (c * tkc + kloc <= qpos)
        o_ref[c] = jnp.where(sel, 0.0, MASKED).astype(o_ref.dtype)
        return carry

    lax.fori_loop(0, nch, mask_body, 0)

    def fill_body(c, carry):
        o_ref[c] = jnp.full((tqi, tkc), MASKED, o_ref.dtype)
        return carry

    lax.fori_loop(nch, nk, fill_body, 0)


def _select_mask(qi_stack, kt, wi, tkc, topk):
    nq, rows, _ = qi_stack.shape
    nk = kt.shape[0]
    s = nq * IDX_QBLOCK
    return pl.pallas_call(
        functools.partial(_select_kernel, tkc=tkc, topk=topk),
        grid=(nq,),
        in_specs=[pl.BlockSpec((1, rows, IDX_DIM), lambda i: (i, 0, 0)),
                  pl.BlockSpec((nk, IDX_DIM, tkc), lambda i: (0, 0, 0)),
                  pl.BlockSpec((IDX_QBLOCK, IDX_HEADS), lambda i: (i, 0))],
        out_specs=pl.BlockSpec((nk, IDX_QBLOCK, tkc), lambda i: (0, i, 0)),
        out_shape=jax.ShapeDtypeStruct((nk, s, tkc), BF16),
        scratch_shapes=[pltpu.VMEM((nk, IDX_QBLOCK, tkc), I32),
                        pltpu.VMEM((2, rows, tkc), F32),
                        pltpu.VMEM((IDX_HEADS, IDX_QBLOCK, LANES), F32)],
        compiler_params=_cparams(("parallel",)),
        name="indexer_select",
    )(qi_stack, kt, wi)


def _vec_pack(d, *rows):
    rows = [r.reshape(1, d).astype(F32) for r in rows]
    rows += [jnp.zeros((1, d), F32)] * (8 - len(rows))
    return jnp.concatenate(rows, axis=0)


def _pad_cols(w, n):
    return jnp.pad(w, ((0, 0), (0, n - w.shape[1])))


def kernel(x, c, norm_g, mod_w, mod_b, ffn_w1, ffn_w2, rel_table, ab_w_in, ab_w_out, diff_lam,
           diff_subln_g, sg_ln_g, sg_ln_b, sg_w, sg_b, dsa_w_in, dsa_w_out, final_g):
    batch, s, d = x.shape
    depth = norm_g.shape[0]
    d_ff = ffn_w2.shape[2]
    assert batch == 1 and s % IDX_QBLOCK == 0
    ff_pad = -(-d_ff // FFN_COLS) * FFN_COLS if d_ff > FFN_COLS else d_ff

    tq_a = _tile(s, 512)
    tq_c = _tile(s, 256)
    topk = min(TOPK_MAX, s // 4)

    mod = _modulation(c, mod_w, mod_b).reshape(depth, 9, d)
    near_a = _near_bias(rel_table, tq_a, A_HEADS, 2)
    near_c = _near_bias(rel_table, tq_c, C_KV_HEADS, C_GROUP)

    xs = x.reshape(s, d)
    zeros_d = jnp.zeros((d,), F32)
    for li in range(depth):
        def vec(j, li=li):
            last = final_g if (li == depth - 1 and j == 2) else zeros_d
            return _vec_pack(d, norm_g[li, j], mod[li, 3 * j], mod[li, 3 * j + 1], mod[li, 3 * j + 2], last)

        def ffn(xs, j, k, final=False, li=li):
            w1 = ffn_w1[li, k]
            w1g = _col_blocks(_pad_cols(w1[:, :d_ff], ff_pad).astype(BF16), FFN_COLS)
            w1u = _col_blocks(_pad_cols(w1[:, d_ff:], ff_pad).astype(BF16), FFN_COLS)
            w2 = jnp.pad(ffn_w2[li, k], ((0, ff_pad - d_ff), (0, 0))).astype(BF16)
            return _ffn(xs, vec(j), w1g, w1u, w2, final=final)

        xs = ffn(xs, 0, 0)

        v1 = vec(1)
        jj = li // 2
        if li % 2 == 0:
            w_in = ab_w_in[jj]
            w_qkv = jnp.concatenate([w_in[:, :A_WIDTH] * np.float32(A_QK_DIM ** -0.5 * LOG2E),
                                     w_in[:, A_WIDTH:3 * A_WIDTH]], axis=1).astype(BF16)
            w_zb = w_in[:, 3 * A_WIDTH:].astype(BF16)
            qkv = _proj(xs, v1, _col_blocks(w_qkv, PROJ_COLS), BF16)
            zb = _proj(xs, v1, _col_blocks(w_zb, PROJ_COLS), F32)
            lam_init = 0.8 - 0.6 * math.exp(-0.3 * li)
            lp = diff_lam[jj].astype(F32)
            lam = jnp.exp(jnp.sum(lp[0] * lp[1])) - jnp.exp(jnp.sum(lp[2] * lp[3])) + lam_init
            ya = _diff_attention(qkv, near_a, lam.reshape(1), diff_subln_g[jj].reshape(1, A_V_DIM),
                                 1.0 - lam_init, tq_a)
            yb = _spatial_gating(zb, sg_ln_g[jj], sg_ln_b[jj], sg_w[jj], sg_b[jj])
            w_out = ab_w_out[jj].astype(BF16)
            xs = _outproj(xs, v1, [ya, yb], [_col_blocks(w_out[:A_WIDTH], PROJ_COLS),
                                             _col_blocks(w_out[A_WIDTH:], PROJ_COLS)])
        else:
            w_in = dsa_w_in[jj]
            o_idx = C_WIDTH + 2 * C_KV_WIDTH
            o_ki = o_idx + IDX_HEADS * IDX_DIM
            w_main = jnp.concatenate([w_in[:, :C_WIDTH] * np.float32(C_HEAD_DIM ** -0.5 * LOG2E),
                                      w_in[:, C_WIDTH:o_ki]], axis=1).astype(BF16)
            w_kiw = _pad_cols(w_in[:, o_ki:], LANES).astype(BF16)
            main = _proj(xs, v1, _col_blocks(w_main, PROJ_COLS), BF16)
            kiw = _proj(xs, v1, _col_blocks(w_kiw, PROJ_COLS), F32)
            nq = s // IDX_QBLOCK
            qi = main[:, o_idx:o_ki].reshape(nq, IDX_QBLOCK, IDX_HEADS, IDX_DIM)
            qi = qi.transpose(0, 2, 1, 3).reshape(nq, IDX_HEADS * IDX_QBLOCK, IDX_DIM)
            kt = kiw[:, :IDX_DIM].astype(BF16).reshape(s // tq_c, tq_c, IDX_DIM).transpose(0, 2, 1)
            wi = kiw[:, IDX_DIM:IDX_DIM + IDX_HEADS]
            mask = _select_mask(qi, kt, wi, tq_c, topk)
            yc = _masked_attention(main, near_c, mask, tq_c)
            xs = _outproj(xs, v1, [yc], [_col_blocks(dsa_w_out[jj].astype(BF16), PROJ_COLS)])

        xs = ffn(xs, 2, 1, final=(li == depth - 1))
    return xs.reshape(batch, s, d)
```

```python
import functools
import math

import jax
import jax.numpy as jnp
import numpy as np
from jax import lax
from jax.experimental import pallas as pl
from jax.experimental.pallas import tpu as pltpu

F32 = jnp.float32
BF16 = jnp.bfloat16
I32 = jnp.int32

EPS = 1e-6
MASKED = -1e30
LANES = 128
INT_MIN = -(2 ** 31)
LOG2E = math.log2(math.e)
ATTN_SUB_ROWS = 128

A_HEADS = 8
A_QK_DIM = 64
A_V_DIM = 128
A_WIDTH = A_HEADS * A_V_DIM
B_GROUPS = 8
B_GROUP_DIM = 128
B_WIDTH = B_GROUPS * B_GROUP_DIM
CHUNK = 128
C_HEADS = 16
C_KV_HEADS = 4
C_GROUP = C_HEADS // C_KV_HEADS
C_HEAD_DIM = 128
C_WIDTH = C_HEADS * C_HEAD_DIM
C_KV_WIDTH = C_KV_HEADS * C_HEAD_DIM
IDX_HEADS = 16
IDX_DIM = 64
TOPK_MAX = 256
REL_BUCKETS = 32
REL_MAX_DIST = 128
IDX_QBLOCK = 128
SEARCH_PROBE_DROP = 2 ** 24

VMEM_LIMIT = 56 * 1024 * 1024


def _cparams(sem):
    return pltpu.CompilerParams(dimension_semantics=sem, vmem_limit_bytes=VMEM_LIMIT)


FFN_ROWS, FFN_COLS = 512, 512
PROJ_ROWS, PROJ_COLS = 1024, 512


def _col_blocks(w, tn):
    k, n = w.shape
    tn = _tile(n, tn)
    return w.reshape(k, n // tn, tn).transpose(1, 0, 2)


def _tile(n, want):
    if n <= want:
        return n
    t = want
    while n % t:
        t //= 2
    return t


def _mod_kernel(c_ref, w_ref, b_ref, o_ref):
    c = c_ref[...]
    cs = c * (1.0 / (1.0 + jnp.exp(-c)))
    o_ref[0] = jnp.sum(cs * w_ref[0], axis=0, keepdims=True) + b_ref[0]


def _modulation(c, mod_w, mod_b):
    depth, d, n = mod_w.shape
    tn = _tile(n, 1024)
    out = pl.pallas_call(
        _mod_kernel,
        grid=(depth, n // tn),
        in_specs=[pl.BlockSpec((d, 1), lambda l, j: (0, 0)),
                  pl.BlockSpec((1, d, tn), lambda l, j: (l, 0, j)),
                  pl.BlockSpec((1, 1, tn), lambda l, j: (l, 0, j))],
        out_specs=pl.BlockSpec((1, 1, tn), lambda l, j: (l, 0, j)),
        out_shape=jax.ShapeDtypeStruct((depth, 1, n), F32),
        compiler_params=_cparams(("arbitrary", "arbitrary")),
        name="adaln_mod",
    )(c.reshape(d, 1), mod_w, mod_b.reshape(depth, 1, n))
    return out.reshape(depth, n)


def _prenorm(x, vec_ref):
    ms = jnp.mean(x * x, axis=-1, keepdims=True)
    y = x * lax.rsqrt(ms + EPS) * vec_ref[0:1, :]
    return y * (1.0 + vec_ref[2:3, :]) + vec_ref[1:2, :]


def _ffn_kernel(x_ref, vec_ref, w1g_ref, w1u_ref, w2_ref, o_ref, hn_sc, *, nf, final):
    f = pl.program_id(1)

    @pl.when(f == 0)
    def _():
        hn_sc[...] = _prenorm(x_ref[...], vec_ref).astype(BF16)
        o_ref[...] = jnp.zeros_like(o_ref)

    hn = hn_sc[...]
    g = jnp.dot(hn, w1g_ref[0], preferred_element_type=F32)
    u = jnp.dot(hn, w1u_ref[0], preferred_element_type=F32)
    a = (g * (1.0 / (1.0 + jnp.exp(-g))) * u).astype(BF16)
    o_ref[...] += jnp.dot(a, w2_ref[...], preferred_element_type=F32)

    @pl.when(f == nf - 1)
    def _():
        y = x_ref[...] + 0.5 * (1.0 + vec_ref[3:4, :]) * o_ref[...]
        if final:
            ms = jnp.mean(y * y, axis=-1, keepdims=True)
            y = y * lax.rsqrt(ms + EPS) * vec_ref[4:5, :]
        o_ref[...] = y


def _ffn(x, vec, w1g, w1u, w2, *, final):
    s, d = x.shape
    nf, _, tf = w1g.shape
    tm = _tile(s, FFN_ROWS)
    return pl.pallas_call(
        functools.partial(_ffn_kernel, nf=nf, final=final),
        grid=(s // tm, nf),
        in_specs=[pl.BlockSpec((tm, d), lambda i, f: (i, 0)),
                  pl.BlockSpec((8, d), lambda i, f: (0, 0)),
                  pl.BlockSpec((1, d, tf), lambda i, f: (f, 0, 0)),
                  pl.BlockSpec((1, d, tf), lambda i, f: (f, 0, 0)),
                  pl.BlockSpec((tf, d), lambda i, f: (f, 0))],
        out_specs=pl.BlockSpec((tm, d), lambda i, f: (i, 0)),
        out_shape=jax.ShapeDtypeStruct((s, d), F32),
        scratch_shapes=[pltpu.VMEM((tm, d), BF16)],
        compiler_params=_cparams(("parallel", "arbitrary")),
        name="swiglu_halfstep",
    )(x, vec, w1g, w1u, w2)


def _proj_kernel(x_ref, vec_ref, w_ref, o_ref, hn_sc):
    @pl.when(pl.program_id(1) == 0)
    def _():
        hn_sc[...] = _prenorm(x_ref[...], vec_ref).astype(BF16)

    o_ref[...] = jnp.dot(hn_sc[...], w_ref[0], preferred_element_type=F32).astype(o_ref.dtype)


def _proj(x, vec, w, out_dtype):
    s, d = x.shape
    nn, _, tn = w.shape
    n = nn * tn
    tm = _tile(s, PROJ_ROWS)
    return pl.pallas_call(
        _proj_kernel,
        grid=(s // tm, nn),
        in_specs=[pl.BlockSpec((tm, d), lambda i, j: (i, 0)),
                  pl.BlockSpec((8, d), lambda i, j: (0, 0)),
                  pl.BlockSpec((1, d, tn), lambda i, j: (j, 0, 0))],
        out_specs=pl.BlockSpec((tm, tn), lambda i, j: (i, j)),
        out_shape=jax.ShapeDtypeStruct((s, n), out_dtype),
        scratch_shapes=[pltpu.VMEM((tm, d), BF16)],
        compiler_params=_cparams(("parallel", "arbitrary")),
        name="norm_mod_proj",
    )(x, vec, w)


def _outproj_kernel(*refs, n_in):
    x_ref, vec_ref = refs[0], refs[1]
    lhs = refs[2:2 + n_in]
    ws = refs[2 + n_in:2 + 2 * n_in]
    o_ref = refs[2 + 2 * n_in]
    acc = jnp.dot(lhs[0][...], ws[0][0], preferred_element_type=F32)
    for a, w in zip(lhs[1:], ws[1:]):
        acc += jnp.dot(a[...], w[0], preferred_element_type=F32)
    o_ref[...] = x_ref[...] + (1.0 + vec_ref[3:4, :]) * acc


def _outproj(x, vec, lhs, ws):
    s, d = x.shape
    tm = _tile(s, PROJ_ROWS)
    tn = ws[0].shape[2]
    n_in = len(lhs)
    in_specs = [pl.BlockSpec((tm, tn), lambda i, j: (i, j)),
                pl.BlockSpec((8, tn), lambda i, j: (0, j))]
    in_specs += [pl.BlockSpec((tm, a.shape[1]), lambda i, j: (i, 0)) for a in lhs]
    in_specs += [pl.BlockSpec((1, w.shape[1], tn), lambda i, j: (j, 0, 0)) for w in ws]
    return pl.pallas_call(
        functools.partial(_outproj_kernel, n_in=n_in),
        grid=(s // tm, d // tn),
        in_specs=in_specs,
        out_specs=pl.BlockSpec((tm, tn), lambda i, j: (i, j)),
        out_shape=jax.ShapeDtypeStruct((s, d), F32),
        compiler_params=_cparams(("parallel", "arbitrary")),
        name="outproj_residual",
    )(x, vec, *lhs, *ws)


def _flash_kernel(*refs, nh, tq, sub, diff, lam_scale):
    if diff:
        lam_ref, q_ref, k_ref, v_ref, nb_ref, g_ref, o_ref, qs_sc, m_sc, acc_sc, s_sc, p_sc, al_sc = refs
        mask_ref = None
    else:
        q_ref, k_ref, v_ref, nb_ref, mask_ref, o_ref, qs_sc, m_sc, acc_sc, s_sc, p_sc, al_sc = refs
    tk = tq
    rows = nh * tq
    hd = k_ref.shape[1]
    assert hd == LANES and rows % sub == 0 and tq % sub == 0
    i = pl.program_id(1)

    if diff:
        q = q_ref[...]
        lane = lax.broadcasted_iota(I32, q.shape, 1)
        zero = jnp.zeros_like(q)
        qs_sc[0:tq, :] = jnp.where(lane < A_QK_DIM, q, zero)
        qs_sc[tq:2 * tq, :] = jnp.where(lane >= A_QK_DIM, q, zero)
    else:
        for r in range(nh):
            qs_sc[r * tq:(r + 1) * tq, :] = q_ref[:, r * hd:(r + 1) * hd]
    m_sc[...] = jnp.full(m_sc.shape, -jnp.inf, F32)
    acc_sc[...] = jnp.zeros(acc_sc.shape, F32)

    nblk = i + 1

    def bias_index(j):
        return jnp.where(j >= nblk, 3, jnp.clip(j - (i - 2), 0, 2))

    def stage_qk(j, slot):
        start = pl.multiple_of(jnp.minimum(j, i) * tk, tk)
        kb = k_ref[pl.ds(start, tk), :]
        s_sc[slot] = lax.dot_general(qs_sc[...], kb, (((1,), (1,)), ((), ())), preferred_element_type=F32)

    def stage_softmax(j, slot, biased):
        jc = jnp.minimum(j, i)
        col = bias_index(j) if biased else None
        for r in range(rows // sub):
            rs = slice(r * sub, (r + 1) * sub)
            s = s_sc[slot, rs, :]
            if biased:
                s = s + nb_ref[0, col, rs, :]
            if mask_ref is not None:
                off = (r * sub) % tq
                s = s + mask_ref[jc, off:off + sub, :].astype(F32)
            tiles = [s[:, t * LANES:(t + 1) * LANES] for t in range(tk // LANES)]
            cmax = tiles[0]
            for t in tiles[1:]:
                cmax = jnp.maximum(cmax, t)
            m_old = m_sc[rs, :]
            m_new = jnp.maximum(m_old, jnp.max(cmax, axis=1, keepdims=True))
            al_sc[slot, rs, :] = jnp.exp2(m_old - m_new)
            p_sc[slot, rs, :] = jnp.concatenate([jnp.exp2(t - m_new) for t in tiles], axis=1).astype(BF16)
            m_sc[rs, :] = m_new

    def stage_pv(j, slot):
        start = pl.multiple_of(jnp.minimum(j, i) * tk, tk)
        vbe = jnp.concatenate([v_ref[pl.ds(start, tk), :], jnp.ones((tk, hd), BF16)], axis=1)
        pv = jnp.dot(p_sc[slot], vbe, preferred_element_type=F32)
        alpha = al_sc[slot]
        acc_sc[...] = jnp.concatenate([alpha, alpha], axis=1) * acc_sc[...] + pv

    stage_qk(0, 0)
    stage_qk(1, 1)
    stage_softmax(0, 0, True)

    npairs = jnp.maximum((i - 2) // 2, 0)

    def far_pair(u, carry):
        t = 2 * u
        stage_pv(t, 0)
        stage_softmax(t + 1, 1, False)
        stage_qk(t + 2, 0)
        stage_pv(t + 1, 1)
        stage_softmax(t + 2, 0, False)
        stage_qk(t + 3, 1)
        return carry

    lax.fori_loop(0, npairs, far_pair, 0)

    t0 = 2 * npairs
    stage_pv(t0, 0)
    stage_softmax(t0 + 1, 1, True)
    stage_qk(t0 + 2, 0)
    stage_pv(t0 + 1, 1)
    stage_softmax(t0 + 2, 0, True)
    stage_qk(t0 + 3, 1)
    stage_pv(t0 + 2, 0)
    stage_softmax(t0 + 3, 1, True)
    stage_pv(t0 + 3, 1)

    if diff:
        o0 = acc_sc[0:tq, 0:hd] / acc_sc[0:tq, hd:2 * hd]
        o1 = acc_sc[tq:2 * tq, 0:hd] / acc_sc[tq:2 * tq, hd:2 * hd]
        dlt = o0 - lam_ref[0] * o1
        ms = jnp.mean(dlt * dlt, axis=-1, keepdims=True)
        o_ref[...] = ((dlt * lax.rsqrt(ms + EPS) * g_ref[...]) * lam_scale).astype(o_ref.dtype)
    else:
        for r in range(nh):
            rs = slice(r * tq, (r + 1) * tq)
            o_ref[:, r * hd:(r + 1) * hd] = (acc_sc[rs, 0:hd] / acc_sc[rs, hd:2 * hd]).astype(o_ref.dtype)


def _rel_bucket(dist):
    n = jnp.maximum(dist, 0)
    max_exact = REL_BUCKETS // 2
    nf = jnp.maximum(n, 1).astype(F32)
    large = max_exact + (jnp.log(nf / max_exact) / math.log(REL_MAX_DIST / max_exact)
                         * (REL_BUCKETS - max_exact)).astype(I32)
    large = jnp.minimum(large, REL_BUCKETS - 1)
    return jnp.where(n < max_exact, n, large)


def _near_bias(rel_table, tq, groups, nh):
    assert tq >= LANES, "keys older than one block must all fall in the last bucket"
    r = jnp.arange(tq, dtype=I32)[:, None]
    c = jnp.arange(2 * tq, dtype=I32)[None, :]
    dist = r + tq - c
    rel = (rel_table - rel_table[REL_BUCKETS - 1][None, :]) * np.float32(LOG2E)
    onehot = jax.nn.one_hot(_rel_bucket(dist), REL_BUCKETS, dtype=F32)
    b = jnp.einsum("rcb,bh->hrc", onehot, rel, precision=lax.Precision.HIGHEST)
    b = jnp.where((dist >= 0)[None], b, MASKED)
    heads = b.shape[0]
    tiles = jnp.stack([jnp.zeros((heads, tq, tq), F32), b[:, :, :tq], b[:, :, tq:],
                       jnp.full((heads, tq, tq), MASKED, F32)], axis=1)
    tiles = tiles.reshape(groups, nh, 4, tq, tq).transpose(0, 2, 1, 3, 4)
    return tiles.reshape(groups, 4, nh * tq, tq)


def _diff_attention(qkv, near, lam, subln_g, lam_scale, tq):
    s = qkv.shape[0]
    hd = A_V_DIM
    nh = 2
    kcol = A_WIDTH // hd
    return pl.pallas_call(
        functools.partial(_flash_kernel, nh=nh, tq=tq, sub=min(tq, ATTN_SUB_ROWS), diff=True,
                          lam_scale=lam_scale),
        grid=(A_HEADS, s // tq),
        in_specs=[pl.BlockSpec(memory_space=pltpu.SMEM),
                  pl.BlockSpec((tq, hd), lambda h, i: (i, h)),
                  pl.BlockSpec((s, hd), lambda h, i: (0, kcol + h)),
                  pl.BlockSpec((s, hd), lambda h, i: (0, 2 * kcol + h)),
                  pl.BlockSpec((1, 4, nh * tq, tq), lambda h, i: (h, 0, 0, 0)),
                  pl.BlockSpec((1, hd), lambda h, i: (0, 0))],
        out_specs=pl.BlockSpec((tq, hd), lambda h, i: (i, h)),
        out_shape=jax.ShapeDtypeStruct((s, A_WIDTH), BF16),
        scratch_shapes=[pltpu.VMEM((nh * tq, hd), BF16),
                        pltpu.VMEM((nh * tq, LANES), F32),
                        pltpu.VMEM((nh * tq, 2 * hd), F32),
                        pltpu.VMEM((2, nh * tq, tq), F32),
                        pltpu.VMEM((2, nh * tq, tq), BF16),
                        pltpu.VMEM((2, nh * tq, LANES), F32)],
        compiler_params=_cparams(("parallel", "arbitrary")),
        name="diff_attention",
    )(lam, qkv, qkv, qkv, near, subln_g)


def _masked_attention(qkv, near, mask, tq):
    s = qkv.shape[0]
    hd = C_HEAD_DIM
    nh = C_GROUP
    kcol = C_WIDTH // hd
    vcol = kcol + C_KV_HEADS
    nk = s // tq
    return pl.pallas_call(
        functools.partial(_flash_kernel, nh=nh, tq=tq, sub=min(tq, ATTN_SUB_ROWS), diff=False,
                          lam_scale=1.0),
        grid=(C_KV_HEADS, s // tq),
        in_specs=[pl.BlockSpec((tq, nh * hd), lambda g, i: (i, g)),
                  pl.BlockSpec((s, hd), lambda g, i: (0, kcol + g)),
                  pl.BlockSpec((s, hd), lambda g, i: (0, vcol + g)),
                  pl.BlockSpec((1, 4, nh * tq, tq), lambda g, i: (g, 0, 0, 0)),
                  pl.BlockSpec((nk, tq, tq), lambda g, i: (0, i, 0))],
        out_specs=pl.BlockSpec((tq, nh * hd), lambda g, i: (i, g)),
        out_shape=jax.ShapeDtypeStruct((s, C_WIDTH), BF16),
        scratch_shapes=[pltpu.VMEM((nh * tq, hd), BF16),
                        pltpu.VMEM((nh * tq, LANES), F32),
                        pltpu.VMEM((nh * tq, 2 * hd), F32),
                        pltpu.VMEM((2, nh * tq, tq), F32),
                        pltpu.VMEM((2, nh * tq, tq), BF16),
                        pltpu.VMEM((2, nh * tq, LANES), F32)],
        compiler_params=_cparams(("parallel", "arbitrary")),
        name="selected_attention",
    )(qkv, qkv, qkv, near, mask)


def _sg_kernel(zb_ref, lng_ref, lnb_ref, w_ref, bs_ref, o_ref, *, nchunk):
    zb = zb_ref[...]
    gl = zb * (0.5 * (1.0 + jnp.tanh(np.float32(np.sqrt(2.0 / np.pi)) * (zb + 0.044715 * (zb * zb * zb)))))
    u = gl[:, :B_WIDTH]
    z = gl[:, B_WIDTH:]
    mu = jnp.mean(z, axis=-1, keepdims=True)
    zc = z - mu
    var = jnp.mean(zc * zc, axis=-1, keepdims=True)
    zn = (zc * lax.rsqrt(var + EPS) * lng_ref[...] + lnb_ref[...]).astype(BF16)
    row = lax.broadcasted_iota(I32, (CHUNK, CHUNK), 0)
    col = lax.broadcasted_iota(I32, (CHUNK, CHUNK), 1)
    for g in range(B_GROUPS):
        w = jnp.where(row >= col, w_ref[g], 0.0).astype(BF16)
        bias = bs_ref[g]
        lo = g * B_GROUP_DIM
        for c in range(nchunk):
            r0 = c * CHUNK
            sz = jnp.dot(w, zn[r0:r0 + CHUNK, lo:lo + B_GROUP_DIM], preferred_element_type=F32) + bias
            o_ref[r0:r0 + CHUNK, lo:lo + B_GROUP_DIM] = (u[r0:r0 + CHUNK, lo:lo + B_GROUP_DIM] * sz).astype(o_ref.dtype)


def _spatial_gating(zb, ln_g, ln_b, w_s, b_s):
    s = zb.shape[0]
    t = _tile(s, 256)
    return pl.pallas_call(
        functools.partial(_sg_kernel, nchunk=t // CHUNK),
        grid=(s // t,),
        in_specs=[pl.BlockSpec((t, 2 * B_WIDTH), lambda i: (i, 0)),
                  pl.BlockSpec((1, B_WIDTH), lambda i: (0, 0)),
                  pl.BlockSpec((1, B_WIDTH), lambda i: (0, 0)),
                  pl.BlockSpec((B_GROUPS, CHUNK, CHUNK), lambda i: (0, 0, 0)),
                  pl.BlockSpec((B_GROUPS, CHUNK, 1), lambda i: (0, 0, 0))],
        out_specs=pl.BlockSpec((t, B_WIDTH), lambda i: (i, 0)),
        out_shape=jax.ShapeDtypeStruct((s, B_WIDTH), BF16),
        compiler_params=_cparams(("parallel",)),
        name="spatial_gating",
    )(zb, ln_g.reshape(1, B_WIDTH), ln_b.reshape(1, B_WIDTH), w_s, b_s.reshape(B_GROUPS, CHUNK, 1))


def _select_kernel(qi_ref, kt_ref, w_ref, o_ref, keys_sc, sc_sc, wb_sc, mx_sc, *, tkc, topk):
    tqi = IDX_QBLOCK
    i = pl.program_id(0)
    nk = o_ref.shape[0]
    nch = (i * tqi + tqi + tkc - 1) // tkc
    qpos = i * tqi + lax.broadcasted_iota(I32, (tqi, tkc), 0)
    kloc = lax.broadcasted_iota(I32, (tqi, tkc), 1)
    qrow = i * tqi + lax.broadcasted_iota(I32, (tqi, LANES), 0)
    klane = lax.broadcasted_iota(I32, (tqi, LANES), 1)

    wgt = w_ref[...] * np.float32(IDX_DIM ** -0.5)
    for h in range(IDX_HEADS):
        wb_sc[h] = jnp.broadcast_to(wgt[:, h:h + 1], (tqi, LANES))
    mx_sc[...] = jnp.full(mx_sc.shape, -jnp.inf, F32)

    def stage_dot(c, slot):
        sc_sc[slot] = jnp.dot(qi_ref[0], kt_ref[jnp.minimum(c, nch - 1)], preferred_element_type=F32)

    def stage_reduce(c, slot):
        c = jnp.minimum(c, nch - 1)
        for u in range(tkc // LANES):
            ls = slice(u * LANES, (u + 1) * LANES)
            acc = jnp.zeros((tqi, LANES), F32)
            for h in range(IDX_HEADS):
                acc += jnp.maximum(sc_sc[slot, h * tqi:(h + 1) * tqi, ls], 0.0) * wb_sc[h]
            acc = acc + 0.0
            bits = pltpu.bitcast(acc, I32)
            ordered = jnp.where(bits < 0, bits ^ jnp.int32(0x7FFFFFFF), bits)
            causal = c * tkc + u * LANES + klane <= qrow
            keys_sc[c, :, ls] = jnp.where(causal, ordered, jnp.int32(INT_MIN))
            mx_sc[...] = jnp.maximum(mx_sc[...], jnp.where(causal, acc, -jnp.inf))

    stage_dot(0, 0)

    def score_pair(u, carry):
        c = 2 * u
        stage_dot(c + 1, 1)
        stage_reduce(c, 0)
        stage_dot(c + 2, 0)
        stage_reduce(c + 1, 1)
        return carry

    lax.fori_loop(0, (nch + 1) // 2, score_pair, 0)

    def count_ge(cand):
        candb = jnp.broadcast_to(cand, (tqi, LANES))

        def count_body(c, cnt):
            kk = keys_sc[c]
            for u in range(tkc // LANES):
                cnt += jnp.where(kk[:, u * LANES:(u + 1) * LANES] >= candb, 1, 0)
            return cnt

        cnt = lax.fori_loop(0, nch, count_body, jnp.zeros((tqi, LANES), I32))
        return jnp.sum(cnt.astype(F32), axis=1, keepdims=True)

    want = np.float32(topk)
    fbits = pltpu.bitcast(jnp.max(mx_sc[...], axis=1, keepdims=True), I32)
    kmax = jnp.where(fbits < 0, fbits ^ jnp.int32(0x7FFFFFFF), fbits)
    few = i * tqi + lax.broadcasted_iota(I32, (tqi, 1), 0) + 1 <= topk
    lo0 = jnp.full((tqi, 1), INT_MIN, I32)
    hi0 = jnp.where(few, lo0 + 1, kmax + 1)

    def narrow(state, cand):
        lo, hi, active = state
        total = count_ge(cand)
        open_ = active > 0.0
        up = open_ & (total >= want)
        lo = jnp.where(up, cand, lo)
        hi = jnp.where(open_ & (~up), cand, hi)
        open_ = open_ & (~(up & (total == want))) & ((hi - lo) != 1)
        return lo, hi, jnp.where(open_, 1.0, 0.0)

    def midpoint(state):
        lo, hi, _ = state
        return lo + lax.shift_right_logical(hi - lo, jnp.int32(1))

    probe = jnp.maximum(kmax, jnp.int32(INT_MIN + SEARCH_PROBE_DROP + 1)) - jnp.int32(SEARCH_PROBE_DROP)
    state = narrow((lo0, hi0, jnp.where(few, 0.0, 1.0)), jnp.where(few, lo0, probe))

    def bisect_body(carry):
        state, _ = carry
        state = narrow(state, midpoint(state))
        state = narrow(state, midpoint(state))
        return state, jnp.sum(state[2])

    (thr, _, _), _ = lax.while_loop(lambda carry: carry[1] > 0.0, bisect_body, (state, jnp.float32(1.0)))
    thrb = jnp.broadcast_to(thr, (tqi, tkc))

    def mask_body(c, carry):
        sel = (keys_sc[c] >= thrb) & (c * tkc + kloc <= qpos)
        o_ref[c] = jnp.where(sel, 0.0, MASKED).astype(o_ref.dtype)
        return carry

    lax.fori_loop(0, nch, mask_body, 0)

    def fill_body(c, carry):
        o_ref[c] = jnp.full((tqi, tkc), MASKED, o_ref.dtype)
        return carry

    lax.fori_loop(nch, nk, fill_body, 0)


def _select_mask(qi_stack, kt, wi, tkc, topk):
    nq, rows, _ = qi_stack.shape
    nk = kt.shape[0]
    s = nq * IDX_QBLOCK
    return pl.pallas_call(
        functools.partial(_select_kernel, tkc=tkc, topk=topk),
        grid=(nq,),
        in_specs=[pl.BlockSpec((1, rows, IDX_DIM), lambda i: (i, 0, 0)),
                  pl.BlockSpec((nk, IDX_DIM, tkc), lambda i: (0, 0, 0)),
                  pl.BlockSpec((IDX_QBLOCK, IDX_HEADS), lambda i: (i, 0))],
        out_specs=pl.BlockSpec((nk, IDX_QBLOCK, tkc), lambda i: (0, i, 0)),
        out_shape=jax.ShapeDtypeStruct((nk, s, tkc), BF16),
        scratch_shapes=[pltpu.VMEM((nk, IDX_QBLOCK, tkc), I32),
                        pltpu.VMEM((2, rows, tkc), F32),
                        pltpu.VMEM((IDX_HEADS, IDX_QBLOCK, LANES), F32),
                        pltpu.VMEM((IDX_QBLOCK, LANES), F32)],
        compiler_params=_cparams(("parallel",)),
        name="indexer_select",
    )(qi_stack, kt, wi)


def _vec_pack(d, *rows):
    rows = [r.reshape(1, d).astype(F32) for r in rows]
    rows += [jnp.zeros((1, d), F32)] * (8 - len(rows))
    return jnp.concatenate(rows, axis=0)


def _pad_cols(w, n):
    return jnp.pad(w, ((0, 0), (0, n - w.shape[1])))


def kernel(x, c, norm_g, mod_w, mod_b, ffn_w1, ffn_w2, rel_table, ab_w_in, ab_w_out, diff_lam,
           diff_subln_g, sg_ln_g, sg_ln_b, sg_w, sg_b, dsa_w_in, dsa_w_out, final_g):
    batch, s, d = x.shape
    depth = norm_g.shape[0]
    d_ff = ffn_w2.shape[2]
    assert batch == 1 and s % IDX_QBLOCK == 0
    ff_pad = -(-d_ff // FFN_COLS) * FFN_COLS if d_ff > FFN_COLS else d_ff

    tq_a = _tile(s, 512)
    tq_c = _tile(s, 256)
    topk = min(TOPK_MAX, s // 4)

    mod = _modulation(c, mod_w, mod_b).reshape(depth, 9, d)
    near_a = _near_bias(rel_table, tq_a, A_HEADS, 2)
    near_c = _near_bias(rel_table, tq_c, C_KV_HEADS, C_GROUP)

    xs = x.reshape(s, d)
    zeros_d = jnp.zeros((d,), F32)
    for li in range(depth):
        def vec(j, li=li):
            last = final_g if (li == depth - 1 and j == 2) else zeros_d
            return _vec_pack(d, norm_g[li, j], mod[li, 3 * j], mod[li, 3 * j + 1], mod[li, 3 * j + 2], last)

        def ffn(xs, j, k, final=False, li=li):
            w1 = ffn_w1[li, k]
            w1g = _col_blocks(_pad_cols(w1[:, :d_ff], ff_pad).astype(BF16), FFN_COLS)
            w1u = _col_blocks(_pad_cols(w1[:, d_ff:], ff_pad).astype(BF16), FFN_COLS)
            w2 = jnp.pad(ffn_w2[li, k], ((0, ff_pad - d_ff), (0, 0))).astype(BF16)
            return _ffn(xs, vec(j), w1g, w1u, w2, final=final)

        xs = ffn(xs, 0, 0)

        v1 = vec(1)
        jj = li // 2
        if li % 2 == 0:
            w_in = ab_w_in[jj]
            w_qkv = jnp.concatenate([w_in[:, :A_WIDTH] * np.float32(A_QK_DIM ** -0.5 * LOG2E),
                                     w_in[:, A_WIDTH:3 * A_WIDTH]], axis=1).astype(BF16)
            w_zb = w_in[:, 3 * A_WIDTH:].astype(BF16)
            qkv = _proj(xs, v1, _col_blocks(w_qkv, PROJ_COLS), BF16)
            zb = _proj(xs, v1, _col_blocks(w_zb, PROJ_COLS), F32)
            lam_init = 0.8 - 0.6 * math.exp(-0.3 * li)
            lp = diff_lam[jj].astype(F32)
            lam = jnp.exp(jnp.sum(lp[0] * lp[1])) - jnp.exp(jnp.sum(lp[2] * lp[3])) + lam_init
            ya = _diff_attention(qkv, near_a, lam.reshape(1), diff_subln_g[jj].reshape(1, A_V_DIM),
                                 1.0 - lam_init, tq_a)
            yb = _spatial_gating(zb, sg_ln_g[jj], sg_ln_b[jj], sg_w[jj], sg_b[jj])
            w_out = ab_w_out[jj].astype(BF16)
            xs = _outproj(xs, v1, [ya, yb], [_col_blocks(w_out[:A_WIDTH], PROJ_COLS),
                                             _col_blocks(w_out[A_WIDTH:], PROJ_COLS)])
        else:
            w_in = dsa_w_in[jj]
            o_idx = C_WIDTH + 2 * C_KV_WIDTH
            o_ki = o_idx + IDX_HEADS * IDX_DIM
            w_main = jnp.concatenate([w_in[:, :C_WIDTH] * np.float32(C_HEAD_DIM ** -0.5 * LOG2E),
                                      w_in[:, C_WIDTH:o_ki]], axis=1).astype(BF16)
            w_kiw = _pad_cols(w_in[:, o_ki:], LANES).astype(BF16)
            main = _proj(xs, v1, _col_blocks(w_main, PROJ_COLS), BF16)
            kiw = _proj(xs, v1, _col_blocks(w_kiw, PROJ_COLS), F32)
            nq = s // IDX_QBLOCK
            qi = main[:, o_idx:o_ki].reshape(nq, IDX_QBLOCK, IDX_HEADS, IDX_DIM)
            qi = qi.transpose(0, 2, 1, 3).reshape(nq, IDX_HEADS * IDX_QBLOCK, IDX_DIM)
            kt = kiw[:, :IDX_DIM].astype(BF16).reshape(s // tq_c, tq_c, IDX_DIM).transpose(0, 2, 1)
            wi = kiw[:, IDX_DIM:IDX_DIM + IDX_HEADS]
            mask = _select_mask(qi, kt, wi, tq_c, topk)
            yc = _masked_attention(main, near_c, mask, tq_c)
            xs = _outproj(xs, v1, [yc], [_col_blocks(dsa_w_out[jj].astype(BF16), PROJ_COLS)])

        xs = ffn(xs, 2, 1, final=(li == depth - 1))
    return xs.reshape(batch, s, d)
```

```python
import functools
import math

import jax
import jax.numpy as jnp
import numpy as np
from jax import lax
from jax.experimental import pallas as pl
from jax.experimental.pallas import tpu as pltpu

F32 = jnp.float32
BF16 = jnp.bfloat16
I32 = jnp.int32

EPS = 1e-6
MASKED = -1e30
LANES = 128
INT_MIN = -(2 ** 31)
LOG2E = math.log2(math.e)
ATTN_SUB_ROWS = 128

A_HEADS = 8
A_QK_DIM = 64
A_V_DIM = 128
A_WIDTH = A_HEADS * A_V_DIM
B_GROUPS = 8
B_GROUP_DIM = 128
B_WIDTH = B_GROUPS * B_GROUP_DIM
CHUNK = 128
C_HEADS = 16
C_KV_HEADS = 4
C_GROUP = C_HEADS // C_KV_HEADS
C_HEAD_DIM = 128
C_WIDTH = C_HEADS * C_HEAD_DIM
C_KV_WIDTH = C_KV_HEADS * C_HEAD_DIM
IDX_HEADS = 16
IDX_DIM = 64
TOPK_MAX = 256
REL_BUCKETS = 32
REL_MAX_DIST = 128
IDX_QBLOCK = 128
SEARCH_PROBE_DROP = 2 ** 24

VMEM_LIMIT = 56 * 1024 * 1024


def _cparams(sem):
    return pltpu.CompilerParams(dimension_semantics=sem, vmem_limit_bytes=VMEM_LIMIT)


FFN_ROWS, FFN_COLS = 512, 512
PROJ_ROWS, PROJ_COLS = 1024, 512


def _col_blocks(w, tn):
    k, n = w.shape
    tn = _tile(n, tn)
    return w.reshape(k, n // tn, tn).transpose(1, 0, 2)


def _tile(n, want):
    if n <= want:
        return n
    t = want
    while n % t:
        t //= 2
    return t


def _mod_kernel(c_ref, w_ref, b_ref, o_ref):
    c = c_ref[...]
    cs = c * (1.0 / (1.0 + jnp.exp(-c)))
    o_ref[0] = jnp.sum(cs * w_ref[0], axis=0, keepdims=True) + b_ref[0]


def _modulation(c, mod_w, mod_b):
    depth, d, n = mod_w.shape
    tn = _tile(n, 1024)
    out = pl.pallas_call(
        _mod_kernel,
        grid=(depth, n // tn),
        in_specs=[pl.BlockSpec((d, 1), lambda l, j: (0, 0)),
                  pl.BlockSpec((1, d, tn), lambda l, j: (l, 0, j)),
                  pl.BlockSpec((1, 1, tn), lambda l, j: (l, 0, j))],
        out_specs=pl.BlockSpec((1, 1, tn), lambda l, j: (l, 0, j)),
        out_shape=jax.ShapeDtypeStruct((depth, 1, n), F32),
        compiler_params=_cparams(("arbitrary", "arbitrary")),
        name="adaln_mod",
    )(c.reshape(d, 1), mod_w, mod_b.reshape(depth, 1, n))
    return out.reshape(depth, n)


def _prenorm(x, vec_ref):
    ms = jnp.mean(x * x, axis=-1, keepdims=True)
    y = x * lax.rsqrt(ms + EPS) * vec_ref[0:1, :]
    return y * (1.0 + vec_ref[2:3, :]) + vec_ref[1:2, :]


def _ffn_kernel(x_ref, vec_ref, w1g_ref, w1u_ref, w2_ref, o_ref, hn_sc, *, nf, final):
    f = pl.program_id(1)

    @pl.when(f == 0)
    def _():
        hn_sc[...] = _prenorm(x_ref[...], vec_ref).astype(BF16)
        o_ref[...] = jnp.zeros_like(o_ref)

    hn = hn_sc[...]
    g = jnp.dot(hn, w1g_ref[0], preferred_element_type=F32)
    u = jnp.dot(hn, w1u_ref[0], preferred_element_type=F32)
    a = (g * (1.0 / (1.0 + jnp.exp(-g))) * u).astype(BF16)
    o_ref[...] += jnp.dot(a, w2_ref[...], preferred_element_type=F32)

    @pl.when(f == nf - 1)
    def _():
        y = x_ref[...] + 0.5 * (1.0 + vec_ref[3:4, :]) * o_ref[...]
        if final:
            ms = jnp.mean(y * y, axis=-1, keepdims=True)
            y = y * lax.rsqrt(ms + EPS) * vec_ref[4:5, :]
        o_ref[...] = y


def _ffn(x, vec, w1g, w1u, w2, *, final):
    s, d = x.shape
    nf, _, tf = w1g.shape
    tm = _tile(s, FFN_ROWS)
    return pl.pallas_call(
        functools.partial(_ffn_kernel, nf=nf, final=final),
        grid=(s // tm, nf),
        in_specs=[pl.BlockSpec((tm, d), lambda i, f: (i, 0)),
                  pl.BlockSpec((8, d), lambda i, f: (0, 0)),
                  pl.BlockSpec((1, d, tf), lambda i, f: (f, 0, 0)),
                  pl.BlockSpec((1, d, tf), lambda i, f: (f, 0, 0)),
                  pl.BlockSpec((tf, d), lambda i, f: (f, 0))],
        out_specs=pl.BlockSpec((tm, d), lambda i, f: (i, 0)),
        out_shape=jax.ShapeDtypeStruct((s, d), F32),
        scratch_shapes=[pltpu.VMEM((tm, d), BF16)],
        compiler_params=_cparams(("parallel", "arbitrary")),
        name="swiglu_halfstep",
    )(x, vec, w1g, w1u, w2)


def _proj_kernel(x_ref, vec_ref, w_ref, o_ref, hn_sc):
    @pl.when(pl.program_id(1) == 0)
    def _():
        hn_sc[...] = _prenorm(x_ref[...], vec_ref).astype(BF16)

    o_ref[...] = jnp.dot(hn_sc[...], w_ref[0], preferred_element_type=F32).astype(o_ref.dtype)


def _proj(x, vec, w, out_dtype):
    s, d = x.shape
    nn, _, tn = w.shape
    n = nn * tn
    tm = _tile(s, PROJ_ROWS)
    return pl.pallas_call(
        _proj_kernel,
        grid=(s // tm, nn),
        in_specs=[pl.BlockSpec((tm, d), lambda i, j: (i, 0)),
                  pl.BlockSpec((8, d), lambda i, j: (0, 0)),
                  pl.BlockSpec((1, d, tn), lambda i, j: (j, 0, 0))],
        out_specs=pl.BlockSpec((tm, tn), lambda i, j: (i, j)),
        out_shape=jax.ShapeDtypeStruct((s, n), out_dtype),
        scratch_shapes=[pltpu.VMEM((tm, d), BF16)],
        compiler_params=_cparams(("parallel", "arbitrary")),
        name="norm_mod_proj",
    )(x, vec, w)


def _outproj_kernel(*refs, n_in):
    x_ref, vec_ref = refs[0], refs[1]
    lhs = refs[2:2 + n_in]
    ws = refs[2 + n_in:2 + 2 * n_in]
    o_ref = refs[2 + 2 * n_in]
    acc = jnp.dot(lhs[0][...], ws[0][0], preferred_element_type=F32)
    for a, w in zip(lhs[1:], ws[1:]):
        acc += jnp.dot(a[...], w[0], preferred_element_type=F32)
    o_ref[...] = x_ref[...] + (1.0 + vec_ref[3:4, :]) * acc


def _outproj(x, vec, lhs, ws):
    s, d = x.shape
    tm = _tile(s, PROJ_ROWS)
    tn = ws[0].shape[2]
    n_in = len(lhs)
    in_specs = [pl.BlockSpec((tm, tn), lambda i, j: (i, j)),
                pl.BlockSpec((8, tn), lambda i, j: (0, j))]
    in_specs += [pl.BlockSpec((tm, a.shape[1]), lambda i, j: (i, 0)) for a in lhs]
    in_specs += [pl.BlockSpec((1, w.shape[1], tn), lambda i, j: (j, 0, 0)) for w in ws]
    return pl.pallas_call(
        functools.partial(_outproj_kernel, n_in=n_in),
        grid=(s // tm, d // tn),
        in_specs=in_specs,
        out_specs=pl.BlockSpec((tm, tn), lambda i, j: (i, j)),
        out_shape=jax.ShapeDtypeStruct((s, d), F32),
        compiler_params=_cparams(("parallel", "arbitrary")),
        name="outproj_residual",
    )(x, vec, *lhs, *ws)


def _flash_kernel(*refs, nh, tq, kb, sub, diff, lam_scale):
    if diff:
        lam_ref, q_ref, k_ref, v_ref, nb_ref, g_ref, o_ref, qs_sc, m_sc, acc_sc, s_sc, p_sc, al_sc = refs
        mask_ref = None
    else:
        q_ref, k_ref, v_ref, nb_ref, mask_ref, o_ref, qs_sc, m_sc, acc_sc, s_sc, p_sc, al_sc = refs
    tk = tq
    rows = nh * tq
    hd = k_ref.shape[1]
    assert hd == LANES and rows % sub == 0 and tq % sub == 0
    i = pl.program_id(1)

    if diff:
        q = q_ref[...]
        lane = lax.broadcasted_iota(I32, q.shape, 1)
        zero = jnp.zeros_like(q)
        qs_sc[0:tq, :] = jnp.where(lane < A_QK_DIM, q, zero)
        qs_sc[tq:2 * tq, :] = jnp.where(lane >= A_QK_DIM, q, zero)
    else:
        for r in range(nh):
            qs_sc[r * tq:(r + 1) * tq, :] = q_ref[:, r * hd:(r + 1) * hd]
    m_sc[...] = jnp.full(m_sc.shape, -jnp.inf, F32)
    acc_sc[...] = jnp.zeros(acc_sc.shape, F32)

    tks = kb * tk
    nk = k_ref.shape[0] // tk
    last_step = i // kb

    def bias_index(j):
        return jnp.where(j > i, 3, jnp.clip(j - (i - 2), 0, 2))

    def stage_qk(t, slot):
        start = pl.multiple_of(jnp.minimum(t, last_step) * tks, tks)
        kblk = k_ref[pl.ds(start, tks), :]
        s_sc[slot] = lax.dot_general(qs_sc[...], kblk, (((1,), (1,)), ((), ())), preferred_element_type=F32)

    def stage_softmax(t, slot, biased):
        tc = jnp.minimum(t, last_step)
        for r in range(rows // sub):
            rs = slice(r * sub, (r + 1) * sub)
            tiles = []
            for b in range(kb):
                j = tc * kb + b
                s = s_sc[slot, rs, b * tk:(b + 1) * tk]
                if biased:
                    col = jnp.where(t > last_step, 3, bias_index(j))
                    s = s + nb_ref[0, col, rs, :]
                if mask_ref is not None:
                    off = (r * sub) % tq
                    s = s + mask_ref[jnp.minimum(j, nk - 1), off:off + sub, :].astype(F32)
                tiles += [s[:, u * LANES:(u + 1) * LANES] for u in range(tk // LANES)]
            cmax = tiles[0]
            for u in tiles[1:]:
                cmax = jnp.maximum(cmax, u)
            m_old = m_sc[rs, :]
            m_new = jnp.maximum(m_old, jnp.max(cmax, axis=1, keepdims=True))
            al_sc[slot, rs, :] = jnp.exp2(m_old - m_new)
            p_sc[slot, rs, :] = jnp.concatenate([jnp.exp2(u - m_new) for u in tiles], axis=1).astype(BF16)
            m_sc[rs, :] = m_new

    def stage_pv(t, slot):
        start = pl.multiple_of(jnp.minimum(t, last_step) * tks, tks)
        vbe = jnp.concatenate([v_ref[pl.ds(start, tks), :], jnp.ones((tks, hd), BF16)], axis=1)
        pv = jnp.dot(p_sc[slot], vbe, preferred_element_type=F32)
        alpha = al_sc[slot]
        acc_sc[...] = jnp.concatenate([alpha, alpha], axis=1) * acc_sc[...] + pv

    stage_qk(0, 0)
    stage_qk(1, 1)
    stage_softmax(0, 0, True)

    nfar = jnp.maximum((i - 1) // kb, 0)
    npairs = jnp.maximum((nfar - 1) // 2, 0)

    def far_pair(u, carry):
        t = 2 * u
        stage_pv(t, 0)
        stage_softmax(t + 1, 1, False)
        stage_qk(t + 2, 0)
        stage_pv(t + 1, 1)
        stage_softmax(t + 2, 0, False)
        stage_qk(t + 3, 1)
        return carry

    lax.fori_loop(0, npairs, far_pair, 0)

    t0 = 2 * npairs
    stage_pv(t0, 0)
    stage_softmax(t0 + 1, 1, True)
    stage_qk(t0 + 2, 0)
    stage_pv(t0 + 1, 1)
    stage_softmax(t0 + 2, 0, True)
    stage_qk(t0 + 3, 1)
    stage_pv(t0 + 2, 0)
    stage_softmax(t0 + 3, 1, True)
    stage_pv(t0 + 3, 1)

    if diff:
        o0 = acc_sc[0:tq, 0:hd] / acc_sc[0:tq, hd:2 * hd]
        o1 = acc_sc[tq:2 * tq, 0:hd] / acc_sc[tq:2 * tq, hd:2 * hd]
        dlt = o0 - lam_ref[0] * o1
        ms = jnp.mean(dlt * dlt, axis=-1, keepdims=True)
        o_ref[...] = ((dlt * lax.rsqrt(ms + EPS) * g_ref[...]) * lam_scale).astype(o_ref.dtype)
    else:
        for r in range(nh):
            rs = slice(r * tq, (r + 1) * tq)
            o_ref[:, r * hd:(r + 1) * hd] = (acc_sc[rs, 0:hd] / acc_sc[rs, hd:2 * hd]).astype(o_ref.dtype)


def _rel_bucket(dist):
    n = jnp.maximum(dist, 0)
    max_exact = REL_BUCKETS // 2
    nf = jnp.maximum(n, 1).astype(F32)
    large = max_exact + (jnp.log(nf / max_exact) / math.log(REL_MAX_DIST / max_exact)
                         * (REL_BUCKETS - max_exact)).astype(I32)
    large = jnp.minimum(large, REL_BUCKETS - 1)
    return jnp.where(n < max_exact, n, large)


def _near_bias(rel_table, tq, groups, nh):
    assert tq >= LANES, "keys older than one block must all fall in the last bucket"
    r = jnp.arange(tq, dtype=I32)[:, None]
    c = jnp.arange(2 * tq, dtype=I32)[None, :]
    dist = r + tq - c
    rel = (rel_table - rel_table[REL_BUCKETS - 1][None, :]) * np.float32(LOG2E)
    onehot = jax.nn.one_hot(_rel_bucket(dist), REL_BUCKETS, dtype=F32)
    b = jnp.einsum("rcb,bh->hrc", onehot, rel, precision=lax.Precision.HIGHEST)
    b = jnp.where((dist >= 0)[None], b, MASKED)
    heads = b.shape[0]
    tiles = jnp.stack([jnp.zeros((heads, tq, tq), F32), b[:, :, :tq], b[:, :, tq:],
                       jnp.full((heads, tq, tq), MASKED, F32)], axis=1)
    tiles = tiles.reshape(groups, nh, 4, tq, tq).transpose(0, 2, 1, 3, 4)
    return tiles.reshape(groups, 4, nh * tq, tq)


def _diff_attention(qkv, near, lam, subln_g, lam_scale, tq):
    s = qkv.shape[0]
    hd = A_V_DIM
    nh = 2
    kb = 1
    kcol = A_WIDTH // hd
    return pl.pallas_call(
        functools.partial(_flash_kernel, nh=nh, tq=tq, kb=kb, sub=min(tq, ATTN_SUB_ROWS), diff=True,
                          lam_scale=lam_scale),
        grid=(A_HEADS, s // tq),
        in_specs=[pl.BlockSpec(memory_space=pltpu.SMEM),
                  pl.BlockSpec((tq, hd), lambda h, i: (i, h)),
                  pl.BlockSpec((s, hd), lambda h, i: (0, kcol + h)),
                  pl.BlockSpec((s, hd), lambda h, i: (0, 2 * kcol + h)),
                  pl.BlockSpec((1, 4, nh * tq, tq), lambda h, i: (h, 0, 0, 0)),
                  pl.BlockSpec((1, hd), lambda h, i: (0, 0))],
        out_specs=pl.BlockSpec((tq, hd), lambda h, i: (i, h)),
        out_shape=jax.ShapeDtypeStruct((s, A_WIDTH), BF16),
        scratch_shapes=[pltpu.VMEM((nh * tq, hd), BF16),
                        pltpu.VMEM((nh * tq, LANES), F32),
                        pltpu.VMEM((nh * tq, 2 * hd), F32),
                        pltpu.VMEM((2, nh * tq, kb * tq), F32),
                        pltpu.VMEM((2, nh * tq, kb * tq), BF16),
                        pltpu.VMEM((2, nh * tq, LANES), F32)],
        compiler_params=_cparams(("parallel", "arbitrary")),
        name="diff_attention",
    )(lam, qkv, qkv, qkv, near, subln_g)


def _masked_attention(qkv, near, mask, tq):
    s = qkv.shape[0]
    hd = C_HEAD_DIM
    nh = C_GROUP
    kcol = C_WIDTH // hd
    vcol = kcol + C_KV_HEADS
    nk = s // tq
    kb = 2 if nk % 2 == 0 else 1
    return pl.pallas_call(
        functools.partial(_flash_kernel, nh=nh, tq=tq, kb=kb, sub=min(tq, ATTN_SUB_ROWS), diff=False,
                          lam_scale=1.0),
        grid=(C_KV_HEADS, s // tq),
        in_specs=[pl.BlockSpec((tq, nh * hd), lambda g, i: (i, g)),
                  pl.BlockSpec((s, hd), lambda g, i: (0, kcol + g)),
                  pl.BlockSpec((s, hd), lambda g, i: (0, vcol + g)),
                  pl.BlockSpec((1, 4, nh * tq, tq), lambda g, i: (g, 0, 0, 0)),
                  pl.BlockSpec((nk, tq, tq), lambda g, i: (0, i, 0))],
        out_specs=pl.BlockSpec((tq, nh * hd), lambda g, i: (i, g)),
        out_shape=jax.ShapeDtypeStruct((s, C_WIDTH), BF16),
        scratch_shapes=[pltpu.VMEM((nh * tq, hd), BF16),
                        pltpu.VMEM((nh * tq, LANES), F32),
                        pltpu.VMEM((nh * tq, 2 * hd), F32),
                        pltpu.VMEM((2, nh * tq, kb * tq), F32),
                        pltpu.VMEM((2, nh * tq, kb * tq), BF16),
                        pltpu.VMEM((2, nh * tq, LANES), F32)],
        compiler_params=_cparams(("parallel", "arbitrary")),
        name="selected_attention",
    )(qkv, qkv, qkv, near, mask)


def _sg_kernel(zb_ref, lng_ref, lnb_ref, w_ref, bs_ref, o_ref, *, nchunk):
    zb = zb_ref[...]
    gl = zb * (0.5 * (1.0 + jnp.tanh(np.float32(np.sqrt(2.0 / np.pi)) * (zb + 0.044715 * (zb * zb * zb)))))
    u = gl[:, :B_WIDTH]
    z = gl[:, B_WIDTH:]
    mu = jnp.mean(z, axis=-1, keepdims=True)
    zc = z - mu
    var = jnp.mean(zc * zc, axis=-1, keepdims=True)
    zn = (zc * lax.rsqrt(var + EPS) * lng_ref[...] + lnb_ref[...]).astype(BF16)
    row = lax.broadcasted_iota(I32, (CHUNK, CHUNK), 0)
    col = lax.broadcasted_iota(I32, (CHUNK, CHUNK), 1)
    for g in range(B_GROUPS):
        w = jnp.where(row >= col, w_ref[g], 0.0).astype(BF16)
        bias = bs_ref[g]
        lo = g * B_GROUP_DIM
        for c in range(nchunk):
            r0 = c * CHUNK
            sz = jnp.dot(w, zn[r0:r0 + CHUNK, lo:lo + B_GROUP_DIM], preferred_element_type=F32) + bias
            o_ref[r0:r0 + CHUNK, lo:lo + B_GROUP_DIM] = (u[r0:r0 + CHUNK, lo:lo + B_GROUP_DIM] * sz).astype(o_ref.dtype)


def _spatial_gating(zb, ln_g, ln_b, w_s, b_s):
    s = zb.shape[0]
    t = _tile(s, 256)
    return pl.pallas_call(
        functools.partial(_sg_kernel, nchunk=t // CHUNK),
        grid=(s // t,),
        in_specs=[pl.BlockSpec((t, 2 * B_WIDTH), lambda i: (i, 0)),
                  pl.BlockSpec((1, B_WIDTH), lambda i: (0, 0)),
                  pl.BlockSpec((1, B_WIDTH), lambda i: (0, 0)),
                  pl.BlockSpec((B_GROUPS, CHUNK, CHUNK), lambda i: (0, 0, 0)),
                  pl.BlockSpec((B_GROUPS, CHUNK, 1), lambda i: (0, 0, 0))],
        out_specs=pl.BlockSpec((t, B_WIDTH), lambda i: (i, 0)),
        out_shape=jax.ShapeDtypeStruct((s, B_WIDTH), BF16),
        compiler_params=_cparams(("parallel",)),
        name="spatial_gating",
    )(zb, ln_g.reshape(1, B_WIDTH), ln_b.reshape(1, B_WIDTH), w_s, b_s.reshape(B_GROUPS, CHUNK, 1))


def _select_kernel(qi_ref, kt_ref, w_ref, o_ref, keys_sc, sc_sc, wb_sc, mx_sc, *, tkc, topk):
    tqi = IDX_QBLOCK
    i = pl.program_id(0)
    nk = o_ref.shape[0]
    nch = (i * tqi + tqi + tkc - 1) // tkc
    qpos = i * tqi + lax.broadcasted_iota(I32, (tqi, tkc), 0)
    kloc = lax.broadcasted_iota(I32, (tqi, tkc), 1)
    qrow = i * tqi + lax.broadcasted_iota(I32, (tqi, LANES), 0)
    klane = lax.broadcasted_iota(I32, (tqi, LANES), 1)

    wgt = w_ref[...] * np.float32(IDX_DIM ** -0.5)
    for h in range(IDX_HEADS):
        wb_sc[h] = jnp.broadcast_to(wgt[:, h:h + 1], (tqi, LANES))
    mx_sc[...] = jnp.full(mx_sc.shape, -jnp.inf, F32)

    def stage_dot(c, slot):
        sc_sc[slot] = jnp.dot(qi_ref[0], kt_ref[jnp.minimum(c, nch - 1)], preferred_element_type=F32)

    def stage_reduce(c, slot):
        c = jnp.minimum(c, nch - 1)
        for u in range(tkc // LANES):
            ls = slice(u * LANES, (u + 1) * LANES)
            acc = jnp.zeros((tqi, LANES), F32)
            for h in range(IDX_HEADS):
                acc += jnp.maximum(sc_sc[slot, h * tqi:(h + 1) * tqi, ls], 0.0) * wb_sc[h]
            acc = acc + 0.0
            bits = pltpu.bitcast(acc, I32)
            ordered = jnp.where(bits < 0, bits ^ jnp.int32(0x7FFFFFFF), bits)
            causal = c * tkc + u * LANES + klane <= qrow
            keys_sc[c, :, ls] = jnp.where(causal, ordered, jnp.int32(INT_MIN))
            mx_sc[...] = jnp.maximum(mx_sc[...], jnp.where(causal, acc, -jnp.inf))

    stage_dot(0, 0)

    def score_pair(u, carry):
        c = 2 * u
        stage_dot(c + 1, 1)
        stage_reduce(c, 0)
        stage_dot(c + 2, 0)
        stage_reduce(c + 1, 1)
        return carry

    lax.fori_loop(0, (nch + 1) // 2, score_pair, 0)

    def count_ge(cand):
        candb = jnp.broadcast_to(cand, (tqi, LANES))

        def count_body(c, cnt):
            kk = keys_sc[c]
            for u in range(tkc // LANES):
                cnt += jnp.where(kk[:, u * LANES:(u + 1) * LANES] >= candb, 1, 0)
            return cnt

        cnt = lax.fori_loop(0, nch, count_body, jnp.zeros((tqi, LANES), I32))
        return jnp.sum(cnt.astype(F32), axis=1, keepdims=True)

    want = np.float32(topk)
    fbits = pltpu.bitcast(jnp.max(mx_sc[...], axis=1, keepdims=True), I32)
    kmax = jnp.where(fbits < 0, fbits ^ jnp.int32(0x7FFFFFFF), fbits)
    few = i * tqi + lax.broadcasted_iota(I32, (tqi, 1), 0) + 1 <= topk
    lo0 = jnp.full((tqi, 1), INT_MIN, I32)
    hi0 = jnp.where(few, lo0 + 1, kmax + 1)

    def narrow(state, cand):
        lo, hi, active = state
        total = count_ge(cand)
        open_ = active > 0.0
        up = open_ & (total >= want)
        lo = jnp.where(up, cand, lo)
        hi = jnp.where(open_ & (~up), cand, hi)
        open_ = open_ & (~(up & (total == want))) & ((hi - lo) != 1)
        return lo, hi, jnp.where(open_, 1.0, 0.0)

    def midpoint(state):
        lo, hi, _ = state
        return lo + lax.shift_right_logical(hi - lo, jnp.int32(1))

    probe = jnp.maximum(kmax, jnp.int32(INT_MIN + SEARCH_PROBE_DROP + 1)) - jnp.int32(SEARCH_PROBE_DROP)
    state = narrow((lo0, hi0, jnp.where(few, 0.0, 1.0)), jnp.where(few, lo0, probe))

    def bisect_body(carry):
        state, _ = carry
        state = narrow(state, midpoint(state))
        state = narrow(state, midpoint(state))
        return state, jnp.sum(state[2])

    (thr, _, _), _ = lax.while_loop(lambda carry: carry[1] > 0.0, bisect_body, (state, jnp.float32(1.0)))
    thrb = jnp.broadcast_to(thr, (tqi, tkc))

    def mask_body(c, carry):
        sel = (keys_sc[c] >= thrb) & (c * tkc + kloc <= qpos)
        o_ref[c] = jnp.where(sel, 0.0, MASKED).astype(o_ref.dtype)
        return carry

    lax.fori_loop(0, nch, mask_body, 0)

    def fill_body(c, carry):
        o_ref[c] = jnp.full((tqi, tkc), MASKED, o_ref.dtype)
        return carry

    lax.fori_loop(nch, nk, fill_body, 0)


def _select_mask(qi_stack, kt, wi, tkc, topk):
    nq, rows, _ = qi_stack.shape
    nk = kt.shape[0]
    s = nq * IDX_QBLOCK
    return pl.pallas_call(
        functools.partial(_select_kernel, tkc=tkc, topk=topk),
        grid=(nq,),
        in_specs=[pl.BlockSpec((1, rows, IDX_DIM), lambda i: (i, 0, 0)),
                  pl.BlockSpec((nk, IDX_DIM, tkc), lambda i: (0, 0, 0)),
                  pl.BlockSpec((IDX_QBLOCK, IDX_HEADS), lambda i: (i, 0))],
        out_specs=pl.BlockSpec((nk, IDX_QBLOCK, tkc), lambda i: (0, i, 0)),
        out_shape=jax.ShapeDtypeStruct((nk, s, tkc), BF16),
        scratch_shapes=[pltpu.VMEM((nk, IDX_QBLOCK, tkc), I32),
                        pltpu.VMEM((2, rows, tkc), F32),
                        pltpu.VMEM((IDX_HEADS, IDX_QBLOCK, LANES), F32),
                        pltpu.VMEM((IDX_QBLOCK, LANES), F32)],
        compiler_params=_cparams(("parallel",)),
        name="indexer_select",
    )(qi_stack, kt, wi)


def _vec_pack(d, *rows):
    rows = [r.reshape(1, d).astype(F32) for r in rows]
    rows += [jnp.zeros((1, d), F32)] * (8 - len(rows))
    return jnp.concatenate(rows, axis=0)


def _pad_cols(w, n):
    return jnp.pad(w, ((0, 0), (0, n - w.shape[1])))


def kernel(x, c, norm_g, mod_w, mod_b, ffn_w1, ffn_w2, rel_table, ab_w_in, ab_w_out, diff_lam,
           diff_subln_g, sg_ln_g, sg_ln_b, sg_w, sg_b, dsa_w_in, dsa_w_out, final_g):
    batch, s, d = x.shape
    depth = norm_g.shape[0]
    d_ff = ffn_w2.shape[2]
    assert batch == 1 and s % IDX_QBLOCK == 0
    ff_pad = -(-d_ff // FFN_COLS) * FFN_COLS if d_ff > FFN_COLS else d_ff

    tq_a = _tile(s, 512)
    tq_c = _tile(s, 256)
    topk = min(TOPK_MAX, s // 4)

    mod = _modulation(c, mod_w, mod_b).reshape(depth, 9, d)
    near_a = _near_bias(rel_table, tq_a, A_HEADS, 2)
    near_c = _near_bias(rel_table, tq_c, C_KV_HEADS, C_GROUP)

    xs = x.reshape(s, d)
    zeros_d = jnp.zeros((d,), F32)
    for li in range(depth):
        def vec(j, li=li):
            last = final_g if (li == depth - 1 and j == 2) else zeros_d
            return _vec_pack(d, norm_g[li, j], mod[li, 3 * j], mod[li, 3 * j + 1], mod[li, 3 * j + 2], last)

        def ffn(xs, j, k, final=False, li=li):
            w1 = ffn_w1[li, k]
            w1g = _col_blocks(_pad_cols(w1[:, :d_ff], ff_pad).astype(BF16), FFN_COLS)
            w1u = _col_blocks(_pad_cols(w1[:, d_ff:], ff_pad).astype(BF16), FFN_COLS)
            w2 = jnp.pad(ffn_w2[li, k], ((0, ff_pad - d_ff), (0, 0))).astype(BF16)
            return _ffn(xs, vec(j), w1g, w1u, w2, final=final)

        xs = ffn(xs, 0, 0)

        v1 = vec(1)
        jj = li // 2
        if li % 2 == 0:
            w_in = ab_w_in[jj]
            w_qkv = jnp.concatenate([w_in[:, :A_WIDTH] * np.float32(A_QK_DIM ** -0.5 * LOG2E),
                                     w_in[:, A_WIDTH:3 * A_WIDTH]], axis=1).astype(BF16)
            w_zb = w_in[:, 3 * A_WIDTH:].astype(BF16)
            qkv = _proj(xs, v1, _col_blocks(w_qkv, PROJ_COLS), BF16)
            zb = _proj(xs, v1, _col_blocks(w_zb, PROJ_COLS), F32)
            lam_init = 0.8 - 0.6 * math.exp(-0.3 * li)
            lp = diff_lam[jj].astype(F32)
            lam = jnp.exp(jnp.sum(lp[0] * lp[1])) - jnp.exp(jnp.sum(lp[2] * lp[3])) + lam_init
            ya = _diff_attention(qkv, near_a, lam.reshape(1), diff_subln_g[jj].reshape(1, A_V_DIM),
                                 1.0 - lam_init, tq_a)
            yb = _spatial_gating(zb, sg_ln_g[jj], sg_ln_b[jj], sg_w[jj], sg_b[jj])
            w_out = ab_w_out[jj].astype(BF16)
            xs = _outproj(xs, v1, [ya, yb], [_col_blocks(w_out[:A_WIDTH], PROJ_COLS),
                                             _col_blocks(w_out[A_WIDTH:], PROJ_COLS)])
        else:
            w_in = dsa_w_in[jj]
            o_idx = C_WIDTH + 2 * C_KV_WIDTH
            o_ki = o_idx + IDX_HEADS * IDX_DIM
            w_main = jnp.concatenate([w_in[:, :C_WIDTH] * np.float32(C_HEAD_DIM ** -0.5 * LOG2E),
                                      w_in[:, C_WIDTH:o_ki]], axis=1).astype(BF16)
            w_kiw = _pad_cols(w_in[:, o_ki:], LANES).astype(BF16)
            main = _proj(xs, v1, _col_blocks(w_main, PROJ_COLS), BF16)
            kiw = _proj(xs, v1, _col_blocks(w_kiw, PROJ_COLS), F32)
            nq = s // IDX_QBLOCK
            qi = main[:, o_idx:o_ki].reshape(nq, IDX_QBLOCK, IDX_HEADS, IDX_DIM)
            qi = qi.transpose(0, 2, 1, 3).reshape(nq, IDX_HEADS * IDX_QBLOCK, IDX_DIM)
            kt = kiw[:, :IDX_DIM].astype(BF16).reshape(s // tq_c, tq_c, IDX_DIM).transpose(0, 2, 1)
            wi = kiw[:, IDX_DIM:IDX_DIM + IDX_HEADS]
            mask = _select_mask(qi, kt, wi, tq_c, topk)
            yc = _masked_attention(main, near_c, mask, tq_c)
            xs = _outproj(xs, v1, [yc], [_col_blocks(dsa_w_out[jj].astype(BF16), PROJ_COLS)])

        xs = ffn(xs, 2, 1, final=(li == depth - 1))
    return xs.reshape(batch, s, d)
```

```python
import functools
import math

import jax
import jax.numpy as jnp
import numpy as np
from jax import lax
from jax.experimental import pallas as pl
from jax.experimental.pallas import tpu as pltpu

F32 = jnp.float32
BF16 = jnp.bfloat16
I32 = jnp.int32

EPS = 1e-6
MASKED = -1e30
LANES = 128
INT_MIN = -(2 ** 31)
LOG2E = math.log2(math.e)
ATTN_SUB_ROWS = 128

A_HEADS = 8
A_QK_DIM = 64
A_V_DIM = 128
A_WIDTH = A_HEADS * A_V_DIM
B_GROUPS = 8
B_GROUP_DIM = 128
B_WIDTH = B_GROUPS * B_GROUP_DIM
CHUNK = 128
C_HEADS = 16
C_KV_HEADS = 4
C_GROUP = C_HEADS // C_KV_HEADS
C_HEAD_DIM = 128
C_WIDTH = C_HEADS * C_HEAD_DIM
C_KV_WIDTH = C_KV_HEADS * C_HEAD_DIM
IDX_HEADS = 16
IDX_DIM = 64
TOPK_MAX = 256
REL_BUCKETS = 32
REL_MAX_DIST = 128
IDX_QBLOCK = 128
SEARCH_PROBE_DROP = 2 ** 24

VMEM_LIMIT = 56 * 1024 * 1024


def _cparams(sem):
    return pltpu.CompilerParams(dimension_semantics=sem, vmem_limit_bytes=VMEM_LIMIT)


FFN_ROWS, FFN_COLS = 512, 512
PROJ_ROWS, PROJ_COLS = 1024, 512


def _col_blocks(w, tn):
    k, n = w.shape
    tn = _tile(n, tn)
    return w.reshape(k, n // tn, tn).transpose(1, 0, 2)


def _tile(n, want):
    if n <= want:
        return n
    t = want
    while n % t:
        t //= 2
    return t


def _mod_kernel(c_ref, w_ref, b_ref, o_ref):
    c = c_ref[...]
    cs = c * (1.0 / (1.0 + jnp.exp(-c)))
    o_ref[0] = jnp.sum(cs * w_ref[0], axis=0, keepdims=True) + b_ref[0]


def _modulation(c, mod_w, mod_b):
    depth, d, n = mod_w.shape
    tn = _tile(n, 1024)
    out = pl.pallas_call(
        _mod_kernel,
        grid=(depth, n // tn),
        in_specs=[pl.BlockSpec((d, 1), lambda l, j: (0, 0)),
                  pl.BlockSpec((1, d, tn), lambda l, j: (l, 0, j)),
                  pl.BlockSpec((1, 1, tn), lambda l, j: (l, 0, j))],
        out_specs=pl.BlockSpec((1, 1, tn), lambda l, j: (l, 0, j)),
        out_shape=jax.ShapeDtypeStruct((depth, 1, n), F32),
        compiler_params=_cparams(("arbitrary", "arbitrary")),
        name="adaln_mod",
    )(c.reshape(d, 1), mod_w, mod_b.reshape(depth, 1, n))
    return out.reshape(depth, n)


def _prenorm(x, vec_ref):
    ms = jnp.mean(x * x, axis=-1, keepdims=True)
    y = x * lax.rsqrt(ms + EPS) * vec_ref[0:1, :]
    return y * (1.0 + vec_ref[2:3, :]) + vec_ref[1:2, :]


def _ffn_kernel(x_ref, vec_ref, w1g_ref, w1u_ref, w2_ref, o_ref, hn_sc, *, nf, final):
    f = pl.program_id(1)

    @pl.when(f == 0)
    def _():
        hn_sc[...] = _prenorm(x_ref[...], vec_ref).astype(BF16)
        o_ref[...] = jnp.zeros_like(o_ref)

    hn = hn_sc[...]
    g = jnp.dot(hn, w1g_ref[...], preferred_element_type=F32)
    u = jnp.dot(hn, w1u_ref[...], preferred_element_type=F32)
    a = (g * (1.0 / (1.0 + jnp.exp(-g))) * u).astype(BF16)
    o_ref[...] += jnp.dot(a, w2_ref[...], preferred_element_type=F32)

    @pl.when(f == nf - 1)
    def _():
        y = x_ref[...] + 0.5 * (1.0 + vec_ref[3:4, :]) * o_ref[...]
        if final:
            ms = jnp.mean(y * y, axis=-1, keepdims=True)
            y = y * lax.rsqrt(ms + EPS) * vec_ref[4:5, :]
        o_ref[...] = y


def _ffn_weights(ffn_w1, ffn_w2):
    depth, two, d, _ = ffn_w1.shape
    d_ff = ffn_w2.shape[2]
    tf = FFN_COLS if d_ff > FFN_COLS else d_ff
    nf = -(-d_ff // tf)
    pad = nf * tf - d_ff
    w1 = jnp.pad(ffn_w1.reshape(depth * two, d, 2, d_ff), ((0, 0), (0, 0), (0, 0), (0, pad)))
    w1 = w1.reshape(depth * two, d, 2, nf, tf).transpose(0, 2, 3, 1, 4).astype(BF16)
    w2 = jnp.pad(ffn_w2.reshape(depth * two, d_ff, d), ((0, 0), (0, pad), (0, 0))).astype(BF16)
    return w1, w2


def _ffn(x, vec, w1, w2, step, *, final):
    s, d = x.shape
    _, _, nf, _, tf = w1.shape
    tm = _tile(s, FFN_ROWS)
    return pl.pallas_call(
        functools.partial(_ffn_kernel, nf=nf, final=final),
        grid=(s // tm, nf),
        in_specs=[pl.BlockSpec((tm, d), lambda i, f: (i, 0)),
                  pl.BlockSpec((8, d), lambda i, f: (0, 0)),
                  pl.BlockSpec((None, None, None, d, tf), lambda i, f: (step, 0, f, 0, 0)),
                  pl.BlockSpec((None, None, None, d, tf), lambda i, f: (step, 1, f, 0, 0)),
                  pl.BlockSpec((None, tf, d), lambda i, f: (step, f, 0))],
        out_specs=pl.BlockSpec((tm, d), lambda i, f: (i, 0)),
        out_shape=jax.ShapeDtypeStruct((s, d), F32),
        scratch_shapes=[pltpu.VMEM((tm, d), BF16)],
        compiler_params=_cparams(("parallel", "arbitrary")),
        name="swiglu_halfstep",
    )(x, vec, w1, w1, w2)


def _proj_kernel(x_ref, vec_ref, w_ref, o_ref, hn_sc):
    @pl.when(pl.program_id(1) == 0)
    def _():
        hn_sc[...] = _prenorm(x_ref[...], vec_ref).astype(BF16)

    o_ref[...] = jnp.dot(hn_sc[...], w_ref[0], preferred_element_type=F32).astype(o_ref.dtype)


def _proj(x, vec, w, out_dtype):
    s, d = x.shape
    nn, _, tn = w.shape
    n = nn * tn
    tm = _tile(s, PROJ_ROWS)
    return pl.pallas_call(
        _proj_kernel,
        grid=(s // tm, nn),
        in_specs=[pl.BlockSpec((tm, d), lambda i, j: (i, 0)),
                  pl.BlockSpec((8, d), lambda i, j: (0, 0)),
                  pl.BlockSpec((1, d, tn), lambda i, j: (j, 0, 0))],
        out_specs=pl.BlockSpec((tm, tn), lambda i, j: (i, j)),
        out_shape=jax.ShapeDtypeStruct((s, n), out_dtype),
        scratch_shapes=[pltpu.VMEM((tm, d), BF16)],
        compiler_params=_cparams(("parallel", "arbitrary")),
        name="norm_mod_proj",
    )(x, vec, w)


def _outproj_kernel(*refs, n_in):
    x_ref, vec_ref = refs[0], refs[1]
    lhs = refs[2:2 + n_in]
    ws = refs[2 + n_in:2 + 2 * n_in]
    o_ref = refs[2 + 2 * n_in]
    acc = jnp.dot(lhs[0][...], ws[0][0], preferred_element_type=F32)
    for a, w in zip(lhs[1:], ws[1:]):
        acc += jnp.dot(a[...], w[0], preferred_element_type=F32)
    o_ref[...] = x_ref[...] + (1.0 + vec_ref[3:4, :]) * acc


def _outproj(x, vec, lhs, ws):
    s, d = x.shape
    tm = _tile(s, PROJ_ROWS)
    tn = ws[0].shape[2]
    n_in = len(lhs)
    in_specs = [pl.BlockSpec((tm, tn), lambda i, j: (i, j)),
                pl.BlockSpec((8, tn), lambda i, j: (0, j))]
    in_specs += [pl.BlockSpec((tm, a.shape[1]), lambda i, j: (i, 0)) for a in lhs]
    in_specs += [pl.BlockSpec((1, w.shape[1], tn), lambda i, j: (j, 0, 0)) for w in ws]
    return pl.pallas_call(
        functools.partial(_outproj_kernel, n_in=n_in),
        grid=(s // tm, d // tn),
        in_specs=in_specs,
        out_specs=pl.BlockSpec((tm, tn), lambda i, j: (i, j)),
        out_shape=jax.ShapeDtypeStruct((s, d), F32),
        compiler_params=_cparams(("parallel", "arbitrary")),
        name="outproj_residual",
    )(x, vec, *lhs, *ws)


def _flash_kernel(*refs, nh, tq, kb, sub, diff, lam_scale):
    if diff:
        lam_ref, q_ref, k_ref, v_ref, nb_ref, g_ref, o_ref, qs_sc, m_sc, acc_sc, s_sc, p_sc, al_sc = refs
        mask_ref = None
    else:
        q_ref, k_ref, v_ref, nb_ref, mask_ref, o_ref, qs_sc, m_sc, acc_sc, s_sc, p_sc, al_sc = refs
    tk = tq
    rows = nh * tq
    hd = k_ref.shape[1]
    assert hd == LANES and rows % sub == 0 and tq % sub == 0
    i = pl.program_id(1)

    if diff:
        q = q_ref[...]
        lane = lax.broadcasted_iota(I32, q.shape, 1)
        zero = jnp.zeros_like(q)
        qs_sc[0:tq, :] = jnp.where(lane < A_QK_DIM, q, zero)
        qs_sc[tq:2 * tq, :] = jnp.where(lane >= A_QK_DIM, q, zero)
    else:
        for r in range(nh):
            qs_sc[r * tq:(r + 1) * tq, :] = q_ref[:, r * hd:(r + 1) * hd]
    m_sc[...] = jnp.full(m_sc.shape, -jnp.inf, F32)
    acc_sc[...] = jnp.zeros(acc_sc.shape, F32)

    tks = kb * tk
    nk = k_ref.shape[0] // tk
    last_step = i // kb

    def bias_index(j):
        return jnp.where(j > i, 3, jnp.clip(j - (i - 2), 0, 2))

    def stage_qk(t, slot):
        start = pl.multiple_of(jnp.minimum(t, last_step) * tks, tks)
        kblk = k_ref[pl.ds(start, tks), :]
        s_sc[slot] = lax.dot_general(qs_sc[...], kblk, (((1,), (1,)), ((), ())), preferred_element_type=F32)

    def stage_softmax(t, slot, biased):
        tc = jnp.minimum(t, last_step)
        for r in range(rows // sub):
            rs = slice(r * sub, (r + 1) * sub)
            tiles = []
            for b in range(kb):
                j = tc * kb + b
                s = s_sc[slot, rs, b * tk:(b + 1) * tk]
                if biased:
                    col = jnp.where(t > last_step, 3, bias_index(j))
                    s = s + nb_ref[0, col, rs, :]
                if mask_ref is not None:
                    off = (r * sub) % tq
                    s = s + mask_ref[jnp.minimum(j, nk - 1), off:off + sub, :].astype(F32)
                tiles += [s[:, u * LANES:(u + 1) * LANES] for u in range(tk // LANES)]
            cmax = tiles[0]
            for u in tiles[1:]:
                cmax = jnp.maximum(cmax, u)
            m_old = m_sc[rs, :]
            m_new = jnp.maximum(m_old, jnp.max(cmax, axis=1, keepdims=True))
            al_sc[slot, rs, :] = jnp.exp2(m_old - m_new)
            p_sc[slot, rs, :] = jnp.concatenate([jnp.exp2(u - m_new) for u in tiles], axis=1).astype(BF16)
            m_sc[rs, :] = m_new

    def stage_pv(t, slot):
        start = pl.multiple_of(jnp.minimum(t, last_step) * tks, tks)
        vbe = jnp.concatenate([v_ref[pl.ds(start, tks), :], jnp.ones((tks, hd), BF16)], axis=1)
        pv = jnp.dot(p_sc[slot], vbe, preferred_element_type=F32)
        alpha = al_sc[slot]
        acc_sc[...] = jnp.concatenate([alpha, alpha], axis=1) * acc_sc[...] + pv

    stage_qk(0, 0)
    stage_qk(1, 1)
    stage_softmax(0, 0, True)

    nfar = jnp.maximum((i - 1) // kb, 0)
    npairs = jnp.maximum((nfar - 1) // 2, 0)

    def far_pair(u, carry):
        t = 2 * u
        stage_pv(t, 0)
        stage_softmax(t + 1, 1, False)
        stage_qk(t + 2, 0)
        stage_pv(t + 1, 1)
        stage_softmax(t + 2, 0, False)
        stage_qk(t + 3, 1)
        return carry

    lax.fori_loop(0, npairs, far_pair, 0)

    t0 = 2 * npairs
    stage_pv(t0, 0)
    stage_softmax(t0 + 1, 1, True)
    stage_qk(t0 + 2, 0)
    stage_pv(t0 + 1, 1)
    stage_softmax(t0 + 2, 0, True)
    stage_qk(t0 + 3, 1)
    stage_pv(t0 + 2, 0)
    stage_softmax(t0 + 3, 1, True)
    stage_pv(t0 + 3, 1)

    if diff:
        o0 = acc_sc[0:tq, 0:hd] / acc_sc[0:tq, hd:2 * hd]
        o1 = acc_sc[tq:2 * tq, 0:hd] / acc_sc[tq:2 * tq, hd:2 * hd]
        dlt = o0 - lam_ref[0] * o1
        ms = jnp.mean(dlt * dlt, axis=-1, keepdims=True)
        o_ref[...] = ((dlt * lax.rsqrt(ms + EPS) * g_ref[...]) * lam_scale).astype(o_ref.dtype)
    else:
        for r in range(nh):
            rs = slice(r * tq, (r + 1) * tq)
            o_ref[:, r * hd:(r + 1) * hd] = (acc_sc[rs, 0:hd] / acc_sc[rs, hd:2 * hd]).astype(o_ref.dtype)


def _rel_bucket(dist):
    n = jnp.maximum(dist, 0)
    max_exact = REL_BUCKETS // 2
    nf = jnp.maximum(n, 1).astype(F32)
    large = max_exact + (jnp.log(nf / max_exact) / math.log(REL_MAX_DIST / max_exact)
                         * (REL_BUCKETS - max_exact)).astype(I32)
    large = jnp.minimum(large, REL_BUCKETS - 1)
    return jnp.where(n < max_exact, n, large)


def _near_bias(rel_table, tq, groups, nh):
    assert tq >= LANES, "keys older than one block must all fall in the last bucket"
    r = jnp.arange(tq, dtype=I32)[:, None]
    c = jnp.arange(2 * tq, dtype=I32)[None, :]
    dist = r + tq - c
    rel = (rel_table - rel_table[REL_BUCKETS - 1][None, :]) * np.float32(LOG2E)
    onehot = jax.nn.one_hot(_rel_bucket(dist), REL_BUCKETS, dtype=F32)
    b = jnp.einsum("rcb,bh->hrc", onehot, rel, precision=lax.Precision.HIGHEST)
    b = jnp.where((dist >= 0)[None], b, MASKED)
    heads = b.shape[0]
    tiles = jnp.stack([jnp.zeros((heads, tq, tq), F32), b[:, :, :tq], b[:, :, tq:],
                       jnp.full((heads, tq, tq), MASKED, F32)], axis=1)
    tiles = tiles.reshape(groups, nh, 4, tq, tq).transpose(0, 2, 1, 3, 4)
    return tiles.reshape(groups, 4, nh * tq, tq)


def _diff_attention(qkv, near, lam, subln_g, lam_scale, tq):
    s = qkv.shape[0]
    hd = A_V_DIM
    nh = 2
    kb = 1
    kcol = A_WIDTH // hd
    return pl.pallas_call(
        functools.partial(_flash_kernel, nh=nh, tq=tq, kb=kb, sub=min(tq, ATTN_SUB_ROWS), diff=True,
                          lam_scale=lam_scale),
        grid=(A_HEADS, s // tq),
        in_specs=[pl.BlockSpec(memory_space=pltpu.SMEM),
                  pl.BlockSpec((tq, hd), lambda h, i: (i, h)),
                  pl.BlockSpec((s, hd), lambda h, i: (0, kcol + h)),
                  pl.BlockSpec((s, hd), lambda h, i: (0, 2 * kcol + h)),
                  pl.BlockSpec((1, 4, nh * tq, tq), lambda h, i: (h, 0, 0, 0)),
                  pl.BlockSpec((1, hd), lambda h, i: (0, 0))],
        out_specs=pl.BlockSpec((tq, hd), lambda h, i: (i, h)),
        out_shape=jax.ShapeDtypeStruct((s, A_WIDTH), BF16),
        scratch_shapes=[pltpu.VMEM((nh * tq, hd), BF16),
                        pltpu.VMEM((nh * tq, LANES), F32),
                        pltpu.VMEM((nh * tq, 2 * hd), F32),
                        pltpu.VMEM((2, nh * tq, kb * tq), F32),
                        pltpu.VMEM((2, nh * tq, kb * tq), BF16),
                        pltpu.VMEM((2, nh * tq, LANES), F32)],
        compiler_params=_cparams(("parallel", "arbitrary")),
        name="diff_attention",
    )(lam, qkv, qkv, qkv, near, subln_g)


def _masked_attention(qkv, near, mask, tq):
    s = qkv.shape[0]
    hd = C_HEAD_DIM
    nh = C_GROUP
    kcol = C_WIDTH // hd
    vcol = kcol + C_KV_HEADS
    nk = s // tq
    kb = 2 if nk % 2 == 0 else 1
    return pl.pallas_call(
        functools.partial(_flash_kernel, nh=nh, tq=tq, kb=kb, sub=min(tq, ATTN_SUB_ROWS), diff=False,
                          lam_scale=1.0),
        grid=(C_KV_HEADS, s // tq),
        in_specs=[pl.BlockSpec((tq, nh * hd), lambda g, i: (i, g)),
                  pl.BlockSpec((s, hd), lambda g, i: (0, kcol + g)),
                  pl.BlockSpec((s, hd), lambda g, i: (0, vcol + g)),
                  pl.BlockSpec((1, 4, nh * tq, tq), lambda g, i: (g, 0, 0, 0)),
                  pl.BlockSpec((nk, tq, tq), lambda g, i: (0, i, 0))],
        out_specs=pl.BlockSpec((tq, nh * hd), lambda g, i: (i, g)),
        out_shape=jax.ShapeDtypeStruct((s, C_WIDTH), BF16),
        scratch_shapes=[pltpu.VMEM((nh * tq, hd), BF16),
                        pltpu.VMEM((nh * tq, LANES), F32),
                        pltpu.VMEM((nh * tq, 2 * hd), F32),
                        pltpu.VMEM((2, nh * tq, kb * tq), F32),
                        pltpu.VMEM((2, nh * tq, kb * tq), BF16),
                        pltpu.VMEM((2, nh * tq, LANES), F32)],
        compiler_params=_cparams(("parallel", "arbitrary")),
        name="selected_attention",
    )(qkv, qkv, qkv, near, mask)


def _sg_kernel(zb_ref, lng_ref, lnb_ref, w_ref, bs_ref, o_ref, *, nchunk):
    zb = zb_ref[...]
    gl = zb * (0.5 * (1.0 + jnp.tanh(np.float32(np.sqrt(2.0 / np.pi)) * (zb + 0.044715 * (zb * zb * zb)))))
    u = gl[:, :B_WIDTH]
    z = gl[:, B_WIDTH:]
    mu = jnp.mean(z, axis=-1, keepdims=True)
    zc = z - mu
    var = jnp.mean(zc * zc, axis=-1, keepdims=True)
    zn = (zc * lax.rsqrt(var + EPS) * lng_ref[...] + lnb_ref[...]).astype(BF16)
    row = lax.broadcasted_iota(I32, (CHUNK, CHUNK), 0)
    col = lax.broadcasted_iota(I32, (CHUNK, CHUNK), 1)
    for g in range(B_GROUPS):
        w = jnp.where(row >= col, w_ref[g], 0.0).astype(BF16)
        bias = bs_ref[g]
        lo = g * B_GROUP_DIM
        for c in range(nchunk):
            r0 = c * CHUNK
            sz = jnp.dot(w, zn[r0:r0 + CHUNK, lo:lo + B_GROUP_DIM], preferred_element_type=F32) + bias
            o_ref[r0:r0 + CHUNK, lo:lo + B_GROUP_DIM] = (u[r0:r0 + CHUNK, lo:lo + B_GROUP_DIM] * sz).astype(o_ref.dtype)


def _spatial_gating(zb, ln_g, ln_b, w_s, b_s):
    s = zb.shape[0]
    t = _tile(s, 256)
    return pl.pallas_call(
        functools.partial(_sg_kernel, nchunk=t // CHUNK),
        grid=(s // t,),
        in_specs=[pl.BlockSpec((t, 2 * B_WIDTH), lambda i: (i, 0)),
                  pl.BlockSpec((1, B_WIDTH), lambda i: (0, 0)),
                  pl.BlockSpec((1, B_WIDTH), lambda i: (0, 0)),
                  pl.BlockSpec((B_GROUPS, CHUNK, CHUNK), lambda i: (0, 0, 0)),
                  pl.BlockSpec((B_GROUPS, CHUNK, 1), lambda i: (0, 0, 0))],
        out_specs=pl.BlockSpec((t, B_WIDTH), lambda i: (i, 0)),
        out_shape=jax.ShapeDtypeStruct((s, B_WIDTH), BF16),
        compiler_params=_cparams(("parallel",)),
        name="spatial_gating",
    )(zb, ln_g.reshape(1, B_WIDTH), ln_b.reshape(1, B_WIDTH), w_s, b_s.reshape(B_GROUPS, CHUNK, 1))


def _select_kernel(qi_ref, kt_ref, w_ref, o_ref, keys_sc, sc_sc, wb_sc, mx_sc, *, tkc, topk):
    tqi = IDX_QBLOCK
    i = pl.program_id(0)
    nk = o_ref.shape[0]
    nch = (i * tqi + tqi + tkc - 1) // tkc
    qpos = i * tqi + lax.broadcasted_iota(I32, (tqi, tkc), 0)
    kloc = lax.broadcasted_iota(I32, (tqi, tkc), 1)
    qrow = i * tqi + lax.broadcasted_iota(I32, (tqi, LANES), 0)
    klane = lax.broadcasted_iota(I32, (tqi, LANES), 1)

    wgt = w_ref[...] * np.float32(IDX_DIM ** -0.5)
    for h in range(IDX_HEADS):
        wb_sc[h] = jnp.broadcast_to(wgt[:, h:h + 1], (tqi, LANES))
    mx_sc[...] = jnp.full(mx_sc.shape, -jnp.inf, F32)

    def stage_dot(c, slot):
        sc_sc[slot] = jnp.dot(qi_ref[0], kt_ref[jnp.minimum(c, nch - 1)], preferred_element_type=F32)

    def stage_reduce(c, slot):
        c = jnp.minimum(c, nch - 1)
        for u in range(tkc // LANES):
            ls = slice(u * LANES, (u + 1) * LANES)
            acc = jnp.zeros((tqi, LANES), F32)
            for h in range(IDX_HEADS):
                acc += jnp.maximum(sc_sc[slot, h * tqi:(h + 1) * tqi, ls], 0.0) * wb_sc[h]
            acc = acc + 0.0
            bits = pltpu.bitcast(acc, I32)
            ordered = jnp.where(bits < 0, bits ^ jnp.int32(0x7FFFFFFF), bits)
            causal = c * tkc + u * LANES + klane <= qrow
            keys_sc[c, :, ls] = jnp.where(causal, ordered, jnp.int32(INT_MIN))
            mx_sc[...] = jnp.maximum(mx_sc[...], jnp.where(causal, acc, -jnp.inf))

    stage_dot(0, 0)

    def score_pair(u, carry):
        c = 2 * u
        stage_dot(c + 1, 1)
        stage_reduce(c, 0)
        stage_dot(c + 2, 0)
        stage_reduce(c + 1, 1)
        return carry

    lax.fori_loop(0, (nch + 1) // 2, score_pair, 0)

    def count_ge(cand):
        candb = jnp.broadcast_to(cand, (tqi, LANES))

        def count_body(c, cnt):
            kk = keys_sc[c]
            for u in range(tkc // LANES):
                cnt += jnp.where(kk[:, u * LANES:(u + 1) * LANES] >= candb, 1, 0)
            return cnt

        cnt = lax.fori_loop(0, nch, count_body, jnp.zeros((tqi, LANES), I32))
        return jnp.sum(cnt.astype(F32), axis=1, keepdims=True)

    want = np.float32(topk)
    fbits = pltpu.bitcast(jnp.max(mx_sc[...], axis=1, keepdims=True), I32)
    kmax = jnp.where(fbits < 0, fbits ^ jnp.int32(0x7FFFFFFF), fbits)
    few = i * tqi + lax.broadcasted_iota(I32, (tqi, 1), 0) + 1 <= topk
    lo0 = jnp.full((tqi, 1), INT_MIN, I32)
    hi0 = jnp.where(few, lo0 + 1, kmax + 1)

    def narrow(state, cand):
        lo, hi, active = state
        total = count_ge(cand)
        open_ = active > 0.0
        up = open_ & (total >= want)
        lo = jnp.where(up, cand, lo)
        hi = jnp.where(open_ & (~up), cand, hi)
        open_ = open_ & (~(up & (total == want))) & ((hi - lo) != 1)
        return lo, hi, jnp.where(open_, 1.0, 0.0)

    def midpoint(state):
        lo, hi, _ = state
        return lo + lax.shift_right_logical(hi - lo, jnp.int32(1))

    probe = jnp.maximum(kmax, jnp.int32(INT_MIN + SEARCH_PROBE_DROP + 1)) - jnp.int32(SEARCH_PROBE_DROP)
    state = narrow((lo0, hi0, jnp.where(few, 0.0, 1.0)), jnp.where(few, lo0, probe))

    def bisect_body(carry):
        state, _ = carry
        state = narrow(state, midpoint(state))
        state = narrow(state, midpoint(state))
        return state, jnp.sum(state[2])

    (thr, _, _), _ = lax.while_loop(lambda carry: carry[1] > 0.0, bisect_body, (state, jnp.float32(1.0)))
    thrb = jnp.broadcast_to(thr, (tqi, tkc))

    def mask_body(c, carry):
        sel = (keys_sc[c] >= thrb) & (c * tkc + kloc <= qpos)
        o_ref[c] = jnp.where(sel, 0.0, MASKED).astype(o_ref.dtype)
        return carry

    lax.fori_loop(0, nch, mask_body, 0)

    def fill_body(c, carry):
        o_ref[c] = jnp.full((tqi, tkc), MASKED, o_ref.dtype)
        return carry

    lax.fori_loop(nch, nk, fill_body, 0)


def _select_mask(qi_stack, kt, wi, tkc, topk):
    nq, rows, _ = qi_stack.shape
    nk = kt.shape[0]
    s = nq * IDX_QBLOCK
    return pl.pallas_call(
        functools.partial(_select_kernel, tkc=tkc, topk=topk),
        grid=(nq,),
        in_specs=[pl.BlockSpec((1, rows, IDX_DIM), lambda i: (i, 0, 0)),
                  pl.BlockSpec((nk, IDX_DIM, tkc), lambda i: (0, 0, 0)),
                  pl.BlockSpec((IDX_QBLOCK, IDX_HEADS), lambda i: (i, 0))],
        out_specs=pl.BlockSpec((nk, IDX_QBLOCK, tkc), lambda i: (0, i, 0)),
        out_shape=jax.ShapeDtypeStruct((nk, s, tkc), BF16),
        scratch_shapes=[pltpu.VMEM((nk, IDX_QBLOCK, tkc), I32),
                        pltpu.VMEM((2, rows, tkc), F32),
                        pltpu.VMEM((IDX_HEADS, IDX_QBLOCK, LANES), F32),
                        pltpu.VMEM((IDX_QBLOCK, LANES), F32)],
        compiler_params=_cparams(("parallel",)),
        name="indexer_select",
    )(qi_stack, kt, wi)


def _vec_pack(d, *rows):
    rows = [r.reshape(1, d).astype(F32) for r in rows]
    rows += [jnp.zeros((1, d), F32)] * (8 - len(rows))
    return jnp.concatenate(rows, axis=0)


def _pad_cols(w, n):
    return jnp.pad(w, ((0, 0), (0, n - w.shape[1])))


def kernel(x, c, norm_g, mod_w, mod_b, ffn_w1, ffn_w2, rel_table, ab_w_in, ab_w_out, diff_lam,
           diff_subln_g, sg_ln_g, sg_ln_b, sg_w, sg_b, dsa_w_in, dsa_w_out, final_g):
    batch, s, d = x.shape
    depth = norm_g.shape[0]
    assert batch == 1 and s % IDX_QBLOCK == 0

    tq_a = _tile(s, 512)
    tq_c = _tile(s, 256)
    topk = min(TOPK_MAX, s // 4)

    mod = _modulation(c, mod_w, mod_b).reshape(depth, 9, d)
    ffn_a, ffn_b = _ffn_weights(ffn_w1, ffn_w2)
    near_a = _near_bias(rel_table, tq_a, A_HEADS, 2)
    near_c = _near_bias(rel_table, tq_c, C_KV_HEADS, C_GROUP)

    xs = x.reshape(s, d)
    zeros_d = jnp.zeros((d,), F32)
    for li in range(depth):
        def vec(j, li=li):
            last = final_g if (li == depth - 1 and j == 2) else zeros_d
            return _vec_pack(d, norm_g[li, j], mod[li, 3 * j], mod[li, 3 * j + 1], mod[li, 3 * j + 2], last)

        def ffn(xs, j, k, final=False, li=li):
            return _ffn(xs, vec(j), ffn_a, ffn_b, 2 * li + k, final=final)

        xs = ffn(xs, 0, 0)

        v1 = vec(1)
        jj = li // 2
        if li % 2 == 0:
            w_in = ab_w_in[jj]
            w_qkv = jnp.concatenate([w_in[:, :A_WIDTH] * np.float32(A_QK_DIM ** -0.5 * LOG2E),
                                     w_in[:, A_WIDTH:3 * A_WIDTH]], axis=1).astype(BF16)
            w_zb = w_in[:, 3 * A_WIDTH:].astype(BF16)
            qkv = _proj(xs, v1, _col_blocks(w_qkv, PROJ_COLS), BF16)
            zb = _proj(xs, v1, _col_blocks(w_zb, PROJ_COLS), F32)
            lam_init = 0.8 - 0.6 * math.exp(-0.3 * li)
            lp = diff_lam[jj].astype(F32)
            lam = jnp.exp(jnp.sum(lp[0] * lp[1])) - jnp.exp(jnp.sum(lp[2] * lp[3])) + lam_init
            ya = _diff_attention(qkv, near_a, lam.reshape(1), diff_subln_g[jj].reshape(1, A_V_DIM),
                                 1.0 - lam_init, tq_a)
            yb = _spatial_gating(zb, sg_ln_g[jj], sg_ln_b[jj], sg_w[jj], sg_b[jj])
            w_out = ab_w_out[jj].astype(BF16)
            xs = _outproj(xs, v1, [ya, yb], [_col_blocks(w_out[:A_WIDTH], PROJ_COLS),
                                             _col_blocks(w_out[A_WIDTH:], PROJ_COLS)])
        else:
            w_in = dsa_w_in[jj]
            o_idx = C_WIDTH + 2 * C_KV_WIDTH
            o_ki = o_idx + IDX_HEADS * IDX_DIM
            w_main = jnp.concatenate([w_in[:, :C_WIDTH] * np.float32(C_HEAD_DIM ** -0.5 * LOG2E),
                                      w_in[:, C_WIDTH:o_ki]], axis=1).astype(BF16)
            w_kiw = _pad_cols(w_in[:, o_ki:], LANES).astype(BF16)
            main = _proj(xs, v1, _col_blocks(w_main, PROJ_COLS), BF16)
            kiw = _proj(xs, v1, _col_blocks(w_kiw, PROJ_COLS), F32)
            nq = s // IDX_QBLOCK
            qi = main[:, o_idx:o_ki].reshape(nq, IDX_QBLOCK, IDX_HEADS, IDX_DIM)
            qi = qi.transpose(0, 2, 1, 3).reshape(nq, IDX_HEADS * IDX_QBLOCK, IDX_DIM)
            kt = kiw[:, :IDX_DIM].astype(BF16).reshape(s // tq_c, tq_c, IDX_DIM).transpose(0, 2, 1)
            wi = kiw[:, IDX_DIM:IDX_DIM + IDX_HEADS]
            mask = _select_mask(qi, kt, wi, tq_c, topk)
            yc = _masked_attention(main, near_c, mask, tq_c)
            xs = _outproj(xs, v1, [yc], [_col_blocks(dsa_w_out[jj].astype(BF16), PROJ_COLS)])

        xs = ffn(xs, 2, 1, final=(li == depth - 1))
    return xs.reshape(batch, s, d)
```

```python
import functools
import math

import jax
import jax.numpy as jnp
import numpy as np
from jax import lax
from jax.experimental import pallas as pl
from jax.experimental.pallas import tpu as pltpu

F32 = jnp.float32
BF16 = jnp.bfloat16
I32 = jnp.int32

EPS = 1e-6
MASKED = -1e30
LANES = 128
INT_MIN = -(2 ** 31)
LOG2E = math.log2(math.e)
ATTN_SUB_ROWS = 128

A_HEADS = 8
A_QK_DIM = 64
A_V_DIM = 128
A_WIDTH = A_HEADS * A_V_DIM
B_GROUPS = 8
B_GROUP_DIM = 128
B_WIDTH = B_GROUPS * B_GROUP_DIM
CHUNK = 128
C_HEADS = 16
C_KV_HEADS = 4
C_GROUP = C_HEADS // C_KV_HEADS
C_HEAD_DIM = 128
C_WIDTH = C_HEADS * C_HEAD_DIM
C_KV_WIDTH = C_KV_HEADS * C_HEAD_DIM
IDX_HEADS = 16
IDX_DIM = 64
TOPK_MAX = 256
REL_BUCKETS = 32
REL_MAX_DIST = 128
IDX_QBLOCK = 128
SEARCH_PROBE_DROP = 2 ** 24

VMEM_LIMIT = 56 * 1024 * 1024


def _cparams(sem):
    return pltpu.CompilerParams(dimension_semantics=sem, vmem_limit_bytes=VMEM_LIMIT)


FFN_ROWS, FFN_COLS = 512, 512
PROJ_ROWS, PROJ_COLS = 1024, 512


def _col_blocks(w, tn):
    k, n = w.shape
    tn = _tile(n, tn)
    return w.reshape(k, n // tn, tn).transpose(1, 0, 2)


def _tile(n, want):
    if n <= want:
        return n
    t = want
    while n % t:
        t //= 2
    return t


def _mod_kernel(c_ref, w_ref, b_ref, o_ref):
    c = c_ref[...]
    cs = c * (1.0 / (1.0 + jnp.exp(-c)))
    o_ref[0] = jnp.sum(cs * w_ref[0], axis=0, keepdims=True) + b_ref[0]


def _modulation(c, mod_w, mod_b):
    depth, d, n = mod_w.shape
    tn = _tile(n, 1024)
    out = pl.pallas_call(
        _mod_kernel,
        grid=(depth, n // tn),
        in_specs=[pl.BlockSpec((d, 1), lambda l, j: (0, 0)),
                  pl.BlockSpec((1, d, tn), lambda l, j: (l, 0, j)),
                  pl.BlockSpec((1, 1, tn), lambda l, j: (l, 0, j))],
        out_specs=pl.BlockSpec((1, 1, tn), lambda l, j: (l, 0, j)),
        out_shape=jax.ShapeDtypeStruct((depth, 1, n), F32),
        compiler_params=_cparams(("arbitrary", "arbitrary")),
        name="adaln_mod",
    )(c.reshape(d, 1), mod_w, mod_b.reshape(depth, 1, n))
    return out.reshape(depth, n)


def _prenorm(x, vec_ref):
    ms = jnp.mean(x * x, axis=-1, keepdims=True)
    y = x * lax.rsqrt(ms + EPS) * vec_ref[0:1, :]
    return y * (1.0 + vec_ref[2:3, :]) + vec_ref[1:2, :]


def _ffn_kernel(x_ref, vec_ref, w1g_ref, w1u_ref, w2_ref, o_ref, hn_sc, *, nf, final):
    f = pl.program_id(1)

    @pl.when(f == 0)
    def _():
        hn_sc[...] = _prenorm(x_ref[...], vec_ref).astype(BF16)
        o_ref[...] = jnp.zeros_like(o_ref)

    hn = hn_sc[...]
    g = jnp.dot(hn, w1g_ref[...], preferred_element_type=F32)
    u = jnp.dot(hn, w1u_ref[...], preferred_element_type=F32)
    a = (g * (1.0 / (1.0 + jnp.exp(-g))) * u).astype(BF16)
    o_ref[...] += jnp.dot(a, w2_ref[...], preferred_element_type=F32)

    @pl.when(f == nf - 1)
    def _():
        y = x_ref[...] + 0.5 * (1.0 + vec_ref[3:4, :]) * o_ref[...]
        if final:
            ms = jnp.mean(y * y, axis=-1, keepdims=True)
            y = y * lax.rsqrt(ms + EPS) * vec_ref[4:5, :]
        o_ref[...] = y


def _ffn_weights(ffn_w1, ffn_w2):
    depth, two, d, _ = ffn_w1.shape
    d_ff = ffn_w2.shape[2]
    tf = FFN_COLS if d_ff > FFN_COLS else d_ff
    nf = -(-d_ff // tf)
    pad = nf * tf - d_ff
    w1 = jnp.pad(ffn_w1.reshape(depth * two, d, 2, d_ff), ((0, 0), (0, 0), (0, 0), (0, pad)))
    w1 = w1.reshape(depth * two, d, 2, nf, tf).transpose(0, 2, 3, 1, 4).astype(BF16)
    w2 = jnp.pad(ffn_w2.reshape(depth * two, d_ff, d), ((0, 0), (0, pad), (0, 0))).astype(BF16)
    return w1, w2


def _ffn(x, vec, w1, w2, step, *, final):
    s, d = x.shape
    _, _, nf, _, tf = w1.shape
    tm = _tile(s, FFN_ROWS)
    return pl.pallas_call(
        functools.partial(_ffn_kernel, nf=nf, final=final),
        grid=(s // tm, nf),
        in_specs=[pl.BlockSpec((tm, d), lambda i, f: (i, 0)),
                  pl.BlockSpec((8, d), lambda i, f: (0, 0)),
                  pl.BlockSpec((None, None, None, d, tf), lambda i, f: (step, 0, f, 0, 0)),
                  pl.BlockSpec((None, None, None, d, tf), lambda i, f: (step, 1, f, 0, 0)),
                  pl.BlockSpec((None, tf, d), lambda i, f: (step, f, 0))],
        out_specs=pl.BlockSpec((tm, d), lambda i, f: (i, 0)),
        out_shape=jax.ShapeDtypeStruct((s, d), F32),
        scratch_shapes=[pltpu.VMEM((tm, d), BF16)],
        compiler_params=_cparams(("parallel", "arbitrary")),
        name="swiglu_halfstep",
    )(x, vec, w1, w1, w2)


def _proj_kernel(x_ref, vec_ref, w_ref, o_ref, hn_sc):
    @pl.when(pl.program_id(1) == 0)
    def _():
        hn_sc[...] = _prenorm(x_ref[...], vec_ref).astype(BF16)

    o_ref[...] = jnp.dot(hn_sc[...], w_ref[0], preferred_element_type=F32).astype(o_ref.dtype)


def _proj(x, vec, w, out_dtype):
    s, d = x.shape
    nn, _, tn = w.shape
    n = nn * tn
    tm = _tile(s, PROJ_ROWS)
    return pl.pallas_call(
        _proj_kernel,
        grid=(s // tm, nn),
        in_specs=[pl.BlockSpec((tm, d), lambda i, j: (i, 0)),
                  pl.BlockSpec((8, d), lambda i, j: (0, 0)),
                  pl.BlockSpec((1, d, tn), lambda i, j: (j, 0, 0))],
        out_specs=pl.BlockSpec((tm, tn), lambda i, j: (i, j)),
        out_shape=jax.ShapeDtypeStruct((s, n), out_dtype),
        scratch_shapes=[pltpu.VMEM((tm, d), BF16)],
        compiler_params=_cparams(("parallel", "arbitrary")),
        name="norm_mod_proj",
    )(x, vec, w)


def _outproj_kernel(*refs, n_in):
    x_ref, vec_ref = refs[0], refs[1]
    lhs = refs[2:2 + n_in]
    ws = refs[2 + n_in:2 + 2 * n_in]
    o_ref = refs[2 + 2 * n_in]
    acc = jnp.dot(lhs[0][...], ws[0][0], preferred_element_type=F32)
    for a, w in zip(lhs[1:], ws[1:]):
        acc += jnp.dot(a[...], w[0], preferred_element_type=F32)
    o_ref[...] = x_ref[...] + (1.0 + vec_ref[3:4, :]) * acc


def _outproj(x, vec, lhs, ws):
    s, d = x.shape
    tm = _tile(s, PROJ_ROWS)
    tn = ws[0].shape[2]
    n_in = len(lhs)
    in_specs = [pl.BlockSpec((tm, tn), lambda i, j: (i, j)),
                pl.BlockSpec((8, tn), lambda i, j: (0, j))]
    in_specs += [pl.BlockSpec((tm, a.shape[1]), lambda i, j: (i, 0)) for a in lhs]
    in_specs += [pl.BlockSpec((1, w.shape[1], tn), lambda i, j: (j, 0, 0)) for w in ws]
    return pl.pallas_call(
        functools.partial(_outproj_kernel, n_in=n_in),
        grid=(s // tm, d // tn),
        in_specs=in_specs,
        out_specs=pl.BlockSpec((tm, tn), lambda i, j: (i, j)),
        out_shape=jax.ShapeDtypeStruct((s, d), F32),
        compiler_params=_cparams(("parallel", "arbitrary")),
        name="outproj_residual",
    )(x, vec, *lhs, *ws)


def _flash_kernel(*refs, nh, tq, kb, sub, diff, lam_scale):
    if diff:
        lam_ref, q_ref, k_ref, v_ref, nb_ref, g_ref, o_ref, qs_sc, m_sc, acc_sc, s_sc, p_sc, al_sc = refs
        mask_ref = None
    else:
        q_ref, k_ref, v_ref, nb_ref, mask_ref, o_ref, qs_sc, m_sc, acc_sc, s_sc, p_sc, al_sc = refs
    tk = tq
    rows = nh * tq
    hd = k_ref.shape[1]
    assert hd == LANES and rows % sub == 0 and tq % sub == 0
    i = pl.program_id(1)

    if diff:
        q = q_ref[...]
        lane = lax.broadcasted_iota(I32, q.shape, 1)
        zero = jnp.zeros_like(q)
        qs_sc[0:tq, :] = jnp.where(lane < A_QK_DIM, q, zero)
        qs_sc[tq:2 * tq, :] = jnp.where(lane >= A_QK_DIM, q, zero)
    else:
        for r in range(nh):
            qs_sc[r * tq:(r + 1) * tq, :] = q_ref[:, r * hd:(r + 1) * hd]
    m_sc[...] = jnp.full(m_sc.shape, -jnp.inf, F32)
    acc_sc[...] = jnp.zeros(acc_sc.shape, F32)

    tks = kb * tk
    nk = k_ref.shape[0] // tk
    last_step = i // kb

    def bias_index(j):
        return jnp.where(j > i, 3, jnp.clip(j - (i - 2), 0, 2))

    def stage_qk(t, slot):
        start = pl.multiple_of(jnp.minimum(t, last_step) * tks, tks)
        kblk = k_ref[pl.ds(start, tks), :]
        s_sc[slot] = lax.dot_general(qs_sc[...], kblk, (((1,), (1,)), ((), ())), preferred_element_type=F32)

    def stage_softmax(t, slot, biased):
        tc = jnp.minimum(t, last_step)
        for r in range(rows // sub):
            rs = slice(r * sub, (r + 1) * sub)
            tiles = []
            for b in range(kb):
                j = tc * kb + b
                s = s_sc[slot, rs, b * tk:(b + 1) * tk]
                if biased:
                    col = jnp.where(t > last_step, 3, bias_index(j))
                    s = s + nb_ref[0, col, rs, :]
                if mask_ref is not None:
                    off = (r * sub) % tq
                    s = s + mask_ref[jnp.minimum(j, nk - 1), off:off + sub, :].astype(F32)
                s = s.astype(BF16)
                tiles += [s[:, u * LANES:(u + 1) * LANES] for u in range(tk // LANES)]
            cmax = tiles[0]
            for u in tiles[1:]:
                cmax = jnp.maximum(cmax, u)
            m_old = m_sc[rs, :]
            m_new = jnp.maximum(m_old, jnp.max(cmax.astype(F32), axis=1, keepdims=True))
            al_sc[slot, rs, :] = jnp.exp2(m_old - m_new)
            m_b = m_new.astype(BF16)
            p_sc[slot, rs, :] = jnp.concatenate([jnp.exp2(u - m_b) for u in tiles], axis=1)
            m_sc[rs, :] = m_new

    def stage_pv(t, slot):
        start = pl.multiple_of(jnp.minimum(t, last_step) * tks, tks)
        vbe = jnp.concatenate([v_ref[pl.ds(start, tks), :], jnp.ones((tks, hd), BF16)], axis=1)
        pv = jnp.dot(p_sc[slot], vbe, preferred_element_type=F32)
        alpha = al_sc[slot]
        acc_sc[...] = jnp.concatenate([alpha, alpha], axis=1) * acc_sc[...] + pv

    stage_qk(0, 0)
    stage_qk(1, 1)
    stage_softmax(0, 0, True)

    nfar = jnp.maximum((i - 1) // kb, 0)
    npairs = jnp.maximum((nfar - 1) // 2, 0)

    def far_pair(u, carry):
        t = 2 * u
        stage_pv(t, 0)
        stage_softmax(t + 1, 1, False)
        stage_qk(t + 2, 0)
        stage_pv(t + 1, 1)
        stage_softmax(t + 2, 0, False)
        stage_qk(t + 3, 1)
        return carry

    lax.fori_loop(0, npairs, far_pair, 0)

    t0 = 2 * npairs
    stage_pv(t0, 0)
    stage_softmax(t0 + 1, 1, True)
    stage_qk(t0 + 2, 0)
    stage_pv(t0 + 1, 1)
    stage_softmax(t0 + 2, 0, True)
    stage_qk(t0 + 3, 1)
    stage_pv(t0 + 2, 0)
    stage_softmax(t0 + 3, 1, True)
    stage_pv(t0 + 3, 1)

    if diff:
        o0 = acc_sc[0:tq, 0:hd] / acc_sc[0:tq, hd:2 * hd]
        o1 = acc_sc[tq:2 * tq, 0:hd] / acc_sc[tq:2 * tq, hd:2 * hd]
        dlt = o0 - lam_ref[0] * o1
        ms = jnp.mean(dlt * dlt, axis=-1, keepdims=True)
        o_ref[...] = ((dlt * lax.rsqrt(ms + EPS) * g_ref[...]) * lam_scale).astype(o_ref.dtype)
    else:
        for r in range(nh):
            rs = slice(r * tq, (r + 1) * tq)
            o_ref[:, r * hd:(r + 1) * hd] = (acc_sc[rs, 0:hd] / acc_sc[rs, hd:2 * hd]).astype(o_ref.dtype)


def _rel_bucket(dist):
    n = jnp.maximum(dist, 0)
    max_exact = REL_BUCKETS // 2
    nf = jnp.maximum(n, 1).astype(F32)
    large = max_exact + (jnp.log(nf / max_exact) / math.log(REL_MAX_DIST / max_exact)
                         * (REL_BUCKETS - max_exact)).astype(I32)
    large = jnp.minimum(large, REL_BUCKETS - 1)
    return jnp.where(n < max_exact, n, large)


def _near_bias(rel_table, tq, groups, nh):
    assert tq >= LANES, "keys older than one block must all fall in the last bucket"
    r = jnp.arange(tq, dtype=I32)[:, None]
    c = jnp.arange(2 * tq, dtype=I32)[None, :]
    dist = r + tq - c
    rel = (rel_table - rel_table[REL_BUCKETS - 1][None, :]) * np.float32(LOG2E)
    onehot = jax.nn.one_hot(_rel_bucket(dist), REL_BUCKETS, dtype=F32)
    b = jnp.einsum("rcb,bh->hrc", onehot, rel, precision=lax.Precision.HIGHEST)
    b = jnp.where((dist >= 0)[None], b, MASKED)
    heads = b.shape[0]
    tiles = jnp.stack([jnp.zeros((heads, tq, tq), F32), b[:, :, :tq], b[:, :, tq:],
                       jnp.full((heads, tq, tq), MASKED, F32)], axis=1)
    tiles = tiles.reshape(groups, nh, 4, tq, tq).transpose(0, 2, 1, 3, 4)
    return tiles.reshape(groups, 4, nh * tq, tq)


def _diff_attention(qkv, near, lam, subln_g, lam_scale, tq):
    s = qkv.shape[0]
    hd = A_V_DIM
    nh = 2
    kb = 1
    kcol = A_WIDTH // hd
    return pl.pallas_call(
        functools.partial(_flash_kernel, nh=nh, tq=tq, kb=kb, sub=min(tq, ATTN_SUB_ROWS), diff=True,
                          lam_scale=lam_scale),
        grid=(A_HEADS, s // tq),
        in_specs=[pl.BlockSpec(memory_space=pltpu.SMEM),
                  pl.BlockSpec((tq, hd), lambda h, i: (i, h)),
                  pl.BlockSpec((s, hd), lambda h, i: (0, kcol + h)),
                  pl.BlockSpec((s, hd), lambda h, i: (0, 2 * kcol + h)),
                  pl.BlockSpec((1, 4, nh * tq, tq), lambda h, i: (h, 0, 0, 0)),
                  pl.BlockSpec((1, hd), lambda h, i: (0, 0))],
        out_specs=pl.BlockSpec((tq, hd), lambda h, i: (i, h)),
        out_shape=jax.ShapeDtypeStruct((s, A_WIDTH), BF16),
        scratch_shapes=[pltpu.VMEM((nh * tq, hd), BF16),
                        pltpu.VMEM((nh * tq, LANES), F32),
                        pltpu.VMEM((nh * tq, 2 * hd), F32),
                        pltpu.VMEM((2, nh * tq, kb * tq), F32),
                        pltpu.VMEM((2, nh * tq, kb * tq), BF16),
                        pltpu.VMEM((2, nh * tq, LANES), F32)],
        compiler_params=_cparams(("parallel", "arbitrary")),
        name="diff_attention",
    )(lam, qkv, qkv, qkv, near, subln_g)


def _masked_attention(qkv, near, mask, tq):
    s = qkv.shape[0]
    hd = C_HEAD_DIM
    nh = C_GROUP
    kcol = C_WIDTH // hd
    vcol = kcol + C_KV_HEADS
    nk = s // tq
    kb = 2 if nk % 2 == 0 else 1
    return pl.pallas_call(
        functools.partial(_flash_kernel, nh=nh, tq=tq, kb=kb, sub=min(tq, ATTN_SUB_ROWS), diff=False,
                          lam_scale=1.0),
        grid=(C_KV_HEADS, s // tq),
        in_specs=[pl.BlockSpec((tq, nh * hd), lambda g, i: (i, g)),
                  pl.BlockSpec((s, hd), lambda g, i: (0, kcol + g)),
                  pl.BlockSpec((s, hd), lambda g, i: (0, vcol + g)),
                  pl.BlockSpec((1, 4, nh * tq, tq), lambda g, i: (g, 0, 0, 0)),
                  pl.BlockSpec((nk, tq, tq), lambda g, i: (0, i, 0))],
        out_specs=pl.BlockSpec((tq, nh * hd), lambda g, i: (i, g)),
        out_shape=jax.ShapeDtypeStruct((s, C_WIDTH), BF16),
        scratch_shapes=[pltpu.VMEM((nh * tq, hd), BF16),
                        pltpu.VMEM((nh * tq, LANES), F32),
                        pltpu.VMEM((nh * tq, 2 * hd), F32),
                        pltpu.VMEM((2, nh * tq, kb * tq), F32),
                        pltpu.VMEM((2, nh * tq, kb * tq), BF16),
                        pltpu.VMEM((2, nh * tq, LANES), F32)],
        compiler_params=_cparams(("parallel", "arbitrary")),
        name="selected_attention",
    )(qkv, qkv, qkv, near, mask)


def _sg_kernel(zb_ref, lng_ref, lnb_ref, w_ref, bs_ref, o_ref, *, nchunk):
    zb = zb_ref[...]
    gl = zb * (0.5 * (1.0 + jnp.tanh(np.float32(np.sqrt(2.0 / np.pi)) * (zb + 0.044715 * (zb * zb * zb)))))
    u = gl[:, :B_WIDTH]
    z = gl[:, B_WIDTH:]
    mu = jnp.mean(z, axis=-1, keepdims=True)
    zc = z - mu
    var = jnp.mean(zc * zc, axis=-1, keepdims=True)
    zn = (zc * lax.rsqrt(var + EPS) * lng_ref[...] + lnb_ref[...]).astype(BF16)
    row = lax.broadcasted_iota(I32, (CHUNK, CHUNK), 0)
    col = lax.broadcasted_iota(I32, (CHUNK, CHUNK), 1)
    for g in range(B_GROUPS):
        w = jnp.where(row >= col, w_ref[g], 0.0).astype(BF16)
        bias = bs_ref[g]
        lo = g * B_GROUP_DIM
        for c in range(nchunk):
            r0 = c * CHUNK
            sz = jnp.dot(w, zn[r0:r0 + CHUNK, lo:lo + B_GROUP_DIM], preferred_element_type=F32) + bias
            o_ref[r0:r0 + CHUNK, lo:lo + B_GROUP_DIM] = (u[r0:r0 + CHUNK, lo:lo + B_GROUP_DIM] * sz).astype(o_ref.dtype)


def _spatial_gating(zb, ln_g, ln_b, w_s, b_s):
    s = zb.shape[0]
    t = _tile(s, 256)
    return pl.pallas_call(
        functools.partial(_sg_kernel, nchunk=t // CHUNK),
        grid=(s // t,),
        in_specs=[pl.BlockSpec((t, 2 * B_WIDTH), lambda i: (i, 0)),
                  pl.BlockSpec((1, B_WIDTH), lambda i: (0, 0)),
                  pl.BlockSpec((1, B_WIDTH), lambda i: (0, 0)),
                  pl.BlockSpec((B_GROUPS, CHUNK, CHUNK), lambda i: (0, 0, 0)),
                  pl.BlockSpec((B_GROUPS, CHUNK, 1), lambda i: (0, 0, 0))],
        out_specs=pl.BlockSpec((t, B_WIDTH), lambda i: (i, 0)),
        out_shape=jax.ShapeDtypeStruct((s, B_WIDTH), BF16),
        compiler_params=_cparams(("parallel",)),
        name="spatial_gating",
    )(zb, ln_g.reshape(1, B_WIDTH), ln_b.reshape(1, B_WIDTH), w_s, b_s.reshape(B_GROUPS, CHUNK, 1))


def _select_kernel(qi_ref, kt_ref, w_ref, o_ref, keys_sc, sc_sc, wb_sc, mx_sc, *, tkc, topk):
    tqi = IDX_QBLOCK
    i = pl.program_id(0)
    nk = o_ref.shape[0]
    nch = (i * tqi + tqi + tkc - 1) // tkc
    qpos = i * tqi + lax.broadcasted_iota(I32, (tqi, tkc), 0)
    kloc = lax.broadcasted_iota(I32, (tqi, tkc), 1)
    qrow = i * tqi + lax.broadcasted_iota(I32, (tqi, LANES), 0)
    klane = lax.broadcasted_iota(I32, (tqi, LANES), 1)

    wgt = w_ref[...] * np.float32(IDX_DIM ** -0.5)
    for h in range(IDX_HEADS):
        wb_sc[h] = jnp.broadcast_to(wgt[:, h:h + 1], (tqi, LANES))
    mx_sc[...] = jnp.full(mx_sc.shape, -jnp.inf, F32)

    def stage_dot(c, slot):
        sc_sc[slot] = jnp.dot(qi_ref[0], kt_ref[jnp.minimum(c, nch - 1)], preferred_element_type=F32)

    def stage_reduce(c, slot):
        c = jnp.minimum(c, nch - 1)
        for u in range(tkc // LANES):
            ls = slice(u * LANES, (u + 1) * LANES)
            acc = jnp.zeros((tqi, LANES), F32)
            for h in range(IDX_HEADS):
                acc += jnp.maximum(sc_sc[slot, h * tqi:(h + 1) * tqi, ls], 0.0) * wb_sc[h]
            acc = acc + 0.0
            bits = pltpu.bitcast(acc, I32)
            ordered = jnp.where(bits < 0, bits ^ jnp.int32(0x7FFFFFFF), bits)
            causal = c * tkc + u * LANES + klane <= qrow
            keys_sc[c, :, ls] = jnp.where(causal, ordered, jnp.int32(INT_MIN))
            mx_sc[...] = jnp.maximum(mx_sc[...], jnp.where(causal, acc, -jnp.inf))

    stage_dot(0, 0)

    def score_pair(u, carry):
        c = 2 * u
        stage_dot(c + 1, 1)
        stage_reduce(c, 0)
        stage_dot(c + 2, 0)
        stage_reduce(c + 1, 1)
        return carry

    lax.fori_loop(0, (nch + 1) // 2, score_pair, 0)

    def count_ge(cand):
        candb = jnp.broadcast_to(cand, (tqi, LANES))

        def count_body(c, cnt):
            kk = keys_sc[c]
            for u in range(tkc // LANES):
                cnt += jnp.where(kk[:, u * LANES:(u + 1) * LANES] >= candb, 1, 0)
            return cnt

        cnt = lax.fori_loop(0, nch, count_body, jnp.zeros((tqi, LANES), I32))
        return jnp.sum(cnt.astype(F32), axis=1, keepdims=True)

    want = np.float32(topk)
    fbits = pltpu.bitcast(jnp.max(mx_sc[...], axis=1, keepdims=True), I32)
    kmax = jnp.where(fbits < 0, fbits ^ jnp.int32(0x7FFFFFFF), fbits)
    few = i * tqi + lax.broadcasted_iota(I32, (tqi, 1), 0) + 1 <= topk
    lo0 = jnp.full((tqi, 1), INT_MIN, I32)
    hi0 = jnp.where(few, lo0 + 1, kmax + 1)

    def narrow(state, cand):
        lo, hi, active = state
        total = count_ge(cand)
        open_ = active > 0.0
        up = open_ & (total >= want)
        lo = jnp.where(up, cand, lo)
        hi = jnp.where(open_ & (~up), cand, hi)
        open_ = open_ & (~(up & (total == want))) & ((hi - lo) != 1)
        return lo, hi, jnp.where(open_, 1.0, 0.0)

    def midpoint(state):
        lo, hi, _ = state
        return lo + lax.shift_right_logical(hi - lo, jnp.int32(1))

    probe = jnp.maximum(kmax, jnp.int32(INT_MIN + SEARCH_PROBE_DROP + 1)) - jnp.int32(SEARCH_PROBE_DROP)
    state = narrow((lo0, hi0, jnp.where(few, 0.0, 1.0)), jnp.where(few, lo0, probe))

    def bisect_body(carry):
        state, _ = carry
        state = narrow(state, midpoint(state))
        state = narrow(state, midpoint(state))
        return state, jnp.sum(state[2])

    (thr, _, _), _ = lax.while_loop(lambda carry: carry[1] > 0.0, bisect_body, (state, jnp.float32(1.0)))
    thrb = jnp.broadcast_to(thr, (tqi, tkc))

    def mask_body(c, carry):
        sel = (keys_sc[c] >= thrb) & (c * tkc + kloc <= qpos)
        o_ref[c] = jnp.where(sel, 0.0, MASKED).astype(o_ref.dtype)
        return carry

    lax.fori_loop(0, nch, mask_body, 0)

    def fill_body(c, carry):
        o_ref[c] = jnp.full((tqi, tkc), MASKED, o_ref.dtype)
        return carry

    lax.fori_loop(nch, nk, fill_body, 0)


def _select_mask(qi_stack, kt, wi, tkc, topk):
    nq, rows, _ = qi_stack.shape
    nk = kt.shape[0]
    s = nq * IDX_QBLOCK
    return pl.pallas_call(
        functools.partial(_select_kernel, tkc=tkc, topk=topk),
        grid=(nq,),
        in_specs=[pl.BlockSpec((1, rows, IDX_DIM), lambda i: (i, 0, 0)),
                  pl.BlockSpec((nk, IDX_DIM, tkc), lambda i: (0, 0, 0)),
                  pl.BlockSpec((IDX_QBLOCK, IDX_HEADS), lambda i: (i, 0))],
        out_specs=pl.BlockSpec((nk, IDX_QBLOCK, tkc), lambda i: (0, i, 0)),
        out_shape=jax.ShapeDtypeStruct((nk, s, tkc), BF16),
        scratch_shapes=[pltpu.VMEM((nk, IDX_QBLOCK, tkc), I32),
                        pltpu.VMEM((2, rows, tkc), F32),
                        pltpu.VMEM((IDX_HEADS, IDX_QBLOCK, LANES), F32),
                        pltpu.VMEM((IDX_QBLOCK, LANES), F32)],
        compiler_params=_cparams(("parallel",)),
        name="indexer_select",
    )(qi_stack, kt, wi)


def _vec_pack(d, *rows):
    rows = [r.reshape(1, d).astype(F32) for r in rows]
    rows += [jnp.zeros((1, d), F32)] * (8 - len(rows))
    return jnp.concatenate(rows, axis=0)


def _pad_cols(w, n):
    return jnp.pad(w, ((0, 0), (0, n - w.shape[1])))


def kernel(x, c, norm_g, mod_w, mod_b, ffn_w1, ffn_w2, rel_table, ab_w_in, ab_w_out, diff_lam,
           diff_subln_g, sg_ln_g, sg_ln_b, sg_w, sg_b, dsa_w_in, dsa_w_out, final_g):
    batch, s, d = x.shape
    depth = norm_g.shape[0]
    assert batch == 1 and s % IDX_QBLOCK == 0

    tq_a = _tile(s, 512)
    tq_c = _tile(s, 256)
    topk = min(TOPK_MAX, s // 4)

    mod = _modulation(c, mod_w, mod_b).reshape(depth, 9, d)
    ffn_a, ffn_b = _ffn_weights(ffn_w1, ffn_w2)
    near_a = _near_bias(rel_table, tq_a, A_HEADS, 2)
    near_c = _near_bias(rel_table, tq_c, C_KV_HEADS, C_GROUP)

    xs = x.reshape(s, d)
    zeros_d = jnp.zeros((d,), F32)
    for li in range(depth):
        def vec(j, li=li):
            last = final_g if (li == depth - 1 and j == 2) else zeros_d
            return _vec_pack(d, norm_g[li, j], mod[li, 3 * j], mod[li, 3 * j + 1], mod[li, 3 * j + 2], last)

        def ffn(xs, j, k, final=False, li=li):
            return _ffn(xs, vec(j), ffn_a, ffn_b, 2 * li + k, final=final)

        xs = ffn(xs, 0, 0)

        v1 = vec(1)
        jj = li // 2
        if li % 2 == 0:
            w_in = ab_w_in[jj]
            w_qkv = jnp.concatenate([w_in[:, :A_WIDTH] * np.float32(A_QK_DIM ** -0.5 * LOG2E),
                                     w_in[:, A_WIDTH:3 * A_WIDTH]], axis=1).astype(BF16)
            w_zb = w_in[:, 3 * A_WIDTH:].astype(BF16)
            qkv = _proj(xs, v1, _col_blocks(w_qkv, PROJ_COLS), BF16)
            zb = _proj(xs, v1, _col_blocks(w_zb, PROJ_COLS), F32)
            lam_init = 0.8 - 0.6 * math.exp(-0.3 * li)
            lp = diff_lam[jj].astype(F32)
            lam = jnp.exp(jnp.sum(lp[0] * lp[1])) - jnp.exp(jnp.sum(lp[2] * lp[3])) + lam_init
            ya = _diff_attention(qkv, near_a, lam.reshape(1), diff_subln_g[jj].reshape(1, A_V_DIM),
                                 1.0 - lam_init, tq_a)
            yb = _spatial_gating(zb, sg_ln_g[jj], sg_ln_b[jj], sg_w[jj], sg_b[jj])
            w_out = ab_w_out[jj].astype(BF16)
            xs = _outproj(xs, v1, [ya, yb], [_col_blocks(w_out[:A_WIDTH], PROJ_COLS),
                                             _col_blocks(w_out[A_WIDTH:], PROJ_COLS)])
        else:
            w_in = dsa_w_in[jj]
            o_idx = C_WIDTH + 2 * C_KV_WIDTH
            o_ki = o_idx + IDX_HEADS * IDX_DIM
            w_main = jnp.concatenate([w_in[:, :C_WIDTH] * np.float32(C_HEAD_DIM ** -0.5 * LOG2E),
                                      w_in[:, C_WIDTH:o_ki]], axis=1).astype(BF16)
            w_kiw = _pad_cols(w_in[:, o_ki:], LANES).astype(BF16)
            main = _proj(xs, v1, _col_blocks(w_main, PROJ_COLS), BF16)
            kiw = _proj(xs, v1, _col_blocks(w_kiw, PROJ_COLS), F32)
            nq = s // IDX_QBLOCK
            qi = main[:, o_idx:o_ki].reshape(nq, IDX_QBLOCK, IDX_HEADS, IDX_DIM)
            qi = qi.transpose(0, 2, 1, 3).reshape(nq, IDX_HEADS * IDX_QBLOCK, IDX_DIM)
            kt = kiw[:, :IDX_DIM].astype(BF16).reshape(s // tq_c, tq_c, IDX_DIM).transpose(0, 2, 1)
            wi = kiw[:, IDX_DIM:IDX_DIM + IDX_HEADS]
            mask = _select_mask(qi, kt, wi, tq_c, topk)
            yc = _masked_attention(main, near_c, mask, tq_c)
            xs = _outproj(xs, v1, [yc], [_col_blocks(dsa_w_out[jj].astype(BF16), PROJ_COLS)])

        xs = ffn(xs, 2, 1, final=(li == depth - 1))
    return xs.reshape(batch, s, d)
```

```python
import functools
import math

import jax
import jax.numpy as jnp
import numpy as np
from jax import lax
from jax.experimental import pallas as pl
from jax.experimental.pallas import tpu as pltpu

F32 = jnp.float32
BF16 = jnp.bfloat16
I32 = jnp.int32

EPS = 1e-6
MASKED = -1e30
LANES = 128
INT_MIN = -(2 ** 31)
LOG2E = math.log2(math.e)
NORM_ROWS = 16
ATTN_SUB_ROWS = 128

A_HEADS = 8
A_QK_DIM = 64
A_V_DIM = 128
A_WIDTH = A_HEADS * A_V_DIM
B_GROUPS = 8
B_GROUP_DIM = 128
B_WIDTH = B_GROUPS * B_GROUP_DIM
CHUNK = 128
C_HEADS = 16
C_KV_HEADS = 4
C_GROUP = C_HEADS // C_KV_HEADS
C_HEAD_DIM = 128
C_WIDTH = C_HEADS * C_HEAD_DIM
C_KV_WIDTH = C_KV_HEADS * C_HEAD_DIM
IDX_HEADS = 16
IDX_DIM = 64
TOPK_MAX = 256
REL_BUCKETS = 32
REL_MAX_DIST = 128
IDX_QBLOCK = 128
SEARCH_PROBE_DROP = 2 ** 24

VMEM_LIMIT = 56 * 1024 * 1024


def _cparams(sem):
    return pltpu.CompilerParams(dimension_semantics=sem, vmem_limit_bytes=VMEM_LIMIT)


FFN_ROWS, FFN_COLS = 512, 512
PROJ_ROWS, PROJ_COLS = 1024, 512


def _col_blocks(w, tn):
    k, n = w.shape
    tn = _tile(n, tn)
    return w.reshape(k, n // tn, tn).transpose(1, 0, 2)


def _tile(n, want):
    if n <= want:
        return n
    t = want
    while n % t:
        t //= 2
    return t


def _mod_kernel(c_ref, w_ref, b_ref, o_ref):
    c = c_ref[...]
    cs = c * (1.0 / (1.0 + jnp.exp(-c)))
    o_ref[0] = jnp.sum(cs * w_ref[0], axis=0, keepdims=True) + b_ref[0]


def _modulation(c, mod_w, mod_b):
    depth, d, n = mod_w.shape
    tn = _tile(n, 1024)
    out = pl.pallas_call(
        _mod_kernel,
        grid=(depth, n // tn),
        in_specs=[pl.BlockSpec((d, 1), lambda l, j: (0, 0)),
                  pl.BlockSpec((1, d, tn), lambda l, j: (l, 0, j)),
                  pl.BlockSpec((1, 1, tn), lambda l, j: (l, 0, j))],
        out_specs=pl.BlockSpec((1, 1, tn), lambda l, j: (l, 0, j)),
        out_shape=jax.ShapeDtypeStruct((depth, 1, n), F32),
        compiler_params=_cparams(("arbitrary", "arbitrary")),
        name="adaln_mod",
    )(c.reshape(d, 1), mod_w, mod_b.reshape(depth, 1, n))
    return out.reshape(depth, n)


def _prenorm_into(x_ref, vec_ref, hn_ref, inv_ref):
    rows, d = x_ref.shape
    gain = vec_ref[0:1, :] * (1.0 + vec_ref[2:3, :])
    shift = vec_ref[1:2, :]

    def scale_body(r, carry):
        rs = pl.ds(pl.multiple_of(r * NORM_ROWS, NORM_ROWS), NORM_ROWS)
        x = x_ref[rs, :]
        ms = jnp.sum(x * x, axis=-1, keepdims=True) * np.float32(1.0 / d)
        inv_ref[rs, :] = jnp.broadcast_to(lax.rsqrt(ms + EPS), (NORM_ROWS, LANES))
        return carry

    lax.fori_loop(0, rows // NORM_ROWS, scale_body, 0, unroll=8)

    def apply_body(r, carry):
        rs = pl.ds(pl.multiple_of(r * NORM_ROWS, NORM_ROWS), NORM_ROWS)
        inv = inv_ref[rs, :]
        inv = jnp.concatenate([inv] * (d // LANES), axis=1)
        hn_ref[rs, :] = (x_ref[rs, :] * inv * gain + shift).astype(hn_ref.dtype)
        return carry

    lax.fori_loop(0, rows // NORM_ROWS, apply_body, 0, unroll=2)


def _ffn_kernel(x_ref, vec_ref, w1g_ref, w1u_ref, w2_ref, o_ref, hn_sc, inv_sc, *, nf, final):
    f = pl.program_id(1)

    @pl.when(f == 0)
    def _():
        _prenorm_into(x_ref, vec_ref, hn_sc, inv_sc)
        o_ref[...] = jnp.zeros_like(o_ref)

    hn = hn_sc[...]
    g = jnp.dot(hn, w1g_ref[...], preferred_element_type=F32)
    u = jnp.dot(hn, w1u_ref[...], preferred_element_type=F32)
    a = (g * (1.0 / (1.0 + jnp.exp(-g))) * u).astype(BF16)
    o_ref[...] += jnp.dot(a, w2_ref[...], preferred_element_type=F32)

    @pl.when(f == nf - 1)
    def _():
        y = x_ref[...] + 0.5 * (1.0 + vec_ref[3:4, :]) * o_ref[...]
        if final:
            ms = jnp.mean(y * y, axis=-1, keepdims=True)
            y = y * lax.rsqrt(ms + EPS) * vec_ref[4:5, :]
        o_ref[...] = y


def _ffn_weights(ffn_w1, ffn_w2):
    depth, two, d, _ = ffn_w1.shape
    d_ff = ffn_w2.shape[2]
    tf = FFN_COLS if d_ff > FFN_COLS else d_ff
    nf = -(-d_ff // tf)
    pad = nf * tf - d_ff
    w1 = jnp.pad(ffn_w1.reshape(depth * two, d, 2, d_ff), ((0, 0), (0, 0), (0, 0), (0, pad)))
    w1 = w1.reshape(depth * two, d, 2, nf, tf).transpose(0, 2, 3, 1, 4).astype(BF16)
    w2 = jnp.pad(ffn_w2.reshape(depth * two, d_ff, d), ((0, 0), (0, pad), (0, 0))).astype(BF16)
    return w1, w2


def _ffn(x, vec, w1, w2, step, *, final):
    s, d = x.shape
    _, _, nf, _, tf = w1.shape
    tm = _tile(s, FFN_ROWS)
    return pl.pallas_call(
        functools.partial(_ffn_kernel, nf=nf, final=final),
        grid=(s // tm, nf),
        in_specs=[pl.BlockSpec((tm, d), lambda i, f: (i, 0)),
                  pl.BlockSpec((8, d), lambda i, f: (0, 0)),
                  pl.BlockSpec((None, None, None, d, tf), lambda i, f: (step, 0, f, 0, 0)),
                  pl.BlockSpec((None, None, None, d, tf), lambda i, f: (step, 1, f, 0, 0)),
                  pl.BlockSpec((None, tf, d), lambda i, f: (step, f, 0))],
        out_specs=pl.BlockSpec((tm, d), lambda i, f: (i, 0)),
        out_shape=jax.ShapeDtypeStruct((s, d), F32),
        scratch_shapes=[pltpu.VMEM((tm, d), BF16), pltpu.VMEM((tm, LANES), F32)],
        compiler_params=_cparams(("parallel", "arbitrary")),
        name="swiglu_halfstep",
    )(x, vec, w1, w1, w2)


def _proj_kernel(x_ref, vec_ref, w_ref, o_ref, hn_sc, inv_sc):
    @pl.when(pl.program_id(1) == 0)
    def _():
        _prenorm_into(x_ref, vec_ref, hn_sc, inv_sc)

    o_ref[...] = jnp.dot(hn_sc[...], w_ref[0], preferred_element_type=F32).astype(o_ref.dtype)


def _proj(x, vec, w, out_dtype):
    s, d = x.shape
    nn, _, tn = w.shape
    n = nn * tn
    tm = _tile(s, PROJ_ROWS)
    return pl.pallas_call(
        _proj_kernel,
        grid=(s // tm, nn),
        in_specs=[pl.BlockSpec((tm, d), lambda i, j: (i, 0)),
                  pl.BlockSpec((8, d), lambda i, j: (0, 0)),
                  pl.BlockSpec((1, d, tn), lambda i, j: (j, 0, 0))],
        out_specs=pl.BlockSpec((tm, tn), lambda i, j: (i, j)),
        out_shape=jax.ShapeDtypeStruct((s, n), out_dtype),
        scratch_shapes=[pltpu.VMEM((tm, d), BF16), pltpu.VMEM((tm, LANES), F32)],
        compiler_params=_cparams(("parallel", "arbitrary")),
        name="norm_mod_proj",
    )(x, vec, w)


def _outproj_kernel(*refs, n_in):
    x_ref, vec_ref = refs[0], refs[1]
    lhs = refs[2:2 + n_in]
    ws = refs[2 + n_in:2 + 2 * n_in]
    o_ref = refs[2 + 2 * n_in]
    acc = jnp.dot(lhs[0][...], ws[0][0], preferred_element_type=F32)
    for a, w in zip(lhs[1:], ws[1:]):
        acc += jnp.dot(a[...], w[0], preferred_element_type=F32)
    o_ref[...] = x_ref[...] + (1.0 + vec_ref[3:4, :]) * acc


def _outproj(x, vec, lhs, ws):
    s, d = x.shape
    tm = _tile(s, PROJ_ROWS)
    tn = ws[0].shape[2]
    n_in = len(lhs)
    in_specs = [pl.BlockSpec((tm, tn), lambda i, j: (i, j)),
                pl.BlockSpec((8, tn), lambda i, j: (0, j))]
    in_specs += [pl.BlockSpec((tm, a.shape[1]), lambda i, j: (i, 0)) for a in lhs]
    in_specs += [pl.BlockSpec((1, w.shape[1], tn), lambda i, j: (j, 0, 0)) for w in ws]
    return pl.pallas_call(
        functools.partial(_outproj_kernel, n_in=n_in),
        grid=(s // tm, d // tn),
        in_specs=in_specs,
        out_specs=pl.BlockSpec((tm, tn), lambda i, j: (i, j)),
        out_shape=jax.ShapeDtypeStruct((s, d), F32),
        compiler_params=_cparams(("parallel", "arbitrary")),
        name="outproj_residual",
    )(x, vec, *lhs, *ws)


def _flash_kernel(*refs, nh, tq, kb, sub, diff, lam_scale):
    if diff:
        lam_ref, q_ref, k_ref, v_ref, nb_ref, g_ref, o_ref, qs_sc, m_sc, acc_sc, s_sc, p_sc, al_sc = refs
        mask_ref = None
    else:
        q_ref, k_ref, v_ref, nb_ref, mask_ref, o_ref, qs_sc, m_sc, acc_sc, s_sc, p_sc, al_sc = refs
    tk = tq
    rows = nh * tq
    hd = k_ref.shape[1]
    assert hd == LANES and rows % sub == 0 and tq % sub == 0
    i = pl.program_id(1)

    if diff:
        q = q_ref[...]
        lane = lax.broadcasted_iota(I32, q.shape, 1)
        zero = jnp.zeros_like(q)
        qs_sc[0:tq, :] = jnp.where(lane < A_QK_DIM, q, zero)
        qs_sc[tq:2 * tq, :] = jnp.where(lane >= A_QK_DIM, q, zero)
    else:
        for r in range(nh):
            qs_sc[r * tq:(r + 1) * tq, :] = q_ref[:, r * hd:(r + 1) * hd]
    m_sc[...] = jnp.full(m_sc.shape, -jnp.inf, F32)
    acc_sc[...] = jnp.zeros(acc_sc.shape, F32)

    tks = kb * tk
    nk = k_ref.shape[0] // tk
    last_step = i // kb

    def bias_index(j):
        return jnp.where(j > i, 3, jnp.clip(j - (i - 2), 0, 2))

    def stage_qk(t, slot):
        start = pl.multiple_of(jnp.minimum(t, last_step) * tks, tks)
        kblk = k_ref[pl.ds(start, tks), :]
        s_sc[slot] = lax.dot_general(qs_sc[...], kblk, (((1,), (1,)), ((), ())), preferred_element_type=F32)

    def stage_softmax(t, slot, biased):
        tc = jnp.minimum(t, last_step)
        for r in range(rows // sub):
            rs = slice(r * sub, (r + 1) * sub)
            tiles = []
            for b in range(kb):
                j = tc * kb + b
                s = s_sc[slot, rs, b * tk:(b + 1) * tk]
                if biased:
                    col = jnp.where(t > last_step, 3, bias_index(j))
                    s = s + nb_ref[0, col, rs, :]
                s = s.astype(BF16)
                if mask_ref is not None:
                    off = (r * sub) % tq
                    s = s + mask_ref[jnp.minimum(j, nk - 1), off:off + sub, :]
                tiles += [s[:, u * LANES:(u + 1) * LANES] for u in range(tk // LANES)]
            cmax = tiles[0]
            for u in tiles[1:]:
                cmax = jnp.maximum(cmax, u)
            m_old = m_sc[rs, :]
            m_new = jnp.maximum(m_old, jnp.max(cmax.astype(F32), axis=1, keepdims=True))
            al_sc[slot, rs, :] = jnp.exp2(m_old - m_new)
            m_b = m_new.astype(BF16)
            p_sc[slot, rs, :] = jnp.concatenate([jnp.exp2(u - m_b) for u in tiles], axis=1)
            m_sc[rs, :] = m_new

    def stage_pv(t, slot):
        start = pl.multiple_of(jnp.minimum(t, last_step) * tks, tks)
        vbe = jnp.concatenate([v_ref[pl.ds(start, tks), :], jnp.ones((tks, hd), BF16)], axis=1)
        pv = jnp.dot(p_sc[slot], vbe, preferred_element_type=F32)
        alpha = al_sc[slot]
        acc_sc[...] = jnp.concatenate([alpha, alpha], axis=1) * acc_sc[...] + pv

    stage_qk(0, 0)
    stage_qk(1, 1)
    stage_softmax(0, 0, True)

    nfar = jnp.maximum((i - 1) // kb, 0)
    npairs = jnp.maximum((nfar - 1) // 2, 0)

    def far_pair(u, carry):
        t = 2 * u
        stage_pv(t, 0)
        stage_softmax(t + 1, 1, False)
        stage_qk(t + 2, 0)
        stage_pv(t + 1, 1)
        stage_softmax(t + 2, 0, False)
        stage_qk(t + 3, 1)
        return carry

    lax.fori_loop(0, npairs, far_pair, 0)

    t0 = 2 * npairs
    stage_pv(t0, 0)
    stage_softmax(t0 + 1, 1, True)
    stage_qk(t0 + 2, 0)
    stage_pv(t0 + 1, 1)
    stage_softmax(t0 + 2, 0, True)
    stage_qk(t0 + 3, 1)
    stage_pv(t0 + 2, 0)
    stage_softmax(t0 + 3, 1, True)
    stage_pv(t0 + 3, 1)

    if diff:
        o0 = acc_sc[0:tq, 0:hd] / acc_sc[0:tq, hd:2 * hd]
        o1 = acc_sc[tq:2 * tq, 0:hd] / acc_sc[tq:2 * tq, hd:2 * hd]
        dlt = o0 - lam_ref[0] * o1
        ms = jnp.mean(dlt * dlt, axis=-1, keepdims=True)
        o_ref[...] = ((dlt * lax.rsqrt(ms + EPS) * g_ref[...]) * lam_scale).astype(o_ref.dtype)
    else:
        for r in range(nh):
            rs = slice(r * tq, (r + 1) * tq)
            o_ref[:, r * hd:(r + 1) * hd] = (acc_sc[rs, 0:hd] / acc_sc[rs, hd:2 * hd]).astype(o_ref.dtype)


def _rel_bucket(dist):
    n = jnp.maximum(dist, 0)
    max_exact = REL_BUCKETS // 2
    nf = jnp.maximum(n, 1).astype(F32)
    large = max_exact + (jnp.log(nf / max_exact) / math.log(REL_MAX_DIST / max_exact)
                         * (REL_BUCKETS - max_exact)).astype(I32)
    large = jnp.minimum(large, REL_BUCKETS - 1)
    return jnp.where(n < max_exact, n, large)


def _near_bias(rel_table, tq, groups, nh):
    assert tq >= LANES, "keys older than one block must all fall in the last bucket"
    r = jnp.arange(tq, dtype=I32)[:, None]
    c = jnp.arange(2 * tq, dtype=I32)[None, :]
    dist = r + tq - c
    rel = (rel_table - rel_table[REL_BUCKETS - 1][None, :]) * np.float32(LOG2E)
    onehot = jax.nn.one_hot(_rel_bucket(dist), REL_BUCKETS, dtype=F32)
    b = jnp.einsum("rcb,bh->hrc", onehot, rel, precision=lax.Precision.HIGHEST)
    b = jnp.where((dist >= 0)[None], b, MASKED)
    heads = b.shape[0]
    tiles = jnp.stack([jnp.zeros((heads, tq, tq), F32), b[:, :, :tq], b[:, :, tq:],
                       jnp.full((heads, tq, tq), MASKED, F32)], axis=1)
    tiles = tiles.reshape(groups, nh, 4, tq, tq).transpose(0, 2, 1, 3, 4)
    return tiles.reshape(groups, 4, nh * tq, tq)


def _diff_attention(qkv, near, lam, subln_g, lam_scale, tq):
    s = qkv.shape[0]
    hd = A_V_DIM
    nh = 2
    kb = 1
    kcol = A_WIDTH // hd
    return pl.pallas_call(
        functools.partial(_flash_kernel, nh=nh, tq=tq, kb=kb, sub=min(tq, ATTN_SUB_ROWS), diff=True,
                          lam_scale=lam_scale),
        grid=(A_HEADS, s // tq),
        in_specs=[pl.BlockSpec(memory_space=pltpu.SMEM),
                  pl.BlockSpec((tq, hd), lambda h, i: (i, h)),
                  pl.BlockSpec((s, hd), lambda h, i: (0, kcol + h)),
                  pl.BlockSpec((s, hd), lambda h, i: (0, 2 * kcol + h)),
                  pl.BlockSpec((1, 4, nh * tq, tq), lambda h, i: (h, 0, 0, 0)),
                  pl.BlockSpec((1, hd), lambda h, i: (0, 0))],
        out_specs=pl.BlockSpec((tq, hd), lambda h, i: (i, h)),
        out_shape=jax.ShapeDtypeStruct((s, A_WIDTH), BF16),
        scratch_shapes=[pltpu.VMEM((nh * tq, hd), BF16),
                        pltpu.VMEM((nh * tq, LANES), F32),
                        pltpu.VMEM((nh * tq, 2 * hd), F32),
                        pltpu.VMEM((2, nh * tq, kb * tq), F32),
                        pltpu.VMEM((2, nh * tq, kb * tq), BF16),
                        pltpu.VMEM((2, nh * tq, LANES), F32)],
        compiler_params=_cparams(("parallel", "arbitrary")),
        name="diff_attention",
    )(lam, qkv, qkv, qkv, near, subln_g)


def _masked_attention(qkv, near, mask, tq):
    s = qkv.shape[0]
    hd = C_HEAD_DIM
    nh = C_GROUP
    kcol = C_WIDTH // hd
    vcol = kcol + C_KV_HEADS
    nk = s // tq
    kb = 2 if nk % 2 == 0 else 1
    return pl.pallas_call(
        functools.partial(_flash_kernel, nh=nh, tq=tq, kb=kb, sub=min(tq, ATTN_SUB_ROWS), diff=False,
                          lam_scale=1.0),
        grid=(C_KV_HEADS, s // tq),
        in_specs=[pl.BlockSpec((tq, nh * hd), lambda g, i: (i, g)),
                  pl.BlockSpec((s, hd), lambda g, i: (0, kcol + g)),
                  pl.BlockSpec((s, hd), lambda g, i: (0, vcol + g)),
                  pl.BlockSpec((1, 4, nh * tq, tq), lambda g, i: (g, 0, 0, 0)),
                  pl.BlockSpec((nk, tq, tq), lambda g, i: (0, i, 0))],
        out_specs=pl.BlockSpec((tq, nh * hd), lambda g, i: (i, g)),
        out_shape=jax.ShapeDtypeStruct((s, C_WIDTH), BF16),
        scratch_shapes=[pltpu.VMEM((nh * tq, hd), BF16),
                        pltpu.VMEM((nh * tq, LANES), F32),
                        pltpu.VMEM((nh * tq, 2 * hd), F32),
                        pltpu.VMEM((2, nh * tq, kb * tq), F32),
                        pltpu.VMEM((2, nh * tq, kb * tq), BF16),
                        pltpu.VMEM((2, nh * tq, LANES), F32)],
        compiler_params=_cparams(("parallel", "arbitrary")),
        name="selected_attention",
    )(qkv, qkv, qkv, near, mask)


def _sg_kernel(zb_ref, lng_ref, lnb_ref, w_ref, bs_ref, o_ref, *, nchunk):
    zb = zb_ref[...]
    gl = zb * (0.5 * (1.0 + jnp.tanh(np.float32(np.sqrt(2.0 / np.pi)) * (zb + 0.044715 * (zb * zb * zb)))))
    u = gl[:, :B_WIDTH]
    z = gl[:, B_WIDTH:]
    mu = jnp.mean(z, axis=-1, keepdims=True)
    zc = z - mu
    var = jnp.mean(zc * zc, axis=-1, keepdims=True)
    zn = (zc * lax.rsqrt(var + EPS) * lng_ref[...] + lnb_ref[...]).astype(BF16)
    row = lax.broadcasted_iota(I32, (CHUNK, CHUNK), 0)
    col = lax.broadcasted_iota(I32, (CHUNK, CHUNK), 1)
    for g in range(B_GROUPS):
        w = jnp.where(row >= col, w_ref[g], 0.0).astype(BF16)
        bias = bs_ref[g]
        lo = g * B_GROUP_DIM
        for c in range(nchunk):
            r0 = c * CHUNK
            sz = jnp.dot(w, zn[r0:r0 + CHUNK, lo:lo + B_GROUP_DIM], preferred_element_type=F32) + bias
            o_ref[r0:r0 + CHUNK, lo:lo + B_GROUP_DIM] = (u[r0:r0 + CHUNK, lo:lo + B_GROUP_DIM] * sz).astype(o_ref.dtype)


def _spatial_gating(zb, ln_g, ln_b, w_s, b_s):
    s = zb.shape[0]
    t = _tile(s, 256)
    return pl.pallas_call(
        functools.partial(_sg_kernel, nchunk=t // CHUNK),
        grid=(s // t,),
        in_specs=[pl.BlockSpec((t, 2 * B_WIDTH), lambda i: (i, 0)),
                  pl.BlockSpec((1, B_WIDTH), lambda i: (0, 0)),
                  pl.BlockSpec((1, B_WIDTH), lambda i: (0, 0)),
                  pl.BlockSpec((B_GROUPS, CHUNK, CHUNK), lambda i: (0, 0, 0)),
                  pl.BlockSpec((B_GROUPS, CHUNK, 1), lambda i: (0, 0, 0))],
        out_specs=pl.BlockSpec((t, B_WIDTH), lambda i: (i, 0)),
        out_shape=jax.ShapeDtypeStruct((s, B_WIDTH), BF16),
        compiler_params=_cparams(("parallel",)),
        name="spatial_gating",
    )(zb, ln_g.reshape(1, B_WIDTH), ln_b.reshape(1, B_WIDTH), w_s, b_s.reshape(B_GROUPS, CHUNK, 1))


def _select_kernel(qi_ref, kt_ref, w_ref, o_ref, keys_sc, sc_sc, wb_sc, mx_sc, *, tkc, topk):
    tqi = IDX_QBLOCK
    i = pl.program_id(0)
    nk = o_ref.shape[0]
    nch = (i * tqi + tqi + tkc - 1) // tkc
    qpos = i * tqi + lax.broadcasted_iota(I32, (tqi, tkc), 0)
    kloc = lax.broadcasted_iota(I32, (tqi, tkc), 1)
    qrow = i * tqi + lax.broadcasted_iota(I32, (tqi, LANES), 0)
    klane = lax.broadcasted_iota(I32, (tqi, LANES), 1)

    wgt = w_ref[...] * np.float32(IDX_DIM ** -0.5)
    for h in range(IDX_HEADS):
        wb_sc[h] = jnp.broadcast_to(wgt[:, h:h + 1], (tqi, LANES))
    mx_sc[...] = jnp.full(mx_sc.shape, -jnp.inf, F32)

    def stage_dot(c, slot):
        sc_sc[slot] = jnp.dot(qi_ref[0], kt_ref[jnp.minimum(c, nch - 1)], preferred_element_type=F32)

    def stage_reduce(c, slot):
        c = jnp.minimum(c, nch - 1)
        for u in range(tkc // LANES):
            ls = slice(u * LANES, (u + 1) * LANES)
            acc = jnp.zeros((tqi, LANES), F32)
            for h in range(IDX_HEADS):
                acc += jnp.maximum(sc_sc[slot, h * tqi:(h + 1) * tqi, ls], 0.0) * wb_sc[h]
            acc = acc + 0.0
            bits = pltpu.bitcast(acc, I32)
            ordered = jnp.where(bits < 0, bits ^ jnp.int32(0x7FFFFFFF), bits)
            causal = c * tkc + u * LANES + klane <= qrow
            keys_sc[c, :, ls] = jnp.where(causal, ordered, jnp.int32(INT_MIN))
            mx_sc[...] = jnp.maximum(mx_sc[...], jnp.where(causal, acc, -jnp.inf))

    stage_dot(0, 0)

    def score_pair(u, carry):
        c = 2 * u
        stage_dot(c + 1, 1)
        stage_reduce(c, 0)
        stage_dot(c + 2, 0)
        stage_reduce(c + 1, 1)
        return carry

    lax.fori_loop(0, (nch + 1) // 2, score_pair, 0)

    def count_ge(cand):
        candb = jnp.broadcast_to(cand, (tqi, LANES))

        def count_body(c, cnt):
            kk = keys_sc[c]
            for u in range(tkc // LANES):
                cnt += jnp.where(kk[:, u * LANES:(u + 1) * LANES] >= candb, 1, 0)
            return cnt

        cnt = lax.fori_loop(0, nch, count_body, jnp.zeros((tqi, LANES), I32))
        return jnp.sum(cnt.astype(F32), axis=1, keepdims=True)

    want = np.float32(topk)
    fbits = pltpu.bitcast(jnp.max(mx_sc[...], axis=1, keepdims=True), I32)
    kmax = jnp.where(fbits < 0, fbits ^ jnp.int32(0x7FFFFFFF), fbits)
    few = i * tqi + lax.broadcasted_iota(I32, (tqi, 1), 0) + 1 <= topk
    lo0 = jnp.full((tqi, 1), INT_MIN, I32)
    hi0 = jnp.where(few, lo0 + 1, kmax + 1)

    def narrow(state, cand):
        lo, hi, active = state
        total = count_ge(cand)
        open_ = active > 0.0
        up = open_ & (total >= want)
        lo = jnp.where(up, cand, lo)
        hi = jnp.where(open_ & (~up), cand, hi)
        open_ = open_ & (~(up & (total == want))) & ((hi - lo) != 1)
        return lo, hi, jnp.where(open_, 1.0, 0.0)

    def midpoint(state):
        lo, hi, _ = state
        return lo + lax.shift_right_logical(hi - lo, jnp.int32(1))

    probe = jnp.maximum(kmax, jnp.int32(INT_MIN + SEARCH_PROBE_DROP + 1)) - jnp.int32(SEARCH_PROBE_DROP)
    state = narrow((lo0, hi0, jnp.where(few, 0.0, 1.0)), jnp.where(few, lo0, probe))

    def bisect_body(carry):
        state, _ = carry
        state = narrow(state, midpoint(state))
        state = narrow(state, midpoint(state))
        return state, jnp.sum(state[2])

    (thr, _, _), _ = lax.while_loop(lambda carry: carry[1] > 0.0, bisect_body, (state, jnp.float32(1.0)))
    thrb = jnp.broadcast_to(thr, (tqi, tkc))

    def mask_body(c, carry):
        sel = (keys_sc[c] >= thrb) & (c * tkc + kloc <= qpos)
        o_ref[c] = jnp.where(sel, 0.0, MASKED).astype(o_ref.dtype)
        return carry

    lax.fori_loop(0, nch, mask_body, 0)

    def fill_body(c, carry):
        o_ref[c] = jnp.full((tqi, tkc), MASKED, o_ref.dtype)
        return carry

    lax.fori_loop(nch, nk, fill_body, 0)


def _select_mask(qi_stack, kt, wi, tkc, topk):
    nq, rows, _ = qi_stack.shape
    nk = kt.shape[0]
    s = nq * IDX_QBLOCK
    return pl.pallas_call(
        functools.partial(_select_kernel, tkc=tkc, topk=topk),
        grid=(nq,),
        in_specs=[pl.BlockSpec((1, rows, IDX_DIM), lambda i: (i, 0, 0)),
                  pl.BlockSpec((nk, IDX_DIM, tkc), lambda i: (0, 0, 0)),
                  pl.BlockSpec((IDX_QBLOCK, IDX_HEADS), lambda i: (i, 0))],
        out_specs=pl.BlockSpec((nk, IDX_QBLOCK, tkc), lambda i: (0, i, 0)),
        out_shape=jax.ShapeDtypeStruct((nk, s, tkc), BF16),
        scratch_shapes=[pltpu.VMEM((nk, IDX_QBLOCK, tkc), I32),
                        pltpu.VMEM((2, rows, tkc), F32),
                        pltpu.VMEM((IDX_HEADS, IDX_QBLOCK, LANES), F32),
                        pltpu.VMEM((IDX_QBLOCK, LANES), F32)],
        compiler_params=_cparams(("parallel",)),
        name="indexer_select",
    )(qi_stack, kt, wi)


def _vec_pack(d, *rows):
    rows = [r.reshape(1, d).astype(F32) for r in rows]
    rows += [jnp.zeros((1, d), F32)] * (8 - len(rows))
    return jnp.concatenate(rows, axis=0)


def _pad_cols(w, n):
    return jnp.pad(w, ((0, 0), (0, n - w.shape[1])))


def kernel(x, c, norm_g, mod_w, mod_b, ffn_w1, ffn_w2, rel_table, ab_w_in, ab_w_out, diff_lam,
           diff_subln_g, sg_ln_g, sg_ln_b, sg_w, sg_b, dsa_w_in, dsa_w_out, final_g):
    batch, s, d = x.shape
    depth = norm_g.shape[0]
    assert batch == 1 and s % IDX_QBLOCK == 0

    tq_a = _tile(s, 512)
    tq_c = _tile(s, 256)
    topk = min(TOPK_MAX, s // 4)

    mod = _modulation(c, mod_w, mod_b).reshape(depth, 9, d)
    ffn_a, ffn_b = _ffn_weights(ffn_w1, ffn_w2)
    near_a = _near_bias(rel_table, tq_a, A_HEADS, 2)
    near_c = _near_bias(rel_table, tq_c, C_KV_HEADS, C_GROUP)

    xs = x.reshape(s, d)
    zeros_d = jnp.zeros((d,), F32)
    for li in range(depth):
        def vec(j, li=li):
            last = final_g if (li == depth - 1 and j == 2) else zeros_d
            return _vec_pack(d, norm_g[li, j], mod[li, 3 * j], mod[li, 3 * j + 1], mod[li, 3 * j + 2], last)

        def ffn(xs, j, k, final=False, li=li):
            return _ffn(xs, vec(j), ffn_a, ffn_b, 2 * li + k, final=final)

        xs = ffn(xs, 0, 0)

        v1 = vec(1)
        jj = li // 2
        if li % 2 == 0:
            w_in = ab_w_in[jj]
            w_qkv = jnp.concatenate([w_in[:, :A_WIDTH] * np.float32(A_QK_DIM ** -0.5 * LOG2E),
                                     w_in[:, A_WIDTH:3 * A_WIDTH]], axis=1).astype(BF16)
            w_zb = w_in[:, 3 * A_WIDTH:].astype(BF16)
            qkv = _proj(xs, v1, _col_blocks(w_qkv, PROJ_COLS), BF16)
            zb = _proj(xs, v1, _col_blocks(w_zb, PROJ_COLS), F32)
            lam_init = 0.8 - 0.6 * math.exp(-0.3 * li)
            lp = diff_lam[jj].astype(F32)
            lam = jnp.exp(jnp.sum(lp[0] * lp[1])) - jnp.exp(jnp.sum(lp[2] * lp[3])) + lam_init
            ya = _diff_attention(qkv, near_a, lam.reshape(1), diff_subln_g[jj].reshape(1, A_V_DIM),
                                 1.0 - lam_init, tq_a)
            yb = _spatial_gating(zb, sg_ln_g[jj], sg_ln_b[jj], sg_w[jj], sg_b[jj])
            w_out = ab_w_out[jj].astype(BF16)
            xs = _outproj(xs, v1, [ya, yb], [_col_blocks(w_out[:A_WIDTH], PROJ_COLS),
                                             _col_blocks(w_out[A_WIDTH:], PROJ_COLS)])
        else:
            w_in = dsa_w_in[jj]
            o_idx = C_WIDTH + 2 * C_KV_WIDTH
            o_ki = o_idx + IDX_HEADS * IDX_DIM
            w_main = jnp.concatenate([w_in[:, :C_WIDTH] * np.float32(C_HEAD_DIM ** -0.5 * LOG2E),
                                      w_in[:, C_WIDTH:o_ki]], axis=1).astype(BF16)
            w_kiw = _pad_cols(w_in[:, o_ki:], LANES).astype(BF16)
            main = _proj(xs, v1, _col_blocks(w_main, PROJ_COLS), BF16)
            kiw = _proj(xs, v1, _col_blocks(w_kiw, PROJ_COLS), F32)
            nq = s // IDX_QBLOCK
            qi = main[:, o_idx:o_ki].reshape(nq, IDX_QBLOCK, IDX_HEADS, IDX_DIM)
            qi = qi.transpose(0, 2, 1, 3).reshape(nq, IDX_HEADS * IDX_QBLOCK, IDX_DIM)
            kt = kiw[:, :IDX_DIM].astype(BF16).reshape(s // tq_c, tq_c, IDX_DIM).transpose(0, 2, 1)
            wi = kiw[:, IDX_DIM:IDX_DIM + IDX_HEADS]
            mask = _select_mask(qi, kt, wi, tq_c, topk)
            yc = _masked_attention(main, near_c, mask, tq_c)
            xs = _outproj(xs, v1, [yc], [_col_blocks(dsa_w_out[jj].astype(BF16), PROJ_COLS)])

        xs = ffn(xs, 2, 1, final=(li == depth - 1))
    return xs.reshape(batch, s, d)
```

```python
import functools
import math

import jax
import jax.numpy as jnp
import numpy as np
from jax import lax
from jax.experimental import pallas as pl
from jax.experimental.pallas import tpu as pltpu

F32 = jnp.float32
BF16 = jnp.bfloat16
I32 = jnp.int32

EPS = 1e-6
MASKED = -1e30
LANES = 128
INT_MIN = -(2 ** 31)
LOG2E = math.log2(math.e)
NORM_ROWS = 16
ATTN_SUB_ROWS = 128

A_HEADS = 8
A_QK_DIM = 64
A_V_DIM = 128
A_WIDTH = A_HEADS * A_V_DIM
B_GROUPS = 8
B_GROUP_DIM = 128
B_WIDTH = B_GROUPS * B_GROUP_DIM
CHUNK = 128
C_HEADS = 16
C_KV_HEADS = 4
C_GROUP = C_HEADS // C_KV_HEADS
C_HEAD_DIM = 128
C_WIDTH = C_HEADS * C_HEAD_DIM
C_KV_WIDTH = C_KV_HEADS * C_HEAD_DIM
IDX_HEADS = 16
IDX_DIM = 64
TOPK_MAX = 256
REL_BUCKETS = 32
REL_MAX_DIST = 128
IDX_QBLOCK = 128
SEARCH_PROBE_DROP = 2 ** 24

VMEM_LIMIT = 56 * 1024 * 1024


def _cparams(sem):
    return pltpu.CompilerParams(dimension_semantics=sem, vmem_limit_bytes=VMEM_LIMIT)


FFN_ROWS, FFN_COLS = 512, 512
PROJ_ROWS, PROJ_COLS = 1024, 1024


def _col_blocks(w, tn):
    k, n = w.shape
    tn = _tile(n, tn)
    return w.reshape(k, n // tn, tn).transpose(1, 0, 2)


def _tile(n, want):
    if n <= want:
        return n
    t = want
    while n % t:
        t //= 2
    return t


def _mod_kernel(c_ref, w_ref, b_ref, o_ref):
    c = c_ref[...]
    cs = c * (1.0 / (1.0 + jnp.exp(-c)))
    o_ref[0] = jnp.sum(cs * w_ref[0], axis=0, keepdims=True) + b_ref[0]


def _modulation(c, mod_w, mod_b):
    depth, d, n = mod_w.shape
    tn = _tile(n, 1024)
    out = pl.pallas_call(
        _mod_kernel,
        grid=(depth, n // tn),
        in_specs=[pl.BlockSpec((d, 1), lambda l, j: (0, 0)),
                  pl.BlockSpec((1, d, tn), lambda l, j: (l, 0, j)),
                  pl.BlockSpec((1, 1, tn), lambda l, j: (l, 0, j))],
        out_specs=pl.BlockSpec((1, 1, tn), lambda l, j: (l, 0, j)),
        out_shape=jax.ShapeDtypeStruct((depth, 1, n), F32),
        compiler_params=_cparams(("arbitrary", "arbitrary")),
        name="adaln_mod",
    )(c.reshape(d, 1), mod_w, mod_b.reshape(depth, 1, n))
    return out.reshape(depth, n)


def _prenorm_into(x_ref, vec_ref, hn_ref, inv_ref):
    rows, d = x_ref.shape
    gain = vec_ref[0:1, :] * (1.0 + vec_ref[2:3, :])
    shift = vec_ref[1:2, :]

    def scale_body(r, carry):
        rs = pl.ds(pl.multiple_of(r * NORM_ROWS, NORM_ROWS), NORM_ROWS)
        x = x_ref[rs, :]
        ms = jnp.sum(x * x, axis=-1, keepdims=True) * np.float32(1.0 / d)
        inv_ref[rs, :] = jnp.broadcast_to(lax.rsqrt(ms + EPS), (NORM_ROWS, LANES))
        return carry

    lax.fori_loop(0, rows // NORM_ROWS, scale_body, 0, unroll=8)

    def apply_body(r, carry):
        rs = pl.ds(pl.multiple_of(r * NORM_ROWS, NORM_ROWS), NORM_ROWS)
        inv = inv_ref[rs, :]
        inv = jnp.concatenate([inv] * (d // LANES), axis=1)
        hn_ref[rs, :] = (x_ref[rs, :] * inv * gain + shift).astype(hn_ref.dtype)
        return carry

    lax.fori_loop(0, rows // NORM_ROWS, apply_body, 0, unroll=2)


def _ffn_kernel(x_ref, vec_ref, w1g_ref, w1u_ref, w2_ref, o_ref, hn_sc, inv_sc, *, nf, final):
    f = pl.program_id(1)

    @pl.when(f == 0)
    def _():
        _prenorm_into(x_ref, vec_ref, hn_sc, inv_sc)
        o_ref[...] = jnp.zeros_like(o_ref)

    hn = hn_sc[...]
    g = jnp.dot(hn, w1g_ref[...], preferred_element_type=F32)
    u = jnp.dot(hn, w1u_ref[...], preferred_element_type=F32)
    a = (g * (1.0 / (1.0 + jnp.exp(-g))) * u).astype(BF16)
    o_ref[...] += jnp.dot(a, w2_ref[...], preferred_element_type=F32)

    @pl.when(f == nf - 1)
    def _():
        y = x_ref[...] + 0.5 * (1.0 + vec_ref[3:4, :]) * o_ref[...]
        if final:
            ms = jnp.mean(y * y, axis=-1, keepdims=True)
            y = y * lax.rsqrt(ms + EPS) * vec_ref[4:5, :]
        o_ref[...] = y


def _ffn_weights(ffn_w1, ffn_w2):
    depth, two, d, _ = ffn_w1.shape
    d_ff = ffn_w2.shape[2]
    tf = FFN_COLS if d_ff > FFN_COLS else d_ff
    nf = -(-d_ff // tf)
    pad = nf * tf - d_ff
    w1 = jnp.pad(ffn_w1.reshape(depth * two, d, 2, d_ff), ((0, 0), (0, 0), (0, 0), (0, pad)))
    w1 = w1.reshape(depth * two, d, 2, nf, tf).transpose(0, 2, 3, 1, 4).astype(BF16)
    w2 = jnp.pad(ffn_w2.reshape(depth * two, d_ff, d), ((0, 0), (0, pad), (0, 0))).astype(BF16)
    return w1, w2


def _ffn(x, vec, w1, w2, step, *, final):
    s, d = x.shape
    _, _, nf, _, tf = w1.shape
    tm = _tile(s, FFN_ROWS)
    return pl.pallas_call(
        functools.partial(_ffn_kernel, nf=nf, final=final),
        grid=(s // tm, nf),
        in_specs=[pl.BlockSpec((tm, d), lambda i, f: (i, 0)),
                  pl.BlockSpec((8, d), lambda i, f: (0, 0)),
                  pl.BlockSpec((None, None, None, d, tf), lambda i, f: (step, 0, f, 0, 0)),
                  pl.BlockSpec((None, None, None, d, tf), lambda i, f: (step, 1, f, 0, 0)),
                  pl.BlockSpec((None, tf, d), lambda i, f: (step, f, 0))],
        out_specs=pl.BlockSpec((tm, d), lambda i, f: (i, 0)),
        out_shape=jax.ShapeDtypeStruct((s, d), F32),
        scratch_shapes=[pltpu.VMEM((tm, d), BF16), pltpu.VMEM((tm, LANES), F32)],
        compiler_params=_cparams(("parallel", "arbitrary")),
        name="swiglu_halfstep",
    )(x, vec, w1, w1, w2)


def _proj_kernel(x_ref, vec_ref, w_ref, o_ref, hn_sc, inv_sc):
    @pl.when(pl.program_id(1) == 0)
    def _():
        _prenorm_into(x_ref, vec_ref, hn_sc, inv_sc)

    o_ref[...] = jnp.dot(hn_sc[...], w_ref[0], preferred_element_type=F32).astype(o_ref.dtype)


def _proj(x, vec, w, out_dtype):
    s, d = x.shape
    nn, _, tn = w.shape
    n = nn * tn
    tm = _tile(s, PROJ_ROWS)
    return pl.pallas_call(
        _proj_kernel,
        grid=(s // tm, nn),
        in_specs=[pl.BlockSpec((tm, d), lambda i, j: (i, 0)),
                  pl.BlockSpec((8, d), lambda i, j: (0, 0)),
                  pl.BlockSpec((1, d, tn), lambda i, j: (j, 0, 0))],
        out_specs=pl.BlockSpec((tm, tn), lambda i, j: (i, j)),
        out_shape=jax.ShapeDtypeStruct((s, n), out_dtype),
        scratch_shapes=[pltpu.VMEM((tm, d), BF16), pltpu.VMEM((tm, LANES), F32)],
        compiler_params=_cparams(("parallel", "arbitrary")),
        name="norm_mod_proj",
    )(x, vec, w)


def _outproj_kernel(*refs, n_in):
    x_ref, vec_ref = refs[0], refs[1]
    lhs = refs[2:2 + n_in]
    ws = refs[2 + n_in:2 + 2 * n_in]
    o_ref = refs[2 + 2 * n_in]
    acc = jnp.dot(lhs[0][...], ws[0][0], preferred_element_type=F32)
    for a, w in zip(lhs[1:], ws[1:]):
        acc += jnp.dot(a[...], w[0], preferred_element_type=F32)
    o_ref[...] = x_ref[...] + (1.0 + vec_ref[3:4, :]) * acc


def _outproj(x, vec, lhs, ws):
    s, d = x.shape
    tm = _tile(s, PROJ_ROWS)
    tn = ws[0].shape[2]
    n_in = len(lhs)
    in_specs = [pl.BlockSpec((tm, tn), lambda i, j: (i, j)),
                pl.BlockSpec((8, tn), lambda i, j: (0, j))]
    in_specs += [pl.BlockSpec((tm, a.shape[1]), lambda i, j: (i, 0)) for a in lhs]
    in_specs += [pl.BlockSpec((1, w.shape[1], tn), lambda i, j: (j, 0, 0)) for w in ws]
    return pl.pallas_call(
        functools.partial(_outproj_kernel, n_in=n_in),
        grid=(s // tm, d // tn),
        in_specs=in_specs,
        out_specs=pl.BlockSpec((tm, tn), lambda i, j: (i, j)),
        out_shape=jax.ShapeDtypeStruct((s, d), F32),
        compiler_params=_cparams(("parallel", "arbitrary")),
        name="outproj_residual",
    )(x, vec, *lhs, *ws)


def _flash_kernel(*refs, nh, tq, kb, sub, diff, lam_scale):
    if diff:
        lam_ref, q_ref, k_ref, v_ref, nb_ref, g_ref, o_ref, qs_sc, m_sc, acc_sc, s_sc, p_sc, al_sc = refs
        mask_ref = None
    else:
        q_ref, k_ref, v_ref, nb_ref, mask_ref, o_ref, qs_sc, m_sc, acc_sc, s_sc, p_sc, al_sc = refs
    tk = tq
    rows = nh * tq
    hd = k_ref.shape[1]
    assert hd == LANES and rows % sub == 0 and tq % sub == 0
    i = pl.program_id(1)

    if diff:
        q = q_ref[...]
        lane = lax.broadcasted_iota(I32, q.shape, 1)
        zero = jnp.zeros_like(q)
        qs_sc[0:tq, :] = jnp.where(lane < A_QK_DIM, q, zero)
        qs_sc[tq:2 * tq, :] = jnp.where(lane >= A_QK_DIM, q, zero)
    else:
        for r in range(nh):
            qs_sc[r * tq:(r + 1) * tq, :] = q_ref[:, r * hd:(r + 1) * hd]
    m_sc[...] = jnp.full(m_sc.shape, -jnp.inf, F32)
    acc_sc[...] = jnp.zeros(acc_sc.shape, F32)

    tks = kb * tk
    nk = k_ref.shape[0] // tk
    last_step = i // kb

    def bias_index(j):
        return jnp.where(j > i, 3, jnp.clip(j - (i - 2), 0, 2))

    def stage_qk(t, slot):
        start = pl.multiple_of(jnp.minimum(t, last_step) * tks, tks)
        kblk = k_ref[pl.ds(start, tks), :]
        s_sc[slot] = lax.dot_general(qs_sc[...], kblk, (((1,), (1,)), ((), ())), preferred_element_type=F32)

    def stage_softmax(t, slot, biased):
        tc = jnp.minimum(t, last_step)
        for r in range(rows // sub):
            rs = slice(r * sub, (r + 1) * sub)
            tiles = []
            for b in range(kb):
                j = tc * kb + b
                s = s_sc[slot, rs, b * tk:(b + 1) * tk]
                if biased:
                    col = jnp.where(t > last_step, 3, bias_index(j))
                    s = s + nb_ref[0, col, rs, :]
                s = s.astype(BF16)
                if mask_ref is not None:
                    off = (r * sub) % tq
                    s = s + mask_ref[jnp.minimum(j, nk - 1), off:off + sub, :]
                tiles += [s[:, u * LANES:(u + 1) * LANES] for u in range(tk // LANES)]
            cmax = tiles[0]
            for u in tiles[1:]:
                cmax = jnp.maximum(cmax, u)
            m_old = m_sc[rs, :]
            m_new = jnp.maximum(m_old, jnp.max(cmax.astype(F32), axis=1, keepdims=True))
            al_sc[slot, rs, :] = jnp.exp2(m_old - m_new)
            m_b = m_new.astype(BF16)
            p_sc[slot, rs, :] = jnp.concatenate([jnp.exp2(u - m_b) for u in tiles], axis=1)
            m_sc[rs, :] = m_new

    def stage_pv(t, slot):
        start = pl.multiple_of(jnp.minimum(t, last_step) * tks, tks)
        vbe = jnp.concatenate([v_ref[pl.ds(start, tks), :], jnp.ones((tks, hd), BF16)], axis=1)
        pv = jnp.dot(p_sc[slot], vbe, preferred_element_type=F32)
        alpha = al_sc[slot]
        acc_sc[...] = jnp.concatenate([alpha, alpha], axis=1) * acc_sc[...] + pv

    stage_qk(0, 0)
    stage_qk(1, 1)
    stage_softmax(0, 0, True)

    nfar = jnp.maximum((i - 1) // kb, 0)
    npairs = jnp.maximum((nfar - 1) // 2, 0)

    def far_pair(u, carry):
        t = 2 * u
        stage_pv(t, 0)
        stage_softmax(t + 1, 1, False)
        stage_qk(t + 2, 0)
        stage_pv(t + 1, 1)
        stage_softmax(t + 2, 0, False)
        stage_qk(t + 3, 1)
        return carry

    lax.fori_loop(0, npairs, far_pair, 0)

    t0 = 2 * npairs
    stage_pv(t0, 0)
    stage_softmax(t0 + 1, 1, True)
    stage_qk(t0 + 2, 0)
    stage_pv(t0 + 1, 1)
    stage_softmax(t0 + 2, 0, True)
    stage_qk(t0 + 3, 1)
    stage_pv(t0 + 2, 0)
    stage_softmax(t0 + 3, 1, True)
    stage_pv(t0 + 3, 1)

    if diff:
        o0 = acc_sc[0:tq, 0:hd] / acc_sc[0:tq, hd:2 * hd]
        o1 = acc_sc[tq:2 * tq, 0:hd] / acc_sc[tq:2 * tq, hd:2 * hd]
        dlt = o0 - lam_ref[0] * o1
        ms = jnp.mean(dlt * dlt, axis=-1, keepdims=True)
        o_ref[...] = ((dlt * lax.rsqrt(ms + EPS) * g_ref[...]) * lam_scale).astype(o_ref.dtype)
    else:
        for r in range(nh):
            rs = slice(r * tq, (r + 1) * tq)
            o_ref[:, r * hd:(r + 1) * hd] = (acc_sc[rs, 0:hd] / acc_sc[rs, hd:2 * hd]).astype(o_ref.dtype)


def _rel_bucket(dist):
    n = jnp.maximum(dist, 0)
    max_exact = REL_BUCKETS // 2
    nf = jnp.maximum(n, 1).astype(F32)
    large = max_exact + (jnp.log(nf / max_exact) / math.log(REL_MAX_DIST / max_exact)
                         * (REL_BUCKETS - max_exact)).astype(I32)
    large = jnp.minimum(large, REL_BUCKETS - 1)
    return jnp.where(n < max_exact, n, large)


def _near_bias(rel_table, tq, groups, nh):
    assert tq >= LANES, "keys older than one block must all fall in the last bucket"
    r = jnp.arange(tq, dtype=I32)[:, None]
    c = jnp.arange(2 * tq, dtype=I32)[None, :]
    dist = r + tq - c
    rel = (rel_table - rel_table[REL_BUCKETS - 1][None, :]) * np.float32(LOG2E)
    onehot = jax.nn.one_hot(_rel_bucket(dist), REL_BUCKETS, dtype=F32)
    b = jnp.einsum("rcb,bh->hrc", onehot, rel, precision=lax.Precision.HIGHEST)
    b = jnp.where((dist >= 0)[None], b, MASKED)
    heads = b.shape[0]
    tiles = jnp.stack([jnp.zeros((heads, tq, tq), F32), b[:, :, :tq], b[:, :, tq:],
                       jnp.full((heads, tq, tq), MASKED, F32)], axis=1)
    tiles = tiles.reshape(groups, nh, 4, tq, tq).transpose(0, 2, 1, 3, 4)
    return tiles.reshape(groups, 4, nh * tq, tq)


def _diff_attention(qkv, near, lam, subln_g, lam_scale, tq):
    s = qkv.shape[0]
    hd = A_V_DIM
    nh = 2
    kb = 1
    kcol = A_WIDTH // hd
    return pl.pallas_call(
        functools.partial(_flash_kernel, nh=nh, tq=tq, kb=kb, sub=min(tq, ATTN_SUB_ROWS), diff=True,
                          lam_scale=lam_scale),
        grid=(A_HEADS, s // tq),
        in_specs=[pl.BlockSpec(memory_space=pltpu.SMEM),
                  pl.BlockSpec((tq, hd), lambda h, i: (i, h)),
                  pl.BlockSpec((s, hd), lambda h, i: (0, kcol + h)),
                  pl.BlockSpec((s, hd), lambda h, i: (0, 2 * kcol + h)),
                  pl.BlockSpec((1, 4, nh * tq, tq), lambda h, i: (h, 0, 0, 0)),
                  pl.BlockSpec((1, hd), lambda h, i: (0, 0))],
        out_specs=pl.BlockSpec((tq, hd), lambda h, i: (i, h)),
        out_shape=jax.ShapeDtypeStruct((s, A_WIDTH), BF16),
        scratch_shapes=[pltpu.VMEM((nh * tq, hd), BF16),
                        pltpu.VMEM((nh * tq, LANES), F32),
                        pltpu.VMEM((nh * tq, 2 * hd), F32),
                        pltpu.VMEM((2, nh * tq, kb * tq), F32),
                        pltpu.VMEM((2, nh * tq, kb * tq), BF16),
                        pltpu.VMEM((2, nh * tq, LANES), F32)],
        compiler_params=_cparams(("parallel", "arbitrary")),
        name="diff_attention",
    )(lam, qkv, qkv, qkv, near, subln_g)


def _masked_attention(qkv, near, mask, tq):
    s = qkv.shape[0]
    hd = C_HEAD_DIM
    nh = C_GROUP
    kcol = C_WIDTH // hd
    vcol = kcol + C_KV_HEADS
    nk = s // tq
    kb = 2 if nk % 2 == 0 else 1
    return pl.pallas_call(
        functools.partial(_flash_kernel, nh=nh, tq=tq, kb=kb, sub=min(tq, ATTN_SUB_ROWS), diff=False,
                          lam_scale=1.0),
        grid=(C_KV_HEADS, s // tq),
        in_specs=[pl.BlockSpec((tq, nh * hd), lambda g, i: (i, g)),
                  pl.BlockSpec((s, hd), lambda g, i: (0, kcol + g)),
                  pl.BlockSpec((s, hd), lambda g, i: (0, vcol + g)),
                  pl.BlockSpec((1, 4, nh * tq, tq), lambda g, i: (g, 0, 0, 0)),
                  pl.BlockSpec((nk, tq, tq), lambda g, i: (0, i, 0))],
        out_specs=pl.BlockSpec((tq, nh * hd), lambda g, i: (i, g)),
        out_shape=jax.ShapeDtypeStruct((s, C_WIDTH), BF16),
        scratch_shapes=[pltpu.VMEM((nh * tq, hd), BF16),
                        pltpu.VMEM((nh * tq, LANES), F32),
                        pltpu.VMEM((nh * tq, 2 * hd), F32),
                        pltpu.VMEM((2, nh * tq, kb * tq), F32),
                        pltpu.VMEM((2, nh * tq, kb * tq), BF16),
                        pltpu.VMEM((2, nh * tq, LANES), F32)],
        compiler_params=_cparams(("parallel", "arbitrary")),
        name="selected_attention",
    )(qkv, qkv, qkv, near, mask)


def _sg_kernel(zb_ref, lng_ref, lnb_ref, w_ref, bs_ref, o_ref, *, nchunk):
    zb = zb_ref[...]
    gl = zb * (0.5 * (1.0 + jnp.tanh(np.float32(np.sqrt(2.0 / np.pi)) * (zb + 0.044715 * (zb * zb * zb)))))
    u = gl[:, :B_WIDTH]
    z = gl[:, B_WIDTH:]
    mu = jnp.mean(z, axis=-1, keepdims=True)
    zc = z - mu
    var = jnp.mean(zc * zc, axis=-1, keepdims=True)
    zn = (zc * lax.rsqrt(var + EPS) * lng_ref[...] + lnb_ref[...]).astype(BF16)
    row = lax.broadcasted_iota(I32, (CHUNK, CHUNK), 0)
    col = lax.broadcasted_iota(I32, (CHUNK, CHUNK), 1)
    for g in range(B_GROUPS):
        w = jnp.where(row >= col, w_ref[g], 0.0).astype(BF16)
        bias = bs_ref[g]
        lo = g * B_GROUP_DIM
        for c in range(nchunk):
            r0 = c * CHUNK
            sz = jnp.dot(w, zn[r0:r0 + CHUNK, lo:lo + B_GROUP_DIM], preferred_element_type=F32) + bias
            o_ref[r0:r0 + CHUNK, lo:lo + B_GROUP_DIM] = (u[r0:r0 + CHUNK, lo:lo + B_GROUP_DIM] * sz).astype(o_ref.dtype)


def _spatial_gating(zb, ln_g, ln_b, w_s, b_s):
    s = zb.shape[0]
    t = _tile(s, 256)
    return pl.pallas_call(
        functools.partial(_sg_kernel, nchunk=t // CHUNK),
        grid=(s // t,),
        in_specs=[pl.BlockSpec((t, 2 * B_WIDTH), lambda i: (i, 0)),
                  pl.BlockSpec((1, B_WIDTH), lambda i: (0, 0)),
                  pl.BlockSpec((1, B_WIDTH), lambda i: (0, 0)),
                  pl.BlockSpec((B_GROUPS, CHUNK, CHUNK), lambda i: (0, 0, 0)),
                  pl.BlockSpec((B_GROUPS, CHUNK, 1), lambda i: (0, 0, 0))],
        out_specs=pl.BlockSpec((t, B_WIDTH), lambda i: (i, 0)),
        out_shape=jax.ShapeDtypeStruct((s, B_WIDTH), BF16),
        compiler_params=_cparams(("parallel",)),
        name="spatial_gating",
    )(zb, ln_g.reshape(1, B_WIDTH), ln_b.reshape(1, B_WIDTH), w_s, b_s.reshape(B_GROUPS, CHUNK, 1))


def _select_kernel(qi_ref, kt_ref, w_ref, o_ref, keys_sc, sc_sc, wb_sc, mx_sc, *, tkc, topk):
    tqi = IDX_QBLOCK
    i = pl.program_id(0)
    nk = o_ref.shape[0]
    nch = (i * tqi + tqi + tkc - 1) // tkc
    qpos = i * tqi + lax.broadcasted_iota(I32, (tqi, tkc), 0)
    kloc = lax.broadcasted_iota(I32, (tqi, tkc), 1)
    qrow = i * tqi + lax.broadcasted_iota(I32, (tqi, LANES), 0)
    klane = lax.broadcasted_iota(I32, (tqi, LANES), 1)

    wgt = w_ref[...] * np.float32(IDX_DIM ** -0.5)
    for h in range(IDX_HEADS):
        wb_sc[h] = jnp.broadcast_to(wgt[:, h:h + 1], (tqi, LANES))
    mx_sc[...] = jnp.full(mx_sc.shape, -jnp.inf, F32)

    def stage_dot(c, slot):
        sc_sc[slot] = jnp.dot(qi_ref[0], kt_ref[jnp.minimum(c, nch - 1)], preferred_element_type=F32)

    def stage_reduce(c, slot):
        c = jnp.minimum(c, nch - 1)
        for u in range(tkc // LANES):
            ls = slice(u * LANES, (u + 1) * LANES)
            acc = jnp.zeros((tqi, LANES), F32)
            for h in range(IDX_HEADS):
                acc += jnp.maximum(sc_sc[slot, h * tqi:(h + 1) * tqi, ls], 0.0) * wb_sc[h]
            acc = acc + 0.0
            bits = pltpu.bitcast(acc, I32)
            ordered = jnp.where(bits < 0, bits ^ jnp.int32(0x7FFFFFFF), bits)
            causal = c * tkc + u * LANES + klane <= qrow
            keys_sc[c, :, ls] = jnp.where(causal, ordered, jnp.int32(INT_MIN))
            mx_sc[...] = jnp.maximum(mx_sc[...], jnp.where(causal, acc, -jnp.inf))

    stage_dot(0, 0)

    def score_pair(u, carry):
        c = 2 * u
        stage_dot(c + 1, 1)
        stage_reduce(c, 0)
        stage_dot(c + 2, 0)
        stage_reduce(c + 1, 1)
        return carry

    lax.fori_loop(0, (nch + 1) // 2, score_pair, 0)

    def count_ge(cand):
        candb = jnp.broadcast_to(cand, (tqi, LANES))

        def count_body(c, cnt):
            kk = keys_sc[c]
            for u in range(tkc // LANES):
                cnt += jnp.where(kk[:, u * LANES:(u + 1) * LANES] >= candb, 1, 0)
            return cnt

        cnt = lax.fori_loop(0, nch, count_body, jnp.zeros((tqi, LANES), I32))
        return jnp.sum(cnt.astype(F32), axis=1, keepdims=True)

    want = np.float32(topk)
    fbits = pltpu.bitcast(jnp.max(mx_sc[...], axis=1, keepdims=True), I32)
    kmax = jnp.where(fbits < 0, fbits ^ jnp.int32(0x7FFFFFFF), fbits)
    few = i * tqi + lax.broadcasted_iota(I32, (tqi, 1), 0) + 1 <= topk
    lo0 = jnp.full((tqi, 1), INT_MIN, I32)
    hi0 = jnp.where(few, lo0 + 1, kmax + 1)

    def narrow(state, cand):
        lo, hi, active = state
        total = count_ge(cand)
        open_ = active > 0.0
        up = open_ & (total >= want)
        lo = jnp.where(up, cand, lo)
        hi = jnp.where(open_ & (~up), cand, hi)
        open_ = open_ & (~(up & (total == want))) & ((hi - lo) != 1)
        return lo, hi, jnp.where(open_, 1.0, 0.0)

    def midpoint(state):
        lo, hi, _ = state
        return lo + lax.shift_right_logical(hi - lo, jnp.int32(1))

    probe = jnp.maximum(kmax, jnp.int32(INT_MIN + SEARCH_PROBE_DROP + 1)) - jnp.int32(SEARCH_PROBE_DROP)
    state = narrow((lo0, hi0, jnp.where(few, 0.0, 1.0)), jnp.where(few, lo0, probe))

    def bisect_body(carry):
        state, _ = carry
        state = narrow(state, midpoint(state))
        state = narrow(state, midpoint(state))
        return state, jnp.sum(state[2])

    (thr, _, _), _ = lax.while_loop(lambda carry: carry[1] > 0.0, bisect_body, (state, jnp.float32(1.0)))
    thrb = jnp.broadcast_to(thr, (tqi, tkc))

    def mask_body(c, carry):
        sel = (keys_sc[c] >= thrb) & (c * tkc + kloc <= qpos)
        o_ref[c] = jnp.where(sel, 0.0, MASKED).astype(o_ref.dtype)
        return carry

    lax.fori_loop(0, nch, mask_body, 0)

    def fill_body(c, carry):
        o_ref[c] = jnp.full((tqi, tkc), MASKED, o_ref.dtype)
        return carry

    lax.fori_loop(nch, nk, fill_body, 0)


def _select_mask(qi_stack, kt, wi, tkc, topk):
    nq, rows, _ = qi_stack.shape
    nk = kt.shape[0]
    s = nq * IDX_QBLOCK
    return pl.pallas_call(
        functools.partial(_select_kernel, tkc=tkc, topk=topk),
        grid=(nq,),
        in_specs=[pl.BlockSpec((1, rows, IDX_DIM), lambda i: (i, 0, 0)),
                  pl.BlockSpec((nk, IDX_DIM, tkc), lambda i: (0, 0, 0)),
                  pl.BlockSpec((IDX_QBLOCK, IDX_HEADS), lambda i: (i, 0))],
        out_specs=pl.BlockSpec((nk, IDX_QBLOCK, tkc), lambda i: (0, i, 0)),
        out_shape=jax.ShapeDtypeStruct((nk, s, tkc), BF16),
        scratch_shapes=[pltpu.VMEM((nk, IDX_QBLOCK, tkc), I32),
                        pltpu.VMEM((2, rows, tkc), F32),
                        pltpu.VMEM((IDX_HEADS, IDX_QBLOCK, LANES), F32),
                        pltpu.VMEM((IDX_QBLOCK, LANES), F32)],
        compiler_params=_cparams(("parallel",)),
        name="indexer_select",
    )(qi_stack, kt, wi)


def _vec_pack(d, *rows):
    rows = [r.reshape(1, d).astype(F32) for r in rows]
    rows += [jnp.zeros((1, d), F32)] * (8 - len(rows))
    return jnp.concatenate(rows, axis=0)


def _pad_cols(w, n):
    return jnp.pad(w, ((0, 0), (0, n - w.shape[1])))


def kernel(x, c, norm_g, mod_w, mod_b, ffn_w1, ffn_w2, rel_table, ab_w_in, ab_w_out, diff_lam,
           diff_subln_g, sg_ln_g, sg_ln_b, sg_w, sg_b, dsa_w_in, dsa_w_out, final_g):
    batch, s, d = x.shape
    depth = norm_g.shape[0]
    assert batch == 1 and s % IDX_QBLOCK == 0

    tq_a = _tile(s, 512)
    tq_c = _tile(s, 256)
    topk = min(TOPK_MAX, s // 4)

    mod = _modulation(c, mod_w, mod_b).reshape(depth, 9, d)
    ffn_a, ffn_b = _ffn_weights(ffn_w1, ffn_w2)
    near_a = _near_bias(rel_table, tq_a, A_HEADS, 2)
    near_c = _near_bias(rel_table, tq_c, C_KV_HEADS, C_GROUP)

    xs = x.reshape(s, d)
    zeros_d = jnp.zeros((d,), F32)
    for li in range(depth):
        def vec(j, li=li):
            last = final_g if (li == depth - 1 and j == 2) else zeros_d
            return _vec_pack(d, norm_g[li, j], mod[li, 3 * j], mod[li, 3 * j + 1], mod[li, 3 * j + 2], last)

        def ffn(xs, j, k, final=False, li=li):
            return _ffn(xs, vec(j), ffn_a, ffn_b, 2 * li + k, final=final)

        xs = ffn(xs, 0, 0)

        v1 = vec(1)
        jj = li // 2
        if li % 2 == 0:
            w_in = ab_w_in[jj]
            w_qkv = jnp.concatenate([w_in[:, :A_WIDTH] * np.float32(A_QK_DIM ** -0.5 * LOG2E),
                                     w_in[:, A_WIDTH:3 * A_WIDTH]], axis=1).astype(BF16)
            w_zb = w_in[:, 3 * A_WIDTH:].astype(BF16)
            qkv = _proj(xs, v1, _col_blocks(w_qkv, PROJ_COLS), BF16)
            zb = _proj(xs, v1, _col_blocks(w_zb, PROJ_COLS), F32)
            lam_init = 0.8 - 0.6 * math.exp(-0.3 * li)
            lp = diff_lam[jj].astype(F32)
            lam = jnp.exp(jnp.sum(lp[0] * lp[1])) - jnp.exp(jnp.sum(lp[2] * lp[3])) + lam_init
            ya = _diff_attention(qkv, near_a, lam.reshape(1), diff_subln_g[jj].reshape(1, A_V_DIM),
                                 1.0 - lam_init, tq_a)
            yb = _spatial_gating(zb, sg_ln_g[jj], sg_ln_b[jj], sg_w[jj], sg_b[jj])
            w_out = ab_w_out[jj].astype(BF16)
            xs = _outproj(xs, v1, [ya, yb], [_col_blocks(w_out[:A_WIDTH], PROJ_COLS),
                                             _col_blocks(w_out[A_WIDTH:], PROJ_COLS)])
        else:
            w_in = dsa_w_in[jj]
            o_idx = C_WIDTH + 2 * C_KV_WIDTH
            o_ki = o_idx + IDX_HEADS * IDX_DIM
            w_main = jnp.concatenate([w_in[:, :C_WIDTH] * np.float32(C_HEAD_DIM ** -0.5 * LOG2E),
                                      w_in[:, C_WIDTH:o_ki]], axis=1).astype(BF16)
            w_kiw = _pad_cols(w_in[:, o_ki:], LANES).astype(BF16)
            main = _proj(xs, v1, _col_blocks(w_main, PROJ_COLS), BF16)
            kiw = _proj(xs, v1, _col_blocks(w_kiw, PROJ_COLS), F32)
            nq = s // IDX_QBLOCK
            qi = main[:, o_idx:o_ki].reshape(nq, IDX_QBLOCK, IDX_HEADS, IDX_DIM)
            qi = qi.transpose(0, 2, 1, 3).reshape(nq, IDX_HEADS * IDX_QBLOCK, IDX_DIM)
            kt = kiw[:, :IDX_DIM].astype(BF16).reshape(s // tq_c, tq_c, IDX_DIM).transpose(0, 2, 1)
            wi = kiw[:, IDX_DIM:IDX_DIM + IDX_HEADS]
            mask = _select_mask(qi, kt, wi, tq_c, topk)
            yc = _masked_attention(main, near_c, mask, tq_c)
            xs = _outproj(xs, v1, [yc], [_col_blocks(dsa_w_out[jj].astype(BF16), PROJ_COLS)])

        xs = ffn(xs, 2, 1, final=(li == depth - 1))
    return xs.reshape(batch, s, d)
```

```python
import functools
import math

import jax
import jax.numpy as jnp
import numpy as np
from jax import lax
from jax.experimental import pallas as pl
from jax.experimental.pallas import tpu as pltpu

F32 = jnp.float32
BF16 = jnp.bfloat16
I32 = jnp.int32

EPS = 1e-6
MASKED = -1e30
LANES = 128
INT_MIN = -(2 ** 31)
LOG2E = math.log2(math.e)
NORM_ROWS = 16
ATTN_SUB_ROWS = 128

A_HEADS = 8
A_QK_DIM = 64
A_V_DIM = 128
A_WIDTH = A_HEADS * A_V_DIM
B_GROUPS = 8
B_GROUP_DIM = 128
B_WIDTH = B_GROUPS * B_GROUP_DIM
CHUNK = 128
C_HEADS = 16
C_KV_HEADS = 4
C_GROUP = C_HEADS // C_KV_HEADS
C_HEAD_DIM = 128
C_WIDTH = C_HEADS * C_HEAD_DIM
C_KV_WIDTH = C_KV_HEADS * C_HEAD_DIM
IDX_HEADS = 16
IDX_DIM = 64
TOPK_MAX = 256
REL_BUCKETS = 32
REL_MAX_DIST = 128
IDX_QBLOCK = 128
SEARCH_PROBE_DROP = 2 ** 24

VMEM_LIMIT = 56 * 1024 * 1024


def _cparams(sem):
    return pltpu.CompilerParams(dimension_semantics=sem, vmem_limit_bytes=VMEM_LIMIT)


FFN_ROWS, FFN_COLS = 512, 512
PROJ_ROWS, PROJ_COLS = 1024, 1024


def _col_blocks(w, tn):
    k, n = w.shape
    tn = _tile(n, tn)
    return w.reshape(k, n // tn, tn).transpose(1, 0, 2)


def _tile(n, want):
    if n <= want:
        return n
    t = want
    while n % t:
        t //= 2
    return t


def _mod_kernel(c_ref, w_ref, b_ref, o_ref):
    c = c_ref[...]
    cs = c * (1.0 / (1.0 + jnp.exp(-c)))
    o_ref[0] = jnp.sum(cs * w_ref[0], axis=0, keepdims=True) + b_ref[0]


def _modulation(c, mod_w, mod_b):
    depth, d, n = mod_w.shape
    tn = _tile(n, 1024)
    out = pl.pallas_call(
        _mod_kernel,
        grid=(depth, n // tn),
        in_specs=[pl.BlockSpec((d, 1), lambda l, j: (0, 0)),
                  pl.BlockSpec((1, d, tn), lambda l, j: (l, 0, j)),
                  pl.BlockSpec((1, 1, tn), lambda l, j: (l, 0, j))],
        out_specs=pl.BlockSpec((1, 1, tn), lambda l, j: (l, 0, j)),
        out_shape=jax.ShapeDtypeStruct((depth, 1, n), F32),
        compiler_params=_cparams(("arbitrary", "arbitrary")),
        name="adaln_mod",
    )(c.reshape(d, 1), mod_w, mod_b.reshape(depth, 1, n))
    return out.reshape(depth, n)


def _prenorm_into(x_ref, vec_ref, hn_ref, inv_ref):
    rows, d = x_ref.shape
    gain = vec_ref[0:1, :] * (1.0 + vec_ref[2:3, :])
    shift = vec_ref[1:2, :]

    def scale_body(r, carry):
        rs = pl.ds(pl.multiple_of(r * NORM_ROWS, NORM_ROWS), NORM_ROWS)
        x = x_ref[rs, :]
        ms = jnp.sum(x * x, axis=-1, keepdims=True) * np.float32(1.0 / d)
        inv_ref[rs, :] = jnp.broadcast_to(lax.rsqrt(ms + EPS), (NORM_ROWS, LANES))
        return carry

    lax.fori_loop(0, rows // NORM_ROWS, scale_body, 0, unroll=8)

    def apply_body(r, carry):
        rs = pl.ds(pl.multiple_of(r * NORM_ROWS, NORM_ROWS), NORM_ROWS)
        inv = inv_ref[rs, :]
        inv = jnp.concatenate([inv] * (d // LANES), axis=1)
        hn_ref[rs, :] = (x_ref[rs, :] * inv * gain + shift).astype(hn_ref.dtype)
        return carry

    lax.fori_loop(0, rows // NORM_ROWS, apply_body, 0, unroll=2)


def _ffn_kernel(x_ref, vec_ref, w1g_ref, w1u_ref, w2_ref, o_ref, hn_sc, inv_sc, *, nf, final):
    f = pl.program_id(1)

    @pl.when(f == 0)
    def _():
        _prenorm_into(x_ref, vec_ref, hn_sc, inv_sc)
        o_ref[...] = jnp.zeros_like(o_ref)

    hn = hn_sc[...]
    g = jnp.dot(hn, w1g_ref[...], preferred_element_type=F32)
    u = jnp.dot(hn, w1u_ref[...], preferred_element_type=F32)
    a = (g * (1.0 / (1.0 + jnp.exp(-g))) * u).astype(BF16)
    o_ref[...] += jnp.dot(a, w2_ref[...], preferred_element_type=F32)

    @pl.when(f == nf - 1)
    def _():
        y = x_ref[...] + 0.5 * (1.0 + vec_ref[3:4, :]) * o_ref[...]
        if final:
            ms = jnp.mean(y * y, axis=-1, keepdims=True)
            y = y * lax.rsqrt(ms + EPS) * vec_ref[4:5, :]
        o_ref[...] = y


def _ffn_weights(ffn_w1, ffn_w2):
    depth, two, d, _ = ffn_w1.shape
    d_ff = ffn_w2.shape[2]
    tf = FFN_COLS if d_ff > FFN_COLS else d_ff
    nf = -(-d_ff // tf)
    pad = nf * tf - d_ff
    w1 = jnp.pad(ffn_w1.reshape(depth * two, d, 2, d_ff), ((0, 0), (0, 0), (0, 0), (0, pad)))
    w1 = w1.reshape(depth * two, d, 2, nf, tf).transpose(0, 2, 3, 1, 4).astype(BF16)
    w2 = jnp.pad(ffn_w2.reshape(depth * two, d_ff, d), ((0, 0), (0, pad), (0, 0))).astype(BF16)
    return w1, w2


def _ffn(x, vec, w1, w2, step, *, final):
    s, d = x.shape
    _, _, nf, _, tf = w1.shape
    tm = _tile(s, FFN_ROWS)
    return pl.pallas_call(
        functools.partial(_ffn_kernel, nf=nf, final=final),
        grid=(s // tm, nf),
        in_specs=[pl.BlockSpec((tm, d), lambda i, f: (i, 0)),
                  pl.BlockSpec((8, d), lambda i, f: (0, 0)),
                  pl.BlockSpec((None, None, None, d, tf), lambda i, f: (step, 0, f, 0, 0)),
                  pl.BlockSpec((None, None, None, d, tf), lambda i, f: (step, 1, f, 0, 0)),
                  pl.BlockSpec((None, tf, d), lambda i, f: (step, f, 0))],
        out_specs=pl.BlockSpec((tm, d), lambda i, f: (i, 0)),
        out_shape=jax.ShapeDtypeStruct((s, d), F32),
        scratch_shapes=[pltpu.VMEM((tm, d), BF16), pltpu.VMEM((tm, LANES), F32)],
        compiler_params=_cparams(("parallel", "arbitrary")),
        name="swiglu_halfstep",
    )(x, vec, w1, w1, w2)


def _proj_kernel(x_ref, vec_ref, w_ref, o_ref, hn_sc, inv_sc):
    @pl.when(pl.program_id(1) == 0)
    def _():
        _prenorm_into(x_ref, vec_ref, hn_sc, inv_sc)

    o_ref[...] = jnp.dot(hn_sc[...], w_ref[0], preferred_element_type=F32).astype(o_ref.dtype)


def _proj(x, vec, w, out_dtype):
    s, d = x.shape
    nn, _, tn = w.shape
    n = nn * tn
    tm = _tile(s, PROJ_ROWS)
    return pl.pallas_call(
        _proj_kernel,
        grid=(s // tm, nn),
        in_specs=[pl.BlockSpec((tm, d), lambda i, j: (i, 0)),
                  pl.BlockSpec((8, d), lambda i, j: (0, 0)),
                  pl.BlockSpec((1, d, tn), lambda i, j: (j, 0, 0))],
        out_specs=pl.BlockSpec((tm, tn), lambda i, j: (i, j)),
        out_shape=jax.ShapeDtypeStruct((s, n), out_dtype),
        scratch_shapes=[pltpu.VMEM((tm, d), BF16), pltpu.VMEM((tm, LANES), F32)],
        compiler_params=_cparams(("parallel", "arbitrary")),
        name="norm_mod_proj",
    )(x, vec, w)


def _outproj_kernel(*refs, n_in):
    x_ref, vec_ref = refs[0], refs[1]
    lhs = refs[2:2 + n_in]
    ws = refs[2 + n_in:2 + 2 * n_in]
    o_ref = refs[2 + 2 * n_in]
    acc = jnp.dot(lhs[0][...], ws[0][0], preferred_element_type=F32)
    for a, w in zip(lhs[1:], ws[1:]):
        acc += jnp.dot(a[...], w[0], preferred_element_type=F32)
    o_ref[...] = x_ref[...] + (1.0 + vec_ref[3:4, :]) * acc


def _outproj(x, vec, lhs, ws):
    s, d = x.shape
    tm = _tile(s, PROJ_ROWS)
    tn = ws[0].shape[2]
    n_in = len(lhs)
    in_specs = [pl.BlockSpec((tm, tn), lambda i, j: (i, j)),
                pl.BlockSpec((8, tn), lambda i, j: (0, j))]
    in_specs += [pl.BlockSpec((tm, a.shape[1]), lambda i, j: (i, 0)) for a in lhs]
    in_specs += [pl.BlockSpec((1, w.shape[1], tn), lambda i, j: (j, 0, 0)) for w in ws]
    return pl.pallas_call(
        functools.partial(_outproj_kernel, n_in=n_in),
        grid=(s // tm, d // tn),
        in_specs=in_specs,
        out_specs=pl.BlockSpec((tm, tn), lambda i, j: (i, j)),
        out_shape=jax.ShapeDtypeStruct((s, d), F32),
        compiler_params=_cparams(("parallel", "arbitrary")),
        name="outproj_residual",
    )(x, vec, *lhs, *ws)


def _flash_kernel(*refs, nh, tq, kb, sub, diff, lam_scale):
    if diff:
        lam_ref, q_ref, k_ref, v_ref, nb_ref, g_ref, o_ref, qs_sc, m_sc, acc_sc, s_sc, p_sc, al_sc = refs
        mask_ref = None
    else:
        q_ref, k_ref, v_ref, nb_ref, mask_ref, o_ref, qs_sc, m_sc, acc_sc, s_sc, p_sc, al_sc = refs
    tk = tq
    rows = nh * tq
    hd = k_ref.shape[1]
    assert hd == LANES and rows % sub == 0 and tq % sub == 0
    i = pl.program_id(1)

    if diff:
        q = q_ref[...]
        lane = lax.broadcasted_iota(I32, q.shape, 1)
        zero = jnp.zeros_like(q)
        qs_sc[0:tq, :] = jnp.where(lane < A_QK_DIM, q, zero)
        qs_sc[tq:2 * tq, :] = jnp.where(lane >= A_QK_DIM, q, zero)
    else:
        for r in range(nh):
            qs_sc[r * tq:(r + 1) * tq, :] = q_ref[:, r * hd:(r + 1) * hd]
    m_sc[...] = jnp.full(m_sc.shape, -jnp.inf, F32)
    acc_sc[...] = jnp.zeros(acc_sc.shape, F32)

    tks = kb * tk
    nk = k_ref.shape[0] // tk
    last_step = i // kb

    def bias_index(j):
        return jnp.where(j > i, 3, jnp.clip(j - (i - 2), 0, 2))

    def stage_qk(t, slot):
        start = pl.multiple_of(jnp.minimum(t, last_step) * tks, tks)
        kblk = k_ref[pl.ds(start, tks), :]
        s_sc[slot] = lax.dot_general(qs_sc[...], kblk, (((1,), (1,)), ((), ())), preferred_element_type=F32)

    def stage_softmax(t, slot, biased):
        tc = jnp.minimum(t, last_step)
        for r in range(rows // sub):
            rs = slice(r * sub, (r + 1) * sub)
            tiles = []
            for b in range(kb):
                j = tc * kb + b
                s = s_sc[slot, rs, b * tk:(b + 1) * tk]
                if biased:
                    col = jnp.where(t > last_step, 3, bias_index(j))
                    s = s + nb_ref[0, col, rs, :]
                s = s.astype(BF16)
                if mask_ref is not None:
                    off = (r * sub) % tq
                    s = s + mask_ref[jnp.minimum(j, nk - 1), off:off + sub, :]
                tiles += [s[:, u * LANES:(u + 1) * LANES] for u in range(tk // LANES)]
            cmax = tiles[0]
            for u in tiles[1:]:
                cmax = jnp.maximum(cmax, u)
            m_old = m_sc[rs, :]
            m_new = jnp.maximum(m_old, jnp.max(cmax.astype(F32), axis=1, keepdims=True))
            al_sc[slot, rs, :] = jnp.exp2(m_old - m_new)
            m_b = m_new.astype(BF16)
            p_sc[slot, rs, :] = jnp.concatenate([jnp.exp2(u - m_b) for u in tiles], axis=1)
            m_sc[rs, :] = m_new

    def stage_pv(t, slot):
        start = pl.multiple_of(jnp.minimum(t, last_step) * tks, tks)
        vbe = jnp.concatenate([v_ref[pl.ds(start, tks), :], jnp.ones((tks, hd), BF16)], axis=1)
        pv = jnp.dot(p_sc[slot], vbe, preferred_element_type=F32)
        alpha = al_sc[slot]
        acc_sc[...] = jnp.concatenate([alpha, alpha], axis=1) * acc_sc[...] + pv

    stage_qk(0, 0)
    stage_qk(1, 1)
    stage_softmax(0, 0, True)

    nfar = jnp.maximum((i - 1) // kb, 0)
    npairs = jnp.maximum((nfar - 1) // 2, 0)

    def far_pair(u, carry):
        t = 2 * u
        stage_pv(t, 0)
        stage_softmax(t + 1, 1, False)
        stage_qk(t + 2, 0)
        stage_pv(t + 1, 1)
        stage_softmax(t + 2, 0, False)
        stage_qk(t + 3, 1)
        return carry

    lax.fori_loop(0, npairs, far_pair, 0)

    t0 = 2 * npairs
    stage_pv(t0, 0)
    stage_softmax(t0 + 1, 1, True)
    stage_qk(t0 + 2, 0)
    stage_pv(t0 + 1, 1)
    stage_softmax(t0 + 2, 0, True)
    stage_qk(t0 + 3, 1)
    stage_pv(t0 + 2, 0)
    stage_softmax(t0 + 3, 1, True)
    stage_pv(t0 + 3, 1)

    if diff:
        o0 = acc_sc[0:tq, 0:hd] / acc_sc[0:tq, hd:2 * hd]
        o1 = acc_sc[tq:2 * tq, 0:hd] / acc_sc[tq:2 * tq, hd:2 * hd]
        dlt = o0 - lam_ref[0] * o1
        ms = jnp.mean(dlt * dlt, axis=-1, keepdims=True)
        o_ref[...] = ((dlt * lax.rsqrt(ms + EPS) * g_ref[...]) * lam_scale).astype(o_ref.dtype)
    else:
        for r in range(nh):
            rs = slice(r * tq, (r + 1) * tq)
            o_ref[:, r * hd:(r + 1) * hd] = (acc_sc[rs, 0:hd] / acc_sc[rs, hd:2 * hd]).astype(o_ref.dtype)


def _rel_bucket(dist):
    n = jnp.maximum(dist, 0)
    max_exact = REL_BUCKETS // 2
    nf = jnp.maximum(n, 1).astype(F32)
    large = max_exact + (jnp.log(nf / max_exact) / math.log(REL_MAX_DIST / max_exact)
                         * (REL_BUCKETS - max_exact)).astype(I32)
    large = jnp.minimum(large, REL_BUCKETS - 1)
    return jnp.where(n < max_exact, n, large)


def _near_bias(rel_table, tq, groups, nh):
    assert tq >= LANES, "keys older than one block must all fall in the last bucket"
    r = jnp.arange(tq, dtype=I32)[:, None]
    c = jnp.arange(2 * tq, dtype=I32)[None, :]
    dist = r + tq - c
    rel = (rel_table - rel_table[REL_BUCKETS - 1][None, :]) * np.float32(LOG2E)
    onehot = jax.nn.one_hot(_rel_bucket(dist), REL_BUCKETS, dtype=F32)
    b = jnp.einsum("rcb,bh->hrc", onehot, rel, precision=lax.Precision.HIGHEST)
    b = jnp.where((dist >= 0)[None], b, MASKED)
    heads = b.shape[0]
    tiles = jnp.stack([jnp.zeros((heads, tq, tq), F32), b[:, :, :tq], b[:, :, tq:],
                       jnp.full((heads, tq, tq), MASKED, F32)], axis=1)
    tiles = tiles.reshape(groups, nh, 4, tq, tq).transpose(0, 2, 1, 3, 4)
    return tiles.reshape(groups, 4, nh * tq, tq)


def _diff_attention(qkv, near, lam, subln_g, lam_scale, tq):
    s = qkv.shape[0]
    hd = A_V_DIM
    nh = 2
    kb = 1
    kcol = A_WIDTH // hd
    return pl.pallas_call(
        functools.partial(_flash_kernel, nh=nh, tq=tq, kb=kb, sub=min(tq, ATTN_SUB_ROWS), diff=True,
                          lam_scale=lam_scale),
        grid=(A_HEADS, s // tq),
        in_specs=[pl.BlockSpec(memory_space=pltpu.SMEM),
                  pl.BlockSpec((tq, hd), lambda h, i: (i, h)),
                  pl.BlockSpec((s, hd), lambda h, i: (0, kcol + h)),
                  pl.BlockSpec((s, hd), lambda h, i: (0, 2 * kcol + h)),
                  pl.BlockSpec((1, 4, nh * tq, tq), lambda h, i: (h, 0, 0, 0)),
                  pl.BlockSpec((1, hd), lambda h, i: (0, 0))],
        out_specs=pl.BlockSpec((tq, hd), lambda h, i: (i, h)),
        out_shape=jax.ShapeDtypeStruct((s, A_WIDTH), BF16),
        scratch_shapes=[pltpu.VMEM((nh * tq, hd), BF16),
                        pltpu.VMEM((nh * tq, LANES), F32),
                        pltpu.VMEM((nh * tq, 2 * hd), F32),
                        pltpu.VMEM((2, nh * tq, kb * tq), F32),
                        pltpu.VMEM((2, nh * tq, kb * tq), BF16),
                        pltpu.VMEM((2, nh * tq, LANES), F32)],
        compiler_params=_cparams(("parallel", "arbitrary")),
        name="diff_attention",
    )(lam, qkv, qkv, qkv, near, subln_g)


def _masked_attention(qkv, near, mask, tq):
    s = qkv.shape[0]
    hd = C_HEAD_DIM
    nh = C_GROUP
    kcol = C_WIDTH // hd
    vcol = kcol + C_KV_HEADS
    nk = s // tq
    kb = 2 if nk % 2 == 0 else 1
    return pl.pallas_call(
        functools.partial(_flash_kernel, nh=nh, tq=tq, kb=kb, sub=min(tq, ATTN_SUB_ROWS), diff=False,
                          lam_scale=1.0),
        grid=(C_KV_HEADS, s // tq),
        in_specs=[pl.BlockSpec((tq, nh * hd), lambda g, i: (i, g)),
                  pl.BlockSpec((s, hd), lambda g, i: (0, kcol + g)),
                  pl.BlockSpec((s, hd), lambda g, i: (0, vcol + g)),
                  pl.BlockSpec((1, 4, nh * tq, tq), lambda g, i: (g, 0, 0, 0)),
                  pl.BlockSpec((nk, tq, tq), lambda g, i: (0, i, 0))],
        out_specs=pl.BlockSpec((tq, nh * hd), lambda g, i: (i, g)),
        out_shape=jax.ShapeDtypeStruct((s, C_WIDTH), BF16),
        scratch_shapes=[pltpu.VMEM((nh * tq, hd), BF16),
                        pltpu.VMEM((nh * tq, LANES), F32),
                        pltpu.VMEM((nh * tq, 2 * hd), F32),
                        pltpu.VMEM((2, nh * tq, kb * tq), F32),
                        pltpu.VMEM((2, nh * tq, kb * tq), BF16),
                        pltpu.VMEM((2, nh * tq, LANES), F32)],
        compiler_params=_cparams(("parallel", "arbitrary")),
        name="selected_attention",
    )(qkv, qkv, qkv, near, mask)


def _sg_kernel(zb_ref, lng_ref, lnb_ref, w_ref, bs_ref, o_ref, *, nchunk):
    zb = zb_ref[...]
    gl = zb * (0.5 * (1.0 + jnp.tanh(np.float32(np.sqrt(2.0 / np.pi)) * (zb + 0.044715 * (zb * zb * zb)))))
    u = gl[:, :B_WIDTH]
    z = gl[:, B_WIDTH:]
    mu = jnp.mean(z, axis=-1, keepdims=True)
    zc = z - mu
    var = jnp.mean(zc * zc, axis=-1, keepdims=True)
    zn = (zc * lax.rsqrt(var + EPS) * lng_ref[...] + lnb_ref[...]).astype(BF16)
    row = lax.broadcasted_iota(I32, (CHUNK, CHUNK), 0)
    col = lax.broadcasted_iota(I32, (CHUNK, CHUNK), 1)
    for g in range(B_GROUPS):
        w = jnp.where(row >= col, w_ref[g], 0.0).astype(BF16)
        bias = bs_ref[g]
        lo = g * B_GROUP_DIM
        for c in range(nchunk):
            r0 = c * CHUNK
            sz = jnp.dot(w, zn[r0:r0 + CHUNK, lo:lo + B_GROUP_DIM], preferred_element_type=F32) + bias
            o_ref[r0:r0 + CHUNK, lo:lo + B_GROUP_DIM] = (u[r0:r0 + CHUNK, lo:lo + B_GROUP_DIM] * sz).astype(o_ref.dtype)


def _spatial_gating(zb, ln_g, ln_b, w_s, b_s):
    s = zb.shape[0]
    t = _tile(s, 256)
    return pl.pallas_call(
        functools.partial(_sg_kernel, nchunk=t // CHUNK),
        grid=(s // t,),
        in_specs=[pl.BlockSpec((t, 2 * B_WIDTH), lambda i: (i, 0)),
                  pl.BlockSpec((1, B_WIDTH), lambda i: (0, 0)),
                  pl.BlockSpec((1, B_WIDTH), lambda i: (0, 0)),
                  pl.BlockSpec((B_GROUPS, CHUNK, CHUNK), lambda i: (0, 0, 0)),
                  pl.BlockSpec((B_GROUPS, CHUNK, 1), lambda i: (0, 0, 0))],
        out_specs=pl.BlockSpec((t, B_WIDTH), lambda i: (i, 0)),
        out_shape=jax.ShapeDtypeStruct((s, B_WIDTH), BF16),
        compiler_params=_cparams(("parallel",)),
        name="spatial_gating",
    )(zb, ln_g.reshape(1, B_WIDTH), ln_b.reshape(1, B_WIDTH), w_s, b_s.reshape(B_GROUPS, CHUNK, 1))


def _select_kernel(qi_ref, kt_ref, w_ref, o_ref, keys_sc, sc_sc, wb_sc, mx_sc, *, tkc, topk):
    tqi = IDX_QBLOCK
    i = pl.program_id(0)
    nk = o_ref.shape[0]
    nch = (i * tqi + tqi + tkc - 1) // tkc
    qpos = i * tqi + lax.broadcasted_iota(I32, (tqi, tkc), 0)
    kloc = lax.broadcasted_iota(I32, (tqi, tkc), 1)
    qrow = i * tqi + lax.broadcasted_iota(I32, (tqi, LANES), 0)
    klane = lax.broadcasted_iota(I32, (tqi, LANES), 1)

    wgt = w_ref[...] * np.float32(IDX_DIM ** -0.5)
    for h in range(IDX_HEADS):
        wb_sc[h] = jnp.broadcast_to(wgt[:, h:h + 1], (tqi, LANES))
    mx_sc[...] = jnp.full(mx_sc.shape, -jnp.inf, F32)

    def stage_dot(c, slot):
        sc_sc[slot] = jnp.dot(qi_ref[0], kt_ref[jnp.minimum(c, nch - 1)], preferred_element_type=F32)

    def stage_reduce(c, slot):
        c = jnp.minimum(c, nch - 1)
        for u in range(tkc // LANES):
            ls = slice(u * LANES, (u + 1) * LANES)
            acc = jnp.zeros((tqi, LANES), F32)
            for h in range(IDX_HEADS):
                acc += jnp.maximum(sc_sc[slot, h * tqi:(h + 1) * tqi, ls], 0.0) * wb_sc[h]
            acc = acc + 0.0
            bits = pltpu.bitcast(acc, I32)
            ordered = jnp.where(bits < 0, bits ^ jnp.int32(0x7FFFFFFF), bits)
            causal = c * tkc + u * LANES + klane <= qrow
            keys_sc[c, :, ls] = jnp.where(causal, ordered, jnp.int32(INT_MIN))
            mx_sc[...] = jnp.maximum(mx_sc[...], jnp.where(causal, acc, -jnp.inf))

    stage_dot(0, 0)

    def score_quad(u, carry):
        c = 4 * u
        for k in range(0, 4, 2):
            stage_dot(c + k + 1, 1)
            stage_reduce(c + k, 0)
            stage_dot(c + k + 2, 0)
            stage_reduce(c + k + 1, 1)
        return carry

    lax.fori_loop(0, (nch + 3) // 4, score_quad, 0)

    def count_ge(cand):
        candb = jnp.broadcast_to(cand, (tqi, LANES))

        def count_body(c, cnt):
            kk = keys_sc[c]
            for u in range(tkc // LANES):
                cnt += jnp.where(kk[:, u * LANES:(u + 1) * LANES] >= candb, 1, 0)
            return cnt

        cnt = lax.fori_loop(0, nch, count_body, jnp.zeros((tqi, LANES), I32))
        return jnp.sum(cnt.astype(F32), axis=1, keepdims=True)

    want = np.float32(topk)
    fbits = pltpu.bitcast(jnp.max(mx_sc[...], axis=1, keepdims=True), I32)
    kmax = jnp.where(fbits < 0, fbits ^ jnp.int32(0x7FFFFFFF), fbits)
    few = i * tqi + lax.broadcasted_iota(I32, (tqi, 1), 0) + 1 <= topk
    lo0 = jnp.full((tqi, 1), INT_MIN, I32)
    hi0 = jnp.where(few, lo0 + 1, kmax + 1)

    def narrow(state, cand):
        lo, hi, active = state
        total = count_ge(cand)
        open_ = active > 0.0
        up = open_ & (total >= want)
        lo = jnp.where(up, cand, lo)
        hi = jnp.where(open_ & (~up), cand, hi)
        open_ = open_ & (~(up & (total == want))) & ((hi - lo) != 1)
        return lo, hi, jnp.where(open_, 1.0, 0.0)

    def midpoint(state):
        lo, hi, _ = state
        return lo + lax.shift_right_logical(hi - lo, jnp.int32(1))

    probe = jnp.maximum(kmax, jnp.int32(INT_MIN + SEARCH_PROBE_DROP + 1)) - jnp.int32(SEARCH_PROBE_DROP)
    state = narrow((lo0, hi0, jnp.where(few, 0.0, 1.0)), jnp.where(few, lo0, probe))

    def bisect_body(carry):
        state, _ = carry
        state = narrow(state, midpoint(state))
        state = narrow(state, midpoint(state))
        return state, jnp.sum(state[2])

    (thr, _, _), _ = lax.while_loop(lambda carry: carry[1] > 0.0, bisect_body, (state, jnp.float32(1.0)))
    thrb = jnp.broadcast_to(thr, (tqi, tkc))

    def mask_body(c, carry):
        sel = (keys_sc[c] >= thrb) & (c * tkc + kloc <= qpos)
        o_ref[c] = jnp.where(sel, 0.0, MASKED).astype(o_ref.dtype)
        return carry

    lax.fori_loop(0, nch, mask_body, 0)

    def fill_body(c, carry):
        o_ref[c] = jnp.full((tqi, tkc), MASKED, o_ref.dtype)
        return carry

    lax.fori_loop(nch, nk, fill_body, 0)


def _select_mask(qi_stack, kt, wi, tkc, topk):
    nq, rows, _ = qi_stack.shape
    nk = kt.shape[0]
    s = nq * IDX_QBLOCK
    return pl.pallas_call(
        functools.partial(_select_kernel, tkc=tkc, topk=topk),
        grid=(nq,),
        in_specs=[pl.BlockSpec((1, rows, IDX_DIM), lambda i: (i, 0, 0)),
                  pl.BlockSpec((nk, IDX_DIM, tkc), lambda i: (0, 0, 0)),
                  pl.BlockSpec((IDX_QBLOCK, IDX_HEADS), lambda i: (i, 0))],
        out_specs=pl.BlockSpec((nk, IDX_QBLOCK, tkc), lambda i: (0, i, 0)),
        out_shape=jax.ShapeDtypeStruct((nk, s, tkc), BF16),
        scratch_shapes=[pltpu.VMEM((nk, IDX_QBLOCK, tkc), I32),
                        pltpu.VMEM((2, rows, tkc), F32),
                        pltpu.VMEM((IDX_HEADS, IDX_QBLOCK, LANES), F32),
                        pltpu.VMEM((IDX_QBLOCK, LANES), F32)],
        compiler_params=_cparams(("parallel",)),
        name="indexer_select",
    )(qi_stack, kt, wi)


def _vec_pack(d, *rows):
    rows = [r.reshape(1, d).astype(F32) for r in rows]
    rows += [jnp.zeros((1, d), F32)] * (8 - len(rows))
    return jnp.concatenate(rows, axis=0)


def _pad_cols(w, n):
    return jnp.pad(w, ((0, 0), (0, n - w.shape[1])))


def kernel(x, c, norm_g, mod_w, mod_b, ffn_w1, ffn_w2, rel_table, ab_w_in, ab_w_out, diff_lam,
           diff_subln_g, sg_ln_g, sg_ln_b, sg_w, sg_b, dsa_w_in, dsa_w_out, final_g):
    batch, s, d = x.shape
    depth = norm_g.shape[0]
    assert batch == 1 and s % IDX_QBLOCK == 0

    tq_a = _tile(s, 512)
    tq_c = _tile(s, 256)
    topk = min(TOPK_MAX, s // 4)

    mod = _modulation(c, mod_w, mod_b).reshape(depth, 9, d)
    ffn_a, ffn_b = _ffn_weights(ffn_w1, ffn_w2)
    near_a = _near_bias(rel_table, tq_a, A_HEADS, 2)
    near_c = _near_bias(rel_table, tq_c, C_KV_HEADS, C_GROUP)

    xs = x.reshape(s, d)
    zeros_d = jnp.zeros((d,), F32)
    for li in range(depth):
        def vec(j, li=li):
            last = final_g if (li == depth - 1 and j == 2) else zeros_d
            return _vec_pack(d, norm_g[li, j], mod[li, 3 * j], mod[li, 3 * j + 1], mod[li, 3 * j + 2], last)

        def ffn(xs, j, k, final=False, li=li):
            return _ffn(xs, vec(j), ffn_a, ffn_b, 2 * li + k, final=final)

        xs = ffn(xs, 0, 0)

        v1 = vec(1)
        jj = li // 2
        if li % 2 == 0:
            w_in = ab_w_in[jj]
            w_qkv = jnp.concatenate([w_in[:, :A_WIDTH] * np.float32(A_QK_DIM ** -0.5 * LOG2E),
                                     w_in[:, A_WIDTH:3 * A_WIDTH]], axis=1).astype(BF16)
            w_zb = w_in[:, 3 * A_WIDTH:].astype(BF16)
            qkv = _proj(xs, v1, _col_blocks(w_qkv, PROJ_COLS), BF16)
            zb = _proj(xs, v1, _col_blocks(w_zb, PROJ_COLS), F32)
            lam_init = 0.8 - 0.6 * math.exp(-0.3 * li)
            lp = diff_lam[jj].astype(F32)
            lam = jnp.exp(jnp.sum(lp[0] * lp[1])) - jnp.exp(jnp.sum(lp[2] * lp[3])) + lam_init
            ya = _diff_attention(qkv, near_a, lam.reshape(1), diff_subln_g[jj].reshape(1, A_V_DIM),
                                 1.0 - lam_init, tq_a)
            yb = _spatial_gating(zb, sg_ln_g[jj], sg_ln_b[jj], sg_w[jj], sg_b[jj])
            w_out = ab_w_out[jj].astype(BF16)
            xs = _outproj(xs, v1, [ya, yb], [_col_blocks(w_out[:A_WIDTH], PROJ_COLS),
                                             _col_blocks(w_out[A_WIDTH:], PROJ_COLS)])
        else:
            w_in = dsa_w_in[jj]
            o_idx = C_WIDTH + 2 * C_KV_WIDTH
            o_ki = o_idx + IDX_HEADS * IDX_DIM
            w_main = jnp.concatenate([w_in[:, :C_WIDTH] * np.float32(C_HEAD_DIM ** -0.5 * LOG2E),
                                      w_in[:, C_WIDTH:o_ki]], axis=1).astype(BF16)
            w_kiw = _pad_cols(w_in[:, o_ki:], LANES).astype(BF16)
            main = _proj(xs, v1, _col_blocks(w_main, PROJ_COLS), BF16)
            kiw = _proj(xs, v1, _col_blocks(w_kiw, PROJ_COLS), F32)
            nq = s // IDX_QBLOCK
            qi = main[:, o_idx:o_ki].reshape(nq, IDX_QBLOCK, IDX_HEADS, IDX_DIM)
            qi = qi.transpose(0, 2, 1, 3).reshape(nq, IDX_HEADS * IDX_QBLOCK, IDX_DIM)
            kt = kiw[:, :IDX_DIM].astype(BF16).reshape(s // tq_c, tq_c, IDX_DIM).transpose(0, 2, 1)
            wi = kiw[:, IDX_DIM:IDX_DIM + IDX_HEADS]
            mask = _select_mask(qi, kt, wi, tq_c, topk)
            yc = _masked_attention(main, near_c, mask, tq_c)
            xs = _outproj(xs, v1, [yc], [_col_blocks(dsa_w_out[jj].astype(BF16), PROJ_COLS)])

        xs = ffn(xs, 2, 1, final=(li == depth - 1))
    return xs.reshape(batch, s, d)
```

```python
import functools
import math

import jax
import jax.numpy as jnp
import numpy as np
from jax import lax
from jax.experimental import pallas as pl
from jax.experimental.pallas import tpu as pltpu

F32 = jnp.float32
BF16 = jnp.bfloat16
I32 = jnp.int32

EPS = 1e-6
MASKED = -1e30
LANES = 128
INT_MIN = -(2 ** 31)
LOG2E = math.log2(math.e)
NORM_ROWS = 16
ATTN_SUB_ROWS = 128

A_HEADS = 8
A_QK_DIM = 64
A_V_DIM = 128
A_WIDTH = A_HEADS * A_V_DIM
B_GROUPS = 8
B_GROUP_DIM = 128
B_WIDTH = B_GROUPS * B_GROUP_DIM
CHUNK = 128
C_HEADS = 16
C_KV_HEADS = 4
C_GROUP = C_HEADS // C_KV_HEADS
C_HEAD_DIM = 128
C_WIDTH = C_HEADS * C_HEAD_DIM
C_KV_WIDTH = C_KV_HEADS * C_HEAD_DIM
IDX_HEADS = 16
IDX_DIM = 64
TOPK_MAX = 256
REL_BUCKETS = 32
REL_MAX_DIST = 128
IDX_QBLOCK = 128
SEARCH_PROBE_DROP = 2 ** 24

VMEM_LIMIT = 56 * 1024 * 1024


def _cparams(sem):
    return pltpu.CompilerParams(dimension_semantics=sem, vmem_limit_bytes=VMEM_LIMIT)


FFN_ROWS, FFN_COLS = 512, 512
PROJ_ROWS, PROJ_COLS = 1024, 1024


def _col_blocks(w, tn):
    k, n = w.shape
    tn = _tile(n, tn)
    return w.reshape(k, n // tn, tn).transpose(1, 0, 2)


def _tile(n, want):
    if n <= want:
        return n
    t = want
    while n % t:
        t //= 2
    return t


def _mod_kernel(c_ref, w_ref, b_ref, o_ref):
    c = c_ref[...]
    cs = c * (1.0 / (1.0 + jnp.exp(-c)))
    o_ref[0] = jnp.sum(cs * w_ref[0], axis=0, keepdims=True) + b_ref[0]


def _modulation(c, mod_w, mod_b):
    depth, d, n = mod_w.shape
    tn = _tile(n, 1024)
    out = pl.pallas_call(
        _mod_kernel,
        grid=(depth, n // tn),
        in_specs=[pl.BlockSpec((d, 1), lambda l, j: (0, 0)),
                  pl.BlockSpec((1, d, tn), lambda l, j: (l, 0, j)),
                  pl.BlockSpec((1, 1, tn), lambda l, j: (l, 0, j))],
        out_specs=pl.BlockSpec((1, 1, tn), lambda l, j: (l, 0, j)),
        out_shape=jax.ShapeDtypeStruct((depth, 1, n), F32),
        compiler_params=_cparams(("arbitrary", "arbitrary")),
        name="adaln_mod",
    )(c.reshape(d, 1), mod_w, mod_b.reshape(depth, 1, n))
    return out.reshape(depth, n)


def _prenorm_into(x_ref, vec_ref, hn_ref, inv_ref):
    rows, d = x_ref.shape
    gain = vec_ref[0:1, :] * (1.0 + vec_ref[2:3, :])
    shift = vec_ref[1:2, :]

    def scale_body(r, carry):
        rs = pl.ds(pl.multiple_of(r * NORM_ROWS, NORM_ROWS), NORM_ROWS)
        x = x_ref[rs, :]
        ms = jnp.sum(x * x, axis=-1, keepdims=True) * np.float32(1.0 / d)
        inv_ref[rs, :] = jnp.broadcast_to(lax.rsqrt(ms + EPS), (NORM_ROWS, LANES))
        return carry

    lax.fori_loop(0, rows // NORM_ROWS, scale_body, 0, unroll=8)

    def apply_body(r, carry):
        rs = pl.ds(pl.multiple_of(r * NORM_ROWS, NORM_ROWS), NORM_ROWS)
        inv = inv_ref[rs, :]
        inv = jnp.concatenate([inv] * (d // LANES), axis=1)
        hn_ref[rs, :] = (x_ref[rs, :] * inv * gain + shift).astype(hn_ref.dtype)
        return carry

    lax.fori_loop(0, rows // NORM_ROWS, apply_body, 0, unroll=2)


def _ffn_kernel(x_ref, vec_ref, w1g_ref, w1u_ref, w2_ref, o_ref, hn_sc, inv_sc, *, nf, final):
    f = pl.program_id(1)

    @pl.when(f == 0)
    def _():
        _prenorm_into(x_ref, vec_ref, hn_sc, inv_sc)
        o_ref[...] = jnp.zeros_like(o_ref)

    hn = hn_sc[...]
    g = jnp.dot(hn, w1g_ref[...], preferred_element_type=F32)
    u = jnp.dot(hn, w1u_ref[...], preferred_element_type=F32)
    a = (g * (1.0 / (1.0 + jnp.exp(-g))) * u).astype(BF16)
    o_ref[...] += jnp.dot(a, w2_ref[...], preferred_element_type=F32)

    @pl.when(f == nf - 1)
    def _():
        y = x_ref[...] + 0.5 * (1.0 + vec_ref[3:4, :]) * o_ref[...]
        if final:
            ms = jnp.mean(y * y, axis=-1, keepdims=True)
            y = y * lax.rsqrt(ms + EPS) * vec_ref[4:5, :]
        o_ref[...] = y


def _ffn_weights(ffn_w1, ffn_w2):
    depth, two, d, _ = ffn_w1.shape
    d_ff = ffn_w2.shape[2]
    tf = FFN_COLS if d_ff > FFN_COLS else d_ff
    nf = -(-d_ff // tf)
    pad = nf * tf - d_ff
    w1 = ffn_w1.reshape(depth * two, d, 2 * d_ff).astype(BF16)
    w2 = ffn_w2.reshape(depth * two, d_ff, d).astype(BF16)
    if pad:
        zc = jnp.zeros((depth * two, d, pad), BF16)
        w1 = jnp.concatenate([w1[:, :, :d_ff], zc, w1[:, :, d_ff:], zc], axis=2)
        w2 = jnp.concatenate([w2, jnp.zeros((depth * two, pad, d), BF16)], axis=1)
    return w1, w2, nf


def _ffn(x, vec, w1, w2, nf, step, *, final):
    s, d = x.shape
    tf = w2.shape[1] // nf
    tm = _tile(s, FFN_ROWS)
    return pl.pallas_call(
        functools.partial(_ffn_kernel, nf=nf, final=final),
        grid=(s // tm, nf),
        in_specs=[pl.BlockSpec((tm, d), lambda i, f: (i, 0)),
                  pl.BlockSpec((8, d), lambda i, f: (0, 0)),
                  pl.BlockSpec((None, d, tf), lambda i, f: (step, 0, f)),
                  pl.BlockSpec((None, d, tf), lambda i, f: (step, 0, nf + f)),
                  pl.BlockSpec((None, tf, d), lambda i, f: (step, f, 0))],
        out_specs=pl.BlockSpec((tm, d), lambda i, f: (i, 0)),
        out_shape=jax.ShapeDtypeStruct((s, d), F32),
        scratch_shapes=[pltpu.VMEM((tm, d), BF16), pltpu.VMEM((tm, LANES), F32)],
        compiler_params=_cparams(("parallel", "arbitrary")),
        name="swiglu_halfstep",
    )(x, vec, w1, w1, w2)


def _proj_kernel(x_ref, vec_ref, w_ref, o_ref, hn_sc, inv_sc):
    @pl.when(pl.program_id(1) == 0)
    def _():
        _prenorm_into(x_ref, vec_ref, hn_sc, inv_sc)

    o_ref[...] = jnp.dot(hn_sc[...], w_ref[0], preferred_element_type=F32).astype(o_ref.dtype)


def _proj(x, vec, w, out_dtype):
    s, d = x.shape
    nn, _, tn = w.shape
    n = nn * tn
    tm = _tile(s, PROJ_ROWS)
    return pl.pallas_call(
        _proj_kernel,
        grid=(s // tm, nn),
        in_specs=[pl.BlockSpec((tm, d), lambda i, j: (i, 0)),
                  pl.BlockSpec((8, d), lambda i, j: (0, 0)),
                  pl.BlockSpec((1, d, tn), lambda i, j: (j, 0, 0))],
        out_specs=pl.BlockSpec((tm, tn), lambda i, j: (i, j)),
        out_shape=jax.ShapeDtypeStruct((s, n), out_dtype),
        scratch_shapes=[pltpu.VMEM((tm, d), BF16), pltpu.VMEM((tm, LANES), F32)],
        compiler_params=_cparams(("parallel", "arbitrary")),
        name="norm_mod_proj",
    )(x, vec, w)


def _outproj_kernel(*refs, n_in):
    x_ref, vec_ref = refs[0], refs[1]
    lhs = refs[2:2 + n_in]
    ws = refs[2 + n_in:2 + 2 * n_in]
    o_ref = refs[2 + 2 * n_in]
    acc = jnp.dot(lhs[0][...], ws[0][0], preferred_element_type=F32)
    for a, w in zip(lhs[1:], ws[1:]):
        acc += jnp.dot(a[...], w[0], preferred_element_type=F32)
    o_ref[...] = x_ref[...] + (1.0 + vec_ref[3:4, :]) * acc


def _outproj(x, vec, lhs, ws):
    s, d = x.shape
    tm = _tile(s, PROJ_ROWS)
    tn = ws[0].shape[2]
    n_in = len(lhs)
    in_specs = [pl.BlockSpec((tm, tn), lambda i, j: (i, j)),
                pl.BlockSpec((8, tn), lambda i, j: (0, j))]
    in_specs += [pl.BlockSpec((tm, a.shape[1]), lambda i, j: (i, 0)) for a in lhs]
    in_specs += [pl.BlockSpec((1, w.shape[1], tn), lambda i, j: (j, 0, 0)) for w in ws]
    return pl.pallas_call(
        functools.partial(_outproj_kernel, n_in=n_in),
        grid=(s // tm, d // tn),
        in_specs=in_specs,
        out_specs=pl.BlockSpec((tm, tn), lambda i, j: (i, j)),
        out_shape=jax.ShapeDtypeStruct((s, d), F32),
        compiler_params=_cparams(("parallel", "arbitrary")),
        name="outproj_residual",
    )(x, vec, *lhs, *ws)


def _flash_kernel(*refs, nh, tq, kb, sub, diff, lam_scale):
    if diff:
        lam_ref, q_ref, k_ref, v_ref, nb_ref, g_ref, o_ref, qs_sc, m_sc, acc_sc, s_sc, p_sc, al_sc = refs
        mask_ref = None
    else:
        q_ref, k_ref, v_ref, nb_ref, mask_ref, o_ref, qs_sc, m_sc, acc_sc, s_sc, p_sc, al_sc = refs
    tk = tq
    rows = nh * tq
    hd = k_ref.shape[1]
    assert hd == LANES and rows % sub == 0 and tq % sub == 0
    i = pl.program_id(1)

    if diff:
        q = q_ref[...]
        lane = lax.broadcasted_iota(I32, q.shape, 1)
        zero = jnp.zeros_like(q)
        qs_sc[0:tq, :] = jnp.where(lane < A_QK_DIM, q, zero)
        qs_sc[tq:2 * tq, :] = jnp.where(lane >= A_QK_DIM, q, zero)
    else:
        for r in range(nh):
            qs_sc[r * tq:(r + 1) * tq, :] = q_ref[:, r * hd:(r + 1) * hd]
    m_sc[...] = jnp.full(m_sc.shape, -jnp.inf, F32)
    acc_sc[...] = jnp.zeros(acc_sc.shape, F32)

    tks = kb * tk
    nk = k_ref.shape[0] // tk
    last_step = i // kb

    def bias_index(j):
        return jnp.where(j > i, 3, jnp.clip(j - (i - 2), 0, 2))

    def stage_qk(t, slot):
        start = pl.multiple_of(jnp.minimum(t, last_step) * tks, tks)
        kblk = k_ref[pl.ds(start, tks), :]
        s_sc[slot] = lax.dot_general(qs_sc[...], kblk, (((1,), (1,)), ((), ())), preferred_element_type=F32)

    def stage_softmax(t, slot, biased):
        tc = jnp.minimum(t, last_step)
        for r in range(rows // sub):
            rs = slice(r * sub, (r + 1) * sub)
            tiles = []
            for b in range(kb):
                j = tc * kb + b
                s = s_sc[slot, rs, b * tk:(b + 1) * tk]
                if biased:
                    col = jnp.where(t > last_step, 3, bias_index(j))
                    s = s + nb_ref[0, col, rs, :]
                s = s.astype(BF16)
                if mask_ref is not None:
                    off = (r * sub) % tq
                    s = s + mask_ref[jnp.minimum(j, nk - 1), off:off + sub, :]
                tiles += [s[:, u * LANES:(u + 1) * LANES] for u in range(tk // LANES)]
            cmax = tiles[0]
            for u in tiles[1:]:
                cmax = jnp.maximum(cmax, u)
            m_old = m_sc[rs, :]
            m_new = jnp.maximum(m_old, jnp.max(cmax.astype(F32), axis=1, keepdims=True))
            al_sc[slot, rs, :] = jnp.exp2(m_old - m_new)
            m_b = m_new.astype(BF16)
            p_sc[slot, rs, :] = jnp.concatenate([jnp.exp2(u - m_b) for u in tiles], axis=1)
            m_sc[rs, :] = m_new

    def stage_pv(t, slot):
        start = pl.multiple_of(jnp.minimum(t, last_step) * tks, tks)
        vbe = jnp.concatenate([v_ref[pl.ds(start, tks), :], jnp.ones((tks, hd), BF16)], axis=1)
        pv = jnp.dot(p_sc[slot], vbe, preferred_element_type=F32)
        alpha = al_sc[slot]
        acc_sc[...] = jnp.concatenate([alpha, alpha], axis=1) * acc_sc[...] + pv

    stage_qk(0, 0)
    stage_qk(1, 1)
    stage_softmax(0, 0, True)

    nfar = jnp.maximum((i - 1) // kb, 0)
    npairs = jnp.maximum((nfar - 1) // 2, 0)

    def far_pair(u, carry):
        t = 2 * u
        stage_pv(t, 0)
        stage_softmax(t + 1, 1, False)
        stage_qk(t + 2, 0)
        stage_pv(t + 1, 1)
        stage_softmax(t + 2, 0, False)
        stage_qk(t + 3, 1)
        return carry

    lax.fori_loop(0, npairs, far_pair, 0)

    t0 = 2 * npairs
    stage_pv(t0, 0)
    stage_softmax(t0 + 1, 1, True)
    stage_qk(t0 + 2, 0)
    stage_pv(t0 + 1, 1)
    stage_softmax(t0 + 2, 0, True)
    stage_qk(t0 + 3, 1)
    stage_pv(t0 + 2, 0)
    stage_softmax(t0 + 3, 1, True)
    stage_pv(t0 + 3, 1)

    if diff:
        o0 = acc_sc[0:tq, 0:hd] / acc_sc[0:tq, hd:2 * hd]
        o1 = acc_sc[tq:2 * tq, 0:hd] / acc_sc[tq:2 * tq, hd:2 * hd]
        dlt = o0 - lam_ref[0] * o1
        ms = jnp.mean(dlt * dlt, axis=-1, keepdims=True)
        o_ref[...] = ((dlt * lax.rsqrt(ms + EPS) * g_ref[...]) * lam_scale).astype(o_ref.dtype)
    else:
        for r in range(nh):
            rs = slice(r * tq, (r + 1) * tq)
            o_ref[:, r * hd:(r + 1) * hd] = (acc_sc[rs, 0:hd] / acc_sc[rs, hd:2 * hd]).astype(o_ref.dtype)


def _rel_bucket(dist):
    n = jnp.maximum(dist, 0)
    max_exact = REL_BUCKETS // 2
    nf = jnp.maximum(n, 1).astype(F32)
    large = max_exact + (jnp.log(nf / max_exact) / math.log(REL_MAX_DIST / max_exact)
                         * (REL_BUCKETS - max_exact)).astype(I32)
    large = jnp.minimum(large, REL_BUCKETS - 1)
    return jnp.where(n < max_exact, n, large)


def _near_bias(rel_table, tq, groups, nh):
    assert tq >= LANES, "keys older than one block must all fall in the last bucket"
    r = jnp.arange(tq, dtype=I32)[:, None]
    c = jnp.arange(2 * tq, dtype=I32)[None, :]
    dist = r + tq - c
    rel = (rel_table - rel_table[REL_BUCKETS - 1][None, :]) * np.float32(LOG2E)
    onehot = jax.nn.one_hot(_rel_bucket(dist), REL_BUCKETS, dtype=F32)
    b = jnp.einsum("rcb,bh->hrc", onehot, rel, precision=lax.Precision.HIGHEST)
    b = jnp.where((dist >= 0)[None], b, MASKED)
    heads = b.shape[0]
    tiles = jnp.stack([jnp.zeros((heads, tq, tq), F32), b[:, :, :tq], b[:, :, tq:],
                       jnp.full((heads, tq, tq), MASKED, F32)], axis=1)
    tiles = tiles.reshape(groups, nh, 4, tq, tq).transpose(0, 2, 1, 3, 4)
    return tiles.reshape(groups, 4, nh * tq, tq)


def _diff_attention(qkv, near, lam, subln_g, lam_scale, tq):
    s = qkv.shape[0]
    hd = A_V_DIM
    nh = 2
    kb = 1
    kcol = A_WIDTH // hd
    return pl.pallas_call(
        functools.partial(_flash_kernel, nh=nh, tq=tq, kb=kb, sub=min(tq, ATTN_SUB_ROWS), diff=True,
                          lam_scale=lam_scale),
        grid=(A_HEADS, s // tq),
        in_specs=[pl.BlockSpec(memory_space=pltpu.SMEM),
                  pl.BlockSpec((tq, hd), lambda h, i: (i, h)),
                  pl.BlockSpec((s, hd), lambda h, i: (0, kcol + h)),
                  pl.BlockSpec((s, hd), lambda h, i: (0, 2 * kcol + h)),
                  pl.BlockSpec((1, 4, nh * tq, tq), lambda h, i: (h, 0, 0, 0)),
                  pl.BlockSpec((1, hd), lambda h, i: (0, 0))],
        out_specs=pl.BlockSpec((tq, hd), lambda h, i: (i, h)),
        out_shape=jax.ShapeDtypeStruct((s, A_WIDTH), BF16),
        scratch_shapes=[pltpu.VMEM((nh * tq, hd), BF16),
                        pltpu.VMEM((nh * tq, LANES), F32),
                        pltpu.VMEM((nh * tq, 2 * hd), F32),
                        pltpu.VMEM((2, nh * tq, kb * tq), F32),
                        pltpu.VMEM((2, nh * tq, kb * tq), BF16),
                        pltpu.VMEM((2, nh * tq, LANES), F32)],
        compiler_params=_cparams(("parallel", "arbitrary")),
        name="diff_attention",
    )(lam, qkv, qkv, qkv, near, subln_g)


def _masked_attention(qkv, near, mask, tq):
    s = qkv.shape[0]
    hd = C_HEAD_DIM
    nh = C_GROUP
    kcol = C_WIDTH // hd
    vcol = kcol + C_KV_HEADS
    nk = s // tq
    kb = 2 if nk % 2 == 0 else 1
    return pl.pallas_call(
        functools.partial(_flash_kernel, nh=nh, tq=tq, kb=kb, sub=min(tq, ATTN_SUB_ROWS), diff=False,
                          lam_scale=1.0),
        grid=(C_KV_HEADS, s // tq),
        in_specs=[pl.BlockSpec((tq, nh * hd), lambda g, i: (i, g)),
                  pl.BlockSpec((s, hd), lambda g, i: (0, kcol + g)),
                  pl.BlockSpec((s, hd), lambda g, i: (0, vcol + g)),
                  pl.BlockSpec((1, 4, nh * tq, tq), lambda g, i: (g, 0, 0, 0)),
                  pl.BlockSpec((nk, tq, tq), lambda g, i: (0, i, 0))],
        out_specs=pl.BlockSpec((tq, nh * hd), lambda g, i: (i, g)),
        out_shape=jax.ShapeDtypeStruct((s, C_WIDTH), BF16),
        scratch_shapes=[pltpu.VMEM((nh * tq, hd), BF16),
                        pltpu.VMEM((nh * tq, LANES), F32),
                        pltpu.VMEM((nh * tq, 2 * hd), F32),
                        pltpu.VMEM((2, nh * tq, kb * tq), F32),
                        pltpu.VMEM((2, nh * tq, kb * tq), BF16),
                        pltpu.VMEM((2, nh * tq, LANES), F32)],
        compiler_params=_cparams(("parallel", "arbitrary")),
        name="selected_attention",
    )(qkv, qkv, qkv, near, mask)


def _sg_kernel(zb_ref, lng_ref, lnb_ref, w_ref, bs_ref, o_ref, *, nchunk):
    zb = zb_ref[...]
    gl = zb * (0.5 * (1.0 + jnp.tanh(np.float32(np.sqrt(2.0 / np.pi)) * (zb + 0.044715 * (zb * zb * zb)))))
    u = gl[:, :B_WIDTH]
    z = gl[:, B_WIDTH:]
    mu = jnp.mean(z, axis=-1, keepdims=True)
    zc = z - mu
    var = jnp.mean(zc * zc, axis=-1, keepdims=True)
    zn = (zc * lax.rsqrt(var + EPS) * lng_ref[...] + lnb_ref[...]).astype(BF16)
    row = lax.broadcasted_iota(I32, (CHUNK, CHUNK), 0)
    col = lax.broadcasted_iota(I32, (CHUNK, CHUNK), 1)
    for g in range(B_GROUPS):
        w = jnp.where(row >= col, w_ref[g], 0.0).astype(BF16)
        bias = bs_ref[g]
        lo = g * B_GROUP_DIM
        for c in range(nchunk):
            r0 = c * CHUNK
            sz = jnp.dot(w, zn[r0:r0 + CHUNK, lo:lo + B_GROUP_DIM], preferred_element_type=F32) + bias
            o_ref[r0:r0 + CHUNK, lo:lo + B_GROUP_DIM] = (u[r0:r0 + CHUNK, lo:lo + B_GROUP_DIM] * sz).astype(o_ref.dtype)


def _spatial_gating(zb, ln_g, ln_b, w_s, b_s):
    s = zb.shape[0]
    t = _tile(s, 256)
    return pl.pallas_call(
        functools.partial(_sg_kernel, nchunk=t // CHUNK),
        grid=(s // t,),
        in_specs=[pl.BlockSpec((t, 2 * B_WIDTH), lambda i: (i, 0)),
                  pl.BlockSpec((1, B_WIDTH), lambda i: (0, 0)),
                  pl.BlockSpec((1, B_WIDTH), lambda i: (0, 0)),
                  pl.BlockSpec((B_GROUPS, CHUNK, CHUNK), lambda i: (0, 0, 0)),
                  pl.BlockSpec((B_GROUPS, CHUNK, 1), lambda i: (0, 0, 0))],
        out_specs=pl.BlockSpec((t, B_WIDTH), lambda i: (i, 0)),
        out_shape=jax.ShapeDtypeStruct((s, B_WIDTH), BF16),
        compiler_params=_cparams(("parallel",)),
        name="spatial_gating",
    )(zb, ln_g.reshape(1, B_WIDTH), ln_b.reshape(1, B_WIDTH), w_s, b_s.reshape(B_GROUPS, CHUNK, 1))


def _select_kernel(qi_ref, kt_ref, w_ref, o_ref, keys_sc, sc_sc, wb_sc, mx_sc, *, tkc, topk):
    tqi = IDX_QBLOCK
    i = pl.program_id(0)
    nk = o_ref.shape[0]
    nch = (i * tqi + tqi + tkc - 1) // tkc
    qpos = i * tqi + lax.broadcasted_iota(I32, (tqi, tkc), 0)
    kloc = lax.broadcasted_iota(I32, (tqi, tkc), 1)
    qrow = i * tqi + lax.broadcasted_iota(I32, (tqi, LANES), 0)
    klane = lax.broadcasted_iota(I32, (tqi, LANES), 1)

    wgt = w_ref[...] * np.float32(IDX_DIM ** -0.5)
    for h in range(IDX_HEADS):
        wb_sc[h] = jnp.broadcast_to(wgt[:, h:h + 1], (tqi, LANES))
    mx_sc[...] = jnp.full(mx_sc.shape, -jnp.inf, F32)

    def stage_dot(c, slot):
        sc_sc[slot] = jnp.dot(qi_ref[0], kt_ref[jnp.minimum(c, nch - 1)], preferred_element_type=F32)

    def stage_reduce(c, slot):
        c = jnp.minimum(c, nch - 1)
        for u in range(tkc // LANES):
            ls = slice(u * LANES, (u + 1) * LANES)
            acc = jnp.zeros((tqi, LANES), F32)
            for h in range(IDX_HEADS):
                acc += jnp.maximum(sc_sc[slot, h * tqi:(h + 1) * tqi, ls], 0.0) * wb_sc[h]
            acc = acc + 0.0
            bits = pltpu.bitcast(acc, I32)
            ordered = jnp.where(bits < 0, bits ^ jnp.int32(0x7FFFFFFF), bits)
            causal = c * tkc + u * LANES + klane <= qrow
            keys_sc[c, :, ls] = jnp.where(causal, ordered, jnp.int32(INT_MIN))
            mx_sc[...] = jnp.maximum(mx_sc[...], jnp.where(causal, acc, -jnp.inf))

    stage_dot(0, 0)

    def score_quad(u, carry):
        c = 4 * u
        for k in range(0, 4, 2):
            stage_dot(c + k + 1, 1)
            stage_reduce(c + k, 0)
            stage_dot(c + k + 2, 0)
            stage_reduce(c + k + 1, 1)
        return carry

    lax.fori_loop(0, (nch + 3) // 4, score_quad, 0)

    def count_ge(cand):
        candb = jnp.broadcast_to(cand, (tqi, LANES))

        def count_body(c, cnt):
            kk = keys_sc[c]
            for u in range(tkc // LANES):
                cnt += jnp.where(kk[:, u * LANES:(u + 1) * LANES] >= candb, 1, 0)
            return cnt

        cnt = lax.fori_loop(0, nch, count_body, jnp.zeros((tqi, LANES), I32))
        return jnp.sum(cnt.astype(F32), axis=1, keepdims=True)

    want = np.float32(topk)
    fbits = pltpu.bitcast(jnp.max(mx_sc[...], axis=1, keepdims=True), I32)
    kmax = jnp.where(fbits < 0, fbits ^ jnp.int32(0x7FFFFFFF), fbits)
    few = i * tqi + lax.broadcasted_iota(I32, (tqi, 1), 0) + 1 <= topk
    lo0 = jnp.full((tqi, 1), INT_MIN, I32)
    hi0 = jnp.where(few, lo0 + 1, kmax + 1)

    def narrow(state, cand):
        lo, hi, active = state
        total = count_ge(cand)
        open_ = active > 0.0
        up = open_ & (total >= want)
        lo = jnp.where(up, cand, lo)
        hi = jnp.where(open_ & (~up), cand, hi)
        open_ = open_ & (~(up & (total == want))) & ((hi - lo) != 1)
        return lo, hi, jnp.where(open_, 1.0, 0.0)

    def midpoint(state):
        lo, hi, _ = state
        return lo + lax.shift_right_logical(hi - lo, jnp.int32(1))

    probe = jnp.maximum(kmax, jnp.int32(INT_MIN + SEARCH_PROBE_DROP + 1)) - jnp.int32(SEARCH_PROBE_DROP)
    state = narrow((lo0, hi0, jnp.where(few, 0.0, 1.0)), jnp.where(few, lo0, probe))

    def bisect_body(carry):
        state, _ = carry
        state = narrow(state, midpoint(state))
        state = narrow(state, midpoint(state))
        return state, jnp.sum(state[2])

    (thr, _, _), _ = lax.while_loop(lambda carry: carry[1] > 0.0, bisect_body, (state, jnp.float32(1.0)))
    thrb = jnp.broadcast_to(thr, (tqi, tkc))

    def mask_body(c, carry):
        sel = (keys_sc[c] >= thrb) & (c * tkc + kloc <= qpos)
        o_ref[c] = jnp.where(sel, 0.0, MASKED).astype(o_ref.dtype)
        return carry

    lax.fori_loop(0, nch, mask_body, 0)

    def fill_body(c, carry):
        o_ref[c] = jnp.full((tqi, tkc), MASKED, o_ref.dtype)
        return carry

    lax.fori_loop(nch, nk, fill_body, 0)


def _select_mask(qi_stack, kt, wi, tkc, topk):
    nq, rows, _ = qi_stack.shape
    nk = kt.shape[0]
    s = nq * IDX_QBLOCK
    return pl.pallas_call(
        functools.partial(_select_kernel, tkc=tkc, topk=topk),
        grid=(nq,),
        in_specs=[pl.BlockSpec((1, rows, IDX_DIM), lambda i: (i, 0, 0)),
                  pl.BlockSpec((nk, IDX_DIM, tkc), lambda i: (0, 0, 0)),
                  pl.BlockSpec((IDX_QBLOCK, IDX_HEADS), lambda i: (i, 0))],
        out_specs=pl.BlockSpec((nk, IDX_QBLOCK, tkc), lambda i: (0, i, 0)),
        out_shape=jax.ShapeDtypeStruct((nk, s, tkc), BF16),
        scratch_shapes=[pltpu.VMEM((nk, IDX_QBLOCK, tkc), I32),
                        pltpu.VMEM((2, rows, tkc), F32),
                        pltpu.VMEM((IDX_HEADS, IDX_QBLOCK, LANES), F32),
                        pltpu.VMEM((IDX_QBLOCK, LANES), F32)],
        compiler_params=_cparams(("parallel",)),
        name="indexer_select",
    )(qi_stack, kt, wi)


def _vec_pack(d, *rows):
    rows = [r.reshape(1, d).astype(F32) for r in rows]
    rows += [jnp.zeros((1, d), F32)] * (8 - len(rows))
    return jnp.concatenate(rows, axis=0)


def _pad_cols(w, n):
    return jnp.pad(w, ((0, 0), (0, n - w.shape[1])))


def kernel(x, c, norm_g, mod_w, mod_b, ffn_w1, ffn_w2, rel_table, ab_w_in, ab_w_out, diff_lam,
           diff_subln_g, sg_ln_g, sg_ln_b, sg_w, sg_b, dsa_w_in, dsa_w_out, final_g):
    batch, s, d = x.shape
    depth = norm_g.shape[0]
    assert batch == 1 and s % IDX_QBLOCK == 0

    tq_a = _tile(s, 512)
    tq_c = _tile(s, 256)
    topk = min(TOPK_MAX, s // 4)

    mod = _modulation(c, mod_w, mod_b).reshape(depth, 9, d)
    ffn_a, ffn_b, ffn_tiles = _ffn_weights(ffn_w1, ffn_w2)
    near_a = _near_bias(rel_table, tq_a, A_HEADS, 2)
    near_c = _near_bias(rel_table, tq_c, C_KV_HEADS, C_GROUP)

    xs = x.reshape(s, d)
    zeros_d = jnp.zeros((d,), F32)
    for li in range(depth):
        def vec(j, li=li):
            last = final_g if (li == depth - 1 and j == 2) else zeros_d
            return _vec_pack(d, norm_g[li, j], mod[li, 3 * j], mod[li, 3 * j + 1], mod[li, 3 * j + 2], last)

        def ffn(xs, j, k, final=False, li=li):
            return _ffn(xs, vec(j), ffn_a, ffn_b, ffn_tiles, 2 * li + k, final=final)

        xs = ffn(xs, 0, 0)

        v1 = vec(1)
        jj = li // 2
        if li % 2 == 0:
            w_in = ab_w_in[jj]
            w_qkv = jnp.concatenate([w_in[:, :A_WIDTH] * np.float32(A_QK_DIM ** -0.5 * LOG2E),
                                     w_in[:, A_WIDTH:3 * A_WIDTH]], axis=1).astype(BF16)
            w_zb = w_in[:, 3 * A_WIDTH:].astype(BF16)
            qkv = _proj(xs, v1, _col_blocks(w_qkv, PROJ_COLS), BF16)
            zb = _proj(xs, v1, _col_blocks(w_zb, PROJ_COLS), F32)
            lam_init = 0.8 - 0.6 * math.exp(-0.3 * li)
            lp = diff_lam[jj].astype(F32)
            lam = jnp.exp(jnp.sum(lp[0] * lp[1])) - jnp.exp(jnp.sum(lp[2] * lp[3])) + lam_init
            ya = _diff_attention(qkv, near_a, lam.reshape(1), diff_subln_g[jj].reshape(1, A_V_DIM),
                                 1.0 - lam_init, tq_a)
            yb = _spatial_gating(zb, sg_ln_g[jj], sg_ln_b[jj], sg_w[jj], sg_b[jj])
            w_out = ab_w_out[jj].astype(BF16)
            xs = _outproj(xs, v1, [ya, yb], [_col_blocks(w_out[:A_WIDTH], PROJ_COLS),
                                             _col_blocks(w_out[A_WIDTH:], PROJ_COLS)])
        else:
            w_in = dsa_w_in[jj]
            o_idx = C_WIDTH + 2 * C_KV_WIDTH
            o_ki = o_idx + IDX_HEADS * IDX_DIM
            w_main = jnp.concatenate([w_in[:, :C_WIDTH] * np.float32(C_HEAD_DIM ** -0.5 * LOG2E),
                                      w_in[:, C_WIDTH:o_ki]], axis=1).astype(BF16)
            w_kiw = _pad_cols(w_in[:, o_ki:], LANES).astype(BF16)
            main = _proj(xs, v1, _col_blocks(w_main, PROJ_COLS), BF16)
            kiw = _proj(xs, v1, _col_blocks(w_kiw, PROJ_COLS), F32)
            nq = s // IDX_QBLOCK
            qi = main[:, o_idx:o_ki].reshape(nq, IDX_QBLOCK, IDX_HEADS, IDX_DIM)
            qi = qi.transpose(0, 2, 1, 3).reshape(nq, IDX_HEADS * IDX_QBLOCK, IDX_DIM)
            kt = kiw[:, :IDX_DIM].astype(BF16).reshape(s // tq_c, tq_c, IDX_DIM).transpose(0, 2, 1)
            wi = kiw[:, IDX_DIM:IDX_DIM + IDX_HEADS]
            mask = _select_mask(qi, kt, wi, tq_c, topk)
            yc = _masked_attention(main, near_c, mask, tq_c)
            xs = _outproj(xs, v1, [yc], [_col_blocks(dsa_w_out[jj].astype(BF16), PROJ_COLS)])

        xs = ffn(xs, 2, 1, final=(li == depth - 1))
    return xs.reshape(batch, s, d)
```

```python
import functools
import math

import jax
import jax.numpy as jnp
import numpy as np
from jax import lax
from jax.experimental import pallas as pl
from jax.experimental.pallas import tpu as pltpu

F32 = jnp.float32
BF16 = jnp.bfloat16
I32 = jnp.int32

EPS = 1e-6
MASKED = -1e30
LANES = 128
INT_MIN = -(2 ** 31)
LOG2E = math.log2(math.e)
PREP_ROWS = 128
NORM_ROWS = 16
ATTN_SUB_ROWS = 128

A_HEADS = 8
A_QK_DIM = 64
A_V_DIM = 128
A_WIDTH = A_HEADS * A_V_DIM
B_GROUPS = 8
B_GROUP_DIM = 128
B_WIDTH = B_GROUPS * B_GROUP_DIM
CHUNK = 128
C_HEADS = 16
C_KV_HEADS = 4
C_GROUP = C_HEADS // C_KV_HEADS
C_HEAD_DIM = 128
C_WIDTH = C_HEADS * C_HEAD_DIM
C_KV_WIDTH = C_KV_HEADS * C_HEAD_DIM
IDX_HEADS = 16
IDX_DIM = 64
TOPK_MAX = 256
REL_BUCKETS = 32
REL_MAX_DIST = 128
IDX_QBLOCK = 128
SEARCH_PROBE_DROP = 2 ** 24

VMEM_LIMIT = 56 * 1024 * 1024


def _cparams(sem):
    return pltpu.CompilerParams(dimension_semantics=sem, vmem_limit_bytes=VMEM_LIMIT)


FFN_ROWS, FFN_COLS = 512, 512
PROJ_ROWS, PROJ_COLS = 1024, 1024


def _col_blocks(w, tn):
    k, n = w.shape
    tn = _tile(n, tn)
    return w.reshape(k, n // tn, tn).transpose(1, 0, 2)


def _tile(n, want):
    if n <= want:
        return n
    t = want
    while n % t:
        t //= 2
    return t


def _mod_kernel(c_ref, w_ref, b_ref, o_ref):
    c = c_ref[...]
    cs = c * (1.0 / (1.0 + jnp.exp(-c)))
    o_ref[0] = jnp.sum(cs * w_ref[0], axis=0, keepdims=True) + b_ref[0]


def _modulation(c, mod_w, mod_b):
    depth, d, n = mod_w.shape
    tn = _tile(n, 1024)
    out = pl.pallas_call(
        _mod_kernel,
        grid=(depth, n // tn),
        in_specs=[pl.BlockSpec((d, 1), lambda l, j: (0, 0)),
                  pl.BlockSpec((1, d, tn), lambda l, j: (l, 0, j)),
                  pl.BlockSpec((1, 1, tn), lambda l, j: (l, 0, j))],
        out_specs=pl.BlockSpec((1, 1, tn), lambda l, j: (l, 0, j)),
        out_shape=jax.ShapeDtypeStruct((depth, 1, n), F32),
        compiler_params=_cparams(("arbitrary", "arbitrary")),
        name="adaln_mod",
    )(c.reshape(d, 1), mod_w, mod_b.reshape(depth, 1, n))
    return out.reshape(depth, n)


def _prenorm_into(x_ref, vec_ref, hn_ref, inv_ref):
    rows, d = x_ref.shape
    gain = vec_ref[0:1, :] * (1.0 + vec_ref[2:3, :])
    shift = vec_ref[1:2, :]

    def scale_body(r, carry):
        rs = pl.ds(pl.multiple_of(r * NORM_ROWS, NORM_ROWS), NORM_ROWS)
        x = x_ref[rs, :]
        ms = jnp.sum(x * x, axis=-1, keepdims=True) * np.float32(1.0 / d)
        inv_ref[rs, :] = jnp.broadcast_to(lax.rsqrt(ms + EPS), (NORM_ROWS, LANES))
        return carry

    lax.fori_loop(0, rows // NORM_ROWS, scale_body, 0, unroll=8)

    def apply_body(r, carry):
        rs = pl.ds(pl.multiple_of(r * NORM_ROWS, NORM_ROWS), NORM_ROWS)
        inv = inv_ref[rs, :]
        inv = jnp.concatenate([inv] * (d // LANES), axis=1)
        hn_ref[rs, :] = (x_ref[rs, :] * inv * gain + shift).astype(hn_ref.dtype)
        return carry

    lax.fori_loop(0, rows // NORM_ROWS, apply_body, 0, unroll=2)


def _ffn_kernel(x_ref, vec_ref, w1g_ref, w1u_ref, w2_ref, o_ref, hn_sc, inv_sc, *, nf, final):
    f = pl.program_id(1)

    @pl.when(f == 0)
    def _():
        _prenorm_into(x_ref, vec_ref, hn_sc, inv_sc)
        o_ref[...] = jnp.zeros_like(o_ref)

    hn = hn_sc[...]
    g = jnp.dot(hn, w1g_ref[...], preferred_element_type=F32)
    u = jnp.dot(hn, w1u_ref[...], preferred_element_type=F32)
    a = (g * (1.0 / (1.0 + jnp.exp(-g))) * u).astype(BF16)
    o_ref[...] += jnp.dot(a, w2_ref[...], preferred_element_type=F32)

    @pl.when(f == nf - 1)
    def _():
        y = x_ref[...] + 0.5 * (1.0 + vec_ref[3:4, :]) * o_ref[...]
        if final:
            ms = jnp.mean(y * y, axis=-1, keepdims=True)
            y = y * lax.rsqrt(ms + EPS) * vec_ref[4:5, :]
        o_ref[...] = y


def _ffn_weights(ffn_w1, ffn_w2):
    depth, two, d, _ = ffn_w1.shape
    d_ff = ffn_w2.shape[2]
    steps = depth * two
    tf = FFN_COLS if d_ff > FFN_COLS else d_ff
    nf = -(-d_ff // tf)
    pad = nf * tf - d_ff
    assert d_ff % LANES == 0 and d_ff % PREP_ROWS == 0
    tr = _tile(d, PREP_ROWS)

    def w1_kernel(w_ref, o_ref):
        o_ref[:, 0:d_ff] = w_ref[:, 0:d_ff].astype(BF16)
        o_ref[:, d_ff + pad:2 * d_ff + pad] = w_ref[:, d_ff:2 * d_ff].astype(BF16)
        if pad:
            o_ref[:, d_ff:d_ff + pad] = jnp.zeros((tr, pad), BF16)
            o_ref[:, 2 * d_ff + pad:] = jnp.zeros((tr, pad), BF16)

    w1 = pl.pallas_call(
        w1_kernel,
        grid=(steps, d // tr),
        in_specs=[pl.BlockSpec((None, tr, 2 * d_ff), lambda s, r: (s, r, 0))],
        out_specs=pl.BlockSpec((None, tr, 2 * nf * tf), lambda s, r: (s, r, 0)),
        out_shape=jax.ShapeDtypeStruct((steps, d, 2 * nf * tf), BF16),
        compiler_params=_cparams(("parallel", "parallel")),
        name="ffn_w1_cast_pad",
    )(ffn_w1.reshape(steps, d, 2 * d_ff))

    row_blocks = d_ff // PREP_ROWS

    def w2_kernel(w_ref, o_ref):
        real = pl.program_id(1) < row_blocks

        @pl.when(real)
        def _():
            o_ref[...] = w_ref[...].astype(BF16)

        @pl.when(jnp.logical_not(real))
        def _():
            o_ref[...] = jnp.zeros(o_ref.shape, BF16)

    w2 = pl.pallas_call(
        w2_kernel,
        grid=(steps, (d_ff + pad) // PREP_ROWS),
        in_specs=[pl.BlockSpec((None, PREP_ROWS, d), lambda s, r: (s, jnp.minimum(r, row_blocks - 1), 0))],
        out_specs=pl.BlockSpec((None, PREP_ROWS, d), lambda s, r: (s, r, 0)),
        out_shape=jax.ShapeDtypeStruct((steps, d_ff + pad, d), BF16),
        compiler_params=_cparams(("parallel", "parallel")),
        name="ffn_w2_cast_pad",
    )(ffn_w2.reshape(steps, d_ff, d))
    return w1, w2, nf


def _ffn(x, vec, w1, w2, nf, step, *, final):
    s, d = x.shape
    tf = w2.shape[1] // nf
    tm = _tile(s, FFN_ROWS)
    return pl.pallas_call(
        functools.partial(_ffn_kernel, nf=nf, final=final),
        grid=(s // tm, nf),
        in_specs=[pl.BlockSpec((tm, d), lambda i, f: (i, 0)),
                  pl.BlockSpec((8, d), lambda i, f: (0, 0)),
                  pl.BlockSpec((None, d, tf), lambda i, f: (step, 0, f)),
                  pl.BlockSpec((None, d, tf), lambda i, f: (step, 0, nf + f)),
                  pl.BlockSpec((None, tf, d), lambda i, f: (step, f, 0))],
        out_specs=pl.BlockSpec((tm, d), lambda i, f: (i, 0)),
        out_shape=jax.ShapeDtypeStruct((s, d), F32),
        scratch_shapes=[pltpu.VMEM((tm, d), BF16), pltpu.VMEM((tm, LANES), F32)],
        compiler_params=_cparams(("parallel", "arbitrary")),
        name="swiglu_halfstep",
    )(x, vec, w1, w1, w2)


def _proj_kernel(x_ref, vec_ref, w_ref, o_ref, hn_sc, inv_sc):
    @pl.when(pl.program_id(1) == 0)
    def _():
        _prenorm_into(x_ref, vec_ref, hn_sc, inv_sc)

    o_ref[...] = jnp.dot(hn_sc[...], w_ref[0], preferred_element_type=F32).astype(o_ref.dtype)


def _proj(x, vec, w, out_dtype):
    s, d = x.shape
    nn, _, tn = w.shape
    n = nn * tn
    tm = _tile(s, PROJ_ROWS)
    return pl.pallas_call(
        _proj_kernel,
        grid=(s // tm, nn),
        in_specs=[pl.BlockSpec((tm, d), lambda i, j: (i, 0)),
                  pl.BlockSpec((8, d), lambda i, j: (0, 0)),
                  pl.BlockSpec((1, d, tn), lambda i, j: (j, 0, 0))],
        out_specs=pl.BlockSpec((tm, tn), lambda i, j: (i, j)),
        out_shape=jax.ShapeDtypeStruct((s, n), out_dtype),
        scratch_shapes=[pltpu.VMEM((tm, d), BF16), pltpu.VMEM((tm, LANES), F32)],
        compiler_params=_cparams(("parallel", "arbitrary")),
        name="norm_mod_proj",
    )(x, vec, w)


def _outproj_kernel(*refs, n_in):
    x_ref, vec_ref = refs[0], refs[1]
    lhs = refs[2:2 + n_in]
    ws = refs[2 + n_in:2 + 2 * n_in]
    o_ref = refs[2 + 2 * n_in]
    acc = jnp.dot(lhs[0][...], ws[0][0], preferred_element_type=F32)
    for a, w in zip(lhs[1:], ws[1:]):
        acc += jnp.dot(a[...], w[0], preferred_element_type=F32)
    o_ref[...] = x_ref[...] + (1.0 + vec_ref[3:4, :]) * acc


def _outproj(x, vec, lhs, ws):
    s, d = x.shape
    tm = _tile(s, PROJ_ROWS)
    tn = ws[0].shape[2]
    n_in = len(lhs)
    in_specs = [pl.BlockSpec((tm, tn), lambda i, j: (i, j)),
                pl.BlockSpec((8, tn), lambda i, j: (0, j))]
    in_specs += [pl.BlockSpec((tm, a.shape[1]), lambda i, j: (i, 0)) for a in lhs]
    in_specs += [pl.BlockSpec((1, w.shape[1], tn), lambda i, j: (j, 0, 0)) for w in ws]
    return pl.pallas_call(
        functools.partial(_outproj_kernel, n_in=n_in),
        grid=(s // tm, d // tn),
        in_specs=in_specs,
        out_specs=pl.BlockSpec((tm, tn), lambda i, j: (i, j)),
        out_shape=jax.ShapeDtypeStruct((s, d), F32),
        compiler_params=_cparams(("parallel", "arbitrary")),
        name="outproj_residual",
    )(x, vec, *lhs, *ws)


def _flash_kernel(*refs, nh, tq, kb, sub, diff, lam_scale):
    if diff:
        lam_ref, q_ref, k_ref, v_ref, nb_ref, g_ref, o_ref, qs_sc, m_sc, acc_sc, s_sc, p_sc, al_sc = refs
        mask_ref = None
    else:
        q_ref, k_ref, v_ref, nb_ref, mask_ref, o_ref, qs_sc, m_sc, acc_sc, s_sc, p_sc, al_sc = refs
    tk = tq
    rows = nh * tq
    hd = k_ref.shape[1]
    assert hd == LANES and rows % sub == 0 and tq % sub == 0
    i = pl.program_id(1)

    if diff:
        q = q_ref[...]
        lane = lax.broadcasted_iota(I32, q.shape, 1)
        zero = jnp.zeros_like(q)
        qs_sc[0:tq, :] = jnp.where(lane < A_QK_DIM, q, zero)
        qs_sc[tq:2 * tq, :] = jnp.where(lane >= A_QK_DIM, q, zero)
    else:
        for r in range(nh):
            qs_sc[r * tq:(r + 1) * tq, :] = q_ref[:, r * hd:(r + 1) * hd]
    m_sc[...] = jnp.full(m_sc.shape, -jnp.inf, F32)
    acc_sc[...] = jnp.zeros(acc_sc.shape, F32)

    tks = kb * tk
    nk = k_ref.shape[0] // tk
    last_step = i // kb

    def bias_index(j):
        return jnp.where(j > i, 3, jnp.clip(j - (i - 2), 0, 2))

    def stage_qk(t, slot):
        start = pl.multiple_of(jnp.minimum(t, last_step) * tks, tks)
        kblk = k_ref[pl.ds(start, tks), :]
        s_sc[slot] = lax.dot_general(qs_sc[...], kblk, (((1,), (1,)), ((), ())), preferred_element_type=F32)

    def stage_softmax(t, slot, biased):
        tc = jnp.minimum(t, last_step)
        for r in range(rows // sub):
            rs = slice(r * sub, (r + 1) * sub)
            tiles = []
            for b in range(kb):
                j = tc * kb + b
                s = s_sc[slot, rs, b * tk:(b + 1) * tk]
                if biased:
                    col = jnp.where(t > last_step, 3, bias_index(j))
                    s = s + nb_ref[0, col, rs, :]
                s = s.astype(BF16)
                if mask_ref is not None:
                    off = (r * sub) % tq
                    s = s + mask_ref[jnp.minimum(j, nk - 1), off:off + sub, :]
                tiles += [s[:, u * LANES:(u + 1) * LANES] for u in range(tk // LANES)]
            cmax = tiles[0]
            for u in tiles[1:]:
                cmax = jnp.maximum(cmax, u)
            m_old = m_sc[rs, :]
            m_new = jnp.maximum(m_old, jnp.max(cmax.astype(F32), axis=1, keepdims=True))
            al_sc[slot, rs, :] = jnp.exp2(m_old - m_new)
            m_b = m_new.astype(BF16)
            p_sc[slot, rs, :] = jnp.concatenate([jnp.exp2(u - m_b) for u in tiles], axis=1)
            m_sc[rs, :] = m_new

    def stage_pv(t, slot):
        start = pl.multiple_of(jnp.minimum(t, last_step) * tks, tks)
        vbe = jnp.concatenate([v_ref[pl.ds(start, tks), :], jnp.ones((tks, hd), BF16)], axis=1)
        pv = jnp.dot(p_sc[slot], vbe, preferred_element_type=F32)
        alpha = al_sc[slot]
        acc_sc[...] = jnp.concatenate([alpha, alpha], axis=1) * acc_sc[...] + pv

    stage_qk(0, 0)
    stage_qk(1, 1)
    stage_softmax(0, 0, True)

    nfar = jnp.maximum((i - 1) // kb, 0)
    npairs = jnp.maximum((nfar - 1) // 2, 0)

    def far_pair(u, carry):
        t = 2 * u
        stage_pv(t, 0)
        stage_softmax(t + 1, 1, False)
        stage_qk(t + 2, 0)
        stage_pv(t + 1, 1)
        stage_softmax(t + 2, 0, False)
        stage_qk(t + 3, 1)
        return carry

    lax.fori_loop(0, npairs, far_pair, 0)

    t0 = 2 * npairs
    stage_pv(t0, 0)
    stage_softmax(t0 + 1, 1, True)
    stage_qk(t0 + 2, 0)
    stage_pv(t0 + 1, 1)
    stage_softmax(t0 + 2, 0, True)
    stage_qk(t0 + 3, 1)
    stage_pv(t0 + 2, 0)
    stage_softmax(t0 + 3, 1, True)
    stage_pv(t0 + 3, 1)

    if diff:
        o0 = acc_sc[0:tq, 0:hd] / acc_sc[0:tq, hd:2 * hd]
        o1 = acc_sc[tq:2 * tq, 0:hd] / acc_sc[tq:2 * tq, hd:2 * hd]
        dlt = o0 - lam_ref[0] * o1
        ms = jnp.mean(dlt * dlt, axis=-1, keepdims=True)
        o_ref[...] = ((dlt * lax.rsqrt(ms + EPS) * g_ref[...]) * lam_scale).astype(o_ref.dtype)
    else:
        for r in range(nh):
            rs = slice(r * tq, (r + 1) * tq)
            o_ref[:, r * hd:(r + 1) * hd] = (acc_sc[rs, 0:hd] / acc_sc[rs, hd:2 * hd]).astype(o_ref.dtype)


def _rel_bucket(dist):
    n = jnp.maximum(dist, 0)
    max_exact = REL_BUCKETS // 2
    nf = jnp.maximum(n, 1).astype(F32)
    large = max_exact + (jnp.log(nf / max_exact) / math.log(REL_MAX_DIST / max_exact)
                         * (REL_BUCKETS - max_exact)).astype(I32)
    large = jnp.minimum(large, REL_BUCKETS - 1)
    return jnp.where(n < max_exact, n, large)


def _near_bias(rel_table, tq, groups, nh):
    assert tq >= LANES, "keys older than one block must all fall in the last bucket"
    r = jnp.arange(tq, dtype=I32)[:, None]
    c = jnp.arange(2 * tq, dtype=I32)[None, :]
    dist = r + tq - c
    rel = (rel_table - rel_table[REL_BUCKETS - 1][None, :]) * np.float32(LOG2E)
    onehot = jax.nn.one_hot(_rel_bucket(dist), REL_BUCKETS, dtype=F32)
    b = jnp.einsum("rcb,bh->hrc", onehot, rel, precision=lax.Precision.HIGHEST)
    b = jnp.where((dist >= 0)[None], b, MASKED)
    heads = b.shape[0]
    tiles = jnp.stack([jnp.zeros((heads, tq, tq), F32), b[:, :, :tq], b[:, :, tq:],
                       jnp.full((heads, tq, tq), MASKED, F32)], axis=1)
    tiles = tiles.reshape(groups, nh, 4, tq, tq).transpose(0, 2, 1, 3, 4)
    return tiles.reshape(groups, 4, nh * tq, tq)


def _diff_attention(qkv, near, lam, subln_g, lam_scale, tq):
    s = qkv.shape[0]
    hd = A_V_DIM
    nh = 2
    kb = 1
    kcol = A_WIDTH // hd
    return pl.pallas_call(
        functools.partial(_flash_kernel, nh=nh, tq=tq, kb=kb, sub=min(tq, ATTN_SUB_ROWS), diff=True,
                          lam_scale=lam_scale),
        grid=(A_HEADS, s // tq),
        in_specs=[pl.BlockSpec(memory_space=pltpu.SMEM),
                  pl.BlockSpec((tq, hd), lambda h, i: (i, h)),
                  pl.BlockSpec((s, hd), lambda h, i: (0, kcol + h)),
                  pl.BlockSpec((s, hd), lambda h, i: (0, 2 * kcol + h)),
                  pl.BlockSpec((1, 4, nh * tq, tq), lambda h, i: (h, 0, 0, 0)),
                  pl.BlockSpec((1, hd), lambda h, i: (0, 0))],
        out_specs=pl.BlockSpec((tq, hd), lambda h, i: (i, h)),
        out_shape=jax.ShapeDtypeStruct((s, A_WIDTH), BF16),
        scratch_shapes=[pltpu.VMEM((nh * tq, hd), BF16),
                        pltpu.VMEM((nh * tq, LANES), F32),
                        pltpu.VMEM((nh * tq, 2 * hd), F32),
                        pltpu.VMEM((2, nh * tq, kb * tq), F32),
                        pltpu.VMEM((2, nh * tq, kb * tq), BF16),
                        pltpu.VMEM((2, nh * tq, LANES), F32)],
        compiler_params=_cparams(("parallel", "arbitrary")),
        name="diff_attention",
    )(lam, qkv, qkv, qkv, near, subln_g)


def _masked_attention(qkv, near, mask, tq):
    s = qkv.shape[0]
    hd = C_HEAD_DIM
    nh = C_GROUP
    kcol = C_WIDTH // hd
    vcol = kcol + C_KV_HEADS
    nk = s // tq
    kb = 2 if nk % 2 == 0 else 1
    return pl.pallas_call(
        functools.partial(_flash_kernel, nh=nh, tq=tq, kb=kb, sub=min(tq, ATTN_SUB_ROWS), diff=False,
                          lam_scale=1.0),
        grid=(C_KV_HEADS, s // tq),
        in_specs=[pl.BlockSpec((tq, nh * hd), lambda g, i: (i, g)),
                  pl.BlockSpec((s, hd), lambda g, i: (0, kcol + g)),
                  pl.BlockSpec((s, hd), lambda g, i: (0, vcol + g)),
                  pl.BlockSpec((1, 4, nh * tq, tq), lambda g, i: (g, 0, 0, 0)),
                  pl.BlockSpec((nk, tq, tq), lambda g, i: (0, i, 0))],
        out_specs=pl.BlockSpec((tq, nh * hd), lambda g, i: (i, g)),
        out_shape=jax.ShapeDtypeStruct((s, C_WIDTH), BF16),
        scratch_shapes=[pltpu.VMEM((nh * tq, hd), BF16),
                        pltpu.VMEM((nh * tq, LANES), F32),
                        pltpu.VMEM((nh * tq, 2 * hd), F32),
                        pltpu.VMEM((2, nh * tq, kb * tq), F32),
                        pltpu.VMEM((2, nh * tq, kb * tq), BF16),
                        pltpu.VMEM((2, nh * tq, LANES), F32)],
        compiler_params=_cparams(("parallel", "arbitrary")),
        name="selected_attention",
    )(qkv, qkv, qkv, near, mask)


def _sg_kernel(zb_ref, lng_ref, lnb_ref, w_ref, bs_ref, o_ref, *, nchunk):
    zb = zb_ref[...]
    gl = zb * (0.5 * (1.0 + jnp.tanh(np.float32(np.sqrt(2.0 / np.pi)) * (zb + 0.044715 * (zb * zb * zb)))))
    u = gl[:, :B_WIDTH]
    z = gl[:, B_WIDTH:]
    mu = jnp.mean(z, axis=-1, keepdims=True)
    zc = z - mu
    var = jnp.mean(zc * zc, axis=-1, keepdims=True)
    zn = (zc * lax.rsqrt(var + EPS) * lng_ref[...] + lnb_ref[...]).astype(BF16)
    row = lax.broadcasted_iota(I32, (CHUNK, CHUNK), 0)
    col = lax.broadcasted_iota(I32, (CHUNK, CHUNK), 1)
    for g in range(B_GROUPS):
        w = jnp.where(row >= col, w_ref[g], 0.0).astype(BF16)
        bias = bs_ref[g]
        lo = g * B_GROUP_DIM
        for c in range(nchunk):
            r0 = c * CHUNK
            sz = jnp.dot(w, zn[r0:r0 + CHUNK, lo:lo + B_GROUP_DIM], preferred_element_type=F32) + bias
            o_ref[r0:r0 + CHUNK, lo:lo + B_GROUP_DIM] = (u[r0:r0 + CHUNK, lo:lo + B_GROUP_DIM] * sz).astype(o_ref.dtype)


def _spatial_gating(zb, ln_g, ln_b, w_s, b_s):
    s = zb.shape[0]
    t = _tile(s, 256)
    return pl.pallas_call(
        functools.partial(_sg_kernel, nchunk=t // CHUNK),
        grid=(s // t,),
        in_specs=[pl.BlockSpec((t, 2 * B_WIDTH), lambda i: (i, 0)),
                  pl.BlockSpec((1, B_WIDTH), lambda i: (0, 0)),
                  pl.BlockSpec((1, B_WIDTH), lambda i: (0, 0)),
                  pl.BlockSpec((B_GROUPS, CHUNK, CHUNK), lambda i: (0, 0, 0)),
                  pl.BlockSpec((B_GROUPS, CHUNK, 1), lambda i: (0, 0, 0))],
        out_specs=pl.BlockSpec((t, B_WIDTH), lambda i: (i, 0)),
        out_shape=jax.ShapeDtypeStruct((s, B_WIDTH), BF16),
        compiler_params=_cparams(("parallel",)),
        name="spatial_gating",
    )(zb, ln_g.reshape(1, B_WIDTH), ln_b.reshape(1, B_WIDTH), w_s, b_s.reshape(B_GROUPS, CHUNK, 1))


def _select_kernel(qi_ref, kt_ref, w_ref, o_ref, keys_sc, sc_sc, wb_sc, mx_sc, *, tkc, topk):
    tqi = IDX_QBLOCK
    i = pl.program_id(0)
    nk = o_ref.shape[0]
    nch = (i * tqi + tqi + tkc - 1) // tkc
    qpos = i * tqi + lax.broadcasted_iota(I32, (tqi, tkc), 0)
    kloc = lax.broadcasted_iota(I32, (tqi, tkc), 1)
    qrow = i * tqi + lax.broadcasted_iota(I32, (tqi, LANES), 0)
    klane = lax.broadcasted_iota(I32, (tqi, LANES), 1)

    wgt = w_ref[...] * np.float32(IDX_DIM ** -0.5)
    for h in range(IDX_HEADS):
        wb_sc[h] = jnp.broadcast_to(wgt[:, h:h + 1], (tqi, LANES))
    mx_sc[...] = jnp.full(mx_sc.shape, -jnp.inf, F32)

    def stage_dot(c, slot):
        sc_sc[slot] = jnp.dot(qi_ref[0], kt_ref[jnp.minimum(c, nch - 1)], preferred_element_type=F32)

    def stage_reduce(c, slot):
        c = jnp.minimum(c, nch - 1)
        for u in range(tkc // LANES):
            ls = slice(u * LANES, (u + 1) * LANES)
            acc = jnp.zeros((tqi, LANES), F32)
            for h in range(IDX_HEADS):
                acc += jnp.maximum(sc_sc[slot, h * tqi:(h + 1) * tqi, ls], 0.0) * wb_sc[h]
            acc = acc + 0.0
            bits = pltpu.bitcast(acc, I32)
            ordered = jnp.where(bits < 0, bits ^ jnp.int32(0x7FFFFFFF), bits)
            causal = c * tkc + u * LANES + klane <= qrow
            keys_sc[c, :, ls] = jnp.where(causal, ordered, jnp.int32(INT_MIN))
            mx_sc[...] = jnp.maximum(mx_sc[...], jnp.where(causal, acc, -jnp.inf))

    stage_dot(0, 0)

    def score_quad(u, carry):
        c = 4 * u
        for k in range(0, 4, 2):
            stage_dot(c + k + 1, 1)
            stage_reduce(c + k, 0)
            stage_dot(c + k + 2, 0)
            stage_reduce(c + k + 1, 1)
        return carry

    lax.fori_loop(0, (nch + 3) // 4, score_quad, 0)

    def count_ge(cand):
        candb = jnp.broadcast_to(cand, (tqi, LANES))

        def count_body(c, cnt):
            kk = keys_sc[c]
            for u in range(tkc // LANES):
                cnt += jnp.where(kk[:, u * LANES:(u + 1) * LANES] >= candb, 1, 0)
            return cnt

        cnt = lax.fori_loop(0, nch, count_body, jnp.zeros((tqi, LANES), I32))
        return jnp.sum(cnt.astype(F32), axis=1, keepdims=True)

    want = np.float32(topk)
    fbits = pltpu.bitcast(jnp.max(mx_sc[...], axis=1, keepdims=True), I32)
    kmax = jnp.where(fbits < 0, fbits ^ jnp.int32(0x7FFFFFFF), fbits)
    few = i * tqi + lax.broadcasted_iota(I32, (tqi, 1), 0) + 1 <= topk
    lo0 = jnp.full((tqi, 1), INT_MIN, I32)
    hi0 = jnp.where(few, lo0 + 1, kmax + 1)

    def narrow(state, cand):
        lo, hi, active = state
        total = count_ge(cand)
        open_ = active > 0.0
        up = open_ & (total >= want)
        lo = jnp.where(up, cand, lo)
        hi = jnp.where(open_ & (~up), cand, hi)
        open_ = open_ & (~(up & (total == want))) & ((hi - lo) != 1)
        return lo, hi, jnp.where(open_, 1.0, 0.0)

    def midpoint(state):
        lo, hi, _ = state
        return lo + lax.shift_right_logical(hi - lo, jnp.int32(1))

    probe = jnp.maximum(kmax, jnp.int32(INT_MIN + SEARCH_PROBE_DROP + 1)) - jnp.int32(SEARCH_PROBE_DROP)
    state = narrow((lo0, hi0, jnp.where(few, 0.0, 1.0)), jnp.where(few, lo0, probe))

    def bisect_body(carry):
        state, _ = carry
        state = narrow(state, midpoint(state))
        state = narrow(state, midpoint(state))
        return state, jnp.sum(state[2])

    (thr, _, _), _ = lax.while_loop(lambda carry: carry[1] > 0.0, bisect_body, (state, jnp.float32(1.0)))
    thrb = jnp.broadcast_to(thr, (tqi, tkc))

    def mask_body(c, carry):
        sel = (keys_sc[c] >= thrb) & (c * tkc + kloc <= qpos)
        o_ref[c] = jnp.where(sel, 0.0, MASKED).astype(o_ref.dtype)
        return carry

    lax.fori_loop(0, nch, mask_body, 0)

    def fill_body(c, carry):
        o_ref[c] = jnp.full((tqi, tkc), MASKED, o_ref.dtype)
        return carry

    lax.fori_loop(nch, nk, fill_body, 0)


def _select_mask(qi_stack, kt, wi, tkc, topk):
    nq, rows, _ = qi_stack.shape
    nk = kt.shape[0]
    s = nq * IDX_QBLOCK
    return pl.pallas_call(
        functools.partial(_select_kernel, tkc=tkc, topk=topk),
        grid=(nq,),
        in_specs=[pl.BlockSpec((1, rows, IDX_DIM), lambda i: (i, 0, 0)),
                  pl.BlockSpec((nk, IDX_DIM, tkc), lambda i: (0, 0, 0)),
                  pl.BlockSpec((IDX_QBLOCK, IDX_HEADS), lambda i: (i, 0))],
        out_specs=pl.BlockSpec((nk, IDX_QBLOCK, tkc), lambda i: (0, i, 0)),
        out_shape=jax.ShapeDtypeStruct((nk, s, tkc), BF16),
        scratch_shapes=[pltpu.VMEM((nk, IDX_QBLOCK, tkc), I32),
                        pltpu.VMEM((2, rows, tkc), F32),
                        pltpu.VMEM((IDX_HEADS, IDX_QBLOCK, LANES), F32),
                        pltpu.VMEM((IDX_QBLOCK, LANES), F32)],
        compiler_params=_cparams(("parallel",)),
        name="indexer_select",
    )(qi_stack, kt, wi)


def _vec_pack(d, *rows):
    rows = [r.reshape(1, d).astype(F32) for r in rows]
    rows += [jnp.zeros((1, d), F32)] * (8 - len(rows))
    return jnp.concatenate(rows, axis=0)


def _pad_cols(w, n):
    return jnp.pad(w, ((0, 0), (0, n - w.shape[1])))


def kernel(x, c, norm_g, mod_w, mod_b, ffn_w1, ffn_w2, rel_table, ab_w_in, ab_w_out, diff_lam,
           diff_subln_g, sg_ln_g, sg_ln_b, sg_w, sg_b, dsa_w_in, dsa_w_out, final_g):
    batch, s, d = x.shape
    depth = norm_g.shape[0]
    assert batch == 1 and s % IDX_QBLOCK == 0

    tq_a = _tile(s, 512)
    tq_c = _tile(s, 256)
    topk = min(TOPK_MAX, s // 4)

    mod = _modulation(c, mod_w, mod_b).reshape(depth, 9, d)
    ffn_a, ffn_b, ffn_tiles = _ffn_weights(ffn_w1, ffn_w2)
    near_a = _near_bias(rel_table, tq_a, A_HEADS, 2)
    near_c = _near_bias(rel_table, tq_c, C_KV_HEADS, C_GROUP)

    xs = x.reshape(s, d)
    zeros_d = jnp.zeros((d,), F32)
    for li in range(depth):
        def vec(j, li=li):
            last = final_g if (li == depth - 1 and j == 2) else zeros_d
            return _vec_pack(d, norm_g[li, j], mod[li, 3 * j], mod[li, 3 * j + 1], mod[li, 3 * j + 2], last)

        def ffn(xs, j, k, final=False, li=li):
            return _ffn(xs, vec(j), ffn_a, ffn_b, ffn_tiles, 2 * li + k, final=final)

        xs = ffn(xs, 0, 0)

        v1 = vec(1)
        jj = li // 2
        if li % 2 == 0:
            w_in = ab_w_in[jj]
            w_qkv = jnp.concatenate([w_in[:, :A_WIDTH] * np.float32(A_QK_DIM ** -0.5 * LOG2E),
                                     w_in[:, A_WIDTH:3 * A_WIDTH]], axis=1).astype(BF16)
            w_zb = w_in[:, 3 * A_WIDTH:].astype(BF16)
            qkv = _proj(xs, v1, _col_blocks(w_qkv, PROJ_COLS), BF16)
            zb = _proj(xs, v1, _col_blocks(w_zb, PROJ_COLS), F32)
            lam_init = 0.8 - 0.6 * math.exp(-0.3 * li)
            lp = diff_lam[jj].astype(F32)
            lam = jnp.exp(jnp.sum(lp[0] * lp[1])) - jnp.exp(jnp.sum(lp[2] * lp[3])) + lam_init
            ya = _diff_attention(qkv, near_a, lam.reshape(1), diff_subln_g[jj].reshape(1, A_V_DIM),
                                 1.0 - lam_init, tq_a)
            yb = _spatial_gating(zb, sg_ln_g[jj], sg_ln_b[jj], sg_w[jj], sg_b[jj])
            w_out = ab_w_out[jj].astype(BF16)
            xs = _outproj(xs, v1, [ya, yb], [_col_blocks(w_out[:A_WIDTH], PROJ_COLS),
                                             _col_blocks(w_out[A_WIDTH:], PROJ_COLS)])
        else:
            w_in = dsa_w_in[jj]
            o_idx = C_WIDTH + 2 * C_KV_WIDTH
            o_ki = o_idx + IDX_HEADS * IDX_DIM
            w_main = jnp.concatenate([w_in[:, :C_WIDTH] * np.float32(C_HEAD_DIM ** -0.5 * LOG2E),
                                      w_in[:, C_WIDTH:o_ki]], axis=1).astype(BF16)
            w_kiw = _pad_cols(w_in[:, o_ki:], LANES).astype(BF16)
            main = _proj(xs, v1, _col_blocks(w_main, PROJ_COLS), BF16)
            kiw = _proj(xs, v1, _col_blocks(w_kiw, PROJ_COLS), F32)
            nq = s // IDX_QBLOCK
            qi = main[:, o_idx:o_ki].reshape(nq, IDX_QBLOCK, IDX_HEADS, IDX_DIM)
            qi = qi.transpose(0, 2, 1, 3).reshape(nq, IDX_HEADS * IDX_QBLOCK, IDX_DIM)
            kt = kiw[:, :IDX_DIM].astype(BF16).reshape(s // tq_c, tq_c, IDX_DIM).transpose(0, 2, 1)
            wi = kiw[:, IDX_DIM:IDX_DIM + IDX_HEADS]
            mask = _select_mask(qi, kt, wi, tq_c, topk)
            yc = _masked_attention(main, near_c, mask, tq_c)
            xs = _outproj(xs, v1, [yc], [_col_blocks(dsa_w_out[jj].astype(BF16), PROJ_COLS)])

        xs = ffn(xs, 2, 1, final=(li == depth - 1))
    return xs.reshape(batch, s, d)
```

```python
import functools
import math

import jax
import jax.numpy as jnp
import numpy as np
from jax import lax
from jax.experimental import pallas as pl
from jax.experimental.pallas import tpu as pltpu

F32 = jnp.float32
BF16 = jnp.bfloat16
I32 = jnp.int32

EPS = 1e-6
MASKED = -1e30
LANES = 128
INT_MIN = -(2 ** 31)
LOG2E = math.log2(math.e)
PREP_ROWS = 128
PREP_COLS = 256
NORM_ROWS = 16
ATTN_SUB_ROWS = 128

A_HEADS = 8
A_QK_DIM = 64
A_V_DIM = 128
A_WIDTH = A_HEADS * A_V_DIM
B_GROUPS = 8
B_GROUP_DIM = 128
B_WIDTH = B_GROUPS * B_GROUP_DIM
CHUNK = 128
C_HEADS = 16
C_KV_HEADS = 4
C_GROUP = C_HEADS // C_KV_HEADS
C_HEAD_DIM = 128
C_WIDTH = C_HEADS * C_HEAD_DIM
C_KV_WIDTH = C_KV_HEADS * C_HEAD_DIM
IDX_HEADS = 16
IDX_DIM = 64
TOPK_MAX = 256
REL_BUCKETS = 32
REL_MAX_DIST = 128
IDX_QBLOCK = 128
SEARCH_PROBE_DROP = 2 ** 24

VMEM_LIMIT = 56 * 1024 * 1024


def _cparams(sem):
    return pltpu.CompilerParams(dimension_semantics=sem, vmem_limit_bytes=VMEM_LIMIT)


FFN_ROWS, FFN_COLS = 512, 512
PROJ_ROWS, PROJ_COLS = 1024, 1024


def _col_blocks(w, tn):
    k, n = w.shape
    tn = _tile(n, tn)
    return w.reshape(k, n // tn, tn).transpose(1, 0, 2)


def _tile(n, want):
    if n <= want:
        return n
    t = want
    while n % t:
        t //= 2
    return t


def _mod_kernel(c_ref, w_ref, b_ref, o_ref):
    c = c_ref[...]
    cs = c * (1.0 / (1.0 + jnp.exp(-c)))
    o_ref[0] = jnp.sum(cs * w_ref[0], axis=0, keepdims=True) + b_ref[0]


def _modulation(c, mod_w, mod_b):
    depth, d, n = mod_w.shape
    tn = _tile(n, 1024)
    out = pl.pallas_call(
        _mod_kernel,
        grid=(depth, n // tn),
        in_specs=[pl.BlockSpec((d, 1), lambda l, j: (0, 0)),
                  pl.BlockSpec((1, d, tn), lambda l, j: (l, 0, j)),
                  pl.BlockSpec((1, 1, tn), lambda l, j: (l, 0, j))],
        out_specs=pl.BlockSpec((1, 1, tn), lambda l, j: (l, 0, j)),
        out_shape=jax.ShapeDtypeStruct((depth, 1, n), F32),
        compiler_params=_cparams(("arbitrary", "arbitrary")),
        name="adaln_mod",
    )(c.reshape(d, 1), mod_w, mod_b.reshape(depth, 1, n))
    return out.reshape(depth, n)


def _prenorm_into(x_ref, vec_ref, hn_ref, inv_ref):
    rows, d = x_ref.shape
    gain = vec_ref[0:1, :] * (1.0 + vec_ref[2:3, :])
    shift = vec_ref[1:2, :]

    def scale_body(r, carry):
        rs = pl.ds(pl.multiple_of(r * NORM_ROWS, NORM_ROWS), NORM_ROWS)
        x = x_ref[rs, :]
        ms = jnp.sum(x * x, axis=-1, keepdims=True) * np.float32(1.0 / d)
        inv_ref[rs, :] = jnp.broadcast_to(lax.rsqrt(ms + EPS), (NORM_ROWS, LANES))
        return carry

    lax.fori_loop(0, rows // NORM_ROWS, scale_body, 0, unroll=8)

    def apply_body(r, carry):
        rs = pl.ds(pl.multiple_of(r * NORM_ROWS, NORM_ROWS), NORM_ROWS)
        inv = inv_ref[rs, :]
        inv = jnp.concatenate([inv] * (d // LANES), axis=1)
        hn_ref[rs, :] = (x_ref[rs, :] * inv * gain + shift).astype(hn_ref.dtype)
        return carry

    lax.fori_loop(0, rows // NORM_ROWS, apply_body, 0, unroll=2)


def _ffn_kernel(x_ref, vec_ref, w1g_ref, w1u_ref, w2_ref, o_ref, hn_sc, inv_sc, *, nf, final):
    f = pl.program_id(1)

    @pl.when(f == 0)
    def _():
        _prenorm_into(x_ref, vec_ref, hn_sc, inv_sc)
        o_ref[...] = jnp.zeros_like(o_ref)

    hn = hn_sc[...]
    g = jnp.dot(hn, w1g_ref[...], preferred_element_type=F32)
    u = jnp.dot(hn, w1u_ref[...], preferred_element_type=F32)
    a = (g * (1.0 / (1.0 + jnp.exp(-g))) * u).astype(BF16)
    o_ref[...] += jnp.dot(a, w2_ref[...], preferred_element_type=F32)

    @pl.when(f == nf - 1)
    def _():
        y = x_ref[...] + 0.5 * (1.0 + vec_ref[3:4, :]) * o_ref[...]
        if final:
            ms = jnp.mean(y * y, axis=-1, keepdims=True)
            y = y * lax.rsqrt(ms + EPS) * vec_ref[4:5, :]
        o_ref[...] = y


def _ffn_weights(ffn_w1, ffn_w2):
    depth, two, d, _ = ffn_w1.shape
    d_ff = ffn_w2.shape[2]
    steps = depth * two
    tf = FFN_COLS if d_ff > FFN_COLS else d_ff
    nf = -(-d_ff // tf)
    pad = nf * tf - d_ff
    assert d_ff % LANES == 0
    tr = _tile(d, PREP_ROWS)

    def w1_kernel(w_ref, o_ref):
        o_ref[:, 0:d_ff] = w_ref[:, 0:d_ff].astype(BF16)
        o_ref[:, d_ff + pad:2 * d_ff + pad] = w_ref[:, d_ff:2 * d_ff].astype(BF16)
        if pad:
            o_ref[:, d_ff:d_ff + pad] = jnp.zeros((tr, pad), BF16)
            o_ref[:, 2 * d_ff + pad:] = jnp.zeros((tr, pad), BF16)

    w1 = pl.pallas_call(
        w1_kernel,
        grid=(steps, d // tr),
        in_specs=[pl.BlockSpec((None, tr, 2 * d_ff), lambda s, r: (s, r, 0))],
        out_specs=pl.BlockSpec((None, tr, 2 * nf * tf), lambda s, r: (s, r, 0)),
        out_shape=jax.ShapeDtypeStruct((steps, d, 2 * nf * tf), BF16),
        compiler_params=_cparams(("parallel", "parallel")),
        name="ffn_w1_cast_pad",
    )(ffn_w1.reshape(steps, d, 2 * d_ff))

    tc = _tile(d, PREP_COLS)

    def w2_kernel(w_ref, o_ref):
        o_ref[0:d_ff, :] = w_ref[...].astype(BF16)
        if pad:
            o_ref[d_ff:, :] = jnp.zeros((pad, tc), BF16)

    w2 = pl.pallas_call(
        w2_kernel,
        grid=(steps, d // tc),
        in_specs=[pl.BlockSpec((None, d_ff, tc), lambda s, c: (s, 0, c))],
        out_specs=pl.BlockSpec((None, d_ff + pad, tc), lambda s, c: (s, 0, c)),
        out_shape=jax.ShapeDtypeStruct((steps, d_ff + pad, d), BF16),
        compiler_params=_cparams(("parallel", "parallel")),
        name="ffn_w2_cast_pad",
    )(ffn_w2.reshape(steps, d_ff, d))
    return w1, w2, nf


def _ffn(x, vec, w1, w2, nf, step, *, final):
    s, d = x.shape
    tf = w2.shape[1] // nf
    tm = _tile(s, FFN_ROWS)
    return pl.pallas_call(
        functools.partial(_ffn_kernel, nf=nf, final=final),
        grid=(s // tm, nf),
        in_specs=[pl.BlockSpec((tm, d), lambda i, f: (i, 0)),
                  pl.BlockSpec((8, d), lambda i, f: (0, 0)),
                  pl.BlockSpec((None, d, tf), lambda i, f: (step, 0, f)),
                  pl.BlockSpec((None, d, tf), lambda i, f: (step, 0, nf + f)),
                  pl.BlockSpec((None, tf, d), lambda i, f: (step, f, 0))],
        out_specs=pl.BlockSpec((tm, d), lambda i, f: (i, 0)),
        out_shape=jax.ShapeDtypeStruct((s, d), F32),
        scratch_shapes=[pltpu.VMEM((tm, d), BF16), pltpu.VMEM((tm, LANES), F32)],
        compiler_params=_cparams(("parallel", "arbitrary")),
        name="swiglu_halfstep",
    )(x, vec, w1, w1, w2)


def _proj_kernel(x_ref, vec_ref, w_ref, o_ref, hn_sc, inv_sc):
    @pl.when(pl.program_id(1) == 0)
    def _():
        _prenorm_into(x_ref, vec_ref, hn_sc, inv_sc)

    o_ref[...] = jnp.dot(hn_sc[...], w_ref[0], preferred_element_type=F32).astype(o_ref.dtype)


def _proj(x, vec, w, out_dtype):
    s, d = x.shape
    nn, _, tn = w.shape
    n = nn * tn
    tm = _tile(s, PROJ_ROWS)
    return pl.pallas_call(
        _proj_kernel,
        grid=(s // tm, nn),
        in_specs=[pl.BlockSpec((tm, d), lambda i, j: (i, 0)),
                  pl.BlockSpec((8, d), lambda i, j: (0, 0)),
                  pl.BlockSpec((1, d, tn), lambda i, j: (j, 0, 0))],
        out_specs=pl.BlockSpec((tm, tn), lambda i, j: (i, j)),
        out_shape=jax.ShapeDtypeStruct((s, n), out_dtype),
        scratch_shapes=[pltpu.VMEM((tm, d), BF16), pltpu.VMEM((tm, LANES), F32)],
        compiler_params=_cparams(("parallel", "arbitrary")),
        name="norm_mod_proj",
    )(x, vec, w)


def _outproj_kernel(*refs, n_in):
    x_ref, vec_ref = refs[0], refs[1]
    lhs = refs[2:2 + n_in]
    ws = refs[2 + n_in:2 + 2 * n_in]
    o_ref = refs[2 + 2 * n_in]
    acc = jnp.dot(lhs[0][...], ws[0][0], preferred_element_type=F32)
    for a, w in zip(lhs[1:], ws[1:]):
        acc += jnp.dot(a[...], w[0], preferred_element_type=F32)
    o_ref[...] = x_ref[...] + (1.0 + vec_ref[3:4, :]) * acc


def _outproj(x, vec, lhs, ws):
    s, d = x.shape
    tm = _tile(s, PROJ_ROWS)
    tn = ws[0].shape[2]
    n_in = len(lhs)
    in_specs = [pl.BlockSpec((tm, tn), lambda i, j: (i, j)),
                pl.BlockSpec((8, tn), lambda i, j: (0, j))]
    in_specs += [pl.BlockSpec((tm, a.shape[1]), lambda i, j: (i, 0)) for a in lhs]
    in_specs += [pl.BlockSpec((1, w.shape[1], tn), lambda i, j: (j, 0, 0)) for w in ws]
    return pl.pallas_call(
        functools.partial(_outproj_kernel, n_in=n_in),
        grid=(s // tm, d // tn),
        in_specs=in_specs,
        out_specs=pl.BlockSpec((tm, tn), lambda i, j: (i, j)),
        out_shape=jax.ShapeDtypeStruct((s, d), F32),
        compiler_params=_cparams(("parallel", "arbitrary")),
        name="outproj_residual",
    )(x, vec, *lhs, *ws)


def _flash_kernel(*refs, nh, tq, kb, sub, diff, lam_scale):
    if diff:
        lam_ref, q_ref, k_ref, v_ref, nb_ref, g_ref, o_ref, qs_sc, m_sc, acc_sc, s_sc, p_sc, al_sc = refs
        mask_ref = None
    else:
        q_ref, k_ref, v_ref, nb_ref, mask_ref, o_ref, qs_sc, m_sc, acc_sc, s_sc, p_sc, al_sc = refs
    tk = tq
    rows = nh * tq
    hd = k_ref.shape[1]
    assert hd == LANES and rows % sub == 0 and tq % sub == 0
    i = pl.program_id(1)

    if diff:
        q = q_ref[...]
        lane = lax.broadcasted_iota(I32, q.shape, 1)
        zero = jnp.zeros_like(q)
        qs_sc[0:tq, :] = jnp.where(lane < A_QK_DIM, q, zero)
        qs_sc[tq:2 * tq, :] = jnp.where(lane >= A_QK_DIM, q, zero)
    else:
        for r in range(nh):
            qs_sc[r * tq:(r + 1) * tq, :] = q_ref[:, r * hd:(r + 1) * hd]
    m_sc[...] = jnp.full(m_sc.shape, -jnp.inf, F32)
    acc_sc[...] = jnp.zeros(acc_sc.shape, F32)

    tks = kb * tk
    nk = k_ref.shape[0] // tk
    last_step = i // kb

    def bias_index(j):
        return jnp.where(j > i, 3, jnp.clip(j - (i - 2), 0, 2))

    def stage_qk(t, slot):
        start = pl.multiple_of(jnp.minimum(t, last_step) * tks, tks)
        kblk = k_ref[pl.ds(start, tks), :]
        s_sc[slot] = lax.dot_general(qs_sc[...], kblk, (((1,), (1,)), ((), ())), preferred_element_type=F32)

    def stage_softmax(t, slot, biased):
        tc = jnp.minimum(t, last_step)
        for r in range(rows // sub):
            rs = slice(r * sub, (r + 1) * sub)
            tiles = []
            for b in range(kb):
                j = tc * kb + b
                s = s_sc[slot, rs, b * tk:(b + 1) * tk]
                if biased:
                    col = jnp.where(t > last_step, 3, bias_index(j))
                    s = s + nb_ref[0, col, rs, :]
                s = s.astype(BF16)
                if mask_ref is not None:
                    off = (r * sub) % tq
                    s = s + mask_ref[jnp.minimum(j, nk - 1), off:off + sub, :]
                tiles += [s[:, u * LANES:(u + 1) * LANES] for u in range(tk // LANES)]
            cmax = tiles[0]
            for u in tiles[1:]:
                cmax = jnp.maximum(cmax, u)
            m_old = m_sc[rs, :]
            m_new = jnp.maximum(m_old, jnp.max(cmax.astype(F32), axis=1, keepdims=True))
            al_sc[slot, rs, :] = jnp.exp2(m_old - m_new)
            m_b = m_new.astype(BF16)
            p_sc[slot, rs, :] = jnp.concatenate([jnp.exp2(u - m_b) for u in tiles], axis=1)
            m_sc[rs, :] = m_new

    def stage_pv(t, slot):
        start = pl.multiple_of(jnp.minimum(t, last_step) * tks, tks)
        vbe = jnp.concatenate([v_ref[pl.ds(start, tks), :], jnp.ones((tks, hd), BF16)], axis=1)
        pv = jnp.dot(p_sc[slot], vbe, preferred_element_type=F32)
        alpha = al_sc[slot]
        acc_sc[...] = jnp.concatenate([alpha, alpha], axis=1) * acc_sc[...] + pv

    stage_qk(0, 0)
    stage_qk(1, 1)
    stage_softmax(0, 0, True)

    nfar = jnp.maximum((i - 1) // kb, 0)
    npairs = jnp.maximum((nfar - 1) // 2, 0)

    def far_pair(u, carry):
        t = 2 * u
        stage_pv(t, 0)
        stage_softmax(t + 1, 1, False)
        stage_qk(t + 2, 0)
        stage_pv(t + 1, 1)
        stage_softmax(t + 2, 0, False)
        stage_qk(t + 3, 1)
        return carry

    lax.fori_loop(0, npairs, far_pair, 0)

    t0 = 2 * npairs
    stage_pv(t0, 0)
    stage_softmax(t0 + 1, 1, True)
    stage_qk(t0 + 2, 0)
    stage_pv(t0 + 1, 1)
    stage_softmax(t0 + 2, 0, True)
    stage_qk(t0 + 3, 1)
    stage_pv(t0 + 2, 0)
    stage_softmax(t0 + 3, 1, True)
    stage_pv(t0 + 3, 1)

    if diff:
        o0 = acc_sc[0:tq, 0:hd] / acc_sc[0:tq, hd:2 * hd]
        o1 = acc_sc[tq:2 * tq, 0:hd] / acc_sc[tq:2 * tq, hd:2 * hd]
        dlt = o0 - lam_ref[0] * o1
        ms = jnp.mean(dlt * dlt, axis=-1, keepdims=True)
        o_ref[...] = ((dlt * lax.rsqrt(ms + EPS) * g_ref[...]) * lam_scale).astype(o_ref.dtype)
    else:
        for r in range(nh):
            rs = slice(r * tq, (r + 1) * tq)
            o_ref[:, r * hd:(r + 1) * hd] = (acc_sc[rs, 0:hd] / acc_sc[rs, hd:2 * hd]).astype(o_ref.dtype)


def _rel_bucket(dist):
    n = jnp.maximum(dist, 0)
    max_exact = REL_BUCKETS // 2
    nf = jnp.maximum(n, 1).astype(F32)
    large = max_exact + (jnp.log(nf / max_exact) / math.log(REL_MAX_DIST / max_exact)
                         * (REL_BUCKETS - max_exact)).astype(I32)
    large = jnp.minimum(large, REL_BUCKETS - 1)
    return jnp.where(n < max_exact, n, large)


def _near_bias(rel_table, tq, groups, nh):
    assert tq >= LANES, "keys older than one block must all fall in the last bucket"
    r = jnp.arange(tq, dtype=I32)[:, None]
    c = jnp.arange(2 * tq, dtype=I32)[None, :]
    dist = r + tq - c
    rel = (rel_table - rel_table[REL_BUCKETS - 1][None, :]) * np.float32(LOG2E)
    onehot = jax.nn.one_hot(_rel_bucket(dist), REL_BUCKETS, dtype=F32)
    b = jnp.einsum("rcb,bh->hrc", onehot, rel, precision=lax.Precision.HIGHEST)
    b = jnp.where((dist >= 0)[None], b, MASKED)
    heads = b.shape[0]
    tiles = jnp.stack([jnp.zeros((heads, tq, tq), F32), b[:, :, :tq], b[:, :, tq:],
                       jnp.full((heads, tq, tq), MASKED, F32)], axis=1)
    tiles = tiles.reshape(groups, nh, 4, tq, tq).transpose(0, 2, 1, 3, 4)
    return tiles.reshape(groups, 4, nh * tq, tq)


def _diff_attention(qkv, near, lam, subln_g, lam_scale, tq):
    s = qkv.shape[0]
    hd = A_V_DIM
    nh = 2
    kb = 1
    kcol = A_WIDTH // hd
    return pl.pallas_call(
        functools.partial(_flash_kernel, nh=nh, tq=tq, kb=kb, sub=min(tq, ATTN_SUB_ROWS), diff=True,
                          lam_scale=lam_scale),
        grid=(A_HEADS, s // tq),
        in_specs=[pl.BlockSpec(memory_space=pltpu.SMEM),
                  pl.BlockSpec((tq, hd), lambda h, i: (i, h)),
                  pl.BlockSpec((s, hd), lambda h, i: (0, kcol + h)),
                  pl.BlockSpec((s, hd), lambda h, i: (0, 2 * kcol + h)),
                  pl.BlockSpec((1, 4, nh * tq, tq), lambda h, i: (h, 0, 0, 0)),
                  pl.BlockSpec((1, hd), lambda h, i: (0, 0))],
        out_specs=pl.BlockSpec((tq, hd), lambda h, i: (i, h)),
        out_shape=jax.ShapeDtypeStruct((s, A_WIDTH), BF16),
        scratch_shapes=[pltpu.VMEM((nh * tq, hd), BF16),
                        pltpu.VMEM((nh * tq, LANES), F32),
                        pltpu.VMEM((nh * tq, 2 * hd), F32),
                        pltpu.VMEM((2, nh * tq, kb * tq), F32),
                        pltpu.VMEM((2, nh * tq, kb * tq), BF16),
                        pltpu.VMEM((2, nh * tq, LANES), F32)],
        compiler_params=_cparams(("parallel", "arbitrary")),
        name="diff_attention",
    )(lam, qkv, qkv, qkv, near, subln_g)


def _masked_attention(qkv, near, mask, tq):
    s = qkv.shape[0]
    hd = C_HEAD_DIM
    nh = C_GROUP
    kcol = C_WIDTH // hd
    vcol = kcol + C_KV_HEADS
    nk = s // tq
    kb = 2 if nk % 2 == 0 else 1
    return pl.pallas_call(
        functools.partial(_flash_kernel, nh=nh, tq=tq, kb=kb, sub=min(tq, ATTN_SUB_ROWS), diff=False,
                          lam_scale=1.0),
        grid=(C_KV_HEADS, s // tq),
        in_specs=[pl.BlockSpec((tq, nh * hd), lambda g, i: (i, g)),
                  pl.BlockSpec((s, hd), lambda g, i: (0, kcol + g)),
                  pl.BlockSpec((s, hd), lambda g, i: (0, vcol + g)),
                  pl.BlockSpec((1, 4, nh * tq, tq), lambda g, i: (g, 0, 0, 0)),
                  pl.BlockSpec((nk, tq, tq), lambda g, i: (0, i, 0))],
        out_specs=pl.BlockSpec((tq, nh * hd), lambda g, i: (i, g)),
        out_shape=jax.ShapeDtypeStruct((s, C_WIDTH), BF16),
        scratch_shapes=[pltpu.VMEM((nh * tq, hd), BF16),
                        pltpu.VMEM((nh * tq, LANES), F32),
                        pltpu.VMEM((nh * tq, 2 * hd), F32),
                        pltpu.VMEM((2, nh * tq, kb * tq), F32),
                        pltpu.VMEM((2, nh * tq, kb * tq), BF16),
                        pltpu.VMEM((2, nh * tq, LANES), F32)],
        compiler_params=_cparams(("parallel", "arbitrary")),
        name="selected_attention",
    )(qkv, qkv, qkv, near, mask)


def _sg_kernel(zb_ref, lng_ref, lnb_ref, w_ref, bs_ref, o_ref, *, nchunk):
    zb = zb_ref[...]
    gl = zb * (0.5 * (1.0 + jnp.tanh(np.float32(np.sqrt(2.0 / np.pi)) * (zb + 0.044715 * (zb * zb * zb)))))
    u = gl[:, :B_WIDTH]
    z = gl[:, B_WIDTH:]
    mu = jnp.mean(z, axis=-1, keepdims=True)
    zc = z - mu
    var = jnp.mean(zc * zc, axis=-1, keepdims=True)
    zn = (zc * lax.rsqrt(var + EPS) * lng_ref[...] + lnb_ref[...]).astype(BF16)
    row = lax.broadcasted_iota(I32, (CHUNK, CHUNK), 0)
    col = lax.broadcasted_iota(I32, (CHUNK, CHUNK), 1)
    for g in range(B_GROUPS):
        w = jnp.where(row >= col, w_ref[g], 0.0).astype(BF16)
        bias = bs_ref[g]
        lo = g * B_GROUP_DIM
        for c in range(nchunk):
            r0 = c * CHUNK
            sz = jnp.dot(w, zn[r0:r0 + CHUNK, lo:lo + B_GROUP_DIM], preferred_element_type=F32) + bias
            o_ref[r0:r0 + CHUNK, lo:lo + B_GROUP_DIM] = (u[r0:r0 + CHUNK, lo:lo + B_GROUP_DIM] * sz).astype(o_ref.dtype)


def _spatial_gating(zb, ln_g, ln_b, w_s, b_s):
    s = zb.shape[0]
    t = _tile(s, 256)
    return pl.pallas_call(
        functools.partial(_sg_kernel, nchunk=t // CHUNK),
        grid=(s // t,),
        in_specs=[pl.BlockSpec((t, 2 * B_WIDTH), lambda i: (i, 0)),
                  pl.BlockSpec((1, B_WIDTH), lambda i: (0, 0)),
                  pl.BlockSpec((1, B_WIDTH), lambda i: (0, 0)),
                  pl.BlockSpec((B_GROUPS, CHUNK, CHUNK), lambda i: (0, 0, 0)),
                  pl.BlockSpec((B_GROUPS, CHUNK, 1), lambda i: (0, 0, 0))],
        out_specs=pl.BlockSpec((t, B_WIDTH), lambda i: (i, 0)),
        out_shape=jax.ShapeDtypeStruct((s, B_WIDTH), BF16),
        compiler_params=_cparams(("parallel",)),
        name="spatial_gating",
    )(zb, ln_g.reshape(1, B_WIDTH), ln_b.reshape(1, B_WIDTH), w_s, b_s.reshape(B_GROUPS, CHUNK, 1))


def _select_kernel(qi_ref, kt_ref, w_ref, o_ref, keys_sc, sc_sc, wb_sc, mx_sc, *, tkc, topk):
    tqi = IDX_QBLOCK
    i = pl.program_id(0)
    nk = o_ref.shape[0]
    nch = (i * tqi + tqi + tkc - 1) // tkc
    qpos = i * tqi + lax.broadcasted_iota(I32, (tqi, tkc), 0)
    kloc = lax.broadcasted_iota(I32, (tqi, tkc), 1)
    qrow = i * tqi + lax.broadcasted_iota(I32, (tqi, LANES), 0)
    klane = lax.broadcasted_iota(I32, (tqi, LANES), 1)

    wgt = w_ref[...] * np.float32(IDX_DIM ** -0.5)
    for h in range(IDX_HEADS):
        wb_sc[h] = jnp.broadcast_to(wgt[:, h:h + 1], (tqi, LANES))
    mx_sc[...] = jnp.full(mx_sc.shape, -jnp.inf, F32)

    def stage_dot(c, slot):
        sc_sc[slot] = jnp.dot(qi_ref[0], kt_ref[jnp.minimum(c, nch - 1)], preferred_element_type=F32)

    def stage_reduce(c, slot):
        c = jnp.minimum(c, nch - 1)
        for u in range(tkc // LANES):
            ls = slice(u * LANES, (u + 1) * LANES)
            acc = jnp.zeros((tqi, LANES), F32)
            for h in range(IDX_HEADS):
                acc += jnp.maximum(sc_sc[slot, h * tqi:(h + 1) * tqi, ls], 0.0) * wb_sc[h]
            acc = acc + 0.0
            bits = pltpu.bitcast(acc, I32)
            ordered = jnp.where(bits < 0, bits ^ jnp.int32(0x7FFFFFFF), bits)
            causal = c * tkc + u * LANES + klane <= qrow
            keys_sc[c, :, ls] = jnp.where(causal, ordered, jnp.int32(INT_MIN))
            mx_sc[...] = jnp.maximum(mx_sc[...], jnp.where(causal, acc, -jnp.inf))

    stage_dot(0, 0)

    def score_quad(u, carry):
        c = 4 * u
        for k in range(0, 4, 2):
            stage_dot(c + k + 1, 1)
            stage_reduce(c + k, 0)
            stage_dot(c + k + 2, 0)
            stage_reduce(c + k + 1, 1)
        return carry

    lax.fori_loop(0, (nch + 3) // 4, score_quad, 0)

    def count_ge(cand):
        candb = jnp.broadcast_to(cand, (tqi, LANES))

        def count_body(c, cnt):
            kk = keys_sc[c]
            for u in range(tkc // LANES):
                cnt += jnp.where(kk[:, u * LANES:(u + 1) * LANES] >= candb, 1, 0)
            return cnt

        cnt = lax.fori_loop(0, nch, count_body, jnp.zeros((tqi, LANES), I32))
        return jnp.sum(cnt.astype(F32), axis=1, keepdims=True)

    want = np.float32(topk)
    fbits = pltpu.bitcast(jnp.max(mx_sc[...], axis=1, keepdims=True), I32)
    kmax = jnp.where(fbits < 0, fbits ^ jnp.int32(0x7FFFFFFF), fbits)
    few = i * tqi + lax.broadcasted_iota(I32, (tqi, 1), 0) + 1 <= topk
    lo0 = jnp.full((tqi, 1), INT_MIN, I32)
    hi0 = jnp.where(few, lo0 + 1, kmax + 1)

    def narrow(state, cand):
        lo, hi, active = state
        total = count_ge(cand)
        open_ = active > 0.0
        up = open_ & (total >= want)
        lo = jnp.where(up, cand, lo)
        hi = jnp.where(open_ & (~up), cand, hi)
        open_ = open_ & (~(up & (total == want))) & ((hi - lo) != 1)
        return lo, hi, jnp.where(open_, 1.0, 0.0)

    def midpoint(state):
        lo, hi, _ = state
        return lo + lax.shift_right_logical(hi - lo, jnp.int32(1))

    probe = jnp.maximum(kmax, jnp.int32(INT_MIN + SEARCH_PROBE_DROP + 1)) - jnp.int32(SEARCH_PROBE_DROP)
    state = narrow((lo0, hi0, jnp.where(few, 0.0, 1.0)), jnp.where(few, lo0, probe))

    def bisect_body(carry):
        state, _ = carry
        state = narrow(state, midpoint(state))
        state = narrow(state, midpoint(state))
        return state, jnp.sum(state[2])

    (thr, _, _), _ = lax.while_loop(lambda carry: carry[1] > 0.0, bisect_body, (state, jnp.float32(1.0)))
    thrb = jnp.broadcast_to(thr, (tqi, tkc))

    def mask_body(c, carry):
        sel = (keys_sc[c] >= thrb) & (c * tkc + kloc <= qpos)
        o_ref[c] = jnp.where(sel, 0.0, MASKED).astype(o_ref.dtype)
        return carry

    lax.fori_loop(0, nch, mask_body, 0)

    def fill_body(c, carry):
        o_ref[c] = jnp.full((tqi, tkc), MASKED, o_ref.dtype)
        return carry

    lax.fori_loop(nch, nk, fill_body, 0)


def _select_mask(qi_stack, kt, wi, tkc, topk):
    nq, rows, _ = qi_stack.shape
    nk = kt.shape[0]
    s = nq * IDX_QBLOCK
    return pl.pallas_call(
        functools.partial(_select_kernel, tkc=tkc, topk=topk),
        grid=(nq,),
        in_specs=[pl.BlockSpec((1, rows, IDX_DIM), lambda i: (i, 0, 0)),
                  pl.BlockSpec((nk, IDX_DIM, tkc), lambda i: (0, 0, 0)),
                  pl.BlockSpec((IDX_QBLOCK, IDX_HEADS), lambda i: (i, 0))],
        out_specs=pl.BlockSpec((nk, IDX_QBLOCK, tkc), lambda i: (0, i, 0)),
        out_shape=jax.ShapeDtypeStruct((nk, s, tkc), BF16),
        scratch_shapes=[pltpu.VMEM((nk, IDX_QBLOCK, tkc), I32),
                        pltpu.VMEM((2, rows, tkc), F32),
                        pltpu.VMEM((IDX_HEADS, IDX_QBLOCK, LANES), F32),
                        pltpu.VMEM((IDX_QBLOCK, LANES), F32)],
        compiler_params=_cparams(("parallel",)),
        name="indexer_select",
    )(qi_stack, kt, wi)


def _vec_pack(d, *rows):
    rows = [r.reshape(1, d).astype(F32) for r in rows]
    rows += [jnp.zeros((1, d), F32)] * (8 - len(rows))
    return jnp.concatenate(rows, axis=0)


def _pad_cols(w, n):
    return jnp.pad(w, ((0, 0), (0, n - w.shape[1])))


def kernel(x, c, norm_g, mod_w, mod_b, ffn_w1, ffn_w2, rel_table, ab_w_in, ab_w_out, diff_lam,
           diff_subln_g, sg_ln_g, sg_ln_b, sg_w, sg_b, dsa_w_in, dsa_w_out, final_g):
    batch, s, d = x.shape
    depth = norm_g.shape[0]
    assert batch == 1 and s % IDX_QBLOCK == 0

    tq_a = _tile(s, 512)
    tq_c = _tile(s, 256)
    topk = min(TOPK_MAX, s // 4)

    mod = _modulation(c, mod_w, mod_b).reshape(depth, 9, d)
    ffn_a, ffn_b, ffn_tiles = _ffn_weights(ffn_w1, ffn_w2)
    near_a = _near_bias(rel_table, tq_a, A_HEADS, 2)
    near_c = _near_bias(rel_table, tq_c, C_KV_HEADS, C_GROUP)

    xs = x.reshape(s, d)
    zeros_d = jnp.zeros((d,), F32)
    for li in range(depth):
        def vec(j, li=li):
            last = final_g if (li == depth - 1 and j == 2) else zeros_d
            return _vec_pack(d, norm_g[li, j], mod[li, 3 * j], mod[li, 3 * j + 1], mod[li, 3 * j + 2], last)

        def ffn(xs, j, k, final=False, li=li):
            return _ffn(xs, vec(j), ffn_a, ffn_b, ffn_tiles, 2 * li + k, final=final)

        xs = ffn(xs, 0, 0)

        v1 = vec(1)
        jj = li // 2
        if li % 2 == 0:
            w_in = ab_w_in[jj]
            w_qkv = jnp.concatenate([w_in[:, :A_WIDTH] * np.float32(A_QK_DIM ** -0.5 * LOG2E),
                                     w_in[:, A_WIDTH:3 * A_WIDTH]], axis=1).astype(BF16)
            w_zb = w_in[:, 3 * A_WIDTH:].astype(BF16)
            qkv = _proj(xs, v1, _col_blocks(w_qkv, PROJ_COLS), BF16)
            zb = _proj(xs, v1, _col_blocks(w_zb, PROJ_COLS), F32)
            lam_init = 0.8 - 0.6 * math.exp(-0.3 * li)
            lp = diff_lam[jj].astype(F32)
            lam = jnp.exp(jnp.sum(lp[0] * lp[1])) - jnp.exp(jnp.sum(lp[2] * lp[3])) + lam_init
            ya = _diff_attention(qkv, near_a, lam.reshape(1), diff_subln_g[jj].reshape(1, A_V_DIM),
                                 1.0 - lam_init, tq_a)
            yb = _spatial_gating(zb, sg_ln_g[jj], sg_ln_b[jj], sg_w[jj], sg_b[jj])
            w_out = ab_w_out[jj].astype(BF16)
            xs = _outproj(xs, v1, [ya, yb], [_col_blocks(w_out[:A_WIDTH], PROJ_COLS),
                                             _col_blocks(w_out[A_WIDTH:], PROJ_COLS)])
        else:
            w_in = dsa_w_in[jj]
            o_idx = C_WIDTH + 2 * C_KV_WIDTH
            o_ki = o_idx + IDX_HEADS * IDX_DIM
            w_main = jnp.concatenate([w_in[:, :C_WIDTH] * np.float32(C_HEAD_DIM ** -0.5 * LOG2E),
                                      w_in[:, C_WIDTH:o_ki]], axis=1).astype(BF16)
            w_kiw = _pad_cols(w_in[:, o_ki:], LANES).astype(BF16)
            main = _proj(xs, v1, _col_blocks(w_main, PROJ_COLS), BF16)
            kiw = _proj(xs, v1, _col_blocks(w_kiw, PROJ_COLS), F32)
            nq = s // IDX_QBLOCK
            qi = main[:, o_idx:o_ki].reshape(nq, IDX_QBLOCK, IDX_HEADS, IDX_DIM)
            qi = qi.transpose(0, 2, 1, 3).reshape(nq, IDX_HEADS * IDX_QBLOCK, IDX_DIM)
            kt = kiw[:, :IDX_DIM].astype(BF16).reshape(s // tq_c, tq_c, IDX_DIM).transpose(0, 2, 1)
            wi = kiw[:, IDX_DIM:IDX_DIM + IDX_HEADS]
            mask = _select_mask(qi, kt, wi, tq_c, topk)
            yc = _masked_attention(main, near_c, mask, tq_c)
            xs = _outproj(xs, v1, [yc], [_col_blocks(dsa_w_out[jj].astype(BF16), PROJ_COLS)])

        xs = ffn(xs, 2, 1, final=(li == depth - 1))
    return xs.reshape(batch, s, d)
```

```python
import functools
import math

import jax
import jax.numpy as jnp
import numpy as np
from jax import lax
from jax.experimental import pallas as pl
from jax.experimental.pallas import tpu as pltpu

F32 = jnp.float32
BF16 = jnp.bfloat16
I32 = jnp.int32

EPS = 1e-6
MASKED = -1e30
LANES = 128
INT_MIN = -(2 ** 31)
LOG2E = math.log2(math.e)
PREP_ROWS = 128
PREP_COLS = 256
NORM_ROWS = 16
ATTN_SUB_ROWS = 128

A_HEADS = 8
A_QK_DIM = 64
A_V_DIM = 128
A_WIDTH = A_HEADS * A_V_DIM
B_GROUPS = 8
B_GROUP_DIM = 128
B_WIDTH = B_GROUPS * B_GROUP_DIM
CHUNK = 128
C_HEADS = 16
C_KV_HEADS = 4
C_GROUP = C_HEADS // C_KV_HEADS
C_HEAD_DIM = 128
C_WIDTH = C_HEADS * C_HEAD_DIM
C_KV_WIDTH = C_KV_HEADS * C_HEAD_DIM
IDX_HEADS = 16
IDX_DIM = 64
TOPK_MAX = 256
REL_BUCKETS = 32
REL_MAX_DIST = 128
IDX_QBLOCK = 128
SEARCH_PROBE_DROP = 2 ** 24

VMEM_LIMIT = 56 * 1024 * 1024


def _cparams(sem):
    return pltpu.CompilerParams(dimension_semantics=sem, vmem_limit_bytes=VMEM_LIMIT)


FFN_ROWS, FFN_COLS = 512, 512
PROJ_ROWS, PROJ_COLS = 1024, 1024


def _col_blocks(w, tn):
    k, n = w.shape
    tn = _tile(n, tn)
    return w.reshape(k, n // tn, tn).transpose(1, 0, 2)


def _tile(n, want):
    if n <= want:
        return n
    t = want
    while n % t:
        t //= 2
    return t


def _mod_kernel(c_ref, w_ref, b_ref, o_ref):
    c = c_ref[...]
    cs = c * (1.0 / (1.0 + jnp.exp(-c)))
    o_ref[0] = jnp.sum(cs * w_ref[0], axis=0, keepdims=True) + b_ref[0]


def _modulation(c, mod_w, mod_b):
    depth, d, n = mod_w.shape
    tn = _tile(n, 1024)
    out = pl.pallas_call(
        _mod_kernel,
        grid=(depth, n // tn),
        in_specs=[pl.BlockSpec((d, 1), lambda l, j: (0, 0)),
                  pl.BlockSpec((1, d, tn), lambda l, j: (l, 0, j)),
                  pl.BlockSpec((1, 1, tn), lambda l, j: (l, 0, j))],
        out_specs=pl.BlockSpec((1, 1, tn), lambda l, j: (l, 0, j)),
        out_shape=jax.ShapeDtypeStruct((depth, 1, n), F32),
        compiler_params=_cparams(("arbitrary", "arbitrary")),
        name="adaln_mod",
    )(c.reshape(d, 1), mod_w, mod_b.reshape(depth, 1, n))
    return out.reshape(depth, n)


def _prenorm_into(x_ref, vec_ref, hn_ref, inv_ref):
    rows, d = x_ref.shape
    gain = vec_ref[0:1, :] * (1.0 + vec_ref[2:3, :])
    shift = vec_ref[1:2, :]

    def scale_body(r, carry):
        rs = pl.ds(pl.multiple_of(r * NORM_ROWS, NORM_ROWS), NORM_ROWS)
        x = x_ref[rs, :]
        ms = jnp.sum(x * x, axis=-1, keepdims=True) * np.float32(1.0 / d)
        inv_ref[rs, :] = jnp.broadcast_to(lax.rsqrt(ms + EPS), (NORM_ROWS, LANES))
        return carry

    lax.fori_loop(0, rows // NORM_ROWS, scale_body, 0, unroll=8)

    def apply_body(r, carry):
        rs = pl.ds(pl.multiple_of(r * NORM_ROWS, NORM_ROWS), NORM_ROWS)
        inv = inv_ref[rs, :]
        inv = jnp.concatenate([inv] * (d // LANES), axis=1)
        hn_ref[rs, :] = (x_ref[rs, :] * inv * gain + shift).astype(hn_ref.dtype)
        return carry

    lax.fori_loop(0, rows // NORM_ROWS, apply_body, 0, unroll=2)


def _ffn_kernel(x_ref, vec_ref, w1g_ref, w1u_ref, w2_ref, o_ref, hn_sc, inv_sc, *, nf, final):
    f = pl.program_id(1)

    @pl.when(f == 0)
    def _():
        _prenorm_into(x_ref, vec_ref, hn_sc, inv_sc)
        o_ref[...] = jnp.zeros_like(o_ref)

    hn = hn_sc[...]
    g = jnp.dot(hn, w1g_ref[...], preferred_element_type=F32)
    u = jnp.dot(hn, w1u_ref[...], preferred_element_type=F32)
    a = (g * (1.0 / (1.0 + jnp.exp(-g))) * u).astype(BF16)
    o_ref[...] += jnp.dot(a, w2_ref[...], preferred_element_type=F32)

    @pl.when(f == nf - 1)
    def _():
        y = x_ref[...] + 0.5 * (1.0 + vec_ref[3:4, :]) * o_ref[...]
        if final:
            ms = jnp.mean(y * y, axis=-1, keepdims=True)
            y = y * lax.rsqrt(ms + EPS) * vec_ref[4:5, :]
        o_ref[...] = y


def _ffn_weights(ffn_w1, ffn_w2):
    depth, two, d, _ = ffn_w1.shape
    d_ff = ffn_w2.shape[2]
    steps = depth * two
    tf = FFN_COLS if d_ff > FFN_COLS else d_ff
    nf = -(-d_ff // tf)
    pad = nf * tf - d_ff
    assert d_ff % LANES == 0
    tr = _tile(d, PREP_ROWS)

    def w1_kernel(w_ref, o_ref):
        o_ref[:, 0:d_ff] = w_ref[:, 0:d_ff].astype(BF16)
        o_ref[:, d_ff + pad:2 * d_ff + pad] = w_ref[:, d_ff:2 * d_ff].astype(BF16)
        if pad:
            o_ref[:, d_ff:d_ff + pad] = jnp.zeros((tr, pad), BF16)
            o_ref[:, 2 * d_ff + pad:] = jnp.zeros((tr, pad), BF16)

    w1 = pl.pallas_call(
        w1_kernel,
        grid=(steps, d // tr),
        in_specs=[pl.BlockSpec((None, tr, 2 * d_ff), lambda s, r: (s, r, 0))],
        out_specs=pl.BlockSpec((None, tr, 2 * nf * tf), lambda s, r: (s, r, 0)),
        out_shape=jax.ShapeDtypeStruct((steps, d, 2 * nf * tf), BF16),
        compiler_params=_cparams(("parallel", "parallel")),
        name="ffn_w1_cast_pad",
    )(ffn_w1.reshape(steps, d, 2 * d_ff))

    tc = _tile(d, PREP_COLS)

    def w2_kernel(w_ref, o_ref):
        o_ref[0:d_ff, :] = w_ref[...].astype(BF16)
        if pad:
            o_ref[d_ff:, :] = jnp.zeros((pad, tc), BF16)

    w2 = pl.pallas_call(
        w2_kernel,
        grid=(steps, d // tc),
        in_specs=[pl.BlockSpec((None, d_ff, tc), lambda s, c: (s, 0, c))],
        out_specs=pl.BlockSpec((None, d_ff + pad, tc), lambda s, c: (s, 0, c)),
        out_shape=jax.ShapeDtypeStruct((steps, d_ff + pad, d), BF16),
        compiler_params=_cparams(("parallel", "parallel")),
        name="ffn_w2_cast_pad",
    )(ffn_w2.reshape(steps, d_ff, d))
    return w1, w2, nf


def _ffn(x, vec, w1, w2, nf, step, *, final):
    s, d = x.shape
    tf = w2.shape[1] // nf
    tm = _tile(s, FFN_ROWS)
    return pl.pallas_call(
        functools.partial(_ffn_kernel, nf=nf, final=final),
        grid=(s // tm, nf),
        in_specs=[pl.BlockSpec((tm, d), lambda i, f: (i, 0)),
                  pl.BlockSpec((8, d), lambda i, f: (0, 0)),
                  pl.BlockSpec((None, d, tf), lambda i, f: (step, 0, f)),
                  pl.BlockSpec((None, d, tf), lambda i, f: (step, 0, nf + f)),
                  pl.BlockSpec((None, tf, d), lambda i, f: (step, f, 0))],
        out_specs=pl.BlockSpec((tm, d), lambda i, f: (i, 0)),
        out_shape=jax.ShapeDtypeStruct((s, d), F32),
        scratch_shapes=[pltpu.VMEM((tm, d), BF16), pltpu.VMEM((tm, LANES), F32)],
        compiler_params=_cparams(("parallel", "arbitrary")),
        name="swiglu_halfstep",
    )(x, vec, w1, w1, w2)


def _proj_kernel(x_ref, vec_ref, w_ref, o_ref, hn_sc, inv_sc):
    @pl.when(pl.program_id(1) == 0)
    def _():
        _prenorm_into(x_ref, vec_ref, hn_sc, inv_sc)

    o_ref[...] = jnp.dot(hn_sc[...], w_ref[0], preferred_element_type=F32).astype(o_ref.dtype)


def _proj(x, vec, w, out_dtype):
    s, d = x.shape
    nn, _, tn = w.shape
    n = nn * tn
    tm = _tile(s, PROJ_ROWS)
    return pl.pallas_call(
        _proj_kernel,
        grid=(s // tm, nn),
        in_specs=[pl.BlockSpec((tm, d), lambda i, j: (i, 0)),
                  pl.BlockSpec((8, d), lambda i, j: (0, 0)),
                  pl.BlockSpec((1, d, tn), lambda i, j: (j, 0, 0))],
        out_specs=pl.BlockSpec((tm, tn), lambda i, j: (i, j)),
        out_shape=jax.ShapeDtypeStruct((s, n), out_dtype),
        scratch_shapes=[pltpu.VMEM((tm, d), BF16), pltpu.VMEM((tm, LANES), F32)],
        compiler_params=_cparams(("parallel", "arbitrary")),
        name="norm_mod_proj",
    )(x, vec, w)


def _outproj_kernel(*refs, n_in):
    x_ref, vec_ref = refs[0], refs[1]
    lhs = refs[2:2 + n_in]
    ws = refs[2 + n_in:2 + 2 * n_in]
    o_ref = refs[2 + 2 * n_in]
    acc = jnp.dot(lhs[0][...], ws[0][0], preferred_element_type=F32)
    for a, w in zip(lhs[1:], ws[1:]):
        acc += jnp.dot(a[...], w[0], preferred_element_type=F32)
    o_ref[...] = x_ref[...] + (1.0 + vec_ref[3:4, :]) * acc


def _outproj(x, vec, lhs, ws):
    s, d = x.shape
    tm = _tile(s, PROJ_ROWS)
    tn = ws[0].shape[2]
    n_in = len(lhs)
    in_specs = [pl.BlockSpec((tm, tn), lambda i, j: (i, j)),
                pl.BlockSpec((8, tn), lambda i, j: (0, j))]
    in_specs += [pl.BlockSpec((tm, a.shape[1]), lambda i, j: (i, 0)) for a in lhs]
    in_specs += [pl.BlockSpec((1, w.shape[1], tn), lambda i, j: (j, 0, 0)) for w in ws]
    return pl.pallas_call(
        functools.partial(_outproj_kernel, n_in=n_in),
        grid=(s // tm, d // tn),
        in_specs=in_specs,
        out_specs=pl.BlockSpec((tm, tn), lambda i, j: (i, j)),
        out_shape=jax.ShapeDtypeStruct((s, d), F32),
        compiler_params=_cparams(("parallel", "arbitrary")),
        name="outproj_residual",
    )(x, vec, *lhs, *ws)


def _flash_kernel(*refs, nh, tq, kb, sub, diff, lam_scale):
    if diff:
        lam_ref, q_ref, k_ref, v_ref, nb_ref, g_ref, o_ref, qs_sc, m_sc, acc_sc, s_sc, p_sc, al_sc = refs
        mask_ref = None
    else:
        q_ref, k_ref, v_ref, nb_ref, mask_ref, o_ref, qs_sc, m_sc, acc_sc, s_sc, p_sc, al_sc = refs
    tk = tq
    rows = nh * tq
    hd = k_ref.shape[1]
    assert hd == LANES and rows % sub == 0 and tq % sub == 0
    i = pl.program_id(1)

    if diff:
        q = q_ref[...]
        lane = lax.broadcasted_iota(I32, q.shape, 1)
        zero = jnp.zeros_like(q)
        qs_sc[0:tq, :] = jnp.where(lane < A_QK_DIM, q, zero)
        qs_sc[tq:2 * tq, :] = jnp.where(lane >= A_QK_DIM, q, zero)
    else:
        for r in range(nh):
            qs_sc[r * tq:(r + 1) * tq, :] = q_ref[:, r * hd:(r + 1) * hd]
    m_sc[...] = jnp.full(m_sc.shape, -jnp.inf, F32)
    acc_sc[...] = jnp.zeros(acc_sc.shape, F32)

    tks = kb * tk
    nk = k_ref.shape[0] // tk
    last_step = i // kb

    def bias_index(j):
        return jnp.where(j > i, 3, jnp.clip(j - (i - 2), 0, 2))

    def stage_qk(t, slot):
        start = pl.multiple_of(jnp.minimum(t, last_step) * tks, tks)
        kblk = k_ref[pl.ds(start, tks), :]
        s_sc[slot] = lax.dot_general(qs_sc[...], kblk, (((1,), (1,)), ((), ())), preferred_element_type=F32)

    def stage_softmax(t, slot, biased):
        tc = jnp.minimum(t, last_step)
        for r in range(rows // sub):
            rs = slice(r * sub, (r + 1) * sub)
            tiles = []
            for b in range(kb):
                j = tc * kb + b
                s = s_sc[slot, rs, b * tk:(b + 1) * tk]
                if biased:
                    col = jnp.where(t > last_step, 3, bias_index(j))
                    s = s + nb_ref[0, col, rs, :]
                s = s.astype(BF16)
                if mask_ref is not None:
                    off = (r * sub) % tq
                    s = s + mask_ref[jnp.minimum(j, nk - 1), off:off + sub, :]
                tiles += [s[:, u * LANES:(u + 1) * LANES] for u in range(tk // LANES)]
            cmax = tiles[0]
            for u in tiles[1:]:
                cmax = jnp.maximum(cmax, u)
            m_old = m_sc[rs, :]
            m_new = jnp.maximum(m_old, jnp.max(cmax.astype(F32), axis=1, keepdims=True))
            al_sc[slot, rs, :] = jnp.exp2(m_old - m_new)
            m_b = m_new.astype(BF16)
            p_sc[slot, rs, :] = jnp.concatenate([jnp.exp2(u - m_b) for u in tiles], axis=1)
            m_sc[rs, :] = m_new

    def stage_pv(t, slot):
        start = pl.multiple_of(jnp.minimum(t, last_step) * tks, tks)
        vbe = jnp.concatenate([v_ref[pl.ds(start, tks), :], jnp.ones((tks, hd), BF16)], axis=1)
        pv = jnp.dot(p_sc[slot], vbe, preferred_element_type=F32)
        alpha = al_sc[slot]
        acc_sc[...] = jnp.concatenate([alpha, alpha], axis=1) * acc_sc[...] + pv

    stage_qk(0, 0)
    stage_qk(1, 1)
    stage_softmax(0, 0, True)

    nfar = jnp.maximum((i - 1) // kb, 0)
    npairs = jnp.maximum((nfar - 1) // 2, 0)

    def far_pair(u, carry):
        t = 2 * u
        stage_pv(t, 0)
        stage_softmax(t + 1, 1, False)
        stage_qk(t + 2, 0)
        stage_pv(t + 1, 1)
        stage_softmax(t + 2, 0, False)
        stage_qk(t + 3, 1)
        return carry

    lax.fori_loop(0, npairs, far_pair, 0)

    t0 = 2 * npairs
    stage_pv(t0, 0)
    stage_softmax(t0 + 1, 1, True)
    stage_qk(t0 + 2, 0)
    stage_pv(t0 + 1, 1)
    stage_softmax(t0 + 2, 0, True)
    stage_qk(t0 + 3, 1)
    stage_pv(t0 + 2, 0)

    @pl.when(t0 + 3 <= last_step)
    def _():
        stage_softmax(t0 + 3, 1, True)
        stage_pv(t0 + 3, 1)

    if diff:
        o0 = acc_sc[0:tq, 0:hd] / acc_sc[0:tq, hd:2 * hd]
        o1 = acc_sc[tq:2 * tq, 0:hd] / acc_sc[tq:2 * tq, hd:2 * hd]
        dlt = o0 - lam_ref[0] * o1
        ms = jnp.mean(dlt * dlt, axis=-1, keepdims=True)
        o_ref[...] = ((dlt * lax.rsqrt(ms + EPS) * g_ref[...]) * lam_scale).astype(o_ref.dtype)
    else:
        for r in range(nh):
            rs = slice(r * tq, (r + 1) * tq)
            o_ref[:, r * hd:(r + 1) * hd] = (acc_sc[rs, 0:hd] / acc_sc[rs, hd:2 * hd]).astype(o_ref.dtype)


def _rel_bucket(dist):
    n = jnp.maximum(dist, 0)
    max_exact = REL_BUCKETS // 2
    nf = jnp.maximum(n, 1).astype(F32)
    large = max_exact + (jnp.log(nf / max_exact) / math.log(REL_MAX_DIST / max_exact)
                         * (REL_BUCKETS - max_exact)).astype(I32)
    large = jnp.minimum(large, REL_BUCKETS - 1)
    return jnp.where(n < max_exact, n, large)


def _near_bias(rel_table, tq, groups, nh):
    assert tq >= LANES, "keys older than one block must all fall in the last bucket"
    r = jnp.arange(tq, dtype=I32)[:, None]
    c = jnp.arange(2 * tq, dtype=I32)[None, :]
    dist = r + tq - c
    rel = (rel_table - rel_table[REL_BUCKETS - 1][None, :]) * np.float32(LOG2E)
    onehot = jax.nn.one_hot(_rel_bucket(dist), REL_BUCKETS, dtype=F32)
    b = jnp.einsum("rcb,bh->hrc", onehot, rel, precision=lax.Precision.HIGHEST)
    b = jnp.where((dist >= 0)[None], b, MASKED)
    heads = b.shape[0]
    tiles = jnp.stack([jnp.zeros((heads, tq, tq), F32), b[:, :, :tq], b[:, :, tq:],
                       jnp.full((heads, tq, tq), MASKED, F32)], axis=1)
    tiles = tiles.reshape(groups, nh, 4, tq, tq).transpose(0, 2, 1, 3, 4)
    return tiles.reshape(groups, 4, nh * tq, tq)


def _diff_attention(qkv, near, lam, subln_g, lam_scale, tq):
    s = qkv.shape[0]
    hd = A_V_DIM
    nh = 2
    kb = 1
    kcol = A_WIDTH // hd
    return pl.pallas_call(
        functools.partial(_flash_kernel, nh=nh, tq=tq, kb=kb, sub=min(tq, ATTN_SUB_ROWS), diff=True,
                          lam_scale=lam_scale),
        grid=(A_HEADS, s // tq),
        in_specs=[pl.BlockSpec(memory_space=pltpu.SMEM),
                  pl.BlockSpec((tq, hd), lambda h, i: (i, h)),
                  pl.BlockSpec((s, hd), lambda h, i: (0, kcol + h)),
                  pl.BlockSpec((s, hd), lambda h, i: (0, 2 * kcol + h)),
                  pl.BlockSpec((1, 4, nh * tq, tq), lambda h, i: (h, 0, 0, 0)),
                  pl.BlockSpec((1, hd), lambda h, i: (0, 0))],
        out_specs=pl.BlockSpec((tq, hd), lambda h, i: (i, h)),
        out_shape=jax.ShapeDtypeStruct((s, A_WIDTH), BF16),
        scratch_shapes=[pltpu.VMEM((nh * tq, hd), BF16),
                        pltpu.VMEM((nh * tq, LANES), F32),
                        pltpu.VMEM((nh * tq, 2 * hd), F32),
                        pltpu.VMEM((2, nh * tq, kb * tq), F32),
                        pltpu.VMEM((2, nh * tq, kb * tq), BF16),
                        pltpu.VMEM((2, nh * tq, LANES), F32)],
        compiler_params=_cparams(("parallel", "arbitrary")),
        name="diff_attention",
    )(lam, qkv, qkv, qkv, near, subln_g)


def _masked_attention(qkv, near, mask, tq):
    s = qkv.shape[0]
    hd = C_HEAD_DIM
    nh = C_GROUP
    kcol = C_WIDTH // hd
    vcol = kcol + C_KV_HEADS
    nk = s // tq
    kb = 2 if nk % 2 == 0 else 1
    return pl.pallas_call(
        functools.partial(_flash_kernel, nh=nh, tq=tq, kb=kb, sub=min(tq, ATTN_SUB_ROWS), diff=False,
                          lam_scale=1.0),
        grid=(C_KV_HEADS, s // tq),
        in_specs=[pl.BlockSpec((tq, nh * hd), lambda g, i: (i, g)),
                  pl.BlockSpec((s, hd), lambda g, i: (0, kcol + g)),
                  pl.BlockSpec((s, hd), lambda g, i: (0, vcol + g)),
                  pl.BlockSpec((1, 4, nh * tq, tq), lambda g, i: (g, 0, 0, 0)),
                  pl.BlockSpec((nk, tq, tq), lambda g, i: (0, i, 0))],
        out_specs=pl.BlockSpec((tq, nh * hd), lambda g, i: (i, g)),
        out_shape=jax.ShapeDtypeStruct((s, C_WIDTH), BF16),
        scratch_shapes=[pltpu.VMEM((nh * tq, hd), BF16),
                        pltpu.VMEM((nh * tq, LANES), F32),
                        pltpu.VMEM((nh * tq, 2 * hd), F32),
                        pltpu.VMEM((2, nh * tq, kb * tq), F32),
                        pltpu.VMEM((2, nh * tq, kb * tq), BF16),
                        pltpu.VMEM((2, nh * tq, LANES), F32)],
        compiler_params=_cparams(("parallel", "arbitrary")),
        name="selected_attention",
    )(qkv, qkv, qkv, near, mask)


def _sg_kernel(zb_ref, lng_ref, lnb_ref, w_ref, bs_ref, o_ref, *, nchunk):
    zb = zb_ref[...]
    gl = zb * (0.5 * (1.0 + jnp.tanh(np.float32(np.sqrt(2.0 / np.pi)) * (zb + 0.044715 * (zb * zb * zb)))))
    u = gl[:, :B_WIDTH]
    z = gl[:, B_WIDTH:]
    mu = jnp.mean(z, axis=-1, keepdims=True)
    zc = z - mu
    var = jnp.mean(zc * zc, axis=-1, keepdims=True)
    zn = (zc * lax.rsqrt(var + EPS) * lng_ref[...] + lnb_ref[...]).astype(BF16)
    row = lax.broadcasted_iota(I32, (CHUNK, CHUNK), 0)
    col = lax.broadcasted_iota(I32, (CHUNK, CHUNK), 1)
    for g in range(B_GROUPS):
        w = jnp.where(row >= col, w_ref[g], 0.0).astype(BF16)
        bias = bs_ref[g]
        lo = g * B_GROUP_DIM
        for c in range(nchunk):
            r0 = c * CHUNK
            sz = jnp.dot(w, zn[r0:r0 + CHUNK, lo:lo + B_GROUP_DIM], preferred_element_type=F32) + bias
            o_ref[r0:r0 + CHUNK, lo:lo + B_GROUP_DIM] = (u[r0:r0 + CHUNK, lo:lo + B_GROUP_DIM] * sz).astype(o_ref.dtype)


def _spatial_gating(zb, ln_g, ln_b, w_s, b_s):
    s = zb.shape[0]
    t = _tile(s, 256)
    return pl.pallas_call(
        functools.partial(_sg_kernel, nchunk=t // CHUNK),
        grid=(s // t,),
        in_specs=[pl.BlockSpec((t, 2 * B_WIDTH), lambda i: (i, 0)),
                  pl.BlockSpec((1, B_WIDTH), lambda i: (0, 0)),
                  pl.BlockSpec((1, B_WIDTH), lambda i: (0, 0)),
                  pl.BlockSpec((B_GROUPS, CHUNK, CHUNK), lambda i: (0, 0, 0)),
                  pl.BlockSpec((B_GROUPS, CHUNK, 1), lambda i: (0, 0, 0))],
        out_specs=pl.BlockSpec((t, B_WIDTH), lambda i: (i, 0)),
        out_shape=jax.ShapeDtypeStruct((s, B_WIDTH), BF16),
        compiler_params=_cparams(("parallel",)),
        name="spatial_gating",
    )(zb, ln_g.reshape(1, B_WIDTH), ln_b.reshape(1, B_WIDTH), w_s, b_s.reshape(B_GROUPS, CHUNK, 1))


def _select_kernel(qi_ref, kt_ref, w_ref, o_ref, keys_sc, sc_sc, wb_sc, mx_sc, *, tkc, topk):
    tqi = IDX_QBLOCK
    i = pl.program_id(0)
    nk = o_ref.shape[0]
    nch = (i * tqi + tqi + tkc - 1) // tkc
    qpos = i * tqi + lax.broadcasted_iota(I32, (tqi, tkc), 0)
    kloc = lax.broadcasted_iota(I32, (tqi, tkc), 1)
    qrow = i * tqi + lax.broadcasted_iota(I32, (tqi, LANES), 0)
    klane = lax.broadcasted_iota(I32, (tqi, LANES), 1)

    wgt = w_ref[...] * np.float32(IDX_DIM ** -0.5)
    for h in range(IDX_HEADS):
        wb_sc[h] = jnp.broadcast_to(wgt[:, h:h + 1], (tqi, LANES))
    mx_sc[...] = jnp.full(mx_sc.shape, -jnp.inf, F32)

    def stage_dot(c, slot):
        sc_sc[slot] = jnp.dot(qi_ref[0], kt_ref[jnp.minimum(c, nch - 1)], preferred_element_type=F32)

    def stage_reduce(c, slot):
        c = jnp.minimum(c, nch - 1)
        for u in range(tkc // LANES):
            ls = slice(u * LANES, (u + 1) * LANES)
            acc = jnp.zeros((tqi, LANES), F32)
            for h in range(IDX_HEADS):
                acc += jnp.maximum(sc_sc[slot, h * tqi:(h + 1) * tqi, ls], 0.0) * wb_sc[h]
            acc = acc + 0.0
            bits = pltpu.bitcast(acc, I32)
            ordered = jnp.where(bits < 0, bits ^ jnp.int32(0x7FFFFFFF), bits)
            causal = c * tkc + u * LANES + klane <= qrow
            keys_sc[c, :, ls] = jnp.where(causal, ordered, jnp.int32(INT_MIN))
            mx_sc[...] = jnp.maximum(mx_sc[...], jnp.where(causal, acc, -jnp.inf))

    stage_dot(0, 0)

    def score_quad(u, carry):
        c = 4 * u
        for k in range(0, 4, 2):
            stage_dot(c + k + 1, 1)
            stage_reduce(c + k, 0)
            stage_dot(c + k + 2, 0)
            stage_reduce(c + k + 1, 1)
        return carry

    lax.fori_loop(0, (nch + 3) // 4, score_quad, 0)

    def count_ge(cand):
        candb = jnp.broadcast_to(cand, (tqi, LANES))

        def count_body(c, cnt):
            kk = keys_sc[c]
            for u in range(tkc // LANES):
                cnt += jnp.where(kk[:, u * LANES:(u + 1) * LANES] >= candb, 1, 0)
            return cnt

        cnt = lax.fori_loop(0, nch, count_body, jnp.zeros((tqi, LANES), I32))
        return jnp.sum(cnt.astype(F32), axis=1, keepdims=True)

    want = np.float32(topk)
    fbits = pltpu.bitcast(jnp.max(mx_sc[...], axis=1, keepdims=True), I32)
    kmax = jnp.where(fbits < 0, fbits ^ jnp.int32(0x7FFFFFFF), fbits)
    few = i * tqi + lax.broadcasted_iota(I32, (tqi, 1), 0) + 1 <= topk
    lo0 = jnp.full((tqi, 1), INT_MIN, I32)
    hi0 = jnp.where(few, lo0 + 1, kmax + 1)

    def narrow(state, cand):
        lo, hi, active = state
        total = count_ge(cand)
        open_ = active > 0.0
        up = open_ & (total >= want)
        lo = jnp.where(up, cand, lo)
        hi = jnp.where(open_ & (~up), cand, hi)
        open_ = open_ & (~(up & (total == want))) & ((hi - lo) != 1)
        return lo, hi, jnp.where(open_, 1.0, 0.0)

    def midpoint(state):
        lo, hi, _ = state
        return lo + lax.shift_right_logical(hi - lo, jnp.int32(1))

    probe = jnp.maximum(kmax, jnp.int32(INT_MIN + SEARCH_PROBE_DROP + 1)) - jnp.int32(SEARCH_PROBE_DROP)
    state = narrow((lo0, hi0, jnp.where(few, 0.0, 1.0)), jnp.where(few, lo0, probe))

    def bisect_body(carry):
        state, _ = carry
        state = narrow(state, midpoint(state))
        state = narrow(state, midpoint(state))
        return state, jnp.sum(state[2])

    (thr, _, _), _ = lax.while_loop(lambda carry: carry[1] > 0.0, bisect_body, (state, jnp.float32(1.0)))
    thrb = jnp.broadcast_to(thr, (tqi, tkc))

    def mask_body(c, carry):
        sel = (keys_sc[c] >= thrb) & (c * tkc + kloc <= qpos)
        o_ref[c] = jnp.where(sel, 0.0, MASKED).astype(o_ref.dtype)
        return carry

    lax.fori_loop(0, nch, mask_body, 0)

    def fill_body(c, carry):
        o_ref[c] = jnp.full((tqi, tkc), MASKED, o_ref.dtype)
        return carry

    lax.fori_loop(nch, nk, fill_body, 0)


def _select_mask(qi_stack, kt, wi, tkc, topk):
    nq, rows, _ = qi_stack.shape
    nk = kt.shape[0]
    s = nq * IDX_QBLOCK
    return pl.pallas_call(
        functools.partial(_select_kernel, tkc=tkc, topk=topk),
        grid=(nq,),
        in_specs=[pl.BlockSpec((1, rows, IDX_DIM), lambda i: (i, 0, 0)),
                  pl.BlockSpec((nk, IDX_DIM, tkc), lambda i: (0, 0, 0)),
                  pl.BlockSpec((IDX_QBLOCK, IDX_HEADS), lambda i: (i, 0))],
        out_specs=pl.BlockSpec((nk, IDX_QBLOCK, tkc), lambda i: (0, i, 0)),
        out_shape=jax.ShapeDtypeStruct((nk, s, tkc), BF16),
        scratch_shapes=[pltpu.VMEM((nk, IDX_QBLOCK, tkc), I32),
                        pltpu.VMEM((2, rows, tkc), F32),
                        pltpu.VMEM((IDX_HEADS, IDX_QBLOCK, LANES), F32),
                        pltpu.VMEM((IDX_QBLOCK, LANES), F32)],
        compiler_params=_cparams(("parallel",)),
        name="indexer_select",
    )(qi_stack, kt, wi)


def _vec_pack(d, *rows):
    rows = [r.reshape(1, d).astype(F32) for r in rows]
    rows += [jnp.zeros((1, d), F32)] * (8 - len(rows))
    return jnp.concatenate(rows, axis=0)


def _pad_cols(w, n):
    return jnp.pad(w, ((0, 0), (0, n - w.shape[1])))


def kernel(x, c, norm_g, mod_w, mod_b, ffn_w1, ffn_w2, rel_table, ab_w_in, ab_w_out, diff_lam,
           diff_subln_g, sg_ln_g, sg_ln_b, sg_w, sg_b, dsa_w_in, dsa_w_out, final_g):
    batch, s, d = x.shape
    depth = norm_g.shape[0]
    assert batch == 1 and s % IDX_QBLOCK == 0

    tq_a = _tile(s, 512)
    tq_c = _tile(s, 256)
    topk = min(TOPK_MAX, s // 4)

    mod = _modulation(c, mod_w, mod_b).reshape(depth, 9, d)
    ffn_a, ffn_b, ffn_tiles = _ffn_weights(ffn_w1, ffn_w2)
    near_a = _near_bias(rel_table, tq_a, A_HEADS, 2)
    near_c = _near_bias(rel_table, tq_c, C_KV_HEADS, C_GROUP)

    xs = x.reshape(s, d)
    zeros_d = jnp.zeros((d,), F32)
    for li in range(depth):
        def vec(j, li=li):
            last = final_g if (li == depth - 1 and j == 2) else zeros_d
            return _vec_pack(d, norm_g[li, j], mod[li, 3 * j], mod[li, 3 * j + 1], mod[li, 3 * j + 2], last)

        def ffn(xs, j, k, final=False, li=li):
            return _ffn(xs, vec(j), ffn_a, ffn_b, ffn_tiles, 2 * li + k, final=final)

        xs = ffn(xs, 0, 0)

        v1 = vec(1)
        jj = li // 2
        if li % 2 == 0:
            w_in = ab_w_in[jj]
            w_qkv = jnp.concatenate([w_in[:, :A_WIDTH] * np.float32(A_QK_DIM ** -0.5 * LOG2E),
                                     w_in[:, A_WIDTH:3 * A_WIDTH]], axis=1).astype(BF16)
            w_zb = w_in[:, 3 * A_WIDTH:].astype(BF16)
            qkv = _proj(xs, v1, _col_blocks(w_qkv, PROJ_COLS), BF16)
            zb = _proj(xs, v1, _col_blocks(w_zb, PROJ_COLS), F32)
            lam_init = 0.8 - 0.6 * math.exp(-0.3 * li)
            lp = diff_lam[jj].astype(F32)
            lam = jnp.exp(jnp.sum(lp[0] * lp[1])) - jnp.exp(jnp.sum(lp[2] * lp[3])) + lam_init
            ya = _diff_attention(qkv, near_a, lam.reshape(1), diff_subln_g[jj].reshape(1, A_V_DIM),
                                 1.0 - lam_init, tq_a)
            yb = _spatial_gating(zb, sg_ln_g[jj], sg_ln_b[jj], sg_w[jj], sg_b[jj])
            w_out = ab_w_out[jj].astype(BF16)
            xs = _outproj(xs, v1, [ya, yb], [_col_blocks(w_out[:A_WIDTH], PROJ_COLS),
                                             _col_blocks(w_out[A_WIDTH:], PROJ_COLS)])
        else:
            w_in = dsa_w_in[jj]
            o_idx = C_WIDTH + 2 * C_KV_WIDTH
            o_ki = o_idx + IDX_HEADS * IDX_DIM
            w_main = jnp.concatenate([w_in[:, :C_WIDTH] * np.float32(C_HEAD_DIM ** -0.5 * LOG2E),
                                      w_in[:, C_WIDTH:o_ki]], axis=1).astype(BF16)
            w_kiw = _pad_cols(w_in[:, o_ki:], LANES).astype(BF16)
            main = _proj(xs, v1, _col_blocks(w_main, PROJ_COLS), BF16)
            kiw = _proj(xs, v1, _col_blocks(w_kiw, PROJ_COLS), F32)
            nq = s // IDX_QBLOCK
            qi = main[:, o_idx:o_ki].reshape(nq, IDX_QBLOCK, IDX_HEADS, IDX_DIM)
            qi = qi.transpose(0, 2, 1, 3).reshape(nq, IDX_HEADS * IDX_QBLOCK, IDX_DIM)
            kt = kiw[:, :IDX_DIM].astype(BF16).reshape(s // tq_c, tq_c, IDX_DIM).transpose(0, 2, 1)
            wi = kiw[:, IDX_DIM:IDX_DIM + IDX_HEADS]
            mask = _select_mask(qi, kt, wi, tq_c, topk)
            yc = _masked_attention(main, near_c, mask, tq_c)
            xs = _outproj(xs, v1, [yc], [_col_blocks(dsa_w_out[jj].astype(BF16), PROJ_COLS)])

        xs = ffn(xs, 2, 1, final=(li == depth - 1))
    return xs.reshape(batch, s, d)
```

```python
import functools
import math

import jax
import jax.numpy as jnp
import numpy as np
from jax import lax
from jax.experimental import pallas as pl
from jax.experimental.pallas import tpu as pltpu

F32 = jnp.float32
BF16 = jnp.bfloat16
I32 = jnp.int32

EPS = 1e-6
MASKED = -1e30
LANES = 128
INT_MIN = -(2 ** 31)
LOG2E = math.log2(math.e)
PREP_ROWS = 128
PREP_COLS = 256
NORM_ROWS = 16
ATTN_SUB_ROWS = 128

A_HEADS = 8
A_QK_DIM = 64
A_V_DIM = 128
A_WIDTH = A_HEADS * A_V_DIM
B_GROUPS = 8
B_GROUP_DIM = 128
B_WIDTH = B_GROUPS * B_GROUP_DIM
CHUNK = 128
C_HEADS = 16
C_KV_HEADS = 4
C_GROUP = C_HEADS // C_KV_HEADS
C_HEAD_DIM = 128
C_WIDTH = C_HEADS * C_HEAD_DIM
C_KV_WIDTH = C_KV_HEADS * C_HEAD_DIM
IDX_HEADS = 16
IDX_DIM = 64
TOPK_MAX = 256
REL_BUCKETS = 32
REL_MAX_DIST = 128
IDX_QBLOCK = 128
SEARCH_PROBE_DROP = 2 ** 24

VMEM_LIMIT = 56 * 1024 * 1024


def _cparams(sem):
    return pltpu.CompilerParams(dimension_semantics=sem, vmem_limit_bytes=VMEM_LIMIT)


FFN_ROWS, FFN_COLS = 512, 512
PROJ_ROWS, PROJ_COLS = 1024, 1024


def _col_blocks(w, tn):
    k, n = w.shape
    tn = _tile(n, tn)
    return w.reshape(k, n // tn, tn).transpose(1, 0, 2)


def _tile(n, want):
    if n <= want:
        return n
    t = want
    while n % t:
        t //= 2
    return t


def _mod_kernel(c_ref, w_ref, b_ref, o_ref):
    c = c_ref[...]
    cs = c * (1.0 / (1.0 + jnp.exp(-c)))
    o_ref[0] = jnp.sum(cs * w_ref[0], axis=0, keepdims=True) + b_ref[0]


def _modulation(c, mod_w, mod_b):
    depth, d, n = mod_w.shape
    tn = _tile(n, 1024)
    out = pl.pallas_call(
        _mod_kernel,
        grid=(depth, n // tn),
        in_specs=[pl.BlockSpec((d, 1), lambda l, j: (0, 0)),
                  pl.BlockSpec((1, d, tn), lambda l, j: (l, 0, j)),
                  pl.BlockSpec((1, 1, tn), lambda l, j: (l, 0, j))],
        out_specs=pl.BlockSpec((1, 1, tn), lambda l, j: (l, 0, j)),
        out_shape=jax.ShapeDtypeStruct((depth, 1, n), F32),
        compiler_params=_cparams(("arbitrary", "arbitrary")),
        name="adaln_mod",
    )(c.reshape(d, 1), mod_w, mod_b.reshape(depth, 1, n))
    return out.reshape(depth, n)


def _prenorm_into(x_ref, vec_ref, hn_ref, inv_ref):
    rows, d = x_ref.shape
    gain = vec_ref[0:1, :] * (1.0 + vec_ref[2:3, :])
    shift = vec_ref[1:2, :]

    def scale_body(r, carry):
        rs = pl.ds(pl.multiple_of(r * NORM_ROWS, NORM_ROWS), NORM_ROWS)
        x = x_ref[rs, :]
        ms = jnp.sum(x * x, axis=-1, keepdims=True) * np.float32(1.0 / d)
        inv_ref[rs, :] = jnp.broadcast_to(lax.rsqrt(ms + EPS), (NORM_ROWS, LANES))
        return carry

    lax.fori_loop(0, rows // NORM_ROWS, scale_body, 0, unroll=8)

    def apply_body(r, carry):
        rs = pl.ds(pl.multiple_of(r * NORM_ROWS, NORM_ROWS), NORM_ROWS)
        inv = inv_ref[rs, :]
        inv = jnp.concatenate([inv] * (d // LANES), axis=1)
        hn_ref[rs, :] = (x_ref[rs, :] * inv * gain + shift).astype(hn_ref.dtype)
        return carry

    lax.fori_loop(0, rows // NORM_ROWS, apply_body, 0, unroll=2)


def _ffn_kernel(x_ref, vec_ref, w1g_ref, w1u_ref, w2_ref, o_ref, hn_sc, inv_sc, *, nf, final):
    f = pl.program_id(1)

    @pl.when(f == 0)
    def _():
        _prenorm_into(x_ref, vec_ref, hn_sc, inv_sc)
        o_ref[...] = jnp.zeros_like(o_ref)

    hn = hn_sc[...]
    g = jnp.dot(hn, w1g_ref[...], preferred_element_type=F32)
    u = jnp.dot(hn, w1u_ref[...], preferred_element_type=F32)
    a = (g * (1.0 / (1.0 + jnp.exp(-g))) * u).astype(BF16)
    o_ref[...] += jnp.dot(a, w2_ref[...], preferred_element_type=F32)

    @pl.when(f == nf - 1)
    def _():
        y = x_ref[...] + 0.5 * (1.0 + vec_ref[3:4, :]) * o_ref[...]
        if final:
            ms = jnp.mean(y * y, axis=-1, keepdims=True)
            y = y * lax.rsqrt(ms + EPS) * vec_ref[4:5, :]
        o_ref[...] = y


def _ffn_weights(ffn_w1, ffn_w2):
    depth, two, d, _ = ffn_w1.shape
    d_ff = ffn_w2.shape[2]
    steps = depth * two
    tf = FFN_COLS if d_ff > FFN_COLS else d_ff
    nf = -(-d_ff // tf)
    pad = nf * tf - d_ff
    assert d_ff % LANES == 0
    tr = _tile(d, PREP_ROWS)

    def w1_kernel(w_ref, o_ref):
        o_ref[:, 0:d_ff] = w_ref[:, 0:d_ff].astype(BF16)
        o_ref[:, d_ff + pad:2 * d_ff + pad] = w_ref[:, d_ff:2 * d_ff].astype(BF16)
        if pad:
            o_ref[:, d_ff:d_ff + pad] = jnp.zeros((tr, pad), BF16)
            o_ref[:, 2 * d_ff + pad:] = jnp.zeros((tr, pad), BF16)

    w1 = pl.pallas_call(
        w1_kernel,
        grid=(steps, d // tr),
        in_specs=[pl.BlockSpec((None, tr, 2 * d_ff), lambda s, r: (s, r, 0))],
        out_specs=pl.BlockSpec((None, tr, 2 * nf * tf), lambda s, r: (s, r, 0)),
        out_shape=jax.ShapeDtypeStruct((steps, d, 2 * nf * tf), BF16),
        compiler_params=_cparams(("parallel", "parallel")),
        name="ffn_w1_cast_pad",
    )(ffn_w1.reshape(steps, d, 2 * d_ff))

    tc = _tile(d, PREP_COLS)

    def w2_kernel(w_ref, o_ref):
        o_ref[0:d_ff, :] = w_ref[...].astype(BF16)
        if pad:
            o_ref[d_ff:, :] = jnp.zeros((pad, tc), BF16)

    w2 = pl.pallas_call(
        w2_kernel,
        grid=(steps, d // tc),
        in_specs=[pl.BlockSpec((None, d_ff, tc), lambda s, c: (s, 0, c))],
        out_specs=pl.BlockSpec((None, d_ff + pad, tc), lambda s, c: (s, 0, c)),
        out_shape=jax.ShapeDtypeStruct((steps, d_ff + pad, d), BF16),
        compiler_params=_cparams(("parallel", "parallel")),
        name="ffn_w2_cast_pad",
    )(ffn_w2.reshape(steps, d_ff, d))
    return w1, w2, nf


def _ffn(x, vec, w1, w2, nf, step, *, final):
    s, d = x.shape
    tf = w2.shape[1] // nf
    tm = _tile(s, FFN_ROWS)
    return pl.pallas_call(
        functools.partial(_ffn_kernel, nf=nf, final=final),
        grid=(s // tm, nf),
        in_specs=[pl.BlockSpec((tm, d), lambda i, f: (i, 0)),
                  pl.BlockSpec((8, d), lambda i, f: (0, 0)),
                  pl.BlockSpec((None, d, tf), lambda i, f: (step, 0, f)),
                  pl.BlockSpec((None, d, tf), lambda i, f: (step, 0, nf + f)),
                  pl.BlockSpec((None, tf, d), lambda i, f: (step, f, 0))],
        out_specs=pl.BlockSpec((tm, d), lambda i, f: (i, 0)),
        out_shape=jax.ShapeDtypeStruct((s, d), F32),
        scratch_shapes=[pltpu.VMEM((tm, d), BF16), pltpu.VMEM((tm, LANES), F32)],
        compiler_params=_cparams(("parallel", "arbitrary")),
        name="swiglu_halfstep",
    )(x, vec, w1, w1, w2)


def _proj2_kernel(x_ref, vec_ref, wa_ref, wb_ref, oa_ref, ob_ref, hn_sc, inv_sc, *, na):
    j = pl.program_id(1)

    @pl.when(j == 0)
    def _():
        _prenorm_into(x_ref, vec_ref, hn_sc, inv_sc)

    @pl.when(j < na)
    def _():
        oa_ref[...] = jnp.dot(hn_sc[...], wa_ref[0], preferred_element_type=F32).astype(oa_ref.dtype)

    @pl.when(j >= na)
    def _():
        ob_ref[...] = jnp.dot(hn_sc[...], wb_ref[0], preferred_element_type=F32).astype(ob_ref.dtype)


def _proj2(x, vec, wa, dtype_a, wb, dtype_b):
    s, d = x.shape
    na, _, ta = wa.shape
    nb, _, tb = wb.shape
    tm = _tile(s, PROJ_ROWS)
    return pl.pallas_call(
        functools.partial(_proj2_kernel, na=na),
        grid=(s // tm, na + nb),
        in_specs=[pl.BlockSpec((tm, d), lambda i, j: (i, 0)),
                  pl.BlockSpec((8, d), lambda i, j: (0, 0)),
                  pl.BlockSpec((1, d, ta), lambda i, j: (jnp.minimum(j, na - 1), 0, 0)),
                  pl.BlockSpec((1, d, tb), lambda i, j: (jnp.maximum(j - na, 0), 0, 0))],
        out_specs=[pl.BlockSpec((tm, ta), lambda i, j: (i, jnp.minimum(j, na - 1))),
                   pl.BlockSpec((tm, tb), lambda i, j: (i, jnp.maximum(j - na, 0)))],
        out_shape=[jax.ShapeDtypeStruct((s, na * ta), dtype_a),
                   jax.ShapeDtypeStruct((s, nb * tb), dtype_b)],
        scratch_shapes=[pltpu.VMEM((tm, d), BF16), pltpu.VMEM((tm, LANES), F32)],
        compiler_params=_cparams(("parallel", "arbitrary")),
        name="norm_mod_proj2",
    )(x, vec, wa, wb)


def _outproj_kernel(*refs, n_in):
    x_ref, vec_ref = refs[0], refs[1]
    lhs = refs[2:2 + n_in]
    ws = refs[2 + n_in:2 + 2 * n_in]
    o_ref = refs[2 + 2 * n_in]
    acc = jnp.dot(lhs[0][...], ws[0][0], preferred_element_type=F32)
    for a, w in zip(lhs[1:], ws[1:]):
        acc += jnp.dot(a[...], w[0], preferred_element_type=F32)
    o_ref[...] = x_ref[...] + (1.0 + vec_ref[3:4, :]) * acc


def _outproj(x, vec, lhs, ws):
    s, d = x.shape
    tm = _tile(s, PROJ_ROWS)
    tn = ws[0].shape[2]
    n_in = len(lhs)
    in_specs = [pl.BlockSpec((tm, tn), lambda i, j: (i, j)),
                pl.BlockSpec((8, tn), lambda i, j: (0, j))]
    in_specs += [pl.BlockSpec((tm, a.shape[1]), lambda i, j: (i, 0)) for a in lhs]
    in_specs += [pl.BlockSpec((1, w.shape[1], tn), lambda i, j: (j, 0, 0)) for w in ws]
    return pl.pallas_call(
        functools.partial(_outproj_kernel, n_in=n_in),
        grid=(s // tm, d // tn),
        in_specs=in_specs,
        out_specs=pl.BlockSpec((tm, tn), lambda i, j: (i, j)),
        out_shape=jax.ShapeDtypeStruct((s, d), F32),
        compiler_params=_cparams(("parallel", "arbitrary")),
        name="outproj_residual",
    )(x, vec, *lhs, *ws)


def _flash_kernel(*refs, nh, tq, kb, sub, diff, lam_scale):
    if diff:
        lam_ref, q_ref, k_ref, v_ref, nb_ref, g_ref, o_ref, qs_sc, m_sc, acc_sc, s_sc, p_sc, al_sc = refs
        mask_ref = None
    else:
        q_ref, k_ref, v_ref, nb_ref, mask_ref, o_ref, qs_sc, m_sc, acc_sc, s_sc, p_sc, al_sc = refs
    tk = tq
    rows = nh * tq
    hd = k_ref.shape[1]
    assert hd == LANES and rows % sub == 0 and tq % sub == 0
    i = pl.program_id(1)

    if diff:
        q = q_ref[...]
        lane = lax.broadcasted_iota(I32, q.shape, 1)
        zero = jnp.zeros_like(q)
        qs_sc[0:tq, :] = jnp.where(lane < A_QK_DIM, q, zero)
        qs_sc[tq:2 * tq, :] = jnp.where(lane >= A_QK_DIM, q, zero)
    else:
        for r in range(nh):
            qs_sc[r * tq:(r + 1) * tq, :] = q_ref[:, r * hd:(r + 1) * hd]
    m_sc[...] = jnp.full(m_sc.shape, -jnp.inf, F32)
    acc_sc[...] = jnp.zeros(acc_sc.shape, F32)

    tks = kb * tk
    nk = k_ref.shape[0] // tk
    last_step = i // kb

    def bias_index(j):
        return jnp.where(j > i, 3, jnp.clip(j - (i - 2), 0, 2))

    def stage_qk(t, slot):
        start = pl.multiple_of(jnp.minimum(t, last_step) * tks, tks)
        kblk = k_ref[pl.ds(start, tks), :]
        s_sc[slot] = lax.dot_general(qs_sc[...], kblk, (((1,), (1,)), ((), ())), preferred_element_type=F32)

    def stage_softmax(t, slot, biased):
        tc = jnp.minimum(t, last_step)
        for r in range(rows // sub):
            rs = slice(r * sub, (r + 1) * sub)
            tiles = []
            for b in range(kb):
                j = tc * kb + b
                s = s_sc[slot, rs, b * tk:(b + 1) * tk]
                if biased:
                    col = jnp.where(t > last_step, 3, bias_index(j))
                    s = s + nb_ref[0, col, rs, :]
                s = s.astype(BF16)
                if mask_ref is not None:
                    off = (r * sub) % tq
                    s = s + mask_ref[jnp.minimum(j, nk - 1), off:off + sub, :]
                tiles += [s[:, u * LANES:(u + 1) * LANES] for u in range(tk // LANES)]
            cmax = tiles[0]
            for u in tiles[1:]:
                cmax = jnp.maximum(cmax, u)
            m_old = m_sc[rs, :]
            m_new = jnp.maximum(m_old, jnp.max(cmax.astype(F32), axis=1, keepdims=True))
            al_sc[slot, rs, :] = jnp.exp2(m_old - m_new)
            m_b = m_new.astype(BF16)
            p_sc[slot, rs, :] = jnp.concatenate([jnp.exp2(u - m_b) for u in tiles], axis=1)
            m_sc[rs, :] = m_new

    def stage_pv(t, slot):
        start = pl.multiple_of(jnp.minimum(t, last_step) * tks, tks)
        vbe = jnp.concatenate([v_ref[pl.ds(start, tks), :], jnp.ones((tks, hd), BF16)], axis=1)
        pv = jnp.dot(p_sc[slot], vbe, preferred_element_type=F32)
        alpha = al_sc[slot]
        acc_sc[...] = jnp.concatenate([alpha, alpha], axis=1) * acc_sc[...] + pv

    stage_qk(0, 0)
    stage_qk(1, 1)
    stage_softmax(0, 0, True)

    nfar = jnp.maximum((i - 1) // kb, 0)
    npairs = jnp.maximum((nfar - 1) // 2, 0)

    def far_pair(u, carry):
        t = 2 * u
        stage_pv(t, 0)
        stage_softmax(t + 1, 1, False)
        stage_qk(t + 2, 0)
        stage_pv(t + 1, 1)
        stage_softmax(t + 2, 0, False)
        stage_qk(t + 3, 1)
        return carry

    lax.fori_loop(0, npairs, far_pair, 0)

    t0 = 2 * npairs
    stage_pv(t0, 0)
    stage_softmax(t0 + 1, 1, True)
    stage_qk(t0 + 2, 0)
    stage_pv(t0 + 1, 1)
    stage_softmax(t0 + 2, 0, True)
    stage_qk(t0 + 3, 1)
    stage_pv(t0 + 2, 0)

    @pl.when(t0 + 3 <= last_step)
    def _():
        stage_softmax(t0 + 3, 1, True)
        stage_pv(t0 + 3, 1)

    if diff:
        o0 = acc_sc[0:tq, 0:hd] / acc_sc[0:tq, hd:2 * hd]
        o1 = acc_sc[tq:2 * tq, 0:hd] / acc_sc[tq:2 * tq, hd:2 * hd]
        dlt = o0 - lam_ref[0] * o1
        ms = jnp.mean(dlt * dlt, axis=-1, keepdims=True)
        o_ref[...] = ((dlt * lax.rsqrt(ms + EPS) * g_ref[...]) * lam_scale).astype(o_ref.dtype)
    else:
        for r in range(nh):
            rs = slice(r * tq, (r + 1) * tq)
            o_ref[:, r * hd:(r + 1) * hd] = (acc_sc[rs, 0:hd] / acc_sc[rs, hd:2 * hd]).astype(o_ref.dtype)


def _rel_bucket(dist):
    n = jnp.maximum(dist, 0)
    max_exact = REL_BUCKETS // 2
    nf = jnp.maximum(n, 1).astype(F32)
    large = max_exact + (jnp.log(nf / max_exact) / math.log(REL_MAX_DIST / max_exact)
                         * (REL_BUCKETS - max_exact)).astype(I32)
    large = jnp.minimum(large, REL_BUCKETS - 1)
    return jnp.where(n < max_exact, n, large)


def _near_bias(rel_table, tq, groups, nh):
    assert tq >= LANES, "keys older than one block must all fall in the last bucket"
    r = jnp.arange(tq, dtype=I32)[:, None]
    c = jnp.arange(2 * tq, dtype=I32)[None, :]
    dist = r + tq - c
    rel = (rel_table - rel_table[REL_BUCKETS - 1][None, :]) * np.float32(LOG2E)
    onehot = jax.nn.one_hot(_rel_bucket(dist), REL_BUCKETS, dtype=F32)
    b = jnp.einsum("rcb,bh->hrc", onehot, rel, precision=lax.Precision.HIGHEST)
    b = jnp.where((dist >= 0)[None], b, MASKED)
    heads = b.shape[0]
    tiles = jnp.stack([jnp.zeros((heads, tq, tq), F32), b[:, :, :tq], b[:, :, tq:],
                       jnp.full((heads, tq, tq), MASKED, F32)], axis=1)
    tiles = tiles.reshape(groups, nh, 4, tq, tq).transpose(0, 2, 1, 3, 4)
    return tiles.reshape(groups, 4, nh * tq, tq)


def _diff_attention(qkv, near, lam, subln_g, lam_scale, tq):
    s = qkv.shape[0]
    hd = A_V_DIM
    nh = 2
    kb = 1
    kcol = A_WIDTH // hd
    return pl.pallas_call(
        functools.partial(_flash_kernel, nh=nh, tq=tq, kb=kb, sub=min(tq, ATTN_SUB_ROWS), diff=True,
                          lam_scale=lam_scale),
        grid=(A_HEADS, s // tq),
        in_specs=[pl.BlockSpec(memory_space=pltpu.SMEM),
                  pl.BlockSpec((tq, hd), lambda h, i: (i, h)),
                  pl.BlockSpec((s, hd), lambda h, i: (0, kcol + h)),
                  pl.BlockSpec((s, hd), lambda h, i: (0, 2 * kcol + h)),
                  pl.BlockSpec((1, 4, nh * tq, tq), lambda h, i: (h, 0, 0, 0)),
                  pl.BlockSpec((1, hd), lambda h, i: (0, 0))],
        out_specs=pl.BlockSpec((tq, hd), lambda h, i: (i, h)),
        out_shape=jax.ShapeDtypeStruct((s, A_WIDTH), BF16),
        scratch_shapes=[pltpu.VMEM((nh * tq, hd), BF16),
                        pltpu.VMEM((nh * tq, LANES), F32),
                        pltpu.VMEM((nh * tq, 2 * hd), F32),
                        pltpu.VMEM((2, nh * tq, kb * tq), F32),
                        pltpu.VMEM((2, nh * tq, kb * tq), BF16),
                        pltpu.VMEM((2, nh * tq, LANES), F32)],
        compiler_params=_cparams(("parallel", "arbitrary")),
        name="diff_attention",
    )(lam, qkv, qkv, qkv, near, subln_g)


def _masked_attention(qkv, near, mask, tq):
    s = qkv.shape[0]
    hd = C_HEAD_DIM
    nh = C_GROUP
    kcol = C_WIDTH // hd
    vcol = kcol + C_KV_HEADS
    nk = s // tq
    kb = 2 if nk % 2 == 0 else 1
    return pl.pallas_call(
        functools.partial(_flash_kernel, nh=nh, tq=tq, kb=kb, sub=min(tq, ATTN_SUB_ROWS), diff=False,
                          lam_scale=1.0),
        grid=(C_KV_HEADS, s // tq),
        in_specs=[pl.BlockSpec((tq, nh * hd), lambda g, i: (i, g)),
                  pl.BlockSpec((s, hd), lambda g, i: (0, kcol + g)),
                  pl.BlockSpec((s, hd), lambda g, i: (0, vcol + g)),
                  pl.BlockSpec((1, 4, nh * tq, tq), lambda g, i: (g, 0, 0, 0)),
                  pl.BlockSpec((nk, tq, tq), lambda g, i: (0, i, 0))],
        out_specs=pl.BlockSpec((tq, nh * hd), lambda g, i: (i, g)),
        out_shape=jax.ShapeDtypeStruct((s, C_WIDTH), BF16),
        scratch_shapes=[pltpu.VMEM((nh * tq, hd), BF16),
                        pltpu.VMEM((nh * tq, LANES), F32),
                        pltpu.VMEM((nh * tq, 2 * hd), F32),
                        pltpu.VMEM((2, nh * tq, kb * tq), F32),
                        pltpu.VMEM((2, nh * tq, kb * tq), BF16),
                        pltpu.VMEM((2, nh * tq, LANES), F32)],
        compiler_params=_cparams(("parallel", "arbitrary")),
        name="selected_attention",
    )(qkv, qkv, qkv, near, mask)


def _sg_kernel(zb_ref, lng_ref, lnb_ref, w_ref, bs_ref, o_ref, *, nchunk):
    zb = zb_ref[...]
    gl = zb * (0.5 * (1.0 + jnp.tanh(np.float32(np.sqrt(2.0 / np.pi)) * (zb + 0.044715 * (zb * zb * zb)))))
    u = gl[:, :B_WIDTH]
    z = gl[:, B_WIDTH:]
    mu = jnp.mean(z, axis=-1, keepdims=True)
    zc = z - mu
    var = jnp.mean(zc * zc, axis=-1, keepdims=True)
    zn = (zc * lax.rsqrt(var + EPS) * lng_ref[...] + lnb_ref[...]).astype(BF16)
    row = lax.broadcasted_iota(I32, (CHUNK, CHUNK), 0)
    col = lax.broadcasted_iota(I32, (CHUNK, CHUNK), 1)
    for g in range(B_GROUPS):
        w = jnp.where(row >= col, w_ref[g], 0.0).astype(BF16)
        bias = bs_ref[g]
        lo = g * B_GROUP_DIM
        for c in range(nchunk):
            r0 = c * CHUNK
            sz = jnp.dot(w, zn[r0:r0 + CHUNK, lo:lo + B_GROUP_DIM], preferred_element_type=F32) + bias
            o_ref[r0:r0 + CHUNK, lo:lo + B_GROUP_DIM] = (u[r0:r0 + CHUNK, lo:lo + B_GROUP_DIM] * sz).astype(o_ref.dtype)


def _spatial_gating(zb, ln_g, ln_b, w_s, b_s):
    s = zb.shape[0]
    t = _tile(s, 256)
    return pl.pallas_call(
        functools.partial(_sg_kernel, nchunk=t // CHUNK),
        grid=(s // t,),
        in_specs=[pl.BlockSpec((t, 2 * B_WIDTH), lambda i: (i, 0)),
                  pl.BlockSpec((1, B_WIDTH), lambda i: (0, 0)),
                  pl.BlockSpec((1, B_WIDTH), lambda i: (0, 0)),
                  pl.BlockSpec((B_GROUPS, CHUNK, CHUNK), lambda i: (0, 0, 0)),
                  pl.BlockSpec((B_GROUPS, CHUNK, 1), lambda i: (0, 0, 0))],
        out_specs=pl.BlockSpec((t, B_WIDTH), lambda i: (i, 0)),
        out_shape=jax.ShapeDtypeStruct((s, B_WIDTH), BF16),
        compiler_params=_cparams(("parallel",)),
        name="spatial_gating",
    )(zb, ln_g.reshape(1, B_WIDTH), ln_b.reshape(1, B_WIDTH), w_s, b_s.reshape(B_GROUPS, CHUNK, 1))


def _select_kernel(qi_ref, kt_ref, w_ref, o_ref, keys_sc, sc_sc, wb_sc, mx_sc, *, tkc, topk):
    tqi = IDX_QBLOCK
    i = pl.program_id(0)
    nk = o_ref.shape[0]
    nch = (i * tqi + tqi + tkc - 1) // tkc
    qpos = i * tqi + lax.broadcasted_iota(I32, (tqi, tkc), 0)
    kloc = lax.broadcasted_iota(I32, (tqi, tkc), 1)
    qrow = i * tqi + lax.broadcasted_iota(I32, (tqi, LANES), 0)
    klane = lax.broadcasted_iota(I32, (tqi, LANES), 1)

    wgt = w_ref[...] * np.float32(IDX_DIM ** -0.5)
    for h in range(IDX_HEADS):
        wb_sc[h] = jnp.broadcast_to(wgt[:, h:h + 1], (tqi, LANES))
    mx_sc[...] = jnp.full(mx_sc.shape, -jnp.inf, F32)

    def stage_dot(c, slot):
        sc_sc[slot] = jnp.dot(qi_ref[0], kt_ref[jnp.minimum(c, nch - 1)], preferred_element_type=F32)

    def stage_reduce(c, slot):
        c = jnp.minimum(c, nch - 1)
        for u in range(tkc // LANES):
            ls = slice(u * LANES, (u + 1) * LANES)
            acc = jnp.zeros((tqi, LANES), F32)
            for h in range(IDX_HEADS):
                acc += jnp.maximum(sc_sc[slot, h * tqi:(h + 1) * tqi, ls], 0.0) * wb_sc[h]
            acc = acc + 0.0
            bits = pltpu.bitcast(acc, I32)
            ordered = jnp.where(bits < 0, bits ^ jnp.int32(0x7FFFFFFF), bits)
            causal = c * tkc + u * LANES + klane <= qrow
            keys_sc[c, :, ls] = jnp.where(causal, ordered, jnp.int32(INT_MIN))
            mx_sc[...] = jnp.maximum(mx_sc[...], jnp.where(causal, acc, -jnp.inf))

    stage_dot(0, 0)

    def score_quad(u, carry):
        c = 4 * u
        for k in range(0, 4, 2):
            stage_dot(c + k + 1, 1)
            stage_reduce(c + k, 0)
            stage_dot(c + k + 2, 0)
            stage_reduce(c + k + 1, 1)
        return carry

    lax.fori_loop(0, (nch + 3) // 4, score_quad, 0)

    def count_ge(cand):
        candb = jnp.broadcast_to(cand, (tqi, LANES))

        def count_body(c, cnt):
            kk = keys_sc[c]
            for u in range(tkc // LANES):
                cnt += jnp.where(kk[:, u * LANES:(u + 1) * LANES] >= candb, 1, 0)
            return cnt

        cnt = lax.fori_loop(0, nch, count_body, jnp.zeros((tqi, LANES), I32))
        return jnp.sum(cnt.astype(F32), axis=1, keepdims=True)

    want = np.float32(topk)
    fbits = pltpu.bitcast(jnp.max(mx_sc[...], axis=1, keepdims=True), I32)
    kmax = jnp.where(fbits < 0, fbits ^ jnp.int32(0x7FFFFFFF), fbits)
    few = i * tqi + lax.broadcasted_iota(I32, (tqi, 1), 0) + 1 <= topk
    lo0 = jnp.full((tqi, 1), INT_MIN, I32)
    hi0 = jnp.where(few, lo0 + 1, kmax + 1)

    def narrow(state, cand):
        lo, hi, active = state
        total = count_ge(cand)
        open_ = active > 0.0
        up = open_ & (total >= want)
        lo = jnp.where(up, cand, lo)
        hi = jnp.where(open_ & (~up), cand, hi)
        open_ = open_ & (~(up & (total == want))) & ((hi - lo) != 1)
        return lo, hi, jnp.where(open_, 1.0, 0.0)

    def midpoint(state):
        lo, hi, _ = state
        return lo + lax.shift_right_logical(hi - lo, jnp.int32(1))

    probe = jnp.maximum(kmax, jnp.int32(INT_MIN + SEARCH_PROBE_DROP + 1)) - jnp.int32(SEARCH_PROBE_DROP)
    state = narrow((lo0, hi0, jnp.where(few, 0.0, 1.0)), jnp.where(few, lo0, probe))

    def bisect_body(carry):
        state, _ = carry
        state = narrow(state, midpoint(state))
        state = narrow(state, midpoint(state))
        return state, jnp.sum(state[2])

    (thr, _, _), _ = lax.while_loop(lambda carry: carry[1] > 0.0, bisect_body, (state, jnp.float32(1.0)))
    thrb = jnp.broadcast_to(thr, (tqi, tkc))

    def mask_body(c, carry):
        sel = (keys_sc[c] >= thrb) & (c * tkc + kloc <= qpos)
        o_ref[c] = jnp.where(sel, 0.0, MASKED).astype(o_ref.dtype)
        return carry

    lax.fori_loop(0, nch, mask_body, 0)

    def fill_body(c, carry):
        o_ref[c] = jnp.full((tqi, tkc), MASKED, o_ref.dtype)
        return carry

    lax.fori_loop(nch, nk, fill_body, 0)


def _select_mask(qi_stack, kt, wi, tkc, topk):
    nq, rows, _ = qi_stack.shape
    nk = kt.shape[0]
    s = nq * IDX_QBLOCK
    return pl.pallas_call(
        functools.partial(_select_kernel, tkc=tkc, topk=topk),
        grid=(nq,),
        in_specs=[pl.BlockSpec((1, rows, IDX_DIM), lambda i: (i, 0, 0)),
                  pl.BlockSpec((nk, IDX_DIM, tkc), lambda i: (0, 0, 0)),
                  pl.BlockSpec((IDX_QBLOCK, IDX_HEADS), lambda i: (i, 0))],
        out_specs=pl.BlockSpec((nk, IDX_QBLOCK, tkc), lambda i: (0, i, 0)),
        out_shape=jax.ShapeDtypeStruct((nk, s, tkc), BF16),
        scratch_shapes=[pltpu.VMEM((nk, IDX_QBLOCK, tkc), I32),
                        pltpu.VMEM((2, rows, tkc), F32),
                        pltpu.VMEM((IDX_HEADS, IDX_QBLOCK, LANES), F32),
                        pltpu.VMEM((IDX_QBLOCK, LANES), F32)],
        compiler_params=_cparams(("parallel",)),
        name="indexer_select",
    )(qi_stack, kt, wi)


def _vec_pack(d, *rows):
    rows = [r.reshape(1, d).astype(F32) for r in rows]
    rows += [jnp.zeros((1, d), F32)] * (8 - len(rows))
    return jnp.concatenate(rows, axis=0)


def _pad_cols(w, n):
    return jnp.pad(w, ((0, 0), (0, n - w.shape[1])))


def kernel(x, c, norm_g, mod_w, mod_b, ffn_w1, ffn_w2, rel_table, ab_w_in, ab_w_out, diff_lam,
           diff_subln_g, sg_ln_g, sg_ln_b, sg_w, sg_b, dsa_w_in, dsa_w_out, final_g):
    batch, s, d = x.shape
    depth = norm_g.shape[0]
    assert batch == 1 and s % IDX_QBLOCK == 0

    tq_a = _tile(s, 512)
    tq_c = _tile(s, 256)
    topk = min(TOPK_MAX, s // 4)

    mod = _modulation(c, mod_w, mod_b).reshape(depth, 9, d)
    ffn_a, ffn_b, ffn_tiles = _ffn_weights(ffn_w1, ffn_w2)
    near_a = _near_bias(rel_table, tq_a, A_HEADS, 2)
    near_c = _near_bias(rel_table, tq_c, C_KV_HEADS, C_GROUP)

    xs = x.reshape(s, d)
    zeros_d = jnp.zeros((d,), F32)
    for li in range(depth):
        def vec(j, li=li):
            last = final_g if (li == depth - 1 and j == 2) else zeros_d
            return _vec_pack(d, norm_g[li, j], mod[li, 3 * j], mod[li, 3 * j + 1], mod[li, 3 * j + 2], last)

        def ffn(xs, j, k, final=False, li=li):
            return _ffn(xs, vec(j), ffn_a, ffn_b, ffn_tiles, 2 * li + k, final=final)

        xs = ffn(xs, 0, 0)

        v1 = vec(1)
        jj = li // 2
        if li % 2 == 0:
            w_in = ab_w_in[jj]
            w_qkv = jnp.concatenate([w_in[:, :A_WIDTH] * np.float32(A_QK_DIM ** -0.5 * LOG2E),
                                     w_in[:, A_WIDTH:3 * A_WIDTH]], axis=1).astype(BF16)
            w_zb = w_in[:, 3 * A_WIDTH:].astype(BF16)
            qkv, zb = _proj2(xs, v1, _col_blocks(w_qkv, PROJ_COLS), BF16, _col_blocks(w_zb, PROJ_COLS), F32)
            lam_init = 0.8 - 0.6 * math.exp(-0.3 * li)
            lp = diff_lam[jj].astype(F32)
            lam = jnp.exp(jnp.sum(lp[0] * lp[1])) - jnp.exp(jnp.sum(lp[2] * lp[3])) + lam_init
            ya = _diff_attention(qkv, near_a, lam.reshape(1), diff_subln_g[jj].reshape(1, A_V_DIM),
                                 1.0 - lam_init, tq_a)
            yb = _spatial_gating(zb, sg_ln_g[jj], sg_ln_b[jj], sg_w[jj], sg_b[jj])
            w_out = ab_w_out[jj].astype(BF16)
            xs = _outproj(xs, v1, [ya, yb], [_col_blocks(w_out[:A_WIDTH], PROJ_COLS),
                                             _col_blocks(w_out[A_WIDTH:], PROJ_COLS)])
        else:
            w_in = dsa_w_in[jj]
            o_idx = C_WIDTH + 2 * C_KV_WIDTH
            o_ki = o_idx + IDX_HEADS * IDX_DIM
            w_main = jnp.concatenate([w_in[:, :C_WIDTH] * np.float32(C_HEAD_DIM ** -0.5 * LOG2E),
                                      w_in[:, C_WIDTH:o_ki]], axis=1).astype(BF16)
            w_kiw = _pad_cols(w_in[:, o_ki:], LANES).astype(BF16)
            main, kiw = _proj2(xs, v1, _col_blocks(w_main, PROJ_COLS), BF16, _col_blocks(w_kiw, PROJ_COLS), F32)
            nq = s // IDX_QBLOCK
            qi = main[:, o_idx:o_ki].reshape(nq, IDX_QBLOCK, IDX_HEADS, IDX_DIM)
            qi = qi.transpose(0, 2, 1, 3).reshape(nq, IDX_HEADS * IDX_QBLOCK, IDX_DIM)
            kt = kiw[:, :IDX_DIM].astype(BF16).reshape(s // tq_c, tq_c, IDX_DIM).transpose(0, 2, 1)
            wi = kiw[:, IDX_DIM:IDX_DIM + IDX_HEADS]
            mask = _select_mask(qi, kt, wi, tq_c, topk)
            yc = _masked_attention(main, near_c, mask, tq_c)
            xs = _outproj(xs, v1, [yc], [_col_blocks(dsa_w_out[jj].astype(BF16), PROJ_COLS)])

        xs = ffn(xs, 2, 1, final=(li == depth - 1))
    return xs.reshape(batch, s, d)
```

```python
import functools
import math

import jax
import jax.numpy as jnp
import numpy as np
from jax import lax
from jax.experimental import pallas as pl
from jax.experimental.pallas import tpu as pltpu

F32 = jnp.float32
BF16 = jnp.bfloat16
I32 = jnp.int32

EPS = 1e-6
MASKED = -1e30
LANES = 128
INT_MIN = -(2 ** 31)
LOG2E = math.log2(math.e)
PREP_ROWS = 128
PREP_COLS = 256
NORM_ROWS = 16
ATTN_SUB_ROWS = 128

A_HEADS = 8
A_QK_DIM = 64
A_V_DIM = 128
A_WIDTH = A_HEADS * A_V_DIM
B_GROUPS = 8
B_GROUP_DIM = 128
B_WIDTH = B_GROUPS * B_GROUP_DIM
CHUNK = 128
C_HEADS = 16
C_KV_HEADS = 4
C_GROUP = C_HEADS // C_KV_HEADS
C_HEAD_DIM = 128
C_WIDTH = C_HEADS * C_HEAD_DIM
C_KV_WIDTH = C_KV_HEADS * C_HEAD_DIM
IDX_HEADS = 16
IDX_DIM = 64
TOPK_MAX = 256
REL_BUCKETS = 32
REL_MAX_DIST = 128
IDX_QBLOCK = 128
SEARCH_PROBE_DROP = 2 ** 24

VMEM_LIMIT = 56 * 1024 * 1024


def _cparams(sem):
    return pltpu.CompilerParams(dimension_semantics=sem, vmem_limit_bytes=VMEM_LIMIT)


FFN_ROWS, FFN_COLS = 512, 512
PROJ_ROWS, PROJ_COLS = 1024, 1024


def _col_blocks(w, tn):
    k, n = w.shape
    tn = _tile(n, tn)
    return w.reshape(k, n // tn, tn).transpose(1, 0, 2)


def _tile(n, want):
    if n <= want:
        return n
    t = want
    while n % t:
        t //= 2
    return t


def _mod_kernel(c_ref, w_ref, b_ref, o_ref):
    c = c_ref[...]
    cs = c * (1.0 / (1.0 + jnp.exp(-c)))
    o_ref[0] = jnp.sum(cs * w_ref[0], axis=0, keepdims=True) + b_ref[0]


def _modulation(c, mod_w, mod_b):
    depth, d, n = mod_w.shape
    tn = _tile(n, 2048)
    out = pl.pallas_call(
        _mod_kernel,
        grid=(depth, n // tn),
        in_specs=[pl.BlockSpec((d, 1), lambda l, j: (0, 0)),
                  pl.BlockSpec((1, d, tn), lambda l, j: (l, 0, j)),
                  pl.BlockSpec((1, 1, tn), lambda l, j: (l, 0, j))],
        out_specs=pl.BlockSpec((1, 1, tn), lambda l, j: (l, 0, j)),
        out_shape=jax.ShapeDtypeStruct((depth, 1, n), F32),
        compiler_params=_cparams(("arbitrary", "arbitrary")),
        name="adaln_mod",
    )(c.reshape(d, 1), mod_w, mod_b.reshape(depth, 1, n))
    return out.reshape(depth, n)


def _prenorm_into(x_ref, vec_ref, hn_ref, inv_ref):
    rows, d = x_ref.shape
    gain = vec_ref[0:1, :] * (1.0 + vec_ref[2:3, :])
    shift = vec_ref[1:2, :]

    def scale_body(r, carry):
        rs = pl.ds(pl.multiple_of(r * NORM_ROWS, NORM_ROWS), NORM_ROWS)
        x = x_ref[rs, :]
        ms = jnp.sum(x * x, axis=-1, keepdims=True) * np.float32(1.0 / d)
        inv_ref[rs, :] = jnp.broadcast_to(lax.rsqrt(ms + EPS), (NORM_ROWS, LANES))
        return carry

    lax.fori_loop(0, rows // NORM_ROWS, scale_body, 0, unroll=8)

    def apply_body(r, carry):
        rs = pl.ds(pl.multiple_of(r * NORM_ROWS, NORM_ROWS), NORM_ROWS)
        inv = inv_ref[rs, :]
        inv = jnp.concatenate([inv] * (d // LANES), axis=1)
        hn_ref[rs, :] = (x_ref[rs, :] * inv * gain + shift).astype(hn_ref.dtype)
        return carry

    lax.fori_loop(0, rows // NORM_ROWS, apply_body, 0, unroll=2)


def _ffn_kernel(x_ref, vec_ref, w1g_ref, w1u_ref, w2_ref, o_ref, hn_sc, inv_sc, *, nf, final):
    f = pl.program_id(1)

    @pl.when(f == 0)
    def _():
        _prenorm_into(x_ref, vec_ref, hn_sc, inv_sc)
        o_ref[...] = jnp.zeros_like(o_ref)

    hn = hn_sc[...]
    g = jnp.dot(hn, w1g_ref[...], preferred_element_type=F32)
    u = jnp.dot(hn, w1u_ref[...], preferred_element_type=F32)
    a = (g * (1.0 / (1.0 + jnp.exp(-g))) * u).astype(BF16)
    o_ref[...] += jnp.dot(a, w2_ref[...], preferred_element_type=F32)

    @pl.when(f == nf - 1)
    def _():
        y = x_ref[...] + 0.5 * (1.0 + vec_ref[3:4, :]) * o_ref[...]
        if final:
            ms = jnp.mean(y * y, axis=-1, keepdims=True)
            y = y * lax.rsqrt(ms + EPS) * vec_ref[4:5, :]
        o_ref[...] = y


def _ffn_weights(ffn_w1, ffn_w2):
    depth, two, d, _ = ffn_w1.shape
    d_ff = ffn_w2.shape[2]
    steps = depth * two
    tf = FFN_COLS if d_ff > FFN_COLS else d_ff
    nf = -(-d_ff // tf)
    pad = nf * tf - d_ff
    assert d_ff % LANES == 0
    tr = _tile(d, PREP_ROWS)

    def w1_kernel(w_ref, o_ref):
        o_ref[:, 0:d_ff] = w_ref[:, 0:d_ff].astype(BF16)
        o_ref[:, d_ff + pad:2 * d_ff + pad] = w_ref[:, d_ff:2 * d_ff].astype(BF16)
        if pad:
            o_ref[:, d_ff:d_ff + pad] = jnp.zeros((tr, pad), BF16)
            o_ref[:, 2 * d_ff + pad:] = jnp.zeros((tr, pad), BF16)

    w1 = pl.pallas_call(
        w1_kernel,
        grid=(steps, d // tr),
        in_specs=[pl.BlockSpec((None, tr, 2 * d_ff), lambda s, r: (s, r, 0))],
        out_specs=pl.BlockSpec((None, tr, 2 * nf * tf), lambda s, r: (s, r, 0)),
        out_shape=jax.ShapeDtypeStruct((steps, d, 2 * nf * tf), BF16),
        compiler_params=_cparams(("parallel", "parallel")),
        name="ffn_w1_cast_pad",
    )(ffn_w1.reshape(steps, d, 2 * d_ff))

    tc = _tile(d, PREP_COLS)

    def w2_kernel(w_ref, o_ref):
        o_ref[0:d_ff, :] = w_ref[...].astype(BF16)
        if pad:
            o_ref[d_ff:, :] = jnp.zeros((pad, tc), BF16)

    w2 = pl.pallas_call(
        w2_kernel,
        grid=(steps, d // tc),
        in_specs=[pl.BlockSpec((None, d_ff, tc), lambda s, c: (s, 0, c))],
        out_specs=pl.BlockSpec((None, d_ff + pad, tc), lambda s, c: (s, 0, c)),
        out_shape=jax.ShapeDtypeStruct((steps, d_ff + pad, d), BF16),
        compiler_params=_cparams(("parallel", "parallel")),
        name="ffn_w2_cast_pad",
    )(ffn_w2.reshape(steps, d_ff, d))
    return w1, w2, nf


def _ffn(x, vec, w1, w2, nf, step, *, final):
    s, d = x.shape
    tf = w2.shape[1] // nf
    tm = _tile(s, FFN_ROWS)
    return pl.pallas_call(
        functools.partial(_ffn_kernel, nf=nf, final=final),
        grid=(s // tm, nf),
        in_specs=[pl.BlockSpec((tm, d), lambda i, f: (i, 0)),
                  pl.BlockSpec((8, d), lambda i, f: (0, 0)),
                  pl.BlockSpec((None, d, tf), lambda i, f: (step, 0, f)),
                  pl.BlockSpec((None, d, tf), lambda i, f: (step, 0, nf + f)),
                  pl.BlockSpec((None, tf, d), lambda i, f: (step, f, 0))],
        out_specs=pl.BlockSpec((tm, d), lambda i, f: (i, 0)),
        out_shape=jax.ShapeDtypeStruct((s, d), F32),
        scratch_shapes=[pltpu.VMEM((tm, d), BF16), pltpu.VMEM((tm, LANES), F32)],
        compiler_params=_cparams(("parallel", "arbitrary")),
        name="swiglu_halfstep",
    )(x, vec, w1, w1, w2)


def _proj2_kernel(x_ref, vec_ref, wa_ref, wb_ref, oa_ref, ob_ref, hn_sc, inv_sc, *, na):
    j = pl.program_id(1)

    @pl.when(j == 0)
    def _():
        _prenorm_into(x_ref, vec_ref, hn_sc, inv_sc)

    @pl.when(j < na)
    def _():
        oa_ref[...] = jnp.dot(hn_sc[...], wa_ref[0], preferred_element_type=F32).astype(oa_ref.dtype)

    @pl.when(j >= na)
    def _():
        ob_ref[...] = jnp.dot(hn_sc[...], wb_ref[0], preferred_element_type=F32).astype(ob_ref.dtype)


def _proj2(x, vec, wa, dtype_a, wb, dtype_b):
    s, d = x.shape
    na, _, ta = wa.shape
    nb, _, tb = wb.shape
    tm = _tile(s, PROJ_ROWS)
    return pl.pallas_call(
        functools.partial(_proj2_kernel, na=na),
        grid=(s // tm, na + nb),
        in_specs=[pl.BlockSpec((tm, d), lambda i, j: (i, 0)),
                  pl.BlockSpec((8, d), lambda i, j: (0, 0)),
                  pl.BlockSpec((1, d, ta), lambda i, j: (jnp.minimum(j, na - 1), 0, 0)),
                  pl.BlockSpec((1, d, tb), lambda i, j: (jnp.maximum(j - na, 0), 0, 0))],
        out_specs=[pl.BlockSpec((tm, ta), lambda i, j: (i, jnp.minimum(j, na - 1))),
                   pl.BlockSpec((tm, tb), lambda i, j: (i, jnp.maximum(j - na, 0)))],
        out_shape=[jax.ShapeDtypeStruct((s, na * ta), dtype_a),
                   jax.ShapeDtypeStruct((s, nb * tb), dtype_b)],
        scratch_shapes=[pltpu.VMEM((tm, d), BF16), pltpu.VMEM((tm, LANES), F32)],
        compiler_params=_cparams(("parallel", "arbitrary")),
        name="norm_mod_proj2",
    )(x, vec, wa, wb)


def _outproj_kernel(*refs, n_in):
    x_ref, vec_ref = refs[0], refs[1]
    lhs = refs[2:2 + n_in]
    ws = refs[2 + n_in:2 + 2 * n_in]
    o_ref = refs[2 + 2 * n_in]
    acc = jnp.dot(lhs[0][...], ws[0][0], preferred_element_type=F32)
    for a, w in zip(lhs[1:], ws[1:]):
        acc += jnp.dot(a[...], w[0], preferred_element_type=F32)
    o_ref[...] = x_ref[...] + (1.0 + vec_ref[3:4, :]) * acc


def _outproj(x, vec, lhs, ws):
    s, d = x.shape
    tm = _tile(s, PROJ_ROWS)
    tn = ws[0].shape[2]
    n_in = len(lhs)
    in_specs = [pl.BlockSpec((tm, tn), lambda i, j: (i, j)),
                pl.BlockSpec((8, tn), lambda i, j: (0, j))]
    in_specs += [pl.BlockSpec((tm, a.shape[1]), lambda i, j: (i, 0)) for a in lhs]
    in_specs += [pl.BlockSpec((1, w.shape[1], tn), lambda i, j: (j, 0, 0)) for w in ws]
    return pl.pallas_call(
        functools.partial(_outproj_kernel, n_in=n_in),
        grid=(s // tm, d // tn),
        in_specs=in_specs,
        out_specs=pl.BlockSpec((tm, tn), lambda i, j: (i, j)),
        out_shape=jax.ShapeDtypeStruct((s, d), F32),
        compiler_params=_cparams(("parallel", "arbitrary")),
        name="outproj_residual",
    )(x, vec, *lhs, *ws)


def _flash_kernel(*refs, nh, tq, kb, sub, diff, lam_scale):
    if diff:
        lam_ref, q_ref, k_ref, v_ref, nb_ref, g_ref, o_ref, qs_sc, m_sc, acc_sc, s_sc, p_sc, al_sc = refs
        mask_ref = None
    else:
        q_ref, k_ref, v_ref, nb_ref, mask_ref, o_ref, qs_sc, m_sc, acc_sc, s_sc, p_sc, al_sc = refs
    tk = tq
    rows = nh * tq
    hd = k_ref.shape[1]
    assert hd == LANES and rows % sub == 0 and tq % sub == 0
    i = pl.program_id(1)

    if diff:
        q = q_ref[...]
        lane = lax.broadcasted_iota(I32, q.shape, 1)
        zero = jnp.zeros_like(q)
        qs_sc[0:tq, :] = jnp.where(lane < A_QK_DIM, q, zero)
        qs_sc[tq:2 * tq, :] = jnp.where(lane >= A_QK_DIM, q, zero)
    else:
        for r in range(nh):
            qs_sc[r * tq:(r + 1) * tq, :] = q_ref[:, r * hd:(r + 1) * hd]
    m_sc[...] = jnp.full(m_sc.shape, -jnp.inf, F32)
    acc_sc[...] = jnp.zeros(acc_sc.shape, F32)

    tks = kb * tk
    nk = k_ref.shape[0] // tk
    last_step = i // kb

    def bias_index(j):
        return jnp.where(j > i, 3, jnp.clip(j - (i - 2), 0, 2))

    def stage_qk(t, slot):
        start = pl.multiple_of(jnp.minimum(t, last_step) * tks, tks)
        kblk = k_ref[pl.ds(start, tks), :]
        s_sc[slot] = lax.dot_general(qs_sc[...], kblk, (((1,), (1,)), ((), ())), preferred_element_type=F32)

    def stage_softmax(t, slot, biased):
        tc = jnp.minimum(t, last_step)
        for r in range(rows // sub):
            rs = slice(r * sub, (r + 1) * sub)
            tiles = []
            for b in range(kb):
                j = tc * kb + b
                s = s_sc[slot, rs, b * tk:(b + 1) * tk]
                if biased:
                    col = jnp.where(t > last_step, 3, bias_index(j))
                    s = s + nb_ref[0, col, rs, :]
                s = s.astype(BF16)
                if mask_ref is not None:
                    off = (r * sub) % tq
                    s = s + mask_ref[jnp.minimum(j, nk - 1), off:off + sub, :]
                tiles += [s[:, u * LANES:(u + 1) * LANES] for u in range(tk // LANES)]
            cmax = tiles[0]
            for u in tiles[1:]:
                cmax = jnp.maximum(cmax, u)
            m_old = m_sc[rs, :]
            m_new = jnp.maximum(m_old, jnp.max(cmax.astype(F32), axis=1, keepdims=True))
            al_sc[slot, rs, :] = jnp.exp2(m_old - m_new)
            m_b = m_new.astype(BF16)
            p_sc[slot, rs, :] = jnp.concatenate([jnp.exp2(u - m_b) for u in tiles], axis=1)
            m_sc[rs, :] = m_new

    def stage_pv(t, slot):
        start = pl.multiple_of(jnp.minimum(t, last_step) * tks, tks)
        vbe = jnp.concatenate([v_ref[pl.ds(start, tks), :], jnp.ones((tks, hd), BF16)], axis=1)
        pv = jnp.dot(p_sc[slot], vbe, preferred_element_type=F32)
        alpha = al_sc[slot]
        acc_sc[...] = jnp.concatenate([alpha, alpha], axis=1) * acc_sc[...] + pv

    stage_qk(0, 0)
    stage_qk(1, 1)
    stage_softmax(0, 0, True)

    nfar = jnp.maximum((i - 1) // kb, 0)
    npairs = jnp.maximum((nfar - 1) // 2, 0)

    def far_pair(u, carry):
        t = 2 * u
        stage_pv(t, 0)
        stage_softmax(t + 1, 1, False)
        stage_qk(t + 2, 0)
        stage_pv(t + 1, 1)
        stage_softmax(t + 2, 0, False)
        stage_qk(t + 3, 1)
        return carry

    lax.fori_loop(0, npairs, far_pair, 0)

    t0 = 2 * npairs
    stage_pv(t0, 0)
    stage_softmax(t0 + 1, 1, True)
    stage_qk(t0 + 2, 0)
    stage_pv(t0 + 1, 1)
    stage_softmax(t0 + 2, 0, True)
    stage_qk(t0 + 3, 1)
    stage_pv(t0 + 2, 0)

    @pl.when(t0 + 3 <= last_step)
    def _():
        stage_softmax(t0 + 3, 1, True)
        stage_pv(t0 + 3, 1)

    if diff:
        o0 = acc_sc[0:tq, 0:hd] / acc_sc[0:tq, hd:2 * hd]
        o1 = acc_sc[tq:2 * tq, 0:hd] / acc_sc[tq:2 * tq, hd:2 * hd]
        dlt = o0 - lam_ref[0] * o1
        ms = jnp.mean(dlt * dlt, axis=-1, keepdims=True)
        o_ref[...] = ((dlt * lax.rsqrt(ms + EPS) * g_ref[...]) * lam_scale).astype(o_ref.dtype)
    else:
        for r in range(nh):
            rs = slice(r * tq, (r + 1) * tq)
            o_ref[:, r * hd:(r + 1) * hd] = (acc_sc[rs, 0:hd] / acc_sc[rs, hd:2 * hd]).astype(o_ref.dtype)


def _rel_bucket(dist):
    n = jnp.maximum(dist, 0)
    max_exact = REL_BUCKETS // 2
    nf = jnp.maximum(n, 1).astype(F32)
    large = max_exact + (jnp.log(nf / max_exact) / math.log(REL_MAX_DIST / max_exact)
                         * (REL_BUCKETS - max_exact)).astype(I32)
    large = jnp.minimum(large, REL_BUCKETS - 1)
    return jnp.where(n < max_exact, n, large)


def _near_bias(rel_table, tq, groups, nh):
    assert tq >= LANES, "keys older than one block must all fall in the last bucket"
    r = jnp.arange(tq, dtype=I32)[:, None]
    c = jnp.arange(2 * tq, dtype=I32)[None, :]
    dist = r + tq - c
    rel = (rel_table - rel_table[REL_BUCKETS - 1][None, :]) * np.float32(LOG2E)
    onehot = jax.nn.one_hot(_rel_bucket(dist), REL_BUCKETS, dtype=F32)
    b = jnp.einsum("rcb,bh->hrc", onehot, rel, precision=lax.Precision.HIGHEST)
    b = jnp.where((dist >= 0)[None], b, MASKED)
    heads = b.shape[0]
    tiles = jnp.stack([jnp.zeros((heads, tq, tq), F32), b[:, :, :tq], b[:, :, tq:],
                       jnp.full((heads, tq, tq), MASKED, F32)], axis=1)
    tiles = tiles.reshape(groups, nh, 4, tq, tq).transpose(0, 2, 1, 3, 4)
    return tiles.reshape(groups, 4, nh * tq, tq)


def _diff_attention(qkv, near, lam, subln_g, lam_scale, tq):
    s = qkv.shape[0]
    hd = A_V_DIM
    nh = 2
    kb = 1
    kcol = A_WIDTH // hd
    return pl.pallas_call(
        functools.partial(_flash_kernel, nh=nh, tq=tq, kb=kb, sub=min(tq, ATTN_SUB_ROWS), diff=True,
                          lam_scale=lam_scale),
        grid=(A_HEADS, s // tq),
        in_specs=[pl.BlockSpec(memory_space=pltpu.SMEM),
                  pl.BlockSpec((tq, hd), lambda h, i: (i, h)),
                  pl.BlockSpec((s, hd), lambda h, i: (0, kcol + h)),
                  pl.BlockSpec((s, hd), lambda h, i: (0, 2 * kcol + h)),
                  pl.BlockSpec((1, 4, nh * tq, tq), lambda h, i: (h, 0, 0, 0)),
                  pl.BlockSpec((1, hd), lambda h, i: (0, 0))],
        out_specs=pl.BlockSpec((tq, hd), lambda h, i: (i, h)),
        out_shape=jax.ShapeDtypeStruct((s, A_WIDTH), BF16),
        scratch_shapes=[pltpu.VMEM((nh * tq, hd), BF16),
                        pltpu.VMEM((nh * tq, LANES), F32),
                        pltpu.VMEM((nh * tq, 2 * hd), F32),
                        pltpu.VMEM((2, nh * tq, kb * tq), F32),
                        pltpu.VMEM((2, nh * tq, kb * tq), BF16),
                        pltpu.VMEM((2, nh * tq, LANES), F32)],
        compiler_params=_cparams(("parallel", "arbitrary")),
        name="diff_attention",
    )(lam, qkv, qkv, qkv, near, subln_g)


def _masked_attention(qkv, near, mask, tq):
    s = qkv.shape[0]
    hd = C_HEAD_DIM
    nh = C_GROUP
    kcol = C_WIDTH // hd
    vcol = kcol + C_KV_HEADS
    nk = s // tq
    kb = 2 if nk % 2 == 0 else 1
    return pl.pallas_call(
        functools.partial(_flash_kernel, nh=nh, tq=tq, kb=kb, sub=min(tq, ATTN_SUB_ROWS), diff=False,
                          lam_scale=1.0),
        grid=(C_KV_HEADS, s // tq),
        in_specs=[pl.BlockSpec((tq, nh * hd), lambda g, i: (i, g)),
                  pl.BlockSpec((s, hd), lambda g, i: (0, kcol + g)),
                  pl.BlockSpec((s, hd), lambda g, i: (0, vcol + g)),
                  pl.BlockSpec((1, 4, nh * tq, tq), lambda g, i: (g, 0, 0, 0)),
                  pl.BlockSpec((nk, tq, tq), lambda g, i: (0, i, 0))],
        out_specs=pl.BlockSpec((tq, nh * hd), lambda g, i: (i, g)),
        out_shape=jax.ShapeDtypeStruct((s, C_WIDTH), BF16),
        scratch_shapes=[pltpu.VMEM((nh * tq, hd), BF16),
                        pltpu.VMEM((nh * tq, LANES), F32),
                        pltpu.VMEM((nh * tq, 2 * hd), F32),
                        pltpu.VMEM((2, nh * tq, kb * tq), F32),
                        pltpu.VMEM((2, nh * tq, kb * tq), BF16),
                        pltpu.VMEM((2, nh * tq, LANES), F32)],
        compiler_params=_cparams(("parallel", "arbitrary")),
        name="selected_attention",
    )(qkv, qkv, qkv, near, mask)


def _sg_kernel(zb_ref, lng_ref, lnb_ref, w_ref, bs_ref, o_ref, *, nchunk):
    zb = zb_ref[...]
    gl = zb * (0.5 * (1.0 + jnp.tanh(np.float32(np.sqrt(2.0 / np.pi)) * (zb + 0.044715 * (zb * zb * zb)))))
    u = gl[:, :B_WIDTH]
    z = gl[:, B_WIDTH:]
    mu = jnp.mean(z, axis=-1, keepdims=True)
    zc = z - mu
    var = jnp.mean(zc * zc, axis=-1, keepdims=True)
    zn = (zc * lax.rsqrt(var + EPS) * lng_ref[...] + lnb_ref[...]).astype(BF16)
    row = lax.broadcasted_iota(I32, (CHUNK, CHUNK), 0)
    col = lax.broadcasted_iota(I32, (CHUNK, CHUNK), 1)
    for g in range(B_GROUPS):
        w = jnp.where(row >= col, w_ref[g], 0.0).astype(BF16)
        bias = bs_ref[g]
        lo = g * B_GROUP_DIM
        for c in range(nchunk):
            r0 = c * CHUNK
            sz = jnp.dot(w, zn[r0:r0 + CHUNK, lo:lo + B_GROUP_DIM], preferred_element_type=F32) + bias
            o_ref[r0:r0 + CHUNK, lo:lo + B_GROUP_DIM] = (u[r0:r0 + CHUNK, lo:lo + B_GROUP_DIM] * sz).astype(o_ref.dtype)


def _spatial_gating(zb, ln_g, ln_b, w_s, b_s):
    s = zb.shape[0]
    t = _tile(s, 512)
    return pl.pallas_call(
        functools.partial(_sg_kernel, nchunk=t // CHUNK),
        grid=(s // t,),
        in_specs=[pl.BlockSpec((t, 2 * B_WIDTH), lambda i: (i, 0)),
                  pl.BlockSpec((1, B_WIDTH), lambda i: (0, 0)),
                  pl.BlockSpec((1, B_WIDTH), lambda i: (0, 0)),
                  pl.BlockSpec((B_GROUPS, CHUNK, CHUNK), lambda i: (0, 0, 0)),
                  pl.BlockSpec((B_GROUPS, CHUNK, 1), lambda i: (0, 0, 0))],
        out_specs=pl.BlockSpec((t, B_WIDTH), lambda i: (i, 0)),
        out_shape=jax.ShapeDtypeStruct((s, B_WIDTH), BF16),
        compiler_params=_cparams(("parallel",)),
        name="spatial_gating",
    )(zb, ln_g.reshape(1, B_WIDTH), ln_b.reshape(1, B_WIDTH), w_s, b_s.reshape(B_GROUPS, CHUNK, 1))


def _select_kernel(qi_ref, kt_ref, w_ref, o_ref, keys_sc, sc_sc, wb_sc, mx_sc, *, tkc, topk):
    tqi = IDX_QBLOCK
    i = pl.program_id(0)
    nk = o_ref.shape[0]
    nch = (i * tqi + tqi + tkc - 1) // tkc
    qpos = i * tqi + lax.broadcasted_iota(I32, (tqi, tkc), 0)
    kloc = lax.broadcasted_iota(I32, (tqi, tkc), 1)
    qrow = i * tqi + lax.broadcasted_iota(I32, (tqi, LANES), 0)
    klane = lax.broadcasted_iota(I32, (tqi, LANES), 1)

    wgt = w_ref[...] * np.float32(IDX_DIM ** -0.5)
    for h in range(IDX_HEADS):
        wb_sc[h] = jnp.broadcast_to(wgt[:, h:h + 1], (tqi, LANES))
    mx_sc[...] = jnp.full(mx_sc.shape, -jnp.inf, F32)

    def stage_dot(c, slot):
        sc_sc[slot] = jnp.dot(qi_ref[0], kt_ref[jnp.minimum(c, nch - 1)], preferred_element_type=F32)

    def stage_reduce(c, slot):
        c = jnp.minimum(c, nch - 1)
        for u in range(tkc // LANES):
            ls = slice(u * LANES, (u + 1) * LANES)
            acc = jnp.zeros((tqi, LANES), F32)
            for h in range(IDX_HEADS):
                acc += jnp.maximum(sc_sc[slot, h * tqi:(h + 1) * tqi, ls], 0.0) * wb_sc[h]
            acc = acc + 0.0
            bits = pltpu.bitcast(acc, I32)
            ordered = jnp.where(bits < 0, bits ^ jnp.int32(0x7FFFFFFF), bits)
            causal = c * tkc + u * LANES + klane <= qrow
            keys_sc[c, :, ls] = jnp.where(causal, ordered, jnp.int32(INT_MIN))
            mx_sc[...] = jnp.maximum(mx_sc[...], jnp.where(causal, acc, -jnp.inf))

    stage_dot(0, 0)

    def score_quad(u, carry):
        c = 4 * u
        for k in range(0, 4, 2):
            stage_dot(c + k + 1, 1)
            stage_reduce(c + k, 0)
            stage_dot(c + k + 2, 0)
            stage_reduce(c + k + 1, 1)
        return carry

    lax.fori_loop(0, (nch + 3) // 4, score_quad, 0)

    def count_ge(cand):
        candb = jnp.broadcast_to(cand, (tqi, LANES))

        def count_body(c, cnt):
            kk = keys_sc[c]
            for u in range(tkc // LANES):
                cnt += jnp.where(kk[:, u * LANES:(u + 1) * LANES] >= candb, 1, 0)
            return cnt

        cnt = lax.fori_loop(0, nch, count_body, jnp.zeros((tqi, LANES), I32))
        return jnp.sum(cnt.astype(F32), axis=1, keepdims=True)

    want = np.float32(topk)
    fbits = pltpu.bitcast(jnp.max(mx_sc[...], axis=1, keepdims=True), I32)
    kmax = jnp.where(fbits < 0, fbits ^ jnp.int32(0x7FFFFFFF), fbits)
    few = i * tqi + lax.broadcasted_iota(I32, (tqi, 1), 0) + 1 <= topk
    lo0 = jnp.full((tqi, 1), INT_MIN, I32)
    hi0 = jnp.where(few, lo0 + 1, kmax + 1)

    def narrow(state, cand):
        lo, hi, active = state
        total = count_ge(cand)
        open_ = active > 0.0
        up = open_ & (total >= want)
        lo = jnp.where(up, cand, lo)
        hi = jnp.where(open_ & (~up), cand, hi)
        open_ = open_ & (~(up & (total == want))) & ((hi - lo) != 1)
        return lo, hi, jnp.where(open_, 1.0, 0.0)

    def midpoint(state):
        lo, hi, _ = state
        return lo + lax.shift_right_logical(hi - lo, jnp.int32(1))

    probe = jnp.maximum(kmax, jnp.int32(INT_MIN + SEARCH_PROBE_DROP + 1)) - jnp.int32(SEARCH_PROBE_DROP)
    state = narrow((lo0, hi0, jnp.where(few, 0.0, 1.0)), jnp.where(few, lo0, probe))

    def bisect_body(carry):
        state, _ = carry
        state = narrow(state, midpoint(state))
        state = narrow(state, midpoint(state))
        return state, jnp.sum(state[2])

    (thr, _, _), _ = lax.while_loop(lambda carry: carry[1] > 0.0, bisect_body, (state, jnp.float32(1.0)))
    thrb = jnp.broadcast_to(thr, (tqi, tkc))

    def mask_body(c, carry):
        sel = (keys_sc[c] >= thrb) & (c * tkc + kloc <= qpos)
        o_ref[c] = jnp.where(sel, 0.0, MASKED).astype(o_ref.dtype)
        return carry

    lax.fori_loop(0, nch, mask_body, 0)

    def fill_body(c, carry):
        o_ref[c] = jnp.full((tqi, tkc), MASKED, o_ref.dtype)
        return carry

    lax.fori_loop(nch, nk, fill_body, 0)


def _select_mask(qi_stack, kt, wi, tkc, topk):
    nq, rows, _ = qi_stack.shape
    nk = kt.shape[0]
    s = nq * IDX_QBLOCK
    return pl.pallas_call(
        functools.partial(_select_kernel, tkc=tkc, topk=topk),
        grid=(nq,),
        in_specs=[pl.BlockSpec((1, rows, IDX_DIM), lambda i: (i, 0, 0)),
                  pl.BlockSpec((nk, IDX_DIM, tkc), lambda i: (0, 0, 0)),
                  pl.BlockSpec((IDX_QBLOCK, IDX_HEADS), lambda i: (i, 0))],
        out_specs=pl.BlockSpec((nk, IDX_QBLOCK, tkc), lambda i: (0, i, 0)),
        out_shape=jax.ShapeDtypeStruct((nk, s, tkc), BF16),
        scratch_shapes=[pltpu.VMEM((nk, IDX_QBLOCK, tkc), I32),
                        pltpu.VMEM((2, rows, tkc), F32),
                        pltpu.VMEM((IDX_HEADS, IDX_QBLOCK, LANES), F32),
                        pltpu.VMEM((IDX_QBLOCK, LANES), F32)],
        compiler_params=_cparams(("parallel",)),
        name="indexer_select",
    )(qi_stack, kt, wi)


def _vec_pack(d, *rows):
    rows = [r.reshape(1, d).astype(F32) for r in rows]
    rows += [jnp.zeros((1, d), F32)] * (8 - len(rows))
    return jnp.concatenate(rows, axis=0)


def _pad_cols(w, n):
    return jnp.pad(w, ((0, 0), (0, n - w.shape[1])))


def kernel(x, c, norm_g, mod_w, mod_b, ffn_w1, ffn_w2, rel_table, ab_w_in, ab_w_out, diff_lam,
           diff_subln_g, sg_ln_g, sg_ln_b, sg_w, sg_b, dsa_w_in, dsa_w_out, final_g):
    batch, s, d = x.shape
    depth = norm_g.shape[0]
    assert batch == 1 and s % IDX_QBLOCK == 0

    tq_a = _tile(s, 512)
    tq_c = _tile(s, 256)
    topk = min(TOPK_MAX, s // 4)

    mod = _modulation(c, mod_w, mod_b).reshape(depth, 9, d)
    ffn_a, ffn_b, ffn_tiles = _ffn_weights(ffn_w1, ffn_w2)
    near_a = _near_bias(rel_table, tq_a, A_HEADS, 2)
    near_c = _near_bias(rel_table, tq_c, C_KV_HEADS, C_GROUP)

    xs = x.reshape(s, d)
    zeros_d = jnp.zeros((d,), F32)
    for li in range(depth):
        def vec(j, li=li):
            last = final_g if (li == depth - 1 and j == 2) else zeros_d
            return _vec_pack(d, norm_g[li, j], mod[li, 3 * j], mod[li, 3 * j + 1], mod[li, 3 * j + 2], last)

        def ffn(xs, j, k, final=False, li=li):
            return _ffn(xs, vec(j), ffn_a, ffn_b, ffn_tiles, 2 * li + k, final=final)

        xs = ffn(xs, 0, 0)

        v1 = vec(1)
        jj = li // 2
        if li % 2 == 0:
            w_in = ab_w_in[jj]
            w_qkv = jnp.concatenate([w_in[:, :A_WIDTH] * np.float32(A_QK_DIM ** -0.5 * LOG2E),
                                     w_in[:, A_WIDTH:3 * A_WIDTH]], axis=1).astype(BF16)
            w_zb = w_in[:, 3 * A_WIDTH:].astype(BF16)
            qkv, zb = _proj2(xs, v1, _col_blocks(w_qkv, PROJ_COLS), BF16, _col_blocks(w_zb, PROJ_COLS), F32)
            lam_init = 0.8 - 0.6 * math.exp(-0.3 * li)
            lp = diff_lam[jj].astype(F32)
            lam = jnp.exp(jnp.sum(lp[0] * lp[1])) - jnp.exp(jnp.sum(lp[2] * lp[3])) + lam_init
            ya = _diff_attention(qkv, near_a, lam.reshape(1), diff_subln_g[jj].reshape(1, A_V_DIM),
                                 1.0 - lam_init, tq_a)
            yb = _spatial_gating(zb, sg_ln_g[jj], sg_ln_b[jj], sg_w[jj], sg_b[jj])
            w_out = ab_w_out[jj].astype(BF16)
            xs = _outproj(xs, v1, [ya, yb], [_col_blocks(w_out[:A_WIDTH], PROJ_COLS),
                                             _col_blocks(w_out[A_WIDTH:], PROJ_COLS)])
        else:
            w_in = dsa_w_in[jj]
            o_idx = C_WIDTH + 2 * C_KV_WIDTH
            o_ki = o_idx + IDX_HEADS * IDX_DIM
            w_main = jnp.concatenate([w_in[:, :C_WIDTH] * np.float32(C_HEAD_DIM ** -0.5 * LOG2E),
                                      w_in[:, C_WIDTH:o_ki]], axis=1).astype(BF16)
            w_kiw = _pad_cols(w_in[:, o_ki:], LANES).astype(BF16)
            main, kiw = _proj2(xs, v1, _col_blocks(w_main, PROJ_COLS), BF16, _col_blocks(w_kiw, PROJ_COLS), F32)
            nq = s // IDX_QBLOCK
            qi = main[:, o_idx:o_ki].reshape(nq, IDX_QBLOCK, IDX_HEADS, IDX_DIM)
            qi = qi.transpose(0, 2, 1, 3).reshape(nq, IDX_HEADS * IDX_QBLOCK, IDX_DIM)
            kt = kiw[:, :IDX_DIM].astype(BF16).reshape(s // tq_c, tq_c, IDX_DIM).transpose(0, 2, 1)
            wi = kiw[:, IDX_DIM:IDX_DIM + IDX_HEADS]
            mask = _select_mask(qi, kt, wi, tq_c, topk)
            yc = _masked_attention(main, near_c, mask, tq_c)
            xs = _outproj(xs, v1, [yc], [_col_blocks(dsa_w_out[jj].astype(BF16), PROJ_COLS)])

        xs = ffn(xs, 2, 1, final=(li == depth - 1))
    return xs.reshape(batch, s, d)
```
